```python
import jax, jax.numpy as jnp
from jax import lax
import numpy as np

D_MODEL = 2048
BATCH = 16
SEQ = 2048
DEPTH = 2

N_ATTN_HEADS = 8
ATTN_HEAD_DIM = 128
ATTN_ROT_DIM = ATTN_HEAD_DIM // 4
ROPE_THETA = 500000.0
Q_BLOCK = 128
TOPK_MAX = 256
N_IDX_HEADS = 16
IDX_HEAD_DIM = 64
IDX_ROT_DIM = IDX_HEAD_DIM // 4
N_RET_HEADS = 8
RET_KEY_DIM = 64
RET_VAL_DIM = 128
RET_CHUNK = 128
RET_THETA = 10000.0
ATTN_WIDTH = N_ATTN_HEADS * ATTN_HEAD_DIM
RET_WIDTH = N_RET_HEADS * RET_VAL_DIM
MIX_WIDTH = ATTN_WIDTH + RET_WIDTH
D_FF = 4 * D_MODEL
PLE_DIM = 256
RMS_EPS = 1e-6
GN_EPS = 1e-5
SPLIT_SIZES = (
    ATTN_WIDTH,
    ATTN_HEAD_DIM,
    ATTN_HEAD_DIM,
    N_IDX_HEADS * IDX_HEAD_DIM,
    IDX_HEAD_DIM,
    N_IDX_HEADS,
    N_RET_HEADS * RET_KEY_DIM,
    N_RET_HEADS * RET_KEY_DIM,
    RET_WIDTH,
    RET_WIDTH,
)
IN_WIDTH = sum(SPLIT_SIZES)

kernel_name = "hymba_dsa_retention_hybrid"


def rmsnorm(x, w):
    xf = x.astype(jnp.float32)
    y = xf * lax.rsqrt(jnp.mean(xf * xf, axis=-1, keepdims=True) + RMS_EPS)
    return (y * w.astype(jnp.float32)).astype(x.dtype)


def rope_tables(positions, rot_dim, theta):
    half = rot_dim // 2
    inv = theta ** (-jnp.arange(half, dtype=jnp.float32) / half)
    ang = positions.astype(jnp.float32)[..., None] * inv
    return jnp.cos(ang), jnp.sin(ang)


def apply_rope(x, cos, sin):
    half = cos.shape[-1]
    if x.ndim == 4:
        cos, sin = cos[:, :, None, :], sin[:, :, None, :]
    cos, sin = cos.astype(x.dtype), sin.astype(x.dtype)
    x1, x2, rest = x[..., :half], x[..., half:2 * half], x[..., 2 * half:]
    return jnp.concatenate([x1 * cos - x2 * sin, x2 * cos + x1 * sin, rest], axis=-1)


def split_columns(z):
    offsets, acc = [], 0
    for s in SPLIT_SIZES[:-1]:
        acc += s
        offsets.append(acc)
    return jnp.split(z, offsets, axis=-1)


def dsa_attention(q, k, v, iq, ik, iw, topk):
    B, S = q.shape[0], q.shape[1]
    nb = S // Q_BLOCK
    key_pos = jnp.arange(S)
    ikf = ik.astype(jnp.float32)
    idx_scale = IDX_HEAD_DIM ** -0.5
    w_scale = N_IDX_HEADS ** -0.5
    attn_scale = ATTN_HEAD_DIM ** -0.5
    gather = jax.vmap(lambda arr, ids: arr[ids])

    def to_blocks(a):
        return a.reshape((B, nb, Q_BLOCK) + a.shape[2:]).swapaxes(0, 1)

    def one_block(args):
        qb, iqb, iwb, qpos = args
        logits = jnp.einsum('bqhd,bsd->bqhs', iqb.astype(jnp.float32), ikf) * idx_scale
        score = jnp.einsum('bqh,bqhs->bqs', iwb.astype(jnp.float32) * w_scale,
                           jax.nn.relu(logits))
        causal = key_pos[None, :] <= qpos[:, None]
        score = jnp.where(causal[None], score, -jnp.inf)
        _, sel = lax.top_k(score, topk)
        k_sel = gather(k, sel)
        v_sel = gather(v, sel)
        valid = sel <= qpos[None, :, None]
        s = jnp.einsum('bqhd,bqkd->bqhk', qb, k_sel).astype(jnp.float32) * attn_scale
        s = jnp.where(valid[:, :, None, :], s, -jnp.inf)
        pr = jax.nn.softmax(s, axis=-1).astype(v.dtype)
        return jnp.einsum('bqhk,bqkd->bqhd', pr, v_sel)

    qpos = jnp.arange(S).reshape(nb, Q_BLOCK)
    out = lax.map(one_block, (to_blocks(q), to_blocks(iq), to_blocks(iw), qpos))
    return out.swapaxes(0, 1).reshape(B, S, N_ATTN_HEADS, ATTN_HEAD_DIM)


def retention(q, k, v):
    B, S, H, dk = q.shape
    dv = v.shape[-1]
    C = RET_CHUNK
    nc = S // C
    gamma = 1.0 - 2.0 ** (-5.0 - jnp.arange(H, dtype=jnp.float32))
    log_g = jnp.log(gamma)
    i = jnp.arange(C, dtype=jnp.float32)
    diff = i[:, None] - i[None, :]
    decay = jnp.where(diff[None] >= 0, jnp.exp(jnp.maximum(diff, 0.0)[None] * log_g[:, None, None]), 0.0)
    zeta = jnp.exp((C - 1.0 - i)[None, :] * log_g[:, None])
    xi = jnp.exp((i + 1.0)[None, :] * log_g[:, None])
    g_chunk = jnp.exp(C * log_g)

    def to_chunks(a):
        return a.astype(jnp.float32).reshape(B, nc, C, H, a.shape[-1]).transpose(1, 0, 3, 2, 4)

    qc = to_chunks(q)
    kc = to_chunks(k) * (dk ** -0.5)
    vc = to_chunks(v)

    def step(R, inp):
        qb, kb, vb = inp
        inner = jnp.einsum('bhid,bhjd->bhij', qb, kb) * decay[None]
        o = (jnp.einsum('bhij,bhjv->bhiv', inner, vb)
             + jnp.einsum('bhid,bhdv->bhiv', qb, R) * xi[None, :, :, None])
        R = g_chunk[None, :, None, None] * R + jnp.einsum(
            'bhjd,bhjv->bhdv', kb * zeta[None, :, :, None], vb)
        return R, o

    R0 = jnp.zeros((B, H, dk, dv), jnp.float32)
    _, o = lax.scan(step, R0, (qc, kc, vc))
    return o.transpose(1, 0, 3, 2, 4).reshape(B, S, H, dv)


def head_groupnorm(o, w):
    mu = jnp.mean(o, axis=-1, keepdims=True)
    var = jnp.mean(jnp.square(o - mu), axis=-1, keepdims=True)
    y = (o - mu) * lax.rsqrt(var + GN_EPS)
    return y.reshape(o.shape[0], o.shape[1], -1) * w.astype(jnp.float32)


def hybrid_layer(h, p_i, rope_attn, rope_idx, rope_ret, w_in, w_out, w_ff1, w_ff2,
                 w_ple, w_ple_gate, pre_mix_norm, post_mix_norm, pre_ff_norm,
                 post_ff_norm, ple_norm, ret_gn, topk):
    B, S, _ = h.shape
    a = rmsnorm(h, pre_mix_norm)
    z = a @ w_in
    aq, ak, av, iq, ik, iw, rq, rk, rv, rg = split_columns(z)
    aq = apply_rope(aq.reshape(B, S, N_ATTN_HEADS, ATTN_HEAD_DIM), *rope_attn)
    ak = apply_rope(ak, *rope_attn)
    iq = apply_rope(iq.reshape(B, S, N_IDX_HEADS, IDX_HEAD_DIM), *rope_idx)
    ik = apply_rope(ik, *rope_idx)
    attn = dsa_attention(aq, ak, av, iq, ik, iw, topk).reshape(B, S, ATTN_WIDTH)

    rq = apply_rope(rq.reshape(B, S, N_RET_HEADS, RET_KEY_DIM), *rope_ret)
    rk = apply_rope(rk.reshape(B, S, N_RET_HEADS, RET_KEY_DIM), *rope_ret)
    rv = rv.reshape(B, S, N_RET_HEADS, RET_VAL_DIM)
    ret = retention(rq, rk, rv)
    ret = (head_groupnorm(ret, ret_gn) * jax.nn.silu(rg.astype(jnp.float32))).astype(h.dtype)

    mix = jnp.concatenate([attn, ret], axis=-1) @ w_out
    h = h + rmsnorm(mix, post_mix_norm)
    m = rmsnorm(h, pre_ff_norm)
    y = jnp.square(jax.nn.relu(m @ w_ff1)) @ w_ff2
    h = h + rmsnorm(y, post_ff_norm)
    e = p_i @ w_ple
    g = jax.nn.sigmoid(h @ w_ple_gate)
    h = h + rmsnorm(g * e, ple_norm)
    return h


def setup_inputs(seed: int = 0) -> dict:
    key = jax.random.key(seed)
    ks = jax.random.split(key, 16)
    f32 = jnp.float32

    def nrm(k, shape, scale):
        return jax.random.normal(k, shape, f32) * scale

    def gain(k, n):
        return 1.0 + 0.02 * jax.random.normal(k, (DEPTH, n), f32)

    return {
        "x": nrm(ks[0], (BATCH, SEQ, D_MODEL), 1.0),
        "p": nrm(ks[1], (DEPTH, BATCH, SEQ, PLE_DIM), 1.0),
        "positions": jnp.broadcast_to(jnp.arange(SEQ, dtype=jnp.int32), (BATCH, SEQ)),
        "w_in": nrm(ks[2], (DEPTH, D_MODEL, IN_WIDTH), D_MODEL ** -0.5),
        "w_out": nrm(ks[3], (DEPTH, MIX_WIDTH, D_MODEL), MIX_WIDTH ** -0.5),
        "w_ff1": nrm(ks[4], (DEPTH, D_MODEL, D_FF), D_MODEL ** -0.5),
        "w_ff2": nrm(ks[5], (DEPTH, D_FF, D_MODEL), D_FF ** -0.5),
        "w_ple": nrm(ks[6], (DEPTH, PLE_DIM, D_MODEL), PLE_DIM ** -0.5),
        "w_ple_gate": nrm(ks[7], (DEPTH, D_MODEL, D_MODEL), D_MODEL ** -0.5),
        "pre_mix_norm": gain(ks[8], D_MODEL),
        "post_mix_norm": gain(ks[9], D_MODEL),
        "pre_ff_norm": gain(ks[10], D_MODEL),
        "post_ff_norm": gain(ks[11], D_MODEL),
        "ple_norm": gain(ks[12], D_MODEL),
        "ret_gn": gain(ks[13], RET_WIDTH),
    }


def reference(x, p, positions, w_in, w_out, w_ff1, w_ff2, w_ple, w_ple_gate,
              pre_mix_norm, post_mix_norm, pre_ff_norm, post_ff_norm, ple_norm, ret_gn):
    seq = x.shape[1]
    topk = min(TOPK_MAX, seq // 4)
    rope_attn = rope_tables(positions, ATTN_ROT_DIM, ROPE_THETA)
    rope_idx = rope_tables(positions, IDX_ROT_DIM, ROPE_THETA)
    rope_ret = rope_tables(positions, RET_KEY_DIM, RET_THETA)
    h = x
    for i in range(DEPTH):
        h = hybrid_layer(h, p[i], rope_attn, rope_idx, rope_ret,
                         w_in[i], w_out[i], w_ff1[i], w_ff2[i], w_ple[i], w_ple_gate[i],
                         pre_mix_norm[i], post_mix_norm[i], pre_ff_norm[i],
                         post_ff_norm[i], ple_norm[i], ret_gn[i], topk)
    return h
```

```python
import functools
import math

import jax
import jax.numpy as jnp
from jax import lax
from jax.experimental import pallas as pl
from jax.experimental.pallas import tpu as pltpu

N_ATTN_HEADS = 8
ATTN_HEAD_DIM = 128
ATTN_ROT_HALF = 16
ROPE_THETA = 500000.0
TOPK_MAX = 256
N_IDX_HEADS = 16
IDX_HEAD_DIM = 64
IDX_ROT_HALF = 8
N_RET_HEADS = 8
RET_KEY_DIM = 64
RET_VAL_DIM = 128
RET_THETA = 10000.0
ATTN_WIDTH = N_ATTN_HEADS * ATTN_HEAD_DIM
IDX_WIDTH = N_IDX_HEADS * IDX_HEAD_DIM
RET_KEY_WIDTH = N_RET_HEADS * RET_KEY_DIM
RET_WIDTH = N_RET_HEADS * RET_VAL_DIM
RMS_EPS = 1e-6
GN_EPS = 1e-5

LANES = 128
COL_AQ = 0
COL_IQ = COL_AQ + ATTN_WIDTH
COL_RV = COL_IQ + IDX_WIDTH
COL_RG = COL_RV + RET_WIDTH
COL_RQ = COL_RG + RET_WIDTH
COL_RK = COL_RQ + RET_KEY_WIDTH
COL_AK = COL_RK + RET_KEY_WIDTH
COL_AV = COL_AK + ATTN_HEAD_DIM
COL_IKW = COL_AV + ATTN_HEAD_DIM
Z_USED = COL_IKW + LANES
Z_TILE_N = 512
Z_WIDTH = -(-Z_USED // Z_TILE_N) * Z_TILE_N

MASK_NEG = -1e30
THETA_ALL = -3e38
CUT_ALL = 1e9
VMEM_LIMIT = 56 * 1024 * 1024

BF16 = jnp.bfloat16
F32 = jnp.float32


def _cparams(n_grid):
    return pltpu.CompilerParams(dimension_semantics=("arbitrary",) * n_grid,
                                vmem_limit_bytes=VMEM_LIMIT)


def _rms(x, gain):
    ms = jnp.mean(x * x, axis=-1, keepdims=True)
    return x * lax.rsqrt(ms + RMS_EPS) * gain


def _rope(x, tab, shift):
    c, s1, s2 = tab[:, :LANES], tab[:, LANES:2 * LANES], tab[:, 2 * LANES:]
    return x * c + pltpu.roll(x, shift, 1) * s1 + pltpu.roll(x, LANES - shift, 1) * s2


def _dot_t(a, b):
    return lax.dot_general(a, b, (((1,), (1,)), ((), ())), preferred_element_type=F32)


def _dot(a, b):
    return jnp.dot(a, b, preferred_element_type=F32)


def _norm_proj_kernel(h_ref, g_ref, w_ref, z_ref, a_ref, *, row_chunk):
    @pl.when(pl.program_id(1) == 0)
    def _():
        def body(r, _):
            r0 = pl.multiple_of(r * row_chunk, row_chunk)
            a_ref[pl.ds(r0, row_chunk), :] = _rms(h_ref[pl.ds(r0, row_chunk), :], g_ref[...]).astype(BF16)
            return 0
        lax.fori_loop(0, h_ref.shape[0] // row_chunk, body, 0)

    z_ref[...] = _dot(a_ref[...], w_ref[...]).astype(z_ref.dtype)


def _norm_proj(h, gain, w):
    T, D = h.shape
    N = w.shape[1]
    tm = min(1024, T)
    tn = Z_TILE_N
    return pl.pallas_call(
        functools.partial(_norm_proj_kernel, row_chunk=min(128, tm)),
        grid=(T // tm, N // tn),
        in_specs=[pl.BlockSpec((tm, D), lambda i, j: (i, 0)),
                  pl.BlockSpec((1, D), lambda i, j: (0, 0)),
                  pl.BlockSpec((D, tn), lambda i, j: (0, j))],
        out_specs=pl.BlockSpec((tm, tn), lambda i, j: (i, j)),
        out_shape=jax.ShapeDtypeStruct((T, N), BF16),
        scratch_shapes=[pltpu.VMEM((tm, D), BF16)],
        compiler_params=_cparams(2),
        name="norm_proj",
    )(h, gain, w)


def _attn_kernel(aq_ref, iq_ref, iwq_ref, ak_ref, av_ref, ikw_ref,
                 taq_ref, tak_ref, tiq_ref, tik_ref, o_ref,
                 kr_ref, kd0_ref, kd1_ref, qa_ref, qi_ref, wb_ref, score_ref,
                 lo_ref, hi_ref, chi_ref, done_ref, th_ref, tie_ref, cut_ref,
                 m_ref, l_ref, acc_ref, *, topk, tq, tk):
    S = ak_ref.shape[1]
    qi = pl.program_id(1)
    n_chunks = (qi * tq + tq + tk - 1) // tk
    reps = tk // LANES
    lane = lax.broadcasted_iota(jnp.int32, (tq, LANES), 1)

    @pl.when(qi == 0)
    def _():
        rows = min(256, S)

        def body(r, _):
            r0 = pl.multiple_of(r * rows, rows)
            sl = pl.ds(r0, rows)
            kr_ref[sl, :] = _rope(ak_ref[0, sl, :].astype(F32), tak_ref[0, sl, :], ATTN_ROT_HALF).astype(BF16)
            ik = _rope(ikw_ref[0, sl, :].astype(F32), tik_ref[0, sl, :], IDX_ROT_HALF)
            lane_r = lax.broadcasted_iota(jnp.int32, (rows, LANES), 1)
            ik = jnp.where(lane_r < IDX_HEAD_DIM, ik, 0.0)
            kd0_ref[sl, :] = ik.astype(BF16)
            kd1_ref[sl, :] = pltpu.roll(ik, IDX_HEAD_DIM, 1).astype(BF16)
            return 0
        lax.fori_loop(0, S // rows, body, 0)

    ta = taq_ref[0]
    for h in range(N_ATTN_HEADS):
        x = aq_ref[0, :, h * LANES:(h + 1) * LANES].astype(F32)
        qa_ref[h * tq:(h + 1) * tq, :] = (_rope(x, ta, ATTN_ROT_HALF) * (ATTN_HEAD_DIM ** -0.5)).astype(BF16)
    ti = tiq_ref[0]
    for j in range(IDX_WIDTH // LANES):
        x = iq_ref[0, :, j * LANES:(j + 1) * LANES].astype(F32)
        qi_ref[:, j * LANES:(j + 1) * LANES] = _rope(x, ti, IDX_ROT_HALF).astype(BF16)
    iw = iwq_ref[0].astype(F32) * ((N_IDX_HEADS ** -0.5) * (IDX_HEAD_DIM ** -0.5))
    for h in range(N_IDX_HEADS):
        wb_ref[h] = jnp.broadcast_to(iw[:, IDX_HEAD_DIM + h:IDX_HEAD_DIM + h + 1], (tq, LANES))

    row = qi * tq + lax.broadcasted_iota(jnp.int32, (tq, tk), 0)
    col_in_chunk = lax.broadcasted_iota(jnp.int32, (tq, tk), 1)

    def lane_groups(x, op):
        r = x[:, :LANES]
        for g in range(1, reps):
            r = op(r, x[:, g * LANES:(g + 1) * LANES])
        return r

    def tile_lanes(x):
        return jnp.concatenate([x] * reps, axis=1) if reps > 1 else x

    def a_body(c, carry):
        rmax, rmin = carry
        c0 = pl.multiple_of(c * tk, tk)
        k0 = kd0_ref[pl.ds(c0, tk), :]
        k1 = kd1_ref[pl.ds(c0, tk), :]
        acc = jnp.zeros((tq, tk), F32)
        for j in range(IDX_WIDTH // LANES):
            qp = qi_ref[:, j * LANES:(j + 1) * LANES]
            for par, kd in ((0, k0), (1, k1)):
                logits = _dot_t(qp, kd)
                acc = acc + tile_lanes(wb_ref[2 * j + par]) * jnp.maximum(logits, 0.0)
        causal = (c0 + col_in_chunk) <= row
        score_ref[c] = jnp.where(causal, acc, -jnp.inf)
        rmax = jnp.maximum(rmax, lane_groups(jnp.where(causal, acc, -jnp.inf), jnp.maximum))
        rmin = jnp.minimum(rmin, lane_groups(jnp.where(causal, acc, jnp.inf), jnp.minimum))
        return rmax, rmin

    rmax, rmin = lax.fori_loop(0, n_chunks, a_body,
                               (jnp.full((tq, LANES), -jnp.inf, F32), jnp.full((tq, LANES), jnp.inf, F32)))
    full = (tq, LANES)
    hi0 = jnp.broadcast_to(jnp.max(rmax, axis=1, keepdims=True), full)
    lo0 = jnp.broadcast_to(jnp.min(rmin, axis=1, keepdims=True), full)

    n_keys = qi * tq + lax.broadcasted_iota(jnp.int32, full, 0) + 1
    take_all = n_keys <= topk
    lo_ref[...] = lo0
    hi_ref[...] = hi0
    chi_ref[...] = jnp.zeros(full, F32)
    done_ref[...] = jnp.where(take_all, 1.0, 0.0)
    th_ref[...] = jnp.full(full, THETA_ALL, F32)
    tie_ref[...] = jnp.zeros(full, F32)
    ones_b = jnp.ones((LANES, LANES), BF16)
    kf = float(topk)

    def count_ge(mid):
        def body(c, acc):
            sc = score_ref[c]
            for g in range(reps):
                acc = acc + jnp.where(sc[:, g * LANES:(g + 1) * LANES] >= mid, 1.0, 0.0)
            return acc
        acc = lax.fori_loop(0, n_chunks, body, jnp.zeros(full, F32))
        return _dot(acc.astype(BF16), ones_b)

    def b_cond(carry):
        return carry[1] > 0.0

    def b_body(carry):
        it, _ = carry
        lo, hi, done = lo_ref[...], hi_ref[...], done_ref[...]
        mid = jnp.where(it == 0, hi, 0.5 * lo + 0.5 * hi)
        cnt = count_ge(mid)
        active = done == 0.0
        adjacent = jnp.logical_and(it > 0, jnp.logical_or(mid <= lo, mid >= hi))
        tie_now = jnp.logical_and(active, adjacent)
        live = jnp.logical_and(active, jnp.logical_not(adjacent))
        hit = jnp.logical_and(live, cnt == kf)
        above = jnp.logical_and(live, cnt > kf)
        below = jnp.logical_and(live, cnt < kf)
        th_ref[...] = jnp.where(tie_now, lo, jnp.where(hit, mid, th_ref[...]))
        tie_ref[...] = jnp.where(tie_now, 1.0, tie_ref[...])
        lo_ref[...] = jnp.where(above, mid, lo)
        hi_ref[...] = jnp.where(below, mid, hi)
        chi_ref[...] = jnp.where(below, cnt, chi_ref[...])
        new_done = jnp.where(jnp.logical_or(tie_now, hit), 1.0, done)
        done_ref[...] = new_done
        return it + 1, jnp.max(1.0 - new_done)

    lax.while_loop(b_cond, b_body, (jnp.int32(0), jnp.max(1.0 - done_ref[...])))

    cut_ref[...] = jnp.full(full, CUT_ALL, F32)

    @pl.when(jnp.max(tie_ref[...]) > 0.0)
    def _():
        th = th_ref[...]
        need = kf - chi_ref[...]
        lane_f = lane.astype(F32)

        def count_eq_le(cut):
            def body(c, acc):
                sc = score_ref[c]
                base = (c * tk).astype(F32)
                for g in range(reps):
                    colf = lane_f + (base + float(g * LANES))
                    ok = jnp.logical_and(sc[:, g * LANES:(g + 1) * LANES] == th, colf <= cut)
                    acc = acc + jnp.where(ok, 1.0, 0.0)
                return acc
            acc = lax.fori_loop(0, n_chunks, body, jnp.zeros(full, F32))
            return _dot(acc.astype(BF16), ones_b)

        def body(_, carry):
            lo_i, hi_i = carry
            mid = jnp.floor(0.5 * (lo_i + hi_i))
            ge = count_eq_le(mid) >= need
            return jnp.where(ge, lo_i, mid), jnp.where(ge, mid, hi_i)

        n_iter = int(math.ceil(math.log2(S))) + 1
        _, hi_i = lax.fori_loop(0, n_iter, body,
                                (jnp.full(full, -1.0, F32), jnp.full(full, float(S - 1), F32)))
        cut_ref[...] = jnp.where(tie_ref[...] > 0.0, hi_i, CUT_ALL)

    m_ref[...] = jnp.full(m_ref.shape, MASK_NEG, F32)
    l_ref[...] = jnp.zeros(l_ref.shape, F32)
    acc_ref[...] = jnp.zeros(acc_ref.shape, F32)
    th_t = tile_lanes(th_ref[...])
    cut_t = tile_lanes(cut_ref[...])
    colf_in_chunk = col_in_chunk.astype(F32)

    def c_body(c, _):
        c0 = pl.multiple_of(c * tk, tk)
        sc = score_ref[c]
        colf = colf_in_chunk + (c * tk).astype(F32)
        sel = jnp.logical_or(sc > th_t, jnp.logical_and(sc == th_t, colf <= cut_t))
        bias = jnp.where(sel, 0.0, MASK_NEG)
        s = _dot_t(qa_ref[...], kr_ref[pl.ds(c0, tk), :])
        s = (s.reshape(N_ATTN_HEADS, tq, tk) + bias[None]).reshape(N_ATTN_HEADS * tq, tk)
        m_old = m_ref[...]
        m_new = jnp.maximum(m_old, jnp.max(s, axis=1, keepdims=True))
        alpha = jnp.exp(m_old - m_new)
        p = jnp.exp(s - m_new)
        l_ref[...] = alpha * l_ref[...] + jnp.sum(p, axis=1, keepdims=True)
        acc_ref[...] = alpha * acc_ref[...] + _dot(p.astype(BF16), av_ref[0, pl.ds(c0, tk), :])
        m_ref[...] = m_new
        return 0

    lax.fori_loop(0, n_chunks, c_body, 0)
    out = acc_ref[...] / l_ref[...]
    for h in range(N_ATTN_HEADS):
        o_ref[0, :, h * LANES:(h + 1) * LANES] = out[h * tq:(h + 1) * tq, :].astype(o_ref.dtype)


def _attention(z3, tab_a, tab_iq, tab_ik, topk):
    B, S, _ = z3.shape
    tq = min(128, S)
    tk = min(256, S)
    full = (tq, LANES)
    qblk = lambda w, col: pl.BlockSpec((1, tq, w), lambda b, q: (b, q, col // w))
    kblk = lambda w, col: pl.BlockSpec((1, S, w), lambda b, q: (b, 0, col // w))
    tab_q = pl.BlockSpec((1, tq, 3 * LANES), lambda b, q: (b, q, 0))
    tab_k = pl.BlockSpec((1, S, 3 * LANES), lambda b, q: (b, 0, 0))
    rows = N_ATTN_HEADS * tq
    return pl.pallas_call(
        functools.partial(_attn_kernel, topk=topk, tq=tq, tk=tk),
        grid=(B, S // tq),
        in_specs=[qblk(ATTN_WIDTH, COL_AQ), qblk(IDX_WIDTH, COL_IQ), qblk(LANES, COL_IKW),
                  kblk(LANES, COL_AK), kblk(LANES, COL_AV), kblk(LANES, COL_IKW),
                  tab_q, tab_k, tab_q, tab_k],
        out_specs=pl.BlockSpec((1, tq, ATTN_WIDTH), lambda b, q: (b, q, 0)),
        out_shape=jax.ShapeDtypeStruct((B, S, ATTN_WIDTH), BF16),
        scratch_shapes=[pltpu.VMEM((S, LANES), BF16),
                        pltpu.VMEM((S, LANES), BF16),
                        pltpu.VMEM((S, LANES), BF16),
                        pltpu.VMEM((rows, LANES), BF16),
                        pltpu.VMEM((tq, IDX_WIDTH), BF16),
                        pltpu.VMEM((N_IDX_HEADS, tq, LANES), F32),
                        pltpu.VMEM((S // tk, tq, tk), F32),
                        ] + [pltpu.VMEM(full, F32)] * 7 + [
                        pltpu.VMEM((rows, 1), F32), pltpu.VMEM((rows, 1), F32),
                        pltpu.VMEM((rows, LANES), F32)],
        compiler_params=_cparams(2),
        name="dsa_attention",
    )(z3, z3, z3, z3, z3, z3, tab_a, tab_a, tab_iq, tab_ik)


def _ret_kernel(rq_ref, rk_ref, rv_ref, rg_ref, tab_ref, decay_ref, zeta_ref, xi_ref, gch_ref, gn_ref,
                o_ref, state_ref, *, chunk):
    S = rq_ref.shape[1]
    C = chunk
    half = RET_KEY_DIM
    state_ref[...] = jnp.zeros(state_ref.shape, F32)
    lane = lax.broadcasted_iota(jnp.int32, (C, LANES), 1)
    first = lane < half

    def body(c, _):
        r0 = pl.multiple_of(c * C, C)
        sl = pl.ds(r0, C)
        tab = tab_ref[0, sl, :]
        for j in range(N_RET_HEADS // 2):
            q = _rope(rq_ref[0, sl, j * LANES:(j + 1) * LANES].astype(F32), tab, RET_KEY_DIM // 2)
            k = _rope(rk_ref[0, sl, j * LANES:(j + 1) * LANES].astype(F32), tab, RET_KEY_DIM // 2)
            k = k * (RET_KEY_DIM ** -0.5)
            kzt = (k * zeta_ref[j]).T.astype(BF16)
            state = state_ref[j]
            state_b = state.astype(BF16)
            new_rows = []
            for par in range(2):
                h = 2 * j + par
                keep = first if par == 0 else jnp.logical_not(first)
                qh = jnp.where(keep, q, 0.0).astype(BF16)
                kh = jnp.where(keep, k, 0.0).astype(BF16)
                v = rv_ref[0, sl, h * LANES:(h + 1) * LANES]
                inner = _dot_t(qh, kh) * decay_ref[h]
                o = _dot(inner.astype(BF16), v) + _dot(qh, state_b) * xi_ref[h]
                new_rows.append(_dot(kzt[par * half:(par + 1) * half, :], v))
                mu = jnp.mean(o, axis=-1, keepdims=True)
                d = o - mu
                var = jnp.mean(d * d, axis=-1, keepdims=True)
                y = d * lax.rsqrt(var + GN_EPS) * gn_ref[:, h * LANES:(h + 1) * LANES]
                g = rg_ref[0, sl, h * LANES:(h + 1) * LANES].astype(F32)
                o_ref[0, sl, h * LANES:(h + 1) * LANES] = (y * (g * jax.nn.sigmoid(g))).astype(o_ref.dtype)
            state_ref[j] = gch_ref[j] * state + jnp.concatenate(new_rows, axis=0)
        return 0

    lax.fori_loop(0, S // C, body, 0)


def _retention(z3, tab_r, consts, gn_w, chunk):
    B, S, _ = z3.shape
    decay, zeta, xi, gch = consts
    blk = lambda w, col: pl.BlockSpec((1, S, w), lambda b: (b, 0, col // w))
    const = lambda a: pl.BlockSpec(a.shape, lambda b: (0,) * a.ndim)
    return pl.pallas_call(
        functools.partial(_ret_kernel, chunk=chunk),
        grid=(B,),
        in_specs=[blk(RET_KEY_WIDTH, COL_RQ), blk(RET_KEY_WIDTH, COL_RK), blk(RET_WIDTH, COL_RV),
                  blk(RET_WIDTH, COL_RG), pl.BlockSpec((1, S, 3 * LANES), lambda b: (b, 0, 0)),
                  const(decay), const(zeta), const(xi), const(gch), const(gn_w)],
        out_specs=pl.BlockSpec((1, S, RET_WIDTH), lambda b: (b, 0, 0)),
        out_shape=jax.ShapeDtypeStruct((B, S, RET_WIDTH), BF16),
        scratch_shapes=[pltpu.VMEM((N_RET_HEADS // 2, LANES, RET_VAL_DIM), F32)],
        compiler_params=_cparams(1),
        name="retention",
    )(z3, z3, z3, z3, tab_r, decay, zeta, xi, gch, gn_w)


def _retention_consts(chunk):
    C = chunk
    log_g = jnp.log(1.0 - 2.0 ** (-5.0 - jnp.arange(N_RET_HEADS, dtype=F32)))
    i = jnp.arange(C, dtype=F32)
    diff = i[:, None] - i[None, :]
    decay = jnp.where(diff[None] >= 0, jnp.exp(jnp.maximum(diff, 0.0)[None] * log_g[:, None, None]), 0.0)
    zeta = jnp.exp((C - 1.0 - i)[None, :] * log_g[:, None])
    xi = jnp.exp((i + 1.0)[None, :] * log_g[:, None])
    g_chunk = jnp.exp(C * log_g)
    pair = lambda a: a.reshape(N_RET_HEADS // 2, 2, -1)
    zeta_p = jnp.repeat(pair(zeta).transpose(0, 2, 1), RET_KEY_DIM, axis=2)
    xi_b = jnp.broadcast_to(xi[:, :, None], (N_RET_HEADS, C, RET_VAL_DIM))
    gch = jnp.broadcast_to(jnp.repeat(pair(g_chunk), RET_KEY_DIM, axis=1).reshape(N_RET_HEADS // 2, LANES, 1),
                           (N_RET_HEADS // 2, LANES, RET_VAL_DIM))
    return decay.astype(F32), zeta_p.astype(F32), xi_b.astype(F32), gch.astype(F32)


def _out_proj_kernel(attn_ref, ret_ref, h_ref, wa_ref, wr_ref, g1_ref, g2_ref, h1_ref, m_ref):
    mix = _dot(attn_ref[...], wa_ref[...]) + _dot(ret_ref[...], wr_ref[...])
    h1 = h_ref[...] + _rms(mix, g1_ref[...])
    h1_ref[...] = h1
    m_ref[...] = _rms(h1, g2_ref[...]).astype(m_ref.dtype)


def _out_proj(attn, ret, h, wa, wr, g1, g2):
    T, D = h.shape
    tm = min(256, T)
    row = lambda w: pl.BlockSpec((tm, w), lambda i: (i, 0))
    const = lambda a: pl.BlockSpec(a.shape, lambda i: (0, 0))
    return pl.pallas_call(
        _out_proj_kernel,
        grid=(T // tm,),
        in_specs=[row(attn.shape[1]), row(ret.shape[1]), row(D), const(wa), const(wr), const(g1), const(g2)],
        out_specs=[row(D), row(D)],
        out_shape=[jax.ShapeDtypeStruct((T, D), F32), jax.ShapeDtypeStruct((T, D), BF16)],
        compiler_params=_cparams(1),
        name="out_proj",
    )(attn, ret, h, wa, wr, g1, g2)


def _ffn_kernel(m_ref, w1_ref, w2_ref, h_ref, g_ref, o_ref, acc_ref):
    f = pl.program_id(1)
    u = jnp.maximum(_dot(m_ref[...], w1_ref[...]), 0.0)
    part = _dot((u * u).astype(BF16), w2_ref[...])

    @pl.when(f == 0)
    def _():
        acc_ref[...] = part

    @pl.when(f > 0)
    def _():
        acc_ref[...] += part

    @pl.when(f == pl.num_programs(1) - 1)
    def _():
        o_ref[...] = h_ref[...] + _rms(acc_ref[...], g_ref[...])


def _ffn(m, w1, w2, h, g):
    T, D = h.shape
    F = w1.shape[1]
    tm = min(512, T)
    tf = min(512, F)
    return pl.pallas_call(
        _ffn_kernel,
        grid=(T // tm, F // tf),
        in_specs=[pl.BlockSpec((tm, D), lambda i, f: (i, 0)),
                  pl.BlockSpec((D, tf), lambda i, f: (0, f)),
                  pl.BlockSpec((tf, D), lambda i, f: (f, 0)),
                  pl.BlockSpec((tm, D), lambda i, f: (i, 0)),
                  pl.BlockSpec((1, D), lambda i, f: (0, 0))],
        out_specs=pl.BlockSpec((tm, D), lambda i, f: (i, 0)),
        out_shape=jax.ShapeDtypeStruct((T, D), F32),
        scratch_shapes=[pltpu.VMEM((tm, D), F32)],
        compiler_params=_cparams(2),
        name="ffn",
    )(m, w1, w2, h, g)


def _ple_kernel(h_ref, p_ref, wg_ref, wp_ref, g_ref, o_ref):
    h = h_ref[...]
    gate = jax.nn.sigmoid(_dot(h.astype(BF16), wg_ref[...]))
    e = _dot(p_ref[...].astype(BF16), wp_ref[...])
    o_ref[...] = h + _rms(gate * e, g_ref[...])


def _ple(h, p, wg, wp, g):
    T, D = h.shape
    tm = min(256, T)
    row = lambda w: pl.BlockSpec((tm, w), lambda i: (i, 0))
    const = lambda a: pl.BlockSpec(a.shape, lambda i: (0, 0))
    return pl.pallas_call(
        _ple_kernel,
        grid=(T // tm,),
        in_specs=[row(D), row(p.shape[1]), const(wg), const(wp), const(g)],
        out_specs=row(D),
        out_shape=jax.ShapeDtypeStruct((T, D), F32),
        compiler_params=_cparams(1),
        name="ple",
    )(h, p, wg, wp, g)


def _rope_table(positions, half, theta, group, keep_groups=None):
    inv = theta ** (-jnp.arange(half, dtype=F32) / half)
    ang = positions.astype(F32)[..., None] * inv
    cos, sin = jnp.cos(ang), jnp.sin(ang)
    B, S = positions.shape
    rest = group - 2 * half
    ones = jnp.ones((B, S, rest), F32)
    zeros = lambda n: jnp.zeros((B, S, n), F32)
    c = jnp.concatenate([cos, cos, ones], axis=-1)
    s1 = jnp.concatenate([zeros(half), sin, zeros(rest)], axis=-1)
    s2 = jnp.concatenate([-sin, zeros(half + rest)], axis=-1)
    n = LANES // group
    if keep_groups is None:
        keep_groups = n
    def tile(a, fill):
        parts = [a] * keep_groups + [jnp.full((B, S, group), fill, F32)] * (n - keep_groups)
        return jnp.concatenate(parts, axis=-1)
    return jnp.concatenate([tile(c, 1.0), tile(s1, 0.0), tile(s2, 0.0)], axis=-1)


def _reorder_w_in(w):
    D = w.shape[0]
    sizes = (ATTN_WIDTH, ATTN_HEAD_DIM, ATTN_HEAD_DIM, IDX_WIDTH, IDX_HEAD_DIM, N_IDX_HEADS,
             RET_KEY_WIDTH, RET_KEY_WIDTH, RET_WIDTH, RET_WIDTH)
    offs = [0]
    for s in sizes:
        offs.append(offs[-1] + s)
    aq, ak, av, iq, ik, iw, rq, rk, rv, rg = [w[:, offs[i]:offs[i + 1]] for i in range(len(sizes))]
    pad = jnp.zeros((D, Z_WIDTH - Z_USED + LANES - IDX_HEAD_DIM - N_IDX_HEADS), w.dtype)
    return jnp.concatenate([aq, iq, rv, rg, rq, rk, ak, av, ik, iw, pad], axis=1).astype(BF16)


def kernel(x, p, positions, w_in, w_out, w_ff1, w_ff2, w_ple, w_ple_gate, pre_mix_norm, post_mix_norm,
           pre_ff_norm, post_ff_norm, ple_norm, ret_gn):
    B, S, D = x.shape
    depth = w_in.shape[0]
    T = B * S
    topk = min(TOPK_MAX, S // 4)
    ret_chunk = min(128, S)
    tab_a = _rope_table(positions, ATTN_ROT_HALF, ROPE_THETA, ATTN_HEAD_DIM)
    tab_iq = _rope_table(positions, IDX_ROT_HALF, ROPE_THETA, IDX_HEAD_DIM)
    tab_ik = _rope_table(positions, IDX_ROT_HALF, ROPE_THETA, IDX_HEAD_DIM, keep_groups=1)
    tab_r = _rope_table(positions, RET_KEY_DIM // 2, RET_THETA, RET_KEY_DIM)
    ret_consts = _retention_consts(ret_chunk)
    vec = lambda a: a.reshape(1, -1).astype(F32)

    h = x.reshape(T, D)
    for i in range(depth):
        z = _norm_proj(h, vec(pre_mix_norm[i]), _reorder_w_in(w_in[i]))
        z3 = z.reshape(B, S, Z_WIDTH)
        attn = _attention(z3, tab_a, tab_iq, tab_ik, topk)
        ret = _retention(z3, tab_r, ret_consts, vec(ret_gn[i]), ret_chunk)
        wo = w_out[i].astype(BF16)
        h, m = _out_proj(attn.reshape(T, ATTN_WIDTH), ret.reshape(T, RET_WIDTH), h,
                         wo[:ATTN_WIDTH], wo[ATTN_WIDTH:], vec(post_mix_norm[i]), vec(pre_ff_norm[i]))
        h = _ffn(m, w_ff1[i].astype(BF16), w_ff2[i].astype(BF16), h, vec(post_ff_norm[i]))
        h = _ple(h, p[i].reshape(T, -1), w_ple_gate[i].astype(BF16), w_ple[i].astype(BF16), vec(ple_norm[i]))
    return h.reshape(B, S, D)
```

```python
import functools
import math

import jax
import jax.numpy as jnp
from jax import lax
from jax.experimental import pallas as pl
from jax.experimental.pallas import tpu as pltpu

N_ATTN_HEADS = 8
ATTN_HEAD_DIM = 128
ATTN_ROT_HALF = 16
ROPE_THETA = 500000.0
TOPK_MAX = 256
N_IDX_HEADS = 16
IDX_HEAD_DIM = 64
IDX_ROT_HALF = 8
N_RET_HEADS = 8
RET_KEY_DIM = 64
RET_VAL_DIM = 128
RET_THETA = 10000.0
ATTN_WIDTH = N_ATTN_HEADS * ATTN_HEAD_DIM
IDX_WIDTH = N_IDX_HEADS * IDX_HEAD_DIM
RET_KEY_WIDTH = N_RET_HEADS * RET_KEY_DIM
RET_WIDTH = N_RET_HEADS * RET_VAL_DIM
RMS_EPS = 1e-6
GN_EPS = 1e-5

LANES = 128
COL_AQ = 0
COL_IQ = COL_AQ + ATTN_WIDTH
COL_RV = COL_IQ + IDX_WIDTH
COL_RG = COL_RV + RET_WIDTH
COL_RQ = COL_RG + RET_WIDTH
COL_RK = COL_RQ + RET_KEY_WIDTH
COL_AK = COL_RK + RET_KEY_WIDTH
COL_AV = COL_AK + ATTN_HEAD_DIM
COL_IKW = COL_AV + ATTN_HEAD_DIM
Z_USED = COL_IKW + LANES
Z_TILE_N = 512
Z_WIDTH = -(-Z_USED // Z_TILE_N) * Z_TILE_N

MASK_NEG = -1e30
THETA_ALL = -3e38
CUT_ALL = 1e9
VMEM_LIMIT = 56 * 1024 * 1024

BF16 = jnp.bfloat16
F32 = jnp.float32


def _cparams(n_grid):
    return pltpu.CompilerParams(dimension_semantics=("arbitrary",) * n_grid,
                                vmem_limit_bytes=VMEM_LIMIT)


def _rms(x, gain):
    ms = jnp.mean(x * x, axis=-1, keepdims=True)
    return x * lax.rsqrt(ms + RMS_EPS) * gain


def _rope(x, tab, shift):
    c, s1, s2 = tab[:, :LANES], tab[:, LANES:2 * LANES], tab[:, 2 * LANES:]
    return x * c + pltpu.roll(x, shift, 1) * s1 + pltpu.roll(x, LANES - shift, 1) * s2


def _dot_t(a, b):
    return lax.dot_general(a, b, (((1,), (1,)), ((), ())), preferred_element_type=F32)


def _dot(a, b):
    return jnp.dot(a, b, preferred_element_type=F32)


def _norm_proj_kernel(h_ref, g_ref, w_ref, z_ref, a_ref, *, row_chunk):
    @pl.when(pl.program_id(1) == 0)
    def _():
        def body(r, _):
            r0 = pl.multiple_of(r * row_chunk, row_chunk)
            a_ref[pl.ds(r0, row_chunk), :] = _rms(h_ref[pl.ds(r0, row_chunk), :], g_ref[...]).astype(BF16)
            return 0
        lax.fori_loop(0, h_ref.shape[0] // row_chunk, body, 0)

    z_ref[...] = _dot(a_ref[...], w_ref[...]).astype(z_ref.dtype)


def _norm_proj(h, gain, w):
    T, D = h.shape
    N = w.shape[1]
    tm = min(1024, T)
    tn = Z_TILE_N
    return pl.pallas_call(
        functools.partial(_norm_proj_kernel, row_chunk=min(128, tm)),
        grid=(T // tm, N // tn),
        in_specs=[pl.BlockSpec((tm, D), lambda i, j: (i, 0)),
                  pl.BlockSpec((1, D), lambda i, j: (0, 0)),
                  pl.BlockSpec((D, tn), lambda i, j: (0, j))],
        out_specs=pl.BlockSpec((tm, tn), lambda i, j: (i, j)),
        out_shape=jax.ShapeDtypeStruct((T, N), BF16),
        scratch_shapes=[pltpu.VMEM((tm, D), BF16)],
        compiler_params=_cparams(2),
        name="norm_proj",
    )(h, gain, w)


def _attn_kernel(aq_ref, iq_ref, iwq_ref, ak_ref, av_ref, ikw_ref,
                 taq_ref, tak_ref, tiq_ref, tik_ref, o_ref,
                 kr_ref, kd0_ref, kd1_ref, vaug_ref, qa_ref, qi_ref, wb_ref, score_ref, score_t_ref,
                 s_ref, m_ref, acc_ref, *, topk, tq, tk):
    S = ak_ref.shape[1]
    qi = pl.program_id(1)
    n_chunks = (qi * tq + tq + tk - 1) // tk
    reps = tk // LANES
    rows_all = N_ATTN_HEADS * tq

    @pl.when(qi == 0)
    def _():
        rows = min(256, S)

        def body(r, _):
            r0 = pl.multiple_of(r * rows, rows)
            sl = pl.ds(r0, rows)
            kr_ref[sl, :] = _rope(ak_ref[0, sl, :].astype(F32), tak_ref[0, sl, :], ATTN_ROT_HALF).astype(BF16)
            ik = _rope(ikw_ref[0, sl, :].astype(F32), tik_ref[0, sl, :], IDX_ROT_HALF)
            lane_r = lax.broadcasted_iota(jnp.int32, (rows, LANES), 1)
            ik = jnp.where(lane_r < IDX_HEAD_DIM, ik, 0.0)
            kd0_ref[sl, :] = ik.astype(BF16)
            kd1_ref[sl, :] = pltpu.roll(ik, IDX_HEAD_DIM, 1).astype(BF16)
            vaug_ref[sl, :] = jnp.concatenate([av_ref[0, sl, :], jnp.ones((rows, LANES), BF16)], axis=1)
            return 0
        lax.fori_loop(0, S // rows, body, 0)

    ta = taq_ref[0]
    for h in range(N_ATTN_HEADS):
        x = aq_ref[0, :, h * LANES:(h + 1) * LANES].astype(F32)
        qa_ref[h * tq:(h + 1) * tq, :] = (_rope(x, ta, ATTN_ROT_HALF) * (ATTN_HEAD_DIM ** -0.5)).astype(BF16)
    ti = tiq_ref[0]
    for j in range(IDX_WIDTH // LANES):
        x = iq_ref[0, :, j * LANES:(j + 1) * LANES].astype(F32)
        qi_ref[:, j * LANES:(j + 1) * LANES] = _rope(x, ti, IDX_ROT_HALF).astype(BF16)
    iw = iwq_ref[0].astype(F32) * ((N_IDX_HEADS ** -0.5) * (IDX_HEAD_DIM ** -0.5))
    for h in range(N_IDX_HEADS):
        wb_ref[h] = jnp.broadcast_to(iw[:, IDX_HEAD_DIM + h:IDX_HEAD_DIM + h + 1], (tq, LANES))

    row = qi * tq + lax.broadcasted_iota(jnp.int32, (tq, tk), 0)
    col_in_chunk = lax.broadcasted_iota(jnp.int32, (tq, tk), 1)

    def lane_groups(x, op):
        r = x[:, :LANES]
        for g in range(1, reps):
            r = op(r, x[:, g * LANES:(g + 1) * LANES])
        return r

    def tile_lanes(x):
        return jnp.concatenate([x] * reps, axis=1) if reps > 1 else x

    key_t = lax.broadcasted_iota(jnp.int32, (tk, tq), 0)
    qry_t = qi * tq + lax.broadcasted_iota(jnp.int32, (tk, tq), 1)

    def fold8(x, op):
        x3 = x.reshape(tk // 8, 8, tq)
        return jnp.max(x3, axis=0) if op == "max" else (jnp.min(x3, axis=0) if op == "min" else jnp.sum(x3, axis=0))

    def a_body(c, carry):
        rmax, rmin = carry
        c0 = pl.multiple_of(c * tk, tk)
        k0 = kd0_ref[pl.ds(c0, tk), :]
        k1 = kd1_ref[pl.ds(c0, tk), :]
        acc = jnp.zeros((tq, tk), F32)
        for j in range(IDX_WIDTH // LANES):
            qp = qi_ref[:, j * LANES:(j + 1) * LANES]
            for par, kd in ((0, k0), (1, k1)):
                logits = _dot_t(qp, kd)
                acc = acc + tile_lanes(wb_ref[2 * j + par]) * jnp.maximum(logits, 0.0)
        causal = (c0 + col_in_chunk) <= row
        score_ref[c] = jnp.where(causal, acc, -jnp.inf)
        acc_t = acc.T
        causal_t = (c0 + key_t) <= qry_t
        sc_t = jnp.where(causal_t, acc_t, -jnp.inf)
        score_t_ref[c] = sc_t
        rmax = jnp.maximum(rmax, fold8(sc_t, "max"))
        rmin = jnp.minimum(rmin, fold8(jnp.where(causal_t, acc_t, jnp.inf), "min"))
        return rmax, rmin

    rmax, rmin = lax.fori_loop(0, n_chunks, a_body,
                               (jnp.full((8, tq), -jnp.inf, F32), jnp.full((8, tq), jnp.inf, F32)))

    vec = (1, tq)
    kf = float(topk)
    n_keys = qi * tq + lax.broadcasted_iota(jnp.int32, vec, 1) + 1

    def count_t(pred):
        def body(c, acc):
            return acc + fold8(jnp.where(pred(c, score_t_ref[c]), 1.0, 0.0), "sum")
        return jnp.sum(lax.fori_loop(0, n_chunks, body, jnp.zeros((8, tq), F32)), axis=0, keepdims=True)

    def b_cond(carry):
        return carry[1] > 0.0

    def b_body(carry):
        it, _, lo, hi, chi, done, th, tie = carry
        flag = jnp.max(1.0 - done)
        mid = jnp.where(it == 0, hi, 0.5 * lo + 0.5 * hi)
        cnt = count_t(lambda c, sc: sc >= mid)
        active = done == 0.0
        adjacent = jnp.logical_and(it > 0, jnp.logical_or(mid <= lo, mid >= hi))
        tie_now = jnp.logical_and(active, adjacent)
        live = jnp.logical_and(active, jnp.logical_not(adjacent))
        hit = jnp.logical_and(live, cnt == kf)
        above = jnp.logical_and(live, cnt > kf)
        below = jnp.logical_and(live, cnt < kf)
        th = jnp.where(tie_now, lo, jnp.where(hit, mid, th))
        tie = jnp.where(tie_now, 1.0, tie)
        lo = jnp.where(above, mid, lo)
        hi = jnp.where(below, mid, hi)
        chi = jnp.where(below, cnt, chi)
        done = jnp.where(jnp.logical_or(tie_now, hit), 1.0, done)
        return it + 1, flag, lo, hi, chi, done, th, tie

    init = (jnp.int32(0), jnp.float32(1.0),
            jnp.min(rmin, axis=0, keepdims=True), jnp.max(rmax, axis=0, keepdims=True),
            jnp.zeros(vec, F32), jnp.where(n_keys <= topk, 1.0, 0.0),
            jnp.full(vec, THETA_ALL, F32), jnp.zeros(vec, F32))
    _, _, _, _, chi, _, th, tie = lax.while_loop(b_cond, b_body, init)

    def tie_cut():
        need = kf - chi
        key_f = key_t.astype(F32)

        def body(_, carry):
            lo_i, hi_i = carry
            mid = jnp.floor(0.5 * (lo_i + hi_i))
            cnt = count_t(lambda c, sc: jnp.logical_and(sc == th, key_f + (c * tk).astype(F32) <= mid))
            ge = cnt >= need
            return jnp.where(ge, lo_i, mid), jnp.where(ge, mid, hi_i)

        n_iter = int(math.ceil(math.log2(S))) + 1
        _, hi_i = lax.fori_loop(0, n_iter, body, (jnp.full(vec, -1.0, F32), jnp.full(vec, float(S - 1), F32)))
        return jnp.where(tie > 0.0, hi_i, CUT_ALL)

    cut = lax.cond(jnp.max(tie) > 0.0, tie_cut, lambda: jnp.full(vec, CUT_ALL, F32))

    th_c = jnp.broadcast_to(th, (LANES, tq)).T
    cut_c = jnp.broadcast_to(cut, (LANES, tq)).T
    th_t = tile_lanes(th_c)
    cut_t = tile_lanes(cut_c)

    m_ref[...] = jnp.full(m_ref.shape, MASK_NEG, F32)
    colf_in_chunk = col_in_chunk.astype(F32)

    def c1_body(c, _):
        c0 = pl.multiple_of(c * tk, tk)
        sc = score_ref[c]
        colf = colf_in_chunk + (c * tk).astype(F32)
        sel = jnp.logical_or(sc > th_t, jnp.logical_and(sc == th_t, colf <= cut_t))
        bias = jnp.where(sel, 0.0, MASK_NEG)
        s = _dot_t(qa_ref[...], kr_ref[pl.ds(c0, tk), :])
        s = (s.reshape(N_ATTN_HEADS, tq, tk) + bias[None]).reshape(rows_all, tk)
        s_ref[c] = s
        m_ref[...] = jnp.maximum(m_ref[...], lane_groups(s, jnp.maximum))
        return 0

    lax.fori_loop(0, n_chunks, c1_body, 0)
    m_ref[...] = jnp.broadcast_to(jnp.max(m_ref[...], axis=1, keepdims=True), m_ref.shape)

    acc_ref[...] = jnp.zeros(acc_ref.shape, F32)

    def c2_body(c, _):
        c0 = pl.multiple_of(c * tk, tk)
        p = jnp.exp(s_ref[c] - tile_lanes(m_ref[...]))
        acc_ref[...] += _dot(p.astype(BF16), vaug_ref[pl.ds(c0, tk), :])
        return 0

    lax.fori_loop(0, n_chunks, c2_body, 0)
    out = acc_ref[:, :LANES] / acc_ref[:, LANES:]
    for h in range(N_ATTN_HEADS):
        o_ref[0, :, h * LANES:(h + 1) * LANES] = out[h * tq:(h + 1) * tq, :].astype(o_ref.dtype)


def _attention(z3, tab_a, tab_iq, tab_ik, topk):
    B, S, _ = z3.shape
    tq = min(128, S)
    tk = min(256, S)
    qblk = lambda w, col: pl.BlockSpec((1, tq, w), lambda b, q: (b, q, col // w))
    kblk = lambda w, col: pl.BlockSpec((1, S, w), lambda b, q: (b, 0, col // w))
    tab_q = pl.BlockSpec((1, tq, 3 * LANES), lambda b, q: (b, q, 0))
    tab_k = pl.BlockSpec((1, S, 3 * LANES), lambda b, q: (b, 0, 0))
    rows = N_ATTN_HEADS * tq
    return pl.pallas_call(
        functools.partial(_attn_kernel, topk=topk, tq=tq, tk=tk),
        grid=(B, S // tq),
        in_specs=[qblk(ATTN_WIDTH, COL_AQ), qblk(IDX_WIDTH, COL_IQ), qblk(LANES, COL_IKW),
                  kblk(LANES, COL_AK), kblk(LANES, COL_AV), kblk(LANES, COL_IKW),
                  tab_q, tab_k, tab_q, tab_k],
        out_specs=pl.BlockSpec((1, tq, ATTN_WIDTH), lambda b, q: (b, q, 0)),
        out_shape=jax.ShapeDtypeStruct((B, S, ATTN_WIDTH), BF16),
        scratch_shapes=[pltpu.VMEM((S, LANES), BF16),
                        pltpu.VMEM((S, LANES), BF16),
                        pltpu.VMEM((S, LANES), BF16),
                        pltpu.VMEM((S, 2 * LANES), BF16),
                        pltpu.VMEM((rows, LANES), BF16),
                        pltpu.VMEM((tq, IDX_WIDTH), BF16),
                        pltpu.VMEM((N_IDX_HEADS, tq, LANES), F32),
                        pltpu.VMEM((S // tk, tq, tk), F32),
                        pltpu.VMEM((S // tk, tk, tq), F32),
                        pltpu.VMEM((S // tk, rows, tk), F32),
                        pltpu.VMEM((rows, LANES), F32),
                        pltpu.VMEM((rows, 2 * LANES), F32)],
        compiler_params=_cparams(2),
        name="dsa_attention",
    )(z3, z3, z3, z3, z3, z3, tab_a, tab_a, tab_iq, tab_ik)


def _ret_kernel(rq_ref, rk_ref, rv_ref, rg_ref, tab_ref, decay_ref, zeta_ref, xi_ref, gch_ref, gn_ref,
                o_ref, state_ref, *, chunk):
    S = rq_ref.shape[1]
    C = chunk
    half = RET_KEY_DIM
    state_ref[...] = jnp.zeros(state_ref.shape, F32)
    lane = lax.broadcasted_iota(jnp.int32, (C, LANES), 1)
    first = lane < half

    def body(c, _):
        r0 = pl.multiple_of(c * C, C)
        sl = pl.ds(r0, C)
        tab = tab_ref[0, sl, :]
        for j in range(N_RET_HEADS // 2):
            q = _rope(rq_ref[0, sl, j * LANES:(j + 1) * LANES].astype(F32), tab, RET_KEY_DIM // 2)
            k = _rope(rk_ref[0, sl, j * LANES:(j + 1) * LANES].astype(F32), tab, RET_KEY_DIM // 2)
            k = k * (RET_KEY_DIM ** -0.5)
            kzt = (k * zeta_ref[j]).T.astype(BF16)
            state = state_ref[j]
            state_b = state.astype(BF16)
            new_rows = []
            for par in range(2):
                h = 2 * j + par
                keep = first if par == 0 else jnp.logical_not(first)
                qh = jnp.where(keep, q, 0.0).astype(BF16)
                kh = jnp.where(keep, k, 0.0).astype(BF16)
                v = rv_ref[0, sl, h * LANES:(h + 1) * LANES]
                inner = _dot_t(qh, kh) * decay_ref[h]
                o = _dot(inner.astype(BF16), v) + _dot(qh, state_b) * xi_ref[h]
                new_rows.append(_dot(kzt[par * half:(par + 1) * half, :], v))
                mu = jnp.mean(o, axis=-1, keepdims=True)
                d = o - mu
                var = jnp.mean(d * d, axis=-1, keepdims=True)
                y = d * lax.rsqrt(var + GN_EPS) * gn_ref[:, h * LANES:(h + 1) * LANES]
                g = rg_ref[0, sl, h * LANES:(h + 1) * LANES].astype(F32)
                o_ref[0, sl, h * LANES:(h + 1) * LANES] = (y * (g * jax.nn.sigmoid(g))).astype(o_ref.dtype)
            state_ref[j] = gch_ref[j] * state + jnp.concatenate(new_rows, axis=0)
        return 0

    lax.fori_loop(0, S // C, body, 0)


def _retention(z3, tab_r, consts, gn_w, chunk):
    B, S, _ = z3.shape
    decay, zeta, xi, gch = consts
    blk = lambda w, col: pl.BlockSpec((1, S, w), lambda b: (b, 0, col // w))
    const = lambda a: pl.BlockSpec(a.shape, lambda b: (0,) * a.ndim)
    return pl.pallas_call(
        functools.partial(_ret_kernel, chunk=chunk),
        grid=(B,),
        in_specs=[blk(RET_KEY_WIDTH, COL_RQ), blk(RET_KEY_WIDTH, COL_RK), blk(RET_WIDTH, COL_RV),
                  blk(RET_WIDTH, COL_RG), pl.BlockSpec((1, S, 3 * LANES), lambda b: (b, 0, 0)),
                  const(decay), const(zeta), const(xi), const(gch), const(gn_w)],
        out_specs=pl.BlockSpec((1, S, RET_WIDTH), lambda b: (b, 0, 0)),
        out_shape=jax.ShapeDtypeStruct((B, S, RET_WIDTH), BF16),
        scratch_shapes=[pltpu.VMEM((N_RET_HEADS // 2, LANES, RET_VAL_DIM), F32)],
        compiler_params=_cparams(1),
        name="retention",
    )(z3, z3, z3, z3, tab_r, decay, zeta, xi, gch, gn_w)


def _retention_consts(chunk):
    C = chunk
    log_g = jnp.log(1.0 - 2.0 ** (-5.0 - jnp.arange(N_RET_HEADS, dtype=F32)))
    i = jnp.arange(C, dtype=F32)
    diff = i[:, None] - i[None, :]
    decay = jnp.where(diff[None] >= 0, jnp.exp(jnp.maximum(diff, 0.0)[None] * log_g[:, None, None]), 0.0)
    zeta = jnp.exp((C - 1.0 - i)[None, :] * log_g[:, None])
    xi = jnp.exp((i + 1.0)[None, :] * log_g[:, None])
    g_chunk = jnp.exp(C * log_g)
    pair = lambda a: a.reshape(N_RET_HEADS // 2, 2, -1)
    zeta_p = jnp.repeat(pair(zeta).transpose(0, 2, 1), RET_KEY_DIM, axis=2)
    xi_b = jnp.broadcast_to(xi[:, :, None], (N_RET_HEADS, C, RET_VAL_DIM))
    gch = jnp.broadcast_to(jnp.repeat(pair(g_chunk), RET_KEY_DIM, axis=1).reshape(N_RET_HEADS // 2, LANES, 1),
                           (N_RET_HEADS // 2, LANES, RET_VAL_DIM))
    return decay.astype(F32), zeta_p.astype(F32), xi_b.astype(F32), gch.astype(F32)


def _out_proj_kernel(attn_ref, ret_ref, h_ref, wa_ref, wr_ref, g1_ref, g2_ref, h1_ref, m_ref):
    mix = _dot(attn_ref[...], wa_ref[...]) + _dot(ret_ref[...], wr_ref[...])
    h1 = h_ref[...] + _rms(mix, g1_ref[...])
    h1_ref[...] = h1
    m_ref[...] = _rms(h1, g2_ref[...]).astype(m_ref.dtype)


def _out_proj(attn, ret, h, wa, wr, g1, g2):
    T, D = h.shape
    tm = min(256, T)
    row = lambda w: pl.BlockSpec((tm, w), lambda i: (i, 0))
    const = lambda a: pl.BlockSpec(a.shape, lambda i: (0, 0))
    return pl.pallas_call(
        _out_proj_kernel,
        grid=(T // tm,),
        in_specs=[row(attn.shape[1]), row(ret.shape[1]), row(D), const(wa), const(wr), const(g1), const(g2)],
        out_specs=[row(D), row(D)],
        out_shape=[jax.ShapeDtypeStruct((T, D), F32), jax.ShapeDtypeStruct((T, D), BF16)],
        compiler_params=_cparams(1),
        name="out_proj",
    )(attn, ret, h, wa, wr, g1, g2)


def _ffn_kernel(m_ref, w1_ref, w2_ref, h_ref, g_ref, o_ref, acc_ref):
    f = pl.program_id(1)
    u = jnp.maximum(_dot(m_ref[...], w1_ref[...]), 0.0)
    part = _dot((u * u).astype(BF16), w2_ref[...])

    @pl.when(f == 0)
    def _():
        acc_ref[...] = part

    @pl.when(f > 0)
    def _():
        acc_ref[...] += part

    @pl.when(f == pl.num_programs(1) - 1)
    def _():
        o_ref[...] = h_ref[...] + _rms(acc_ref[...], g_ref[...])


def _ffn(m, w1, w2, h, g):
    T, D = h.shape
    F = w1.shape[1]
    tm = min(512, T)
    tf = min(512, F)
    return pl.pallas_call(
        _ffn_kernel,
        grid=(T // tm, F // tf),
        in_specs=[pl.BlockSpec((tm, D), lambda i, f: (i, 0)),
                  pl.BlockSpec((D, tf), lambda i, f: (0, f)),
                  pl.BlockSpec((tf, D), lambda i, f: (f, 0)),
                  pl.BlockSpec((tm, D), lambda i, f: (i, 0)),
                  pl.BlockSpec((1, D), lambda i, f: (0, 0))],
        out_specs=pl.BlockSpec((tm, D), lambda i, f: (i, 0)),
        out_shape=jax.ShapeDtypeStruct((T, D), F32),
        scratch_shapes=[pltpu.VMEM((tm, D), F32)],
        compiler_params=_cparams(2),
        name="ffn",
    )(m, w1, w2, h, g)


def _ple_kernel(h_ref, p_ref, wg_ref, wp_ref, g_ref, o_ref):
    h = h_ref[...]
    gate = jax.nn.sigmoid(_dot(h.astype(BF16), wg_ref[...]))
    e = _dot(p_ref[...].astype(BF16), wp_ref[...])
    o_ref[...] = h + _rms(gate * e, g_ref[...])


def _ple(h, p, wg, wp, g):
    T, D = h.shape
    tm = min(256, T)
    row = lambda w: pl.BlockSpec((tm, w), lambda i: (i, 0))
    const = lambda a: pl.BlockSpec(a.shape, lambda i: (0, 0))
    return pl.pallas_call(
        _ple_kernel,
        grid=(T // tm,),
        in_specs=[row(D), row(p.shape[1]), const(wg), const(wp), const(g)],
        out_specs=row(D),
        out_shape=jax.ShapeDtypeStruct((T, D), F32),
        compiler_params=_cparams(1),
        name="ple",
    )(h, p, wg, wp, g)


def _rope_table(positions, half, theta, group, keep_groups=None):
    inv = theta ** (-jnp.arange(half, dtype=F32) / half)
    ang = positions.astype(F32)[..., None] * inv
    cos, sin = jnp.cos(ang), jnp.sin(ang)
    B, S = positions.shape
    rest = group - 2 * half
    ones = jnp.ones((B, S, rest), F32)
    zeros = lambda n: jnp.zeros((B, S, n), F32)
    c = jnp.concatenate([cos, cos, ones], axis=-1)
    s1 = jnp.concatenate([zeros(half), sin, zeros(rest)], axis=-1)
    s2 = jnp.concatenate([-sin, zeros(half + rest)], axis=-1)
    n = LANES // group
    if keep_groups is None:
        keep_groups = n
    def tile(a, fill):
        parts = [a] * keep_groups + [jnp.full((B, S, group), fill, F32)] * (n - keep_groups)
        return jnp.concatenate(parts, axis=-1)
    return jnp.concatenate([tile(c, 1.0), tile(s1, 0.0), tile(s2, 0.0)], axis=-1)


def _reorder_w_in(w):
    D = w.shape[0]
    sizes = (ATTN_WIDTH, ATTN_HEAD_DIM, ATTN_HEAD_DIM, IDX_WIDTH, IDX_HEAD_DIM, N_IDX_HEADS,
             RET_KEY_WIDTH, RET_KEY_WIDTH, RET_WIDTH, RET_WIDTH)
    offs = [0]
    for s in sizes:
        offs.append(offs[-1] + s)
    aq, ak, av, iq, ik, iw, rq, rk, rv, rg = [w[:, offs[i]:offs[i + 1]] for i in range(len(sizes))]
    pad = jnp.zeros((D, Z_WIDTH - Z_USED + LANES - IDX_HEAD_DIM - N_IDX_HEADS), w.dtype)
    return jnp.concatenate([aq, iq, rv, rg, rq, rk, ak, av, ik, iw, pad], axis=1).astype(BF16)


def kernel(x, p, positions, w_in, w_out, w_ff1, w_ff2, w_ple, w_ple_gate, pre_mix_norm, post_mix_norm,
           pre_ff_norm, post_ff_norm, ple_norm, ret_gn):
    B, S, D = x.shape
    depth = w_in.shape[0]
    T = B * S
    topk = min(TOPK_MAX, S // 4)
    ret_chunk = min(128, S)
    tab_a = _rope_table(positions, ATTN_ROT_HALF, ROPE_THETA, ATTN_HEAD_DIM)
    tab_iq = _rope_table(positions, IDX_ROT_HALF, ROPE_THETA, IDX_HEAD_DIM)
    tab_ik = _rope_table(positions, IDX_ROT_HALF, ROPE_THETA, IDX_HEAD_DIM, keep_groups=1)
    tab_r = _rope_table(positions, RET_KEY_DIM // 2, RET_THETA, RET_KEY_DIM)
    ret_consts = _retention_consts(ret_chunk)
    vec = lambda a: a.reshape(1, -1).astype(F32)

    h = x.reshape(T, D)
    for i in range(depth):
        z = _norm_proj(h, vec(pre_mix_norm[i]), _reorder_w_in(w_in[i]))
        z3 = z.reshape(B, S, Z_WIDTH)
        attn = _attention(z3, tab_a, tab_iq, tab_ik, topk)
        ret = _retention(z3, tab_r, ret_consts, vec(ret_gn[i]), ret_chunk)
        wo = w_out[i].astype(BF16)
        h, m = _out_proj(attn.reshape(T, ATTN_WIDTH), ret.reshape(T, RET_WIDTH), h,
                         wo[:ATTN_WIDTH], wo[ATTN_WIDTH:], vec(post_mix_norm[i]), vec(pre_ff_norm[i]))
        h = _ffn(m, w_ff1[i].astype(BF16), w_ff2[i].astype(BF16), h, vec(post_ff_norm[i]))
        h = _ple(h, p[i].reshape(T, -1), w_ple_gate[i].astype(BF16), w_ple[i].astype(BF16), vec(ple_norm[i]))
    return h.reshape(B, S, D)
```

```python
import functools
import math

import jax
import jax.numpy as jnp
from jax import lax
from jax.experimental import pallas as pl
from jax.experimental.pallas import tpu as pltpu

N_ATTN_HEADS = 8
ATTN_HEAD_DIM = 128
ATTN_ROT_HALF = 16
ROPE_THETA = 500000.0
TOPK_MAX = 256
N_IDX_HEADS = 16
IDX_HEAD_DIM = 64
IDX_ROT_HALF = 8
N_RET_HEADS = 8
RET_KEY_DIM = 64
RET_VAL_DIM = 128
RET_THETA = 10000.0
ATTN_WIDTH = N_ATTN_HEADS * ATTN_HEAD_DIM
IDX_WIDTH = N_IDX_HEADS * IDX_HEAD_DIM
RET_KEY_WIDTH = N_RET_HEADS * RET_KEY_DIM
RET_WIDTH = N_RET_HEADS * RET_VAL_DIM
RMS_EPS = 1e-6
GN_EPS = 1e-5

LANES = 128
COL_AQ = 0
COL_IQ = COL_AQ + ATTN_WIDTH
COL_RV = COL_IQ + IDX_WIDTH
COL_RG = COL_RV + RET_WIDTH
COL_RQ = COL_RG + RET_WIDTH
COL_RK = COL_RQ + RET_KEY_WIDTH
COL_AK = COL_RK + RET_KEY_WIDTH
COL_AV = COL_AK + ATTN_HEAD_DIM
COL_IKW = COL_AV + ATTN_HEAD_DIM
Z_USED = COL_IKW + LANES
Z_TILE_N = 512
Z_WIDTH = -(-Z_USED // Z_TILE_N) * Z_TILE_N

MASK_NEG = -1e30
THETA_ALL = -3e38
CUT_ALL = 1e9
VMEM_LIMIT = 56 * 1024 * 1024

BF16 = jnp.bfloat16
F32 = jnp.float32


def _cparams(n_grid):
    return pltpu.CompilerParams(dimension_semantics=("arbitrary",) * n_grid,
                                vmem_limit_bytes=VMEM_LIMIT)


def _rms(x, gain):
    ms = jnp.mean(x * x, axis=-1, keepdims=True)
    return x * lax.rsqrt(ms + RMS_EPS) * gain


def _rope(x, tab, shift):
    c, s1, s2 = tab[:, :LANES], tab[:, LANES:2 * LANES], tab[:, 2 * LANES:]
    return x * c + pltpu.roll(x, shift, 1) * s1 + pltpu.roll(x, LANES - shift, 1) * s2


def _dot_t(a, b):
    return lax.dot_general(a, b, (((1,), (1,)), ((), ())), preferred_element_type=F32)


def _dot(a, b):
    return jnp.dot(a, b, preferred_element_type=F32)


def _norm_proj_kernel(h_ref, g_ref, w_ref, z_ref, a_ref, *, row_chunk):
    @pl.when(pl.program_id(1) == 0)
    def _():
        def body(r, _):
            r0 = pl.multiple_of(r * row_chunk, row_chunk)
            a_ref[pl.ds(r0, row_chunk), :] = _rms(h_ref[pl.ds(r0, row_chunk), :], g_ref[...]).astype(BF16)
            return 0
        lax.fori_loop(0, h_ref.shape[0] // row_chunk, body, 0)

    z_ref[...] = _dot(a_ref[...], w_ref[...]).astype(z_ref.dtype)


def _norm_proj(h, gain, w):
    T, D = h.shape
    N = w.shape[1]
    tm = min(1024, T)
    tn = Z_TILE_N
    return pl.pallas_call(
        functools.partial(_norm_proj_kernel, row_chunk=min(128, tm)),
        grid=(T // tm, N // tn),
        in_specs=[pl.BlockSpec((tm, D), lambda i, j: (i, 0)),
                  pl.BlockSpec((1, D), lambda i, j: (0, 0)),
                  pl.BlockSpec((D, tn), lambda i, j: (0, j))],
        out_specs=pl.BlockSpec((tm, tn), lambda i, j: (i, j)),
        out_shape=jax.ShapeDtypeStruct((T, N), BF16),
        scratch_shapes=[pltpu.VMEM((tm, D), BF16)],
        compiler_params=_cparams(2),
        name="norm_proj",
    )(h, gain, w)


def _attn_kernel(aq_ref, iq_ref, iwq_ref, ak_ref, av_ref, ikw_ref,
                 taq_ref, tak_ref, tiq_ref, tik_ref, o_ref,
                 kr_ref, kd0_ref, kd1_ref, vaug_ref, qa_ref, qi_ref, wb_ref, score_ref, score_t_ref,
                 th_ref, cut_ref, s_ref, m_ref, acc_ref, *, topk, tq, tk):
    S = ak_ref.shape[1]
    qi = pl.program_id(1)
    n_chunks = (qi * tq + tq + tk - 1) // tk
    reps = tk // LANES
    rows_all = N_ATTN_HEADS * tq

    @pl.when(qi == 0)
    def _():
        rows = min(256, S)

        def body(r, _):
            r0 = pl.multiple_of(r * rows, rows)
            sl = pl.ds(r0, rows)
            kr_ref[sl, :] = _rope(ak_ref[0, sl, :].astype(F32), tak_ref[0, sl, :], ATTN_ROT_HALF).astype(BF16)
            ik = _rope(ikw_ref[0, sl, :].astype(F32), tik_ref[0, sl, :], IDX_ROT_HALF)
            lane_r = lax.broadcasted_iota(jnp.int32, (rows, LANES), 1)
            ik = jnp.where(lane_r < IDX_HEAD_DIM, ik, 0.0)
            kd0_ref[sl, :] = ik.astype(BF16)
            kd1_ref[sl, :] = pltpu.roll(ik, IDX_HEAD_DIM, 1).astype(BF16)
            vaug_ref[sl, :] = jnp.concatenate([av_ref[0, sl, :], jnp.ones((rows, LANES), BF16)], axis=1)
            score_t_ref[sl, :] = jnp.full((rows, tq), -jnp.inf, F32)
            return 0
        lax.fori_loop(0, S // rows, body, 0)

    ta = taq_ref[0]
    for h in range(N_ATTN_HEADS):
        x = aq_ref[0, :, h * LANES:(h + 1) * LANES].astype(F32)
        qa_ref[h * tq:(h + 1) * tq, :] = (_rope(x, ta, ATTN_ROT_HALF) * (ATTN_HEAD_DIM ** -0.5)).astype(BF16)
    ti = tiq_ref[0]
    for j in range(IDX_WIDTH // LANES):
        x = iq_ref[0, :, j * LANES:(j + 1) * LANES].astype(F32)
        qi_ref[:, j * LANES:(j + 1) * LANES] = _rope(x, ti, IDX_ROT_HALF).astype(BF16)
    iw = iwq_ref[0].astype(F32) * ((N_IDX_HEADS ** -0.5) * (IDX_HEAD_DIM ** -0.5))
    for h in range(N_IDX_HEADS):
        wb_ref[h] = jnp.broadcast_to(iw[:, IDX_HEAD_DIM + h:IDX_HEAD_DIM + h + 1], (tq, LANES))

    row = qi * tq + lax.broadcasted_iota(jnp.int32, (tq, tk), 0)
    col_in_chunk = lax.broadcasted_iota(jnp.int32, (tq, tk), 1)

    def lane_groups(x, op):
        r = x[:, :LANES]
        for g in range(1, reps):
            r = op(r, x[:, g * LANES:(g + 1) * LANES])
        return r

    def tile_lanes(x):
        return jnp.concatenate([x] * reps, axis=1) if reps > 1 else x

    key_t = lax.broadcasted_iota(jnp.int32, (tk, tq), 0)
    qry_t = qi * tq + lax.broadcasted_iota(jnp.int32, (tk, tq), 1)

    def fold8(x, op):
        x3 = x.reshape(tk // 8, 8, tq)
        return jnp.max(x3, axis=0) if op == "max" else (jnp.min(x3, axis=0) if op == "min" else jnp.sum(x3, axis=0))

    def a_body(c, carry):
        rmax, rmin = carry
        c0 = pl.multiple_of(c * tk, tk)
        k0 = kd0_ref[pl.ds(c0, tk), :]
        k1 = kd1_ref[pl.ds(c0, tk), :]
        acc = jnp.zeros((tq, tk), F32)
        for j in range(IDX_WIDTH // LANES):
            qp = qi_ref[:, j * LANES:(j + 1) * LANES]
            for par, kd in ((0, k0), (1, k1)):
                logits = _dot_t(qp, kd)
                acc = acc + tile_lanes(wb_ref[2 * j + par]) * jnp.maximum(logits, 0.0)
        causal = (c0 + col_in_chunk) <= row
        score_ref[c] = jnp.where(causal, acc, -jnp.inf)
        acc_t = acc.T
        causal_t = (c0 + key_t) <= qry_t
        sc_t = jnp.where(causal_t, acc_t, -jnp.inf)
        score_t_ref[pl.ds(c0, tk), :] = sc_t
        rmax = jnp.maximum(rmax, fold8(sc_t, "max"))
        rmin = jnp.minimum(rmin, fold8(jnp.where(causal_t, acc_t, jnp.inf), "min"))
        return rmax, rmin

    rmax, rmin = lax.fori_loop(0, n_chunks, a_body,
                               (jnp.full((8, tq), -jnp.inf, F32), jnp.full((8, tq), jnp.inf, F32)))

    vec = (1, tq)
    kf = float(topk)
    n_keys = qi * tq + lax.broadcasted_iota(jnp.int32, vec, 1) + 1

    lo0 = jnp.min(rmin, axis=0, keepdims=True)
    hi0 = jnp.max(rmax, axis=0, keepdims=True)
    key_f = key_t.astype(F32)

    def threshold_search(n_rows):
        def count_t(pred):
            acc = jnp.zeros((8, tq), F32)
            for r0 in range(0, n_rows, tk):
                acc = acc + fold8(jnp.where(pred(r0, score_t_ref[r0:r0 + tk, :]), 1.0, 0.0), "sum")
            return jnp.sum(acc, axis=0, keepdims=True)

        def b_cond(carry):
            return carry[1] > 0.0

        def b_body(carry):
            it, _, lo, hi, chi, done, th, tie = carry
            flag = jnp.max(1.0 - done)
            mid = jnp.where(it == 0, hi, 0.5 * lo + 0.5 * hi)
            cnt = count_t(lambda r0, sc: sc >= mid)
            active = done == 0.0
            adjacent = jnp.logical_and(it > 0, jnp.logical_or(mid <= lo, mid >= hi))
            tie_now = jnp.logical_and(active, adjacent)
            live = jnp.logical_and(active, jnp.logical_not(adjacent))
            hit = jnp.logical_and(live, cnt == kf)
            above = jnp.logical_and(live, cnt > kf)
            below = jnp.logical_and(live, cnt < kf)
            th = jnp.where(tie_now, lo, jnp.where(hit, mid, th))
            tie = jnp.where(tie_now, 1.0, tie)
            lo = jnp.where(above, mid, lo)
            hi = jnp.where(below, mid, hi)
            chi = jnp.where(below, cnt, chi)
            done = jnp.where(jnp.logical_or(tie_now, hit), 1.0, done)
            return it + 1, flag, lo, hi, chi, done, th, tie

        init = (jnp.int32(0), jnp.float32(1.0), lo0, hi0,
                jnp.zeros(vec, F32), jnp.where(n_keys <= topk, 1.0, 0.0),
                jnp.full(vec, THETA_ALL, F32), jnp.zeros(vec, F32))
        _, _, _, _, chi, _, th, tie = lax.while_loop(b_cond, b_body, init)

        def tie_cut():
            need = kf - chi

            def body(_, carry):
                lo_i, hi_i = carry
                mid = jnp.floor(0.5 * (lo_i + hi_i))
                cnt = count_t(lambda r0, sc: jnp.logical_and(sc == th, key_f + float(r0) <= mid))
                ge = cnt >= need
                return jnp.where(ge, lo_i, mid), jnp.where(ge, mid, hi_i)

            n_iter = int(math.ceil(math.log2(S))) + 1
            _, hi_i = lax.fori_loop(0, n_iter, body, (jnp.full(vec, -1.0, F32), jnp.full(vec, float(S - 1), F32)))
            return jnp.where(tie > 0.0, hi_i, CUT_ALL)

        th_ref[...] = th
        cut_ref[...] = lax.cond(jnp.max(tie) > 0.0, tie_cut, lambda: jnp.full(vec, CUT_ALL, F32))

    pair_rows = 2 * tk
    variant = (n_chunks * tk + pair_rows - 1) // pair_rows - 1
    for v in range(-(-S // pair_rows)):
        pl.when(variant == v)(functools.partial(threshold_search, min((v + 1) * pair_rows, S)))

    th_c = jnp.broadcast_to(th_ref[...], (LANES, tq)).T
    cut_c = jnp.broadcast_to(cut_ref[...], (LANES, tq)).T
    th_t = tile_lanes(th_c)
    cut_t = tile_lanes(cut_c)

    m_ref[...] = jnp.full(m_ref.shape, MASK_NEG, F32)
    colf_in_chunk = col_in_chunk.astype(F32)

    def c1_body(c, _):
        c0 = pl.multiple_of(c * tk, tk)
        sc = score_ref[c]
        colf = colf_in_chunk + (c * tk).astype(F32)
        sel = jnp.logical_or(sc > th_t, jnp.logical_and(sc == th_t, colf <= cut_t))
        bias = jnp.where(sel, 0.0, MASK_NEG)
        s = _dot_t(qa_ref[...], kr_ref[pl.ds(c0, tk), :])
        s = (s.reshape(N_ATTN_HEADS, tq, tk) + bias[None]).reshape(rows_all, tk)
        s_ref[c] = s
        m_ref[...] = jnp.maximum(m_ref[...], lane_groups(s, jnp.maximum))
        return 0

    lax.fori_loop(0, n_chunks, c1_body, 0)
    m_ref[...] = jnp.broadcast_to(jnp.max(m_ref[...], axis=1, keepdims=True), m_ref.shape)

    acc_ref[...] = jnp.zeros(acc_ref.shape, F32)

    def c2_body(c, _):
        c0 = pl.multiple_of(c * tk, tk)
        p = jnp.exp(s_ref[c] - tile_lanes(m_ref[...]))
        acc_ref[...] += _dot(p.astype(BF16), vaug_ref[pl.ds(c0, tk), :])
        return 0

    lax.fori_loop(0, n_chunks, c2_body, 0)
    out = acc_ref[:, :LANES] / acc_ref[:, LANES:]
    for h in range(N_ATTN_HEADS):
        o_ref[0, :, h * LANES:(h + 1) * LANES] = out[h * tq:(h + 1) * tq, :].astype(o_ref.dtype)


def _attention(z3, tab_a, tab_iq, tab_ik, topk):
    B, S, _ = z3.shape
    tq = min(128, S)
    tk = min(256, S)
    qblk = lambda w, col: pl.BlockSpec((1, tq, w), lambda b, q: (b, q, col // w))
    kblk = lambda w, col: pl.BlockSpec((1, S, w), lambda b, q: (b, 0, col // w))
    tab_q = pl.BlockSpec((1, tq, 3 * LANES), lambda b, q: (b, q, 0))
    tab_k = pl.BlockSpec((1, S, 3 * LANES), lambda b, q: (b, 0, 0))
    rows = N_ATTN_HEADS * tq
    return pl.pallas_call(
        functools.partial(_attn_kernel, topk=topk, tq=tq, tk=tk),
        grid=(B, S // tq),
        in_specs=[qblk(ATTN_WIDTH, COL_AQ), qblk(IDX_WIDTH, COL_IQ), qblk(LANES, COL_IKW),
                  kblk(LANES, COL_AK), kblk(LANES, COL_AV), kblk(LANES, COL_IKW),
                  tab_q, tab_k, tab_q, tab_k],
        out_specs=pl.BlockSpec((1, tq, ATTN_WIDTH), lambda b, q: (b, q, 0)),
        out_shape=jax.ShapeDtypeStruct((B, S, ATTN_WIDTH), BF16),
        scratch_shapes=[pltpu.VMEM((S, LANES), BF16),
                        pltpu.VMEM((S, LANES), BF16),
                        pltpu.VMEM((S, LANES), BF16),
                        pltpu.VMEM((S, 2 * LANES), BF16),
                        pltpu.VMEM((rows, LANES), BF16),
                        pltpu.VMEM((tq, IDX_WIDTH), BF16),
                        pltpu.VMEM((N_IDX_HEADS, tq, LANES), F32),
                        pltpu.VMEM((S // tk, tq, tk), F32),
                        pltpu.VMEM((S, tq), F32),
                        pltpu.VMEM((1, tq), F32),
                        pltpu.VMEM((1, tq), F32),
                        pltpu.VMEM((S // tk, rows, tk), F32),
                        pltpu.VMEM((rows, LANES), F32),
                        pltpu.VMEM((rows, 2 * LANES), F32)],
        compiler_params=_cparams(2),
        name="dsa_attention",
    )(z3, z3, z3, z3, z3, z3, tab_a, tab_a, tab_iq, tab_ik)


def _ret_kernel(rq_ref, rk_ref, rv_ref, rg_ref, tab_ref, decay_ref, zeta_ref, xi_ref, gch_ref, gn_ref,
                o_ref, state_ref, *, chunk):
    S = rq_ref.shape[1]
    C = chunk
    half = RET_KEY_DIM
    state_ref[...] = jnp.zeros(state_ref.shape, F32)
    lane = lax.broadcasted_iota(jnp.int32, (C, LANES), 1)
    first = lane < half

    def body(c, _):
        r0 = pl.multiple_of(c * C, C)
        sl = pl.ds(r0, C)
        tab = tab_ref[0, sl, :]
        for j in range(N_RET_HEADS // 2):
            q = _rope(rq_ref[0, sl, j * LANES:(j + 1) * LANES].astype(F32), tab, RET_KEY_DIM // 2)
            k = _rope(rk_ref[0, sl, j * LANES:(j + 1) * LANES].astype(F32), tab, RET_KEY_DIM // 2)
            k = k * (RET_KEY_DIM ** -0.5)
            kzt = (k * zeta_ref[j]).T.astype(BF16)
            state = state_ref[j]
            state_b = state.astype(BF16)
            new_rows = []
            for par in range(2):
                h = 2 * j + par
                keep = first if par == 0 else jnp.logical_not(first)
                qh = jnp.where(keep, q, 0.0).astype(BF16)
                kh = jnp.where(keep, k, 0.0).astype(BF16)
                v = rv_ref[0, sl, h * LANES:(h + 1) * LANES]
                inner = _dot_t(qh, kh) * decay_ref[h]
                o = _dot(inner.astype(BF16), v) + _dot(qh, state_b) * xi_ref[h]
                new_rows.append(_dot(kzt[par * half:(par + 1) * half, :], v))
                mu = jnp.mean(o, axis=-1, keepdims=True)
                d = o - mu
                var = jnp.mean(d * d, axis=-1, keepdims=True)
                y = d * lax.rsqrt(var + GN_EPS) * gn_ref[:, h * LANES:(h + 1) * LANES]
                g = rg_ref[0, sl, h * LANES:(h + 1) * LANES].astype(F32)
                o_ref[0, sl, h * LANES:(h + 1) * LANES] = (y * (g * jax.nn.sigmoid(g))).astype(o_ref.dtype)
            state_ref[j] = gch_ref[j] * state + jnp.concatenate(new_rows, axis=0)
        return 0

    lax.fori_loop(0, S // C, body, 0)


def _retention(z3, tab_r, consts, gn_w, chunk):
    B, S, _ = z3.shape
    decay, zeta, xi, gch = consts
    blk = lambda w, col: pl.BlockSpec((1, S, w), lambda b: (b, 0, col // w))
    const = lambda a: pl.BlockSpec(a.shape, lambda b: (0,) * a.ndim)
    return pl.pallas_call(
        functools.partial(_ret_kernel, chunk=chunk),
        grid=(B,),
        in_specs=[blk(RET_KEY_WIDTH, COL_RQ), blk(RET_KEY_WIDTH, COL_RK), blk(RET_WIDTH, COL_RV),
                  blk(RET_WIDTH, COL_RG), pl.BlockSpec((1, S, 3 * LANES), lambda b: (b, 0, 0)),
                  const(decay), const(zeta), const(xi), const(gch), const(gn_w)],
        out_specs=pl.BlockSpec((1, S, RET_WIDTH), lambda b: (b, 0, 0)),
        out_shape=jax.ShapeDtypeStruct((B, S, RET_WIDTH), BF16),
        scratch_shapes=[pltpu.VMEM((N_RET_HEADS // 2, LANES, RET_VAL_DIM), F32)],
        compiler_params=_cparams(1),
        name="retention",
    )(z3, z3, z3, z3, tab_r, decay, zeta, xi, gch, gn_w)


def _retention_consts(chunk):
    C = chunk
    log_g = jnp.log(1.0 - 2.0 ** (-5.0 - jnp.arange(N_RET_HEADS, dtype=F32)))
    i = jnp.arange(C, dtype=F32)
    diff = i[:, None] - i[None, :]
    decay = jnp.where(diff[None] >= 0, jnp.exp(jnp.maximum(diff, 0.0)[None] * log_g[:, None, None]), 0.0)
    zeta = jnp.exp((C - 1.0 - i)[None, :] * log_g[:, None])
    xi = jnp.exp((i + 1.0)[None, :] * log_g[:, None])
    g_chunk = jnp.exp(C * log_g)
    pair = lambda a: a.reshape(N_RET_HEADS // 2, 2, -1)
    zeta_p = jnp.repeat(pair(zeta).transpose(0, 2, 1), RET_KEY_DIM, axis=2)
    xi_b = jnp.broadcast_to(xi[:, :, None], (N_RET_HEADS, C, RET_VAL_DIM))
    gch = jnp.broadcast_to(jnp.repeat(pair(g_chunk), RET_KEY_DIM, axis=1).reshape(N_RET_HEADS // 2, LANES, 1),
                           (N_RET_HEADS // 2, LANES, RET_VAL_DIM))
    return decay.astype(F32), zeta_p.astype(F32), xi_b.astype(F32), gch.astype(F32)


def _out_proj_kernel(attn_ref, ret_ref, h_ref, wa_ref, wr_ref, g1_ref, g2_ref, h1_ref, m_ref):
    mix = _dot(attn_ref[...], wa_ref[...]) + _dot(ret_ref[...], wr_ref[...])
    h1 = h_ref[...] + _rms(mix, g1_ref[...])
    h1_ref[...] = h1
    m_ref[...] = _rms(h1, g2_ref[...]).astype(m_ref.dtype)


def _out_proj(attn, ret, h, wa, wr, g1, g2):
    T, D = h.shape
    tm = min(256, T)
    row = lambda w: pl.BlockSpec((tm, w), lambda i: (i, 0))
    const = lambda a: pl.BlockSpec(a.shape, lambda i: (0, 0))
    return pl.pallas_call(
        _out_proj_kernel,
        grid=(T // tm,),
        in_specs=[row(attn.shape[1]), row(ret.shape[1]), row(D), const(wa), const(wr), const(g1), const(g2)],
        out_specs=[row(D), row(D)],
        out_shape=[jax.ShapeDtypeStruct((T, D), F32), jax.ShapeDtypeStruct((T, D), BF16)],
        compiler_params=_cparams(1),
        name="out_proj",
    )(attn, ret, h, wa, wr, g1, g2)


def _ffn_kernel(m_ref, w1_ref, w2_ref, h_ref, g_ref, o_ref, acc_ref):
    f = pl.program_id(1)

    @pl.when(f == 0)
    def _():
        acc_ref[...] = jnp.zeros(acc_ref.shape, F32)

    u = jnp.maximum(_dot(m_ref[...], w1_ref[...]), 0.0)
    acc_ref[...] += _dot((u * u).astype(BF16), w2_ref[...])

    @pl.when(f == pl.num_programs(1) - 1)
    def _():
        o_ref[...] = h_ref[...] + _rms(acc_ref[...], g_ref[...])


def _ffn(m, w1, w2, h, g):
    T, D = h.shape
    F = w1.shape[1]
    tm = min(512, T)
    tf = min(512, F)
    return pl.pallas_call(
        _ffn_kernel,
        grid=(T // tm, F // tf),
        in_specs=[pl.BlockSpec((tm, D), lambda i, f: (i, 0)),
                  pl.BlockSpec((D, tf), lambda i, f: (0, f)),
                  pl.BlockSpec((tf, D), lambda i, f: (f, 0)),
                  pl.BlockSpec((tm, D), lambda i, f: (i, 0)),
                  pl.BlockSpec((1, D), lambda i, f: (0, 0))],
        out_specs=pl.BlockSpec((tm, D), lambda i, f: (i, 0)),
        out_shape=jax.ShapeDtypeStruct((T, D), F32),
        scratch_shapes=[pltpu.VMEM((tm, D), F32)],
        compiler_params=_cparams(2),
        name="ffn",
    )(m, w1, w2, h, g)


def _ple_kernel(h_ref, p_ref, wg_ref, wp_ref, g_ref, o_ref):
    h = h_ref[...]
    gate = jax.nn.sigmoid(_dot(h.astype(BF16), wg_ref[...]))
    e = _dot(p_ref[...].astype(BF16), wp_ref[...])
    o_ref[...] = h + _rms(gate * e, g_ref[...])


def _ple(h, p, wg, wp, g):
    T, D = h.shape
    tm = min(256, T)
    row = lambda w: pl.BlockSpec((tm, w), lambda i: (i, 0))
    const = lambda a: pl.BlockSpec(a.shape, lambda i: (0, 0))
    return pl.pallas_call(
        _ple_kernel,
        grid=(T // tm,),
        in_specs=[row(D), row(p.shape[1]), const(wg), const(wp), const(g)],
        out_specs=row(D),
        out_shape=jax.ShapeDtypeStruct((T, D), F32),
        compiler_params=_cparams(1),
        name="ple",
    )(h, p, wg, wp, g)


def _rope_table(positions, half, theta, group, keep_groups=None):
    inv = theta ** (-jnp.arange(half, dtype=F32) / half)
    ang = positions.astype(F32)[..., None] * inv
    cos, sin = jnp.cos(ang), jnp.sin(ang)
    B, S = positions.shape
    rest = group - 2 * half
    ones = jnp.ones((B, S, rest), F32)
    zeros = lambda n: jnp.zeros((B, S, n), F32)
    c = jnp.concatenate([cos, cos, ones], axis=-1)
    s1 = jnp.concatenate([zeros(half), sin, zeros(rest)], axis=-1)
    s2 = jnp.concatenate([-sin, zeros(half + rest)], axis=-1)
    n = LANES // group
    if keep_groups is None:
        keep_groups = n
    def tile(a, fill):
        parts = [a] * keep_groups + [jnp.full((B, S, group), fill, F32)] * (n - keep_groups)
        return jnp.concatenate(parts, axis=-1)
    return jnp.concatenate([tile(c, 1.0), tile(s1, 0.0), tile(s2, 0.0)], axis=-1)


def _reorder_w_in(w):
    D = w.shape[0]
    sizes = (ATTN_WIDTH, ATTN_HEAD_DIM, ATTN_HEAD_DIM, IDX_WIDTH, IDX_HEAD_DIM, N_IDX_HEADS,
             RET_KEY_WIDTH, RET_KEY_WIDTH, RET_WIDTH, RET_WIDTH)
    offs = [0]
    for s in sizes:
        offs.append(offs[-1] + s)
    aq, ak, av, iq, ik, iw, rq, rk, rv, rg = [w[:, offs[i]:offs[i + 1]] for i in range(len(sizes))]
    pad = jnp.zeros((D, Z_WIDTH - Z_USED + LANES - IDX_HEAD_DIM - N_IDX_HEADS), w.dtype)
    return jnp.concatenate([aq, iq, rv, rg, rq, rk, ak, av, ik, iw, pad], axis=1).astype(BF16)


def kernel(x, p, positions, w_in, w_out, w_ff1, w_ff2, w_ple, w_ple_gate, pre_mix_norm, post_mix_norm,
           pre_ff_norm, post_ff_norm, ple_norm, ret_gn):
    B, S, D = x.shape
    depth = w_in.shape[0]
    T = B * S
    topk = min(TOPK_MAX, S // 4)
    ret_chunk = min(128, S)
    tab_a = _rope_table(positions, ATTN_ROT_HALF, ROPE_THETA, ATTN_HEAD_DIM)
    tab_iq = _rope_table(positions, IDX_ROT_HALF, ROPE_THETA, IDX_HEAD_DIM)
    tab_ik = _rope_table(positions, IDX_ROT_HALF, ROPE_THETA, IDX_HEAD_DIM, keep_groups=1)
    tab_r = _rope_table(positions, RET_KEY_DIM // 2, RET_THETA, RET_KEY_DIM)
    ret_consts = _retention_consts(ret_chunk)
    vec = lambda a: a.reshape(1, -1).astype(F32)

    h = x.reshape(T, D)
    for i in range(depth):
        z = _norm_proj(h, vec(pre_mix_norm[i]), _reorder_w_in(w_in[i]))
        z3 = z.reshape(B, S, Z_WIDTH)
        attn = _attention(z3, tab_a, tab_iq, tab_ik, topk)
        ret = _retention(z3, tab_r, ret_consts, vec(ret_gn[i]), ret_chunk)
        wo = w_out[i].astype(BF16)
        h, m = _out_proj(attn.reshape(T, ATTN_WIDTH), ret.reshape(T, RET_WIDTH), h,
                         wo[:ATTN_WIDTH], wo[ATTN_WIDTH:], vec(post_mix_norm[i]), vec(pre_ff_norm[i]))
        h = _ffn(m, w_ff1[i].astype(BF16), w_ff2[i].astype(BF16), h, vec(post_ff_norm[i]))
        h = _ple(h, p[i].reshape(T, -1), w_ple_gate[i].astype(BF16), w_ple[i].astype(BF16), vec(ple_norm[i]))
    return h.reshape(B, S, D)
```

```python
import functools
import math

import jax
import jax.numpy as jnp
from jax import lax
from jax.experimental import pallas as pl
from jax.experimental.pallas import tpu as pltpu

N_ATTN_HEADS = 8
ATTN_HEAD_DIM = 128
ATTN_ROT_HALF = 16
ROPE_THETA = 500000.0
TOPK_MAX = 256
N_IDX_HEADS = 16
IDX_HEAD_DIM = 64
IDX_ROT_HALF = 8
N_RET_HEADS = 8
RET_KEY_DIM = 64
RET_VAL_DIM = 128
RET_THETA = 10000.0
ATTN_WIDTH = N_ATTN_HEADS * ATTN_HEAD_DIM
IDX_WIDTH = N_IDX_HEADS * IDX_HEAD_DIM
RET_KEY_WIDTH = N_RET_HEADS * RET_KEY_DIM
RET_WIDTH = N_RET_HEADS * RET_VAL_DIM
RMS_EPS = 1e-6
GN_EPS = 1e-5

LANES = 128
COL_AQ = 0
COL_IQ = COL_AQ + ATTN_WIDTH
COL_RV = COL_IQ + IDX_WIDTH
COL_RG = COL_RV + RET_WIDTH
COL_RQ = COL_RG + RET_WIDTH
COL_RK = COL_RQ + RET_KEY_WIDTH
COL_AK = COL_RK + RET_KEY_WIDTH
COL_AV = COL_AK + ATTN_HEAD_DIM
COL_IKW = COL_AV + ATTN_HEAD_DIM
Z_USED = COL_IKW + LANES
Z_TILE_N = 512
Z_WIDTH = -(-Z_USED // Z_TILE_N) * Z_TILE_N

MASK_NEG = -1e30
THETA_ALL = -3e38
CUT_ALL = 1e9
VMEM_LIMIT = 56 * 1024 * 1024

BF16 = jnp.bfloat16
F32 = jnp.float32


def _cparams(n_grid):
    return pltpu.CompilerParams(dimension_semantics=("arbitrary",) * n_grid,
                                vmem_limit_bytes=VMEM_LIMIT)


def _rms(x, gain):
    ms = jnp.mean(x * x, axis=-1, keepdims=True)
    return x * lax.rsqrt(ms + RMS_EPS) * gain


def _rope(x, tab, shift):
    c, s1, s2 = tab[:, :LANES], tab[:, LANES:2 * LANES], tab[:, 2 * LANES:]
    return x * c + pltpu.roll(x, shift, 1) * s1 + pltpu.roll(x, LANES - shift, 1) * s2


def _rope_mxu(xb, tab, perm):
    c, s1, s2 = tab[:, :LANES], tab[:, LANES:2 * LANES], tab[:, 2 * LANES:]
    r = jnp.dot(xb, perm, preferred_element_type=F32)
    return xb.astype(F32) * c + r[:, :LANES] * s1 + r[:, LANES:] * s2


def _lane_roll_matrices(shift):
    k = jnp.arange(LANES)[:, None]
    l = jnp.arange(LANES)[None, :]
    plus = (k == (l - shift) % LANES)
    minus = (k == (l + shift) % LANES)
    return jnp.concatenate([plus, minus], axis=1).astype(BF16)


def _dot_t(a, b):
    return lax.dot_general(a, b, (((1,), (1,)), ((), ())), preferred_element_type=F32)


def _dot(a, b):
    return jnp.dot(a, b, preferred_element_type=F32)


def _norm_proj_kernel(h_ref, g_ref, w_ref, z_ref, a_ref, *, row_chunk):
    @pl.when(pl.program_id(1) == 0)
    def _():
        def body(r, _):
            r0 = pl.multiple_of(r * row_chunk, row_chunk)
            a_ref[pl.ds(r0, row_chunk), :] = _rms(h_ref[pl.ds(r0, row_chunk), :], g_ref[...]).astype(BF16)
            return 0
        lax.fori_loop(0, h_ref.shape[0] // row_chunk, body, 0)

    z_ref[...] = _dot(a_ref[...], w_ref[...]).astype(z_ref.dtype)


def _norm_proj(h, gain, w):
    T, D = h.shape
    N = w.shape[1]
    tm = min(1024, T)
    tn = Z_TILE_N
    return pl.pallas_call(
        functools.partial(_norm_proj_kernel, row_chunk=min(128, tm)),
        grid=(T // tm, N // tn),
        in_specs=[pl.BlockSpec((tm, D), lambda i, j: (i, 0)),
                  pl.BlockSpec((1, D), lambda i, j: (0, 0)),
                  pl.BlockSpec((D, tn), lambda i, j: (0, j))],
        out_specs=pl.BlockSpec((tm, tn), lambda i, j: (i, j)),
        out_shape=jax.ShapeDtypeStruct((T, N), BF16),
        scratch_shapes=[pltpu.VMEM((tm, D), BF16)],
        compiler_params=_cparams(2),
        name="norm_proj",
    )(h, gain, w)


def _attn_kernel(aq_ref, iq_ref, iwq_ref, ak_ref, av_ref, ikw_ref,
                 taq_ref, tak_ref, tiq_ref, tik_ref, pa_ref, pi_ref, ex_ref, o_ref,
                 kr_ref, kd0_ref, kd1_ref, vaug_ref, qa_ref, qi_ref, wb_ref, score_ref, score_t_ref,
                 th_ref, cut_ref, s_ref, m_ref, acc_ref, *, topk, tq, tk):
    S = ak_ref.shape[1]
    qi = pl.program_id(1)
    n_chunks = (qi * tq + tq + tk - 1) // tk
    reps = tk // LANES
    rows_all = N_ATTN_HEADS * tq

    @pl.when(qi == 0)
    def _():
        rows = min(256, S)

        def body(r, _):
            r0 = pl.multiple_of(r * rows, rows)
            sl = pl.ds(r0, rows)
            kr_ref[sl, :] = _rope(ak_ref[0, sl, :].astype(F32), tak_ref[0, sl, :], ATTN_ROT_HALF).astype(BF16)
            ik = _rope(ikw_ref[0, sl, :].astype(F32), tik_ref[0, sl, :], IDX_ROT_HALF)
            lane_r = lax.broadcasted_iota(jnp.int32, (rows, LANES), 1)
            ik = jnp.where(lane_r < IDX_HEAD_DIM, ik, 0.0)
            kd0_ref[sl, :] = ik.astype(BF16)
            kd1_ref[sl, :] = pltpu.roll(ik, IDX_HEAD_DIM, 1).astype(BF16)
            vaug_ref[sl, :] = jnp.concatenate([av_ref[0, sl, :], jnp.ones((rows, LANES), BF16)], axis=1)
            score_t_ref[sl, :] = jnp.full((rows, tq), -jnp.inf, F32)
            return 0
        lax.fori_loop(0, S // rows, body, 0)

    ta = taq_ref[0]
    for h in range(N_ATTN_HEADS):
        x = aq_ref[0, :, h * LANES:(h + 1) * LANES]
        qa_ref[h * tq:(h + 1) * tq, :] = (_rope_mxu(x, ta, pa_ref[...]) * (ATTN_HEAD_DIM ** -0.5)).astype(BF16)
    ti = tiq_ref[0]
    for j in range(IDX_WIDTH // LANES):
        x = iq_ref[0, :, j * LANES:(j + 1) * LANES]
        qi_ref[:, j * LANES:(j + 1) * LANES] = _rope_mxu(x, ti, pi_ref[...]).astype(BF16)
    iw_b = jnp.dot(iwq_ref[0], ex_ref[...], preferred_element_type=F32)
    for h in range(N_IDX_HEADS):
        wb_ref[h] = iw_b[:, h * LANES:(h + 1) * LANES] * ((N_IDX_HEADS ** -0.5) * (IDX_HEAD_DIM ** -0.5))

    row = qi * tq + lax.broadcasted_iota(jnp.int32, (tq, tk), 0)
    col_in_chunk = lax.broadcasted_iota(jnp.int32, (tq, tk), 1)

    def lane_groups(x, op):
        r = x[:, :LANES]
        for g in range(1, reps):
            r = op(r, x[:, g * LANES:(g + 1) * LANES])
        return r

    def tile_lanes(x):
        return jnp.concatenate([x] * reps, axis=1) if reps > 1 else x

    key_t = lax.broadcasted_iota(jnp.int32, (tk, tq), 0)
    qry_t = qi * tq + lax.broadcasted_iota(jnp.int32, (tk, tq), 1)

    def fold8(x, op):
        x3 = x.reshape(tk // 8, 8, tq)
        return jnp.max(x3, axis=0) if op == "max" else (jnp.min(x3, axis=0) if op == "min" else jnp.sum(x3, axis=0))

    def a_body(c, carry):
        rmax, rmin = carry
        c0 = pl.multiple_of(c * tk, tk)
        k0 = kd0_ref[pl.ds(c0, tk), :]
        k1 = kd1_ref[pl.ds(c0, tk), :]
        acc = jnp.zeros((tq, tk), F32)
        for j in range(IDX_WIDTH // LANES):
            qp = qi_ref[:, j * LANES:(j + 1) * LANES]
            for par, kd in ((0, k0), (1, k1)):
                logits = _dot_t(qp, kd)
                acc = acc + tile_lanes(wb_ref[2 * j + par]) * jnp.maximum(logits, 0.0)
        causal = (c0 + col_in_chunk) <= row
        score_ref[c] = jnp.where(causal, acc, -jnp.inf)
        acc_t = acc.T
        causal_t = (c0 + key_t) <= qry_t
        sc_t = jnp.where(causal_t, acc_t, -jnp.inf)
        score_t_ref[pl.ds(c0, tk), :] = sc_t
        rmax = jnp.maximum(rmax, fold8(sc_t, "max"))
        rmin = jnp.minimum(rmin, fold8(jnp.where(causal_t, acc_t, jnp.inf), "min"))
        return rmax, rmin

    rmax, rmin = lax.fori_loop(0, n_chunks, a_body,
                               (jnp.full((8, tq), -jnp.inf, F32), jnp.full((8, tq), jnp.inf, F32)))

    vec = (1, tq)
    kf = float(topk)
    n_keys = qi * tq + lax.broadcasted_iota(jnp.int32, vec, 1) + 1

    lo0 = jnp.min(rmin, axis=0, keepdims=True)
    hi0 = jnp.max(rmax, axis=0, keepdims=True)
    key_f = key_t.astype(F32)

    def threshold_search(n_rows):
        def count_t(pred):
            accs = [jnp.zeros((8, tq), F32) for _ in range(4)]
            i = 0
            for r0 in range(0, n_rows, tk):
                ind = jnp.where(pred(r0, score_t_ref[r0:r0 + tk, :]), 1.0, 0.0)
                for g in range(tk // 8):
                    accs[i % 4] = accs[i % 4] + ind[g * 8:(g + 1) * 8, :]
                    i += 1
            return jnp.sum((accs[0] + accs[1]) + (accs[2] + accs[3]), axis=0, keepdims=True)

        def b_cond(carry):
            return carry[1] > 0.0

        def b_body(carry):
            state = (carry[0],) + tuple(carry[2:])
            flag = jnp.max(1.0 - carry[5])
            for _ in range(steps_per_check):
                state = b_step(*state)
            return (state[0], flag) + tuple(state[1:])

        def b_step(it, lo, hi, chi, done, th, tie):
            mid = jnp.where(it == 0, hi, 0.5 * lo + 0.5 * hi)
            cnt = count_t(lambda r0, sc: sc >= mid)
            active = done == 0.0
            adjacent = jnp.logical_and(it > 0, jnp.logical_or(mid <= lo, mid >= hi))
            tie_now = jnp.logical_and(active, adjacent)
            live = jnp.logical_and(active, jnp.logical_not(adjacent))
            hit = jnp.logical_and(live, cnt == kf)
            above = jnp.logical_and(live, cnt > kf)
            below = jnp.logical_and(live, cnt < kf)
            th = jnp.where(tie_now, lo, jnp.where(hit, mid, th))
            tie = jnp.where(tie_now, 1.0, tie)
            lo = jnp.where(above, mid, lo)
            hi = jnp.where(below, mid, hi)
            chi = jnp.where(below, cnt, chi)
            done = jnp.where(jnp.logical_or(tie_now, hit), 1.0, done)
            return it + 1, lo, hi, chi, done, th, tie

        steps_per_check = 2 if n_rows <= 4 * tk else 1

        init = (jnp.int32(0), jnp.float32(1.0), lo0, hi0,
                jnp.zeros(vec, F32), jnp.where(n_keys <= topk, 1.0, 0.0),
                jnp.full(vec, THETA_ALL, F32), jnp.zeros(vec, F32))
        _, _, _, _, chi, _, th, tie = lax.while_loop(b_cond, b_body, init)

        def tie_cut():
            need = kf - chi

            def body(_, carry):
                lo_i, hi_i = carry
                mid = jnp.floor(0.5 * (lo_i + hi_i))
                cnt = count_t(lambda r0, sc: jnp.logical_and(sc == th, key_f + float(r0) <= mid))
                ge = cnt >= need
                return jnp.where(ge, lo_i, mid), jnp.where(ge, mid, hi_i)

            n_iter = int(math.ceil(math.log2(S))) + 1
            _, hi_i = lax.fori_loop(0, n_iter, body, (jnp.full(vec, -1.0, F32), jnp.full(vec, float(S - 1), F32)))
            return jnp.where(tie > 0.0, hi_i, CUT_ALL)

        th_ref[...] = th
        cut_ref[...] = lax.cond(jnp.max(tie) > 0.0, tie_cut, lambda: jnp.full(vec, CUT_ALL, F32))

    pair_rows = 2 * tk
    variant = (n_chunks * tk + pair_rows - 1) // pair_rows - 1
    for v in range(-(-S // pair_rows)):
        pl.when(variant == v)(functools.partial(threshold_search, min((v + 1) * pair_rows, S)))

    th_c = jnp.broadcast_to(th_ref[...], (LANES, tq)).T
    cut_c = jnp.broadcast_to(cut_ref[...], (LANES, tq)).T
    th_t = tile_lanes(th_c)
    cut_t = tile_lanes(cut_c)

    m_ref[...] = jnp.full(m_ref.shape, MASK_NEG, F32)
    colf_in_chunk = col_in_chunk.astype(F32)

    def c1_body(c, _):
        c0 = pl.multiple_of(c * tk, tk)
        sc = score_ref[c]
        colf = colf_in_chunk + (c * tk).astype(F32)
        sel = jnp.logical_or(sc > th_t, jnp.logical_and(sc == th_t, colf <= cut_t))
        bias = jnp.where(sel, 0.0, MASK_NEG)
        s = _dot_t(qa_ref[...], kr_ref[pl.ds(c0, tk), :])
        s = (s.reshape(N_ATTN_HEADS, tq, tk) + bias[None]).reshape(rows_all, tk)
        s_ref[c] = s
        m_ref[...] = jnp.maximum(m_ref[...], lane_groups(s, jnp.maximum))
        return 0

    lax.fori_loop(0, n_chunks, c1_body, 0)
    m_ref[...] = jnp.broadcast_to(jnp.max(m_ref[...], axis=1, keepdims=True), m_ref.shape)

    acc_ref[...] = jnp.zeros(acc_ref.shape, F32)

    def c2_body(c, _):
        c0 = pl.multiple_of(c * tk, tk)
        p = jnp.exp(s_ref[c] - tile_lanes(m_ref[...]))
        acc_ref[...] += _dot(p.astype(BF16), vaug_ref[pl.ds(c0, tk), :])
        return 0

    lax.fori_loop(0, n_chunks, c2_body, 0)
    out = acc_ref[:, :LANES] / acc_ref[:, LANES:]
    for h in range(N_ATTN_HEADS):
        o_ref[0, :, h * LANES:(h + 1) * LANES] = out[h * tq:(h + 1) * tq, :].astype(o_ref.dtype)


def _attention(z3, tab_a, tab_iq, tab_ik, topk):
    B, S, _ = z3.shape
    tq = min(128, S)
    tk = min(256, S)
    qblk = lambda w, col: pl.BlockSpec((1, tq, w), lambda b, q: (b, q, col // w))
    kblk = lambda w, col: pl.BlockSpec((1, S, w), lambda b, q: (b, 0, col // w))
    tab_q = pl.BlockSpec((1, tq, 3 * LANES), lambda b, q: (b, q, 0))
    tab_k = pl.BlockSpec((1, S, 3 * LANES), lambda b, q: (b, 0, 0))
    rows = N_ATTN_HEADS * tq
    const = lambda a: pl.BlockSpec(a.shape, lambda b, q: (0, 0))
    perm_a = _lane_roll_matrices(ATTN_ROT_HALF)
    perm_i = _lane_roll_matrices(IDX_ROT_HALF)
    expand = (jnp.arange(LANES)[:, None] == IDX_HEAD_DIM + jnp.arange(N_IDX_HEADS * LANES)[None, :] // LANES).astype(BF16)
    return pl.pallas_call(
        functools.partial(_attn_kernel, topk=topk, tq=tq, tk=tk),
        grid=(B, S // tq),
        in_specs=[qblk(ATTN_WIDTH, COL_AQ), qblk(IDX_WIDTH, COL_IQ), qblk(LANES, COL_IKW),
                  kblk(LANES, COL_AK), kblk(LANES, COL_AV), kblk(LANES, COL_IKW),
                  tab_q, tab_k, tab_q, tab_k, const(perm_a), const(perm_i), const(expand)],
        out_specs=pl.BlockSpec((1, tq, ATTN_WIDTH), lambda b, q: (b, q, 0)),
        out_shape=jax.ShapeDtypeStruct((B, S, ATTN_WIDTH), BF16),
        scratch_shapes=[pltpu.VMEM((S, LANES), BF16),
                        pltpu.VMEM((S, LANES), BF16),
                        pltpu.VMEM((S, LANES), BF16),
                        pltpu.VMEM((S, 2 * LANES), BF16),
                        pltpu.VMEM((rows, LANES), BF16),
                        pltpu.VMEM((tq, IDX_WIDTH), BF16),
                        pltpu.VMEM((N_IDX_HEADS, tq, LANES), F32),
                        pltpu.VMEM((S // tk, tq, tk), F32),
                        pltpu.VMEM((S, tq), F32),
                        pltpu.VMEM((1, tq), F32),
                        pltpu.VMEM((1, tq), F32),
                        pltpu.VMEM((S // tk, rows, tk), F32),
                        pltpu.VMEM((rows, LANES), F32),
                        pltpu.VMEM((rows, 2 * LANES), F32)],
        compiler_params=_cparams(2),
        name="dsa_attention",
    )(z3, z3, z3, z3, z3, z3, tab_a, tab_a, tab_iq, tab_ik, perm_a, perm_i, expand)


def _ret_kernel(rq_ref, rk_ref, rv_ref, rg_ref, tab_ref, decay_ref, zeta_ref, xi_ref, gch_ref, gn_ref,
                o_ref, state_ref, *, chunk):
    S = rq_ref.shape[1]
    C = chunk
    half = RET_KEY_DIM
    state_ref[...] = jnp.zeros(state_ref.shape, F32)
    lane = lax.broadcasted_iota(jnp.int32, (C, LANES), 1)
    first = lane < half

    def body(c, _):
        r0 = pl.multiple_of(c * C, C)
        sl = pl.ds(r0, C)
        tab = tab_ref[0, sl, :]
        for j in range(N_RET_HEADS // 2):
            q = _rope(rq_ref[0, sl, j * LANES:(j + 1) * LANES].astype(F32), tab, RET_KEY_DIM // 2)
            k = _rope(rk_ref[0, sl, j * LANES:(j + 1) * LANES].astype(F32), tab, RET_KEY_DIM // 2)
            k = k * (RET_KEY_DIM ** -0.5)
            kzt = (k * zeta_ref[j]).T.astype(BF16)
            state = state_ref[j]
            state_b = state.astype(BF16)
            new_rows = []
            for par in range(2):
                h = 2 * j + par
                keep = first if par == 0 else jnp.logical_not(first)
                qh = jnp.where(keep, q, 0.0).astype(BF16)
                kh = jnp.where(keep, k, 0.0).astype(BF16)
                v = rv_ref[0, sl, h * LANES:(h + 1) * LANES]
                inner = _dot_t(qh, kh) * decay_ref[h]
                o = _dot(inner.astype(BF16), v) + _dot(qh, state_b) * xi_ref[h]
                new_rows.append(_dot(kzt[par * half:(par + 1) * half, :], v))
                mu = jnp.mean(o, axis=-1, keepdims=True)
                d = o - mu
                var = jnp.mean(d * d, axis=-1, keepdims=True)
                y = d * lax.rsqrt(var + GN_EPS) * gn_ref[:, h * LANES:(h + 1) * LANES]
                g = rg_ref[0, sl, h * LANES:(h + 1) * LANES].astype(F32)
                o_ref[0, sl, h * LANES:(h + 1) * LANES] = (y * (g * jax.nn.sigmoid(g))).astype(o_ref.dtype)
            state_ref[j] = gch_ref[j] * state + jnp.concatenate(new_rows, axis=0)
        return 0

    lax.fori_loop(0, S // C, body, 0)


def _retention(z3, tab_r, consts, gn_w, chunk):
    B, S, _ = z3.shape
    decay, zeta, xi, gch = consts
    blk = lambda w, col: pl.BlockSpec((1, S, w), lambda b: (b, 0, col // w))
    const = lambda a: pl.BlockSpec(a.shape, lambda b: (0,) * a.ndim)
    return pl.pallas_call(
        functools.partial(_ret_kernel, chunk=chunk),
        grid=(B,),
        in_specs=[blk(RET_KEY_WIDTH, COL_RQ), blk(RET_KEY_WIDTH, COL_RK), blk(RET_WIDTH, COL_RV),
                  blk(RET_WIDTH, COL_RG), pl.BlockSpec((1, S, 3 * LANES), lambda b: (b, 0, 0)),
                  const(decay), const(zeta), const(xi), const(gch), const(gn_w)],
        out_specs=pl.BlockSpec((1, S, RET_WIDTH), lambda b: (b, 0, 0)),
        out_shape=jax.ShapeDtypeStruct((B, S, RET_WIDTH), BF16),
        scratch_shapes=[pltpu.VMEM((N_RET_HEADS // 2, LANES, RET_VAL_DIM), F32)],
        compiler_params=_cparams(1),
        name="retention",
    )(z3, z3, z3, z3, tab_r, decay, zeta, xi, gch, gn_w)


def _retention_consts(chunk):
    C = chunk
    log_g = jnp.log(1.0 - 2.0 ** (-5.0 - jnp.arange(N_RET_HEADS, dtype=F32)))
    i = jnp.arange(C, dtype=F32)
    diff = i[:, None] - i[None, :]
    decay = jnp.where(diff[None] >= 0, jnp.exp(jnp.maximum(diff, 0.0)[None] * log_g[:, None, None]), 0.0)
    zeta = jnp.exp((C - 1.0 - i)[None, :] * log_g[:, None])
    xi = jnp.exp((i + 1.0)[None, :] * log_g[:, None])
    g_chunk = jnp.exp(C * log_g)
    pair = lambda a: a.reshape(N_RET_HEADS // 2, 2, -1)
    zeta_p = jnp.repeat(pair(zeta).transpose(0, 2, 1), RET_KEY_DIM, axis=2)
    xi_b = jnp.broadcast_to(xi[:, :, None], (N_RET_HEADS, C, RET_VAL_DIM))
    gch = jnp.broadcast_to(jnp.repeat(pair(g_chunk), RET_KEY_DIM, axis=1).reshape(N_RET_HEADS // 2, LANES, 1),
                           (N_RET_HEADS // 2, LANES, RET_VAL_DIM))
    return decay.astype(F32), zeta_p.astype(F32), xi_b.astype(F32), gch.astype(F32)


def _out_proj_kernel(attn_ref, ret_ref, h_ref, wa_ref, wr_ref, g1_ref, g2_ref, h1_ref, m_ref):
    mix = _dot(attn_ref[...], wa_ref[...]) + _dot(ret_ref[...], wr_ref[...])
    h1 = h_ref[...] + _rms(mix, g1_ref[...])
    h1_ref[...] = h1
    m_ref[...] = _rms(h1, g2_ref[...]).astype(m_ref.dtype)


def _out_proj(attn, ret, h, wa, wr, g1, g2):
    T, D = h.shape
    tm = min(256, T)
    row = lambda w: pl.BlockSpec((tm, w), lambda i: (i, 0))
    const = lambda a: pl.BlockSpec(a.shape, lambda i: (0, 0))
    return pl.pallas_call(
        _out_proj_kernel,
        grid=(T // tm,),
        in_specs=[row(attn.shape[1]), row(ret.shape[1]), row(D), const(wa), const(wr), const(g1), const(g2)],
        out_specs=[row(D), row(D)],
        out_shape=[jax.ShapeDtypeStruct((T, D), F32), jax.ShapeDtypeStruct((T, D), BF16)],
        compiler_params=_cparams(1),
        name="out_proj",
    )(attn, ret, h, wa, wr, g1, g2)


def _ffn_kernel(m_ref, w1_ref, w2_ref, h_ref, g_ref, o_ref, acc_ref):
    f = pl.program_id(1)

    @pl.when(f == 0)
    def _():
        acc_ref[...] = jnp.zeros(acc_ref.shape, F32)

    u = jnp.maximum(_dot(m_ref[...], w1_ref[...]), 0.0)
    acc_ref[...] += _dot((u * u).astype(BF16), w2_ref[...])

    @pl.when(f == pl.num_programs(1) - 1)
    def _():
        o_ref[...] = h_ref[...] + _rms(acc_ref[...], g_ref[...])


def _ffn(m, w1, w2, h, g):
    T, D = h.shape
    F = w1.shape[1]
    tm = min(512, T)
    tf = min(512, F)
    return pl.pallas_call(
        _ffn_kernel,
        grid=(T // tm, F // tf),
        in_specs=[pl.BlockSpec((tm, D), lambda i, f: (i, 0)),
                  pl.BlockSpec((D, tf), lambda i, f: (0, f)),
                  pl.BlockSpec((tf, D), lambda i, f: (f, 0)),
                  pl.BlockSpec((tm, D), lambda i, f: (i, 0)),
                  pl.BlockSpec((1, D), lambda i, f: (0, 0))],
        out_specs=pl.BlockSpec((tm, D), lambda i, f: (i, 0)),
        out_shape=jax.ShapeDtypeStruct((T, D), F32),
        scratch_shapes=[pltpu.VMEM((tm, D), F32)],
        compiler_params=_cparams(2),
        name="ffn",
    )(m, w1, w2, h, g)


def _ple_kernel(h_ref, p_ref, wg_ref, wp_ref, g_ref, o_ref):
    h = h_ref[...]
    gate = jax.nn.sigmoid(_dot(h.astype(BF16), wg_ref[...]))
    e = _dot(p_ref[...].astype(BF16), wp_ref[...])
    o_ref[...] = h + _rms(gate * e, g_ref[...])


def _ple(h, p, wg, wp, g):
    T, D = h.shape
    tm = min(256, T)
    row = lambda w: pl.BlockSpec((tm, w), lambda i: (i, 0))
    const = lambda a: pl.BlockSpec(a.shape, lambda i: (0, 0))
    return pl.pallas_call(
        _ple_kernel,
        grid=(T // tm,),
        in_specs=[row(D), row(p.shape[1]), const(wg), const(wp), const(g)],
        out_specs=row(D),
        out_shape=jax.ShapeDtypeStruct((T, D), F32),
        compiler_params=_cparams(1),
        name="ple",
    )(h, p, wg, wp, g)


def _rope_table(positions, half, theta, group, keep_groups=None):
    inv = theta ** (-jnp.arange(half, dtype=F32) / half)
    ang = positions.astype(F32)[..., None] * inv
    cos, sin = jnp.cos(ang), jnp.sin(ang)
    B, S = positions.shape
    rest = group - 2 * half
    ones = jnp.ones((B, S, rest), F32)
    zeros = lambda n: jnp.zeros((B, S, n), F32)
    c = jnp.concatenate([cos, cos, ones], axis=-1)
    s1 = jnp.concatenate([zeros(half), sin, zeros(rest)], axis=-1)
    s2 = jnp.concatenate([-sin, zeros(half + rest)], axis=-1)
    n = LANES // group
    if keep_groups is None:
        keep_groups = n
    def tile(a, fill):
        parts = [a] * keep_groups + [jnp.full((B, S, group), fill, F32)] * (n - keep_groups)
        return jnp.concatenate(parts, axis=-1)
    return jnp.concatenate([tile(c, 1.0), tile(s1, 0.0), tile(s2, 0.0)], axis=-1)


def _reorder_w_in(w):
    D = w.shape[0]
    sizes = (ATTN_WIDTH, ATTN_HEAD_DIM, ATTN_HEAD_DIM, IDX_WIDTH, IDX_HEAD_DIM, N_IDX_HEADS,
             RET_KEY_WIDTH, RET_KEY_WIDTH, RET_WIDTH, RET_WIDTH)
    offs = [0]
    for s in sizes:
        offs.append(offs[-1] + s)
    aq, ak, av, iq, ik, iw, rq, rk, rv, rg = [w[:, offs[i]:offs[i + 1]] for i in range(len(sizes))]
    pad = jnp.zeros((D, Z_WIDTH - Z_USED + LANES - IDX_HEAD_DIM - N_IDX_HEADS), w.dtype)
    return jnp.concatenate([aq, iq, rv, rg, rq, rk, ak, av, ik, iw, pad], axis=1).astype(BF16)


def kernel(x, p, positions, w_in, w_out, w_ff1, w_ff2, w_ple, w_ple_gate, pre_mix_norm, post_mix_norm,
           pre_ff_norm, post_ff_norm, ple_norm, ret_gn):
    B, S, D = x.shape
    depth = w_in.shape[0]
    T = B * S
    topk = min(TOPK_MAX, S // 4)
    ret_chunk = min(128, S)
    tab_a = _rope_table(positions, ATTN_ROT_HALF, ROPE_THETA, ATTN_HEAD_DIM)
    tab_iq = _rope_table(positions, IDX_ROT_HALF, ROPE_THETA, IDX_HEAD_DIM)
    tab_ik = _rope_table(positions, IDX_ROT_HALF, ROPE_THETA, IDX_HEAD_DIM, keep_groups=1)
    tab_r = _rope_table(positions, RET_KEY_DIM // 2, RET_THETA, RET_KEY_DIM)
    ret_consts = _retention_consts(ret_chunk)
    vec = lambda a: a.reshape(1, -1).astype(F32)

    h = x.reshape(T, D)
    for i in range(depth):
        z = _norm_proj(h, vec(pre_mix_norm[i]), _reorder_w_in(w_in[i]))
        z3 = z.reshape(B, S, Z_WIDTH)
        attn = _attention(z3, tab_a, tab_iq, tab_ik, topk)
        ret = _retention(z3, tab_r, ret_consts, vec(ret_gn[i]), ret_chunk)
        wo = w_out[i].astype(BF16)
        h, m = _out_proj(attn.reshape(T, ATTN_WIDTH), ret.reshape(T, RET_WIDTH), h,
                         wo[:ATTN_WIDTH], wo[ATTN_WIDTH:], vec(post_mix_norm[i]), vec(pre_ff_norm[i]))
        h = _ffn(m, w_ff1[i].astype(BF16), w_ff2[i].astype(BF16), h, vec(post_ff_norm[i]))
        h = _ple(h, p[i].reshape(T, -1), w_ple_gate[i].astype(BF16), w_ple[i].astype(BF16), vec(ple_norm[i]))
    return h.reshape(B, S, D)
```

```python
import functools
import math

import jax
import jax.numpy as jnp
from jax import lax
from jax.experimental import pallas as pl
from jax.experimental.pallas import tpu as pltpu

N_ATTN_HEADS = 8
ATTN_HEAD_DIM = 128
ATTN_ROT_HALF = 16
ROPE_THETA = 500000.0
TOPK_MAX = 256
N_IDX_HEADS = 16
IDX_HEAD_DIM = 64
IDX_ROT_HALF = 8
N_RET_HEADS = 8
RET_KEY_DIM = 64
RET_VAL_DIM = 128
RET_THETA = 10000.0
ATTN_WIDTH = N_ATTN_HEADS * ATTN_HEAD_DIM
IDX_WIDTH = N_IDX_HEADS * IDX_HEAD_DIM
RET_KEY_WIDTH = N_RET_HEADS * RET_KEY_DIM
RET_WIDTH = N_RET_HEADS * RET_VAL_DIM
RMS_EPS = 1e-6
GN_EPS = 1e-5

LANES = 128
COL_AQ = 0
COL_IQ = COL_AQ + ATTN_WIDTH
COL_RV = COL_IQ + IDX_WIDTH
COL_RG = COL_RV + RET_WIDTH
COL_RQ = COL_RG + RET_WIDTH
COL_RK = COL_RQ + RET_KEY_WIDTH
COL_AK = COL_RK + RET_KEY_WIDTH
COL_AV = COL_AK + ATTN_HEAD_DIM
COL_IKW = COL_AV + ATTN_HEAD_DIM
Z_USED = COL_IKW + LANES
Z_TILE_N = 512
Z_WIDTH = -(-Z_USED // Z_TILE_N) * Z_TILE_N

MASK_NEG = -1e30
THETA_ALL = -3e38
CUT_ALL = 1e9
VMEM_LIMIT = 56 * 1024 * 1024

BF16 = jnp.bfloat16
F32 = jnp.float32


def _cparams(n_grid):
    return pltpu.CompilerParams(dimension_semantics=("arbitrary",) * n_grid,
                                vmem_limit_bytes=VMEM_LIMIT)


def _rms(x, gain):
    ms = jnp.mean(x * x, axis=-1, keepdims=True)
    return x * lax.rsqrt(ms + RMS_EPS) * gain


def _rope(xb, cos, sin, swap):
    return xb.astype(F32) * cos + jnp.dot(xb, swap, preferred_element_type=F32) * sin


def _swap_matrix(group, half):
    l = jnp.arange(LANES)
    d = l % group
    partner = jnp.where(d < half, l + half, jnp.where(d < 2 * half, l - half, -1))
    return (jnp.arange(LANES)[:, None] == partner[None, :]).astype(BF16)


def _dot_t(a, b):
    return lax.dot_general(a, b, (((1,), (1,)), ((), ())), preferred_element_type=F32)


def _dot(a, b):
    return jnp.dot(a, b, preferred_element_type=F32)


def _norm_proj_kernel(h_ref, g_ref, w_ref, z_ref, a_ref, *, row_chunk):
    @pl.when(pl.program_id(1) == 0)
    def _():
        def body(r, _):
            r0 = pl.multiple_of(r * row_chunk, row_chunk)
            a_ref[pl.ds(r0, row_chunk), :] = _rms(h_ref[pl.ds(r0, row_chunk), :], g_ref[...]).astype(BF16)
            return 0
        lax.fori_loop(0, h_ref.shape[0] // row_chunk, body, 0)

    z_ref[...] = _dot(a_ref[...], w_ref[...]).astype(z_ref.dtype)


def _norm_proj(h, gain, w):
    T, D = h.shape
    N = w.shape[1]
    tm = min(1024, T)
    tn = Z_TILE_N
    return pl.pallas_call(
        functools.partial(_norm_proj_kernel, row_chunk=min(128, tm)),
        grid=(T // tm, N // tn),
        in_specs=[pl.BlockSpec((tm, D), lambda i, j: (i, 0)),
                  pl.BlockSpec((1, D), lambda i, j: (0, 0)),
                  pl.BlockSpec((D, tn), lambda i, j: (0, j))],
        out_specs=pl.BlockSpec((tm, tn), lambda i, j: (i, j)),
        out_shape=jax.ShapeDtypeStruct((T, N), BF16),
        scratch_shapes=[pltpu.VMEM((tm, D), BF16)],
        compiler_params=_cparams(2),
        name="norm_proj",
    )(h, gain, w)


def _attn_kernel(aq_ref, iq_ref, iwq_ref, ak_ref, av_ref, ikw_ref,
                 taq_ref, tak_ref, tiq_ref, tik_ref, pa_ref, pi_ref, ex_ref, o_ref,
                 kr_ref, kd0_ref, kd1_ref, vaug_ref, qa_ref, qi_ref, wb_ref, score_ref, score_t_ref,
                 th_ref, cut_ref, s_ref, m_ref, acc_ref, *, topk, tq, tk):
    S = ak_ref.shape[1]
    qi = pl.program_id(1)
    n_chunks = (qi * tq + tq + tk - 1) // tk
    reps = tk // LANES
    rows_all = N_ATTN_HEADS * tq

    @pl.when(qi == 0)
    def _():
        rows = min(256, S)

        def body(r, _):
            r0 = pl.multiple_of(r * rows, rows)
            sl = pl.ds(r0, rows)
            kr_ref[sl, :] = _rope(ak_ref[0, sl, :], tak_ref[0, sl, :LANES], tak_ref[0, sl, LANES:],
                                  pa_ref[...]).astype(BF16)
            lane_r = lax.broadcasted_iota(jnp.int32, (rows, LANES), 1)
            ik = _rope(ikw_ref[0, sl, :], tik_ref[0, sl, :LANES], tik_ref[0, sl, LANES:], pi_ref[...])
            ik = jnp.where(lane_r < IDX_HEAD_DIM, ik, 0.0)
            kd0_ref[sl, :] = ik.astype(BF16)
            kd1_ref[sl, :] = pltpu.roll(ik, IDX_HEAD_DIM, 1).astype(BF16)
            vaug_ref[sl, :] = jnp.concatenate([av_ref[0, sl, :], jnp.ones((rows, LANES), BF16)], axis=1)
            score_t_ref[sl, :] = jnp.full((rows, tq), -jnp.inf, F32)
            return 0
        lax.fori_loop(0, S // rows, body, 0)

    ca = taq_ref[0, :, :LANES] * (ATTN_HEAD_DIM ** -0.5)
    sa = taq_ref[0, :, LANES:] * (ATTN_HEAD_DIM ** -0.5)
    for h in range(N_ATTN_HEADS):
        x = aq_ref[0, :, h * LANES:(h + 1) * LANES]
        qa_ref[h * tq:(h + 1) * tq, :] = _rope(x, ca, sa, pa_ref[...]).astype(BF16)
    ci, si = tiq_ref[0, :, :LANES], tiq_ref[0, :, LANES:]
    for j in range(IDX_WIDTH // LANES):
        x = iq_ref[0, :, j * LANES:(j + 1) * LANES]
        qi_ref[:, j * LANES:(j + 1) * LANES] = _rope(x, ci, si, pi_ref[...]).astype(BF16)
    iw_b = jnp.dot(iwq_ref[0], ex_ref[...], preferred_element_type=F32)
    for h in range(N_IDX_HEADS):
        wb_ref[h] = iw_b[:, h * LANES:(h + 1) * LANES]

    row = qi * tq + lax.broadcasted_iota(jnp.int32, (tq, tk), 0)
    col_in_chunk = lax.broadcasted_iota(jnp.int32, (tq, tk), 1)

    def lane_groups(x, op):
        r = x[:, :LANES]
        for g in range(1, reps):
            r = op(r, x[:, g * LANES:(g + 1) * LANES])
        return r

    def tile_lanes(x):
        return jnp.concatenate([x] * reps, axis=1) if reps > 1 else x

    key_t = lax.broadcasted_iota(jnp.int32, (tk, tq), 0)
    qry_t = qi * tq + lax.broadcasted_iota(jnp.int32, (tk, tq), 1)

    def fold8(x, op):
        x3 = x.reshape(tk // 8, 8, tq)
        return jnp.max(x3, axis=0) if op == "max" else (jnp.min(x3, axis=0) if op == "min" else jnp.sum(x3, axis=0))

    def a_body(c, carry):
        rmax, rmin = carry
        c0 = pl.multiple_of(c * tk, tk)
        k0 = kd0_ref[pl.ds(c0, tk), :]
        k1 = kd1_ref[pl.ds(c0, tk), :]
        acc = jnp.zeros((tq, tk), F32)
        for j in range(IDX_WIDTH // LANES):
            qp = qi_ref[:, j * LANES:(j + 1) * LANES]
            for par, kd in ((0, k0), (1, k1)):
                logits = _dot_t(qp, kd)
                acc = acc + tile_lanes(wb_ref[2 * j + par]) * jnp.maximum(logits, 0.0)
        causal = (c0 + col_in_chunk) <= row
        score_ref[c] = jnp.where(causal, acc, -jnp.inf)
        acc_t = acc.T
        causal_t = (c0 + key_t) <= qry_t
        sc_t = jnp.where(causal_t, acc_t, -jnp.inf)
        score_t_ref[pl.ds(c0, tk), :] = sc_t
        rmax = jnp.maximum(rmax, fold8(sc_t, "max"))
        rmin = jnp.minimum(rmin, fold8(jnp.where(causal_t, acc_t, jnp.inf), "min"))
        return rmax, rmin

    rmax, rmin = lax.fori_loop(0, n_chunks, a_body,
                               (jnp.full((8, tq), -jnp.inf, F32), jnp.full((8, tq), jnp.inf, F32)))

    vec = (1, tq)
    kf = float(topk)
    n_keys = qi * tq + lax.broadcasted_iota(jnp.int32, vec, 1) + 1

    lo0 = jnp.min(rmin, axis=0, keepdims=True)
    hi0 = jnp.max(rmax, axis=0, keepdims=True)
    key_f = key_t.astype(F32)

    def threshold_search(n_rows):
        def count_t(pred):
            accs = [jnp.zeros((8, tq), F32) for _ in range(4)]
            i = 0
            for r0 in range(0, n_rows, tk):
                ind = jnp.where(pred(r0, score_t_ref[r0:r0 + tk, :]), 1.0, 0.0)
                for g in range(tk // 8):
                    accs[i % 4] = accs[i % 4] + ind[g * 8:(g + 1) * 8, :]
                    i += 1
            return jnp.sum((accs[0] + accs[1]) + (accs[2] + accs[3]), axis=0, keepdims=True)

        def b_cond(carry):
            return carry[1] > 0.0

        def b_body(carry):
            state = (carry[0],) + tuple(carry[2:])
            flag = jnp.max(1.0 - carry[5])
            for _ in range(steps_per_check):
                state = b_step(*state)
            return (state[0], flag) + tuple(state[1:])

        def b_step(it, lo, hi, chi, done, th, tie):
            mid = jnp.where(it == 0, hi, 0.5 * lo + 0.5 * hi)
            cnt = count_t(lambda r0, sc: sc >= mid)
            active = done == 0.0
            adjacent = jnp.logical_and(it > 0, jnp.logical_or(mid <= lo, mid >= hi))
            tie_now = jnp.logical_and(active, adjacent)
            live = jnp.logical_and(active, jnp.logical_not(adjacent))
            hit = jnp.logical_and(live, cnt == kf)
            above = jnp.logical_and(live, cnt > kf)
            below = jnp.logical_and(live, cnt < kf)
            th = jnp.where(tie_now, lo, jnp.where(hit, mid, th))
            tie = jnp.where(tie_now, 1.0, tie)
            lo = jnp.where(above, mid, lo)
            hi = jnp.where(below, mid, hi)
            chi = jnp.where(below, cnt, chi)
            done = jnp.where(jnp.logical_or(tie_now, hit), 1.0, done)
            return it + 1, lo, hi, chi, done, th, tie

        steps_per_check = 2 if n_rows <= 4 * tk else 1

        init = (jnp.int32(0), jnp.float32(1.0), lo0, hi0,
                jnp.zeros(vec, F32), jnp.where(n_keys <= topk, 1.0, 0.0),
                jnp.full(vec, THETA_ALL, F32), jnp.zeros(vec, F32))
        _, _, _, _, chi, _, th, tie = lax.while_loop(b_cond, b_body, init)

        def tie_cut():
            need = kf - chi

            def body(_, carry):
                lo_i, hi_i = carry
                mid = jnp.floor(0.5 * (lo_i + hi_i))
                cnt = count_t(lambda r0, sc: jnp.logical_and(sc == th, key_f + float(r0) <= mid))
                ge = cnt >= need
                return jnp.where(ge, lo_i, mid), jnp.where(ge, mid, hi_i)

            n_iter = int(math.ceil(math.log2(S))) + 1
            _, hi_i = lax.fori_loop(0, n_iter, body, (jnp.full(vec, -1.0, F32), jnp.full(vec, float(S - 1), F32)))
            return jnp.where(tie > 0.0, hi_i, CUT_ALL)

        th_ref[...] = th
        cut_ref[...] = lax.cond(jnp.max(tie) > 0.0, tie_cut, lambda: jnp.full(vec, CUT_ALL, F32))

    pair_rows = 2 * tk
    variant = (n_chunks * tk + pair_rows - 1) // pair_rows - 1
    for v in range(-(-S // pair_rows)):
        pl.when(variant == v)(functools.partial(threshold_search, min((v + 1) * pair_rows, S)))

    th_c = jnp.broadcast_to(th_ref[...], (LANES, tq)).T
    cut_c = jnp.broadcast_to(cut_ref[...], (LANES, tq)).T
    th_t = tile_lanes(th_c)
    cut_t = tile_lanes(cut_c)

    m_ref[...] = jnp.full(m_ref.shape, MASK_NEG, F32)
    colf_in_chunk = col_in_chunk.astype(F32)

    def c1_body(c, _):
        c0 = pl.multiple_of(c * tk, tk)
        sc = score_ref[c]
        colf = colf_in_chunk + (c * tk).astype(F32)
        sel = jnp.logical_or(sc > th_t, jnp.logical_and(sc == th_t, colf <= cut_t))
        bias = jnp.where(sel, 0.0, MASK_NEG)
        s = _dot_t(qa_ref[...], kr_ref[pl.ds(c0, tk), :])
        s = (s.reshape(N_ATTN_HEADS, tq, tk) + bias[None]).reshape(rows_all, tk)
        s_ref[c] = s
        m_ref[...] = jnp.maximum(m_ref[...], lane_groups(s, jnp.maximum))
        return 0

    lax.fori_loop(0, n_chunks, c1_body, 0)
    m_ref[...] = jnp.broadcast_to(jnp.max(m_ref[...], axis=1, keepdims=True), m_ref.shape)

    acc_ref[...] = jnp.zeros(acc_ref.shape, F32)

    def c2_body(c, _):
        c0 = pl.multiple_of(c * tk, tk)
        p = jnp.exp(s_ref[c] - tile_lanes(m_ref[...]))
        acc_ref[...] += _dot(p.astype(BF16), vaug_ref[pl.ds(c0, tk), :])
        return 0

    lax.fori_loop(0, n_chunks, c2_body, 0)
    out = acc_ref[:, :LANES] / acc_ref[:, LANES:]
    for h in range(N_ATTN_HEADS):
        o_ref[0, :, h * LANES:(h + 1) * LANES] = out[h * tq:(h + 1) * tq, :].astype(o_ref.dtype)


def _attention(z3, tab_a, tab_i, topk):
    B, S, _ = z3.shape
    tq = min(128, S)
    tk = min(256, S)
    qblk = lambda w, col: pl.BlockSpec((1, tq, w), lambda b, q: (b, q, col // w))
    kblk = lambda w, col: pl.BlockSpec((1, S, w), lambda b, q: (b, 0, col // w))
    tab_q = pl.BlockSpec((1, tq, 2 * LANES), lambda b, q: (b, q, 0))
    tab_k = pl.BlockSpec((1, S, 2 * LANES), lambda b, q: (b, 0, 0))
    rows = N_ATTN_HEADS * tq
    const = lambda a: pl.BlockSpec(a.shape, lambda b, q: (0, 0))
    perm_a = _swap_matrix(ATTN_HEAD_DIM, ATTN_ROT_HALF)
    perm_i = _swap_matrix(IDX_HEAD_DIM, IDX_ROT_HALF)
    idx_w_scale = (N_IDX_HEADS ** -0.5) * (IDX_HEAD_DIM ** -0.5)
    assert math.frexp(idx_w_scale)[0] == 0.5, "the folded scale must be a power of two to stay exact in bf16"
    expand = jnp.where(jnp.arange(LANES)[:, None] == IDX_HEAD_DIM + jnp.arange(N_IDX_HEADS * LANES)[None, :] // LANES,
                       idx_w_scale, 0.0).astype(BF16)
    return pl.pallas_call(
        functools.partial(_attn_kernel, topk=topk, tq=tq, tk=tk),
        grid=(B, S // tq),
        in_specs=[qblk(ATTN_WIDTH, COL_AQ), qblk(IDX_WIDTH, COL_IQ), qblk(LANES, COL_IKW),
                  kblk(LANES, COL_AK), kblk(LANES, COL_AV), kblk(LANES, COL_IKW),
                  tab_q, tab_k, tab_q, tab_k, const(perm_a), const(perm_i), const(expand)],
        out_specs=pl.BlockSpec((1, tq, ATTN_WIDTH), lambda b, q: (b, q, 0)),
        out_shape=jax.ShapeDtypeStruct((B, S, ATTN_WIDTH), BF16),
        scratch_shapes=[pltpu.VMEM((S, LANES), BF16),
                        pltpu.VMEM((S, LANES), BF16),
                        pltpu.VMEM((S, LANES), BF16),
                        pltpu.VMEM((S, 2 * LANES), BF16),
                        pltpu.VMEM((rows, LANES), BF16),
                        pltpu.VMEM((tq, IDX_WIDTH), BF16),
                        pltpu.VMEM((N_IDX_HEADS, tq, LANES), F32),
                        pltpu.VMEM((S // tk, tq, tk), F32),
                        pltpu.VMEM((S, tq), F32),
                        pltpu.VMEM((1, tq), F32),
                        pltpu.VMEM((1, tq), F32),
                        pltpu.VMEM((S // tk, rows, tk), F32),
                        pltpu.VMEM((rows, LANES), F32),
                        pltpu.VMEM((rows, 2 * LANES), F32)],
        compiler_params=_cparams(2),
        name="dsa_attention",
    )(z3, z3, z3, z3, z3, z3, tab_a, tab_a, tab_i, tab_i, perm_a, perm_i, expand)


def _ret_kernel(rq_ref, rk_ref, rv_ref, rg_ref, tab_ref, decay_ref, zeta_ref, xi_ref, gch_ref, gn_ref,
                o_ref, state_ref, *, chunk):
    S = rq_ref.shape[1]
    C = chunk
    half = RET_KEY_DIM
    state_ref[...] = jnp.zeros(state_ref.shape, F32)
    lane = lax.broadcasted_iota(jnp.int32, (C, LANES), 1)
    first = lane < half

    def body(c, _):
        r0 = pl.multiple_of(c * C, C)
        sl = pl.ds(r0, C)
        cos, sin = tab_ref[0, sl, :LANES], tab_ref[0, sl, LANES:]
        cos_k, sin_k = cos * (RET_KEY_DIM ** -0.5), sin * (RET_KEY_DIM ** -0.5)
        first_half = lane % RET_KEY_DIM < RET_KEY_DIM // 2

        def rope(x, c, s):
            partner = jnp.where(first_half, pltpu.roll(x, LANES - RET_KEY_DIM // 2, 1),
                                pltpu.roll(x, RET_KEY_DIM // 2, 1))
            return x * c + partner * s

        for j in range(N_RET_HEADS // 2):
            q = rope(rq_ref[0, sl, j * LANES:(j + 1) * LANES].astype(F32), cos, sin)
            k = rope(rk_ref[0, sl, j * LANES:(j + 1) * LANES].astype(F32), cos_k, sin_k)
            kzt = (k * zeta_ref[j]).T.astype(BF16)
            state = state_ref[j]
            state_b = state.astype(BF16)
            new_rows = []
            for par in range(2):
                h = 2 * j + par
                keep = first if par == 0 else jnp.logical_not(first)
                qh = jnp.where(keep, q, 0.0).astype(BF16)
                kh = jnp.where(keep, k, 0.0).astype(BF16)
                v = rv_ref[0, sl, h * LANES:(h + 1) * LANES]
                inner = _dot_t(qh, kh) * decay_ref[h]
                o = _dot(inner.astype(BF16), v) + _dot(qh, state_b) * xi_ref[h]
                new_rows.append(_dot(kzt[par * half:(par + 1) * half, :], v))
                mu = jnp.mean(o, axis=-1, keepdims=True)
                d = o - mu
                var = jnp.mean(d * d, axis=-1, keepdims=True)
                y = d * lax.rsqrt(var + GN_EPS) * gn_ref[:, h * LANES:(h + 1) * LANES]
                g = rg_ref[0, sl, h * LANES:(h + 1) * LANES].astype(F32)
                o_ref[0, sl, h * LANES:(h + 1) * LANES] = (y * (g * jax.nn.sigmoid(g))).astype(o_ref.dtype)
            state_ref[j] = gch_ref[j] * state + jnp.concatenate(new_rows, axis=0)
        return 0

    lax.fori_loop(0, S // C, body, 0)


def _retention(z3, tab_r, consts, gn_w, chunk):
    B, S, _ = z3.shape
    decay, zeta, xi, gch = consts
    blk = lambda w, col: pl.BlockSpec((1, S, w), lambda b: (b, 0, col // w))
    const = lambda a: pl.BlockSpec(a.shape, lambda b: (0,) * a.ndim)
    return pl.pallas_call(
        functools.partial(_ret_kernel, chunk=chunk),
        grid=(B,),
        in_specs=[blk(RET_KEY_WIDTH, COL_RQ), blk(RET_KEY_WIDTH, COL_RK), blk(RET_WIDTH, COL_RV),
                  blk(RET_WIDTH, COL_RG), pl.BlockSpec((1, S, 2 * LANES), lambda b: (b, 0, 0)),
                  const(decay), const(zeta), const(xi), const(gch), const(gn_w)],
        out_specs=pl.BlockSpec((1, S, RET_WIDTH), lambda b: (b, 0, 0)),
        out_shape=jax.ShapeDtypeStruct((B, S, RET_WIDTH), BF16),
        scratch_shapes=[pltpu.VMEM((N_RET_HEADS // 2, LANES, RET_VAL_DIM), F32)],
        compiler_params=_cparams(1),
        name="retention",
    )(z3, z3, z3, z3, tab_r, decay, zeta, xi, gch, gn_w)


def _retention_consts(chunk):
    C = chunk
    log_g = jnp.log(1.0 - 2.0 ** (-5.0 - jnp.arange(N_RET_HEADS, dtype=F32)))
    i = jnp.arange(C, dtype=F32)
    diff = i[:, None] - i[None, :]
    decay = jnp.where(diff[None] >= 0, jnp.exp(jnp.maximum(diff, 0.0)[None] * log_g[:, None, None]), 0.0)
    zeta = jnp.exp((C - 1.0 - i)[None, :] * log_g[:, None])
    xi = jnp.exp((i + 1.0)[None, :] * log_g[:, None])
    g_chunk = jnp.exp(C * log_g)
    pair = lambda a: a.reshape(N_RET_HEADS // 2, 2, -1)
    zeta_p = jnp.repeat(pair(zeta).transpose(0, 2, 1), RET_KEY_DIM, axis=2)
    xi_b = jnp.broadcast_to(xi[:, :, None], (N_RET_HEADS, C, RET_VAL_DIM))
    gch = jnp.broadcast_to(jnp.repeat(pair(g_chunk), RET_KEY_DIM, axis=1).reshape(N_RET_HEADS // 2, LANES, 1),
                           (N_RET_HEADS // 2, LANES, RET_VAL_DIM))
    return decay.astype(F32), zeta_p.astype(F32), xi_b.astype(F32), gch.astype(F32)


def _out_proj_kernel(attn_ref, ret_ref, h_ref, wa_ref, wr_ref, g1_ref, g2_ref, h1_ref, m_ref):
    mix = _dot(attn_ref[...], wa_ref[...]) + _dot(ret_ref[...], wr_ref[...])
    h1 = h_ref[...] + _rms(mix, g1_ref[...])
    h1_ref[...] = h1
    m_ref[...] = _rms(h1, g2_ref[...]).astype(m_ref.dtype)


def _out_proj(attn, ret, h, wa, wr, g1, g2):
    T, D = h.shape
    tm = min(256, T)
    row = lambda w: pl.BlockSpec((tm, w), lambda i: (i, 0))
    const = lambda a: pl.BlockSpec(a.shape, lambda i: (0, 0))
    return pl.pallas_call(
        _out_proj_kernel,
        grid=(T // tm,),
        in_specs=[row(attn.shape[1]), row(ret.shape[1]), row(D), const(wa), const(wr), const(g1), const(g2)],
        out_specs=[row(D), row(D)],
        out_shape=[jax.ShapeDtypeStruct((T, D), F32), jax.ShapeDtypeStruct((T, D), BF16)],
        compiler_params=_cparams(1),
        name="out_proj",
    )(attn, ret, h, wa, wr, g1, g2)


def _ffn_kernel(m_ref, w1_ref, w2_ref, h_ref, g_ref, o_ref, acc_ref):
    f = pl.program_id(1)

    @pl.when(f == 0)
    def _():
        acc_ref[...] = jnp.zeros(acc_ref.shape, F32)

    u = jnp.maximum(_dot(m_ref[...], w1_ref[...]), 0.0)
    acc_ref[...] += _dot((u * u).astype(BF16), w2_ref[...])

    @pl.when(f == pl.num_programs(1) - 1)
    def _():
        o_ref[...] = h_ref[...] + _rms(acc_ref[...], g_ref[...])


def _ffn(m, w1, w2, h, g):
    T, D = h.shape
    F = w1.shape[1]
    tm = min(512, T)
    tf = min(512, F)
    return pl.pallas_call(
        _ffn_kernel,
        grid=(T // tm, F // tf),
        in_specs=[pl.BlockSpec((tm, D), lambda i, f: (i, 0)),
                  pl.BlockSpec((D, tf), lambda i, f: (0, f)),
                  pl.BlockSpec((tf, D), lambda i, f: (f, 0)),
                  pl.BlockSpec((tm, D), lambda i, f: (i, 0)),
                  pl.BlockSpec((1, D), lambda i, f: (0, 0))],
        out_specs=pl.BlockSpec((tm, D), lambda i, f: (i, 0)),
        out_shape=jax.ShapeDtypeStruct((T, D), F32),
        scratch_shapes=[pltpu.VMEM((tm, D), F32)],
        compiler_params=_cparams(2),
        name="ffn",
    )(m, w1, w2, h, g)


def _ple_kernel(h_ref, p_ref, wg_ref, wp_ref, g_ref, o_ref):
    h = h_ref[...]
    gate = jax.nn.sigmoid(_dot(h.astype(BF16), wg_ref[...]))
    e = _dot(p_ref[...].astype(BF16), wp_ref[...])
    o_ref[...] = h + _rms(gate * e, g_ref[...])


def _ple(h, p, wg, wp, g):
    T, D = h.shape
    tm = min(256, T)
    row = lambda w: pl.BlockSpec((tm, w), lambda i: (i, 0))
    const = lambda a: pl.BlockSpec(a.shape, lambda i: (0, 0))
    return pl.pallas_call(
        _ple_kernel,
        grid=(T // tm,),
        in_specs=[row(D), row(p.shape[1]), const(wg), const(wp), const(g)],
        out_specs=row(D),
        out_shape=jax.ShapeDtypeStruct((T, D), F32),
        compiler_params=_cparams(1),
        name="ple",
    )(h, p, wg, wp, g)


def _rope_table(positions, half, theta, group):
    lane = jnp.arange(2 * LANES) % LANES
    d = lane % group
    inv = jnp.where(d < 2 * half, theta ** (-(d % half).astype(F32) / half), 0.0)
    sign = jnp.where(d < half, -1.0, 1.0)
    ang = positions.astype(F32)[..., None] * inv
    return jnp.where(jnp.arange(2 * LANES) < LANES, jnp.cos(ang), jnp.sin(ang) * sign)


def _reorder_w_in(w):
    w = w.astype(BF16)
    a0 = ATTN_WIDTH
    a1 = a0 + 2 * ATTN_HEAD_DIM
    a2 = a1 + IDX_WIDTH
    a3 = a2 + IDX_HEAD_DIM + N_IDX_HEADS
    a4 = a3 + 2 * RET_KEY_WIDTH
    pad = jnp.zeros((w.shape[0], Z_WIDTH - Z_USED + LANES - IDX_HEAD_DIM - N_IDX_HEADS), BF16)
    return jnp.concatenate([w[:, :a0], w[:, a1:a2], w[:, a4:], w[:, a3:a4], w[:, a0:a1], w[:, a2:a3], pad], axis=1)


def kernel(x, p, positions, w_in, w_out, w_ff1, w_ff2, w_ple, w_ple_gate, pre_mix_norm, post_mix_norm,
           pre_ff_norm, post_ff_norm, ple_norm, ret_gn):
    B, S, D = x.shape
    depth = w_in.shape[0]
    T = B * S
    topk = min(TOPK_MAX, S // 4)
    ret_chunk = min(128, S)
    tab_a = _rope_table(positions, ATTN_ROT_HALF, ROPE_THETA, ATTN_HEAD_DIM)
    tab_i = _rope_table(positions, IDX_ROT_HALF, ROPE_THETA, IDX_HEAD_DIM)
    tab_r = _rope_table(positions, RET_KEY_DIM // 2, RET_THETA, RET_KEY_DIM)
    ret_consts = _retention_consts(ret_chunk)
    vec = lambda a: a.reshape(1, -1).astype(F32)

    h = x.reshape(T, D)
    for i in range(depth):
        z = _norm_proj(h, vec(pre_mix_norm[i]), _reorder_w_in(w_in[i]))
        z3 = z.reshape(B, S, Z_WIDTH)
        attn = _attention(z3, tab_a, tab_i, topk)
        ret = _retention(z3, tab_r, ret_consts, vec(ret_gn[i]), ret_chunk)
        wo = w_out[i].astype(BF16)
        h, m = _out_proj(attn.reshape(T, ATTN_WIDTH), ret.reshape(T, RET_WIDTH), h,
                         wo[:ATTN_WIDTH], wo[ATTN_WIDTH:], vec(post_mix_norm[i]), vec(pre_ff_norm[i]))
        h = _ffn(m, w_ff1[i].astype(BF16), w_ff2[i].astype(BF16), h, vec(post_ff_norm[i]))
        h = _ple(h, p[i].reshape(T, -1), w_ple_gate[i].astype(BF16), w_ple[i].astype(BF16), vec(ple_norm[i]))
    return h.reshape(B, S, D)
```

```python
import functools
import math

import jax
import jax.numpy as jnp
import numpy as np
from jax import lax
from jax.experimental import pallas as pl
from jax.experimental.pallas import tpu as pltpu

N_ATTN_HEADS = 8
ATTN_HEAD_DIM = 128
ATTN_ROT_HALF = 16
ROPE_THETA = 500000.0
TOPK_MAX = 256
N_IDX_HEADS = 16
IDX_HEAD_DIM = 64
IDX_ROT_HALF = 8
N_RET_HEADS = 8
RET_KEY_DIM = 64
RET_VAL_DIM = 128
RET_THETA = 10000.0
ATTN_WIDTH = N_ATTN_HEADS * ATTN_HEAD_DIM
IDX_WIDTH = N_IDX_HEADS * IDX_HEAD_DIM
RET_KEY_WIDTH = N_RET_HEADS * RET_KEY_DIM
RET_WIDTH = N_RET_HEADS * RET_VAL_DIM
RMS_EPS = 1e-6
GN_EPS = 1e-5

LANES = 128
COL_AQ = 0
COL_IQ = COL_AQ + ATTN_WIDTH
COL_RV = COL_IQ + IDX_WIDTH
COL_RG = COL_RV + RET_WIDTH
COL_RQ = COL_RG + RET_WIDTH
COL_RK = COL_RQ + RET_KEY_WIDTH
COL_AK = COL_RK + RET_KEY_WIDTH
COL_AV = COL_AK + ATTN_HEAD_DIM
COL_IKW = COL_AV + ATTN_HEAD_DIM
Z_USED = COL_IKW + LANES
Z_TILE_N = 512
Z_WIDTH = -(-Z_USED // Z_TILE_N) * Z_TILE_N

MASK_NEG = -1e30
THETA_ALL = -3e38
CUT_ALL = 1e9
VMEM_LIMIT = 56 * 1024 * 1024

BF16 = jnp.bfloat16
F32 = jnp.float32


def _cparams(n_grid):
    return pltpu.CompilerParams(dimension_semantics=("arbitrary",) * n_grid,
                                vmem_limit_bytes=VMEM_LIMIT)


def _rms(x, gain):
    ms = jnp.mean(x * x, axis=-1, keepdims=True)
    return x * lax.rsqrt(ms + RMS_EPS) * gain


def _rope(xb, cos, sin, swap):
    return xb.astype(F32) * cos + jnp.dot(xb, swap, preferred_element_type=F32) * sin


def _swap_matrix(group, half):
    l = jnp.arange(LANES)
    d = l % group
    partner = jnp.where(d < half, l + half, jnp.where(d < 2 * half, l - half, -1))
    return (jnp.arange(LANES)[:, None] == partner[None, :]).astype(BF16)


def _dot_t(a, b):
    return lax.dot_general(a, b, (((1,), (1,)), ((), ())), preferred_element_type=F32)


def _dot(a, b):
    return jnp.dot(a, b, preferred_element_type=F32)


def _norm_proj_kernel(h_ref, g_ref, w_ref, z_ref, a_ref, *, row_chunk):
    @pl.when(pl.program_id(1) == 0)
    def _():
        def body(r, _):
            r0 = pl.multiple_of(r * row_chunk, row_chunk)
            a_ref[pl.ds(r0, row_chunk), :] = _rms(h_ref[pl.ds(r0, row_chunk), :], g_ref[...]).astype(BF16)
            return 0
        lax.fori_loop(0, h_ref.shape[0] // row_chunk, body, 0)

    z_ref[...] = _dot(a_ref[...], w_ref[...]).astype(z_ref.dtype)


def _norm_proj(h, gain, w):
    T, D = h.shape
    N = w.shape[1]
    tm = min(1024, T)
    tn = Z_TILE_N
    return pl.pallas_call(
        functools.partial(_norm_proj_kernel, row_chunk=min(128, tm)),
        grid=(T // tm, N // tn),
        in_specs=[pl.BlockSpec((tm, D), lambda i, j: (i, 0)),
                  pl.BlockSpec((1, D), lambda i, j: (0, 0)),
                  pl.BlockSpec((D, tn), lambda i, j: (0, j))],
        out_specs=pl.BlockSpec((tm, tn), lambda i, j: (i, j)),
        out_shape=jax.ShapeDtypeStruct((T, N), BF16),
        scratch_shapes=[pltpu.VMEM((tm, D), BF16)],
        compiler_params=_cparams(2),
        name="norm_proj",
    )(h, gain, w)


def _attn_kernel(aq_ref, iq_ref, iwq_ref, ak_ref, av_ref, ikw_ref,
                 taq_ref, tak_ref, tiq_ref, tik_ref, pa_ref, pi_ref, ex_ref, o_ref,
                 kr_ref, kd0_ref, kd1_ref, vaug_ref, qa_ref, qi_ref, wb_ref, score_ref, score_t_ref,
                 th_ref, cut_ref, s_ref, m_ref, acc_ref, *, topk, tq, tk):
    S = ak_ref.shape[1]
    qi = pl.program_id(1)
    n_chunks = (qi * tq + tq + tk - 1) // tk
    reps = tk // LANES
    rows_all = N_ATTN_HEADS * tq

    @pl.when(qi == 0)
    def _():
        rows = min(256, S)

        def body(r, _):
            r0 = pl.multiple_of(r * rows, rows)
            sl = pl.ds(r0, rows)
            kr_ref[sl, :] = _rope(ak_ref[0, sl, :], tak_ref[0, sl, :LANES], tak_ref[0, sl, LANES:],
                                  pa_ref[...]).astype(BF16)
            lane_r = lax.broadcasted_iota(jnp.int32, (rows, LANES), 1)
            ik = _rope(ikw_ref[0, sl, :], tik_ref[0, sl, :LANES], tik_ref[0, sl, LANES:], pi_ref[...])
            ik = jnp.where(lane_r < IDX_HEAD_DIM, ik, 0.0)
            kd0_ref[sl, :] = ik.astype(BF16)
            kd1_ref[sl, :] = pltpu.roll(ik, IDX_HEAD_DIM, 1).astype(BF16)
            vaug_ref[sl, :] = jnp.concatenate([av_ref[0, sl, :], jnp.ones((rows, LANES), BF16)], axis=1)
            score_t_ref[sl, :] = jnp.full((rows, tq), -jnp.inf, F32)
            return 0
        lax.fori_loop(0, S // rows, body, 0)

    ca = taq_ref[0, :, :LANES] * (ATTN_HEAD_DIM ** -0.5)
    sa = taq_ref[0, :, LANES:] * (ATTN_HEAD_DIM ** -0.5)
    for h in range(N_ATTN_HEADS):
        x = aq_ref[0, :, h * LANES:(h + 1) * LANES]
        qa_ref[h * tq:(h + 1) * tq, :] = _rope(x, ca, sa, pa_ref[...]).astype(BF16)
    ci, si = tiq_ref[0, :, :LANES], tiq_ref[0, :, LANES:]
    for j in range(IDX_WIDTH // LANES):
        x = iq_ref[0, :, j * LANES:(j + 1) * LANES]
        qi_ref[:, j * LANES:(j + 1) * LANES] = _rope(x, ci, si, pi_ref[...]).astype(BF16)
    iw_b = jnp.dot(iwq_ref[0], ex_ref[...], preferred_element_type=F32)
    for h in range(N_IDX_HEADS):
        wb_ref[h] = iw_b[:, h * LANES:(h + 1) * LANES]

    row = qi * tq + lax.broadcasted_iota(jnp.int32, (tq, tk), 0)
    col_in_chunk = lax.broadcasted_iota(jnp.int32, (tq, tk), 1)

    def lane_groups(x, op):
        r = x[:, :LANES]
        for g in range(1, reps):
            r = op(r, x[:, g * LANES:(g + 1) * LANES])
        return r

    def tile_lanes(x):
        return jnp.concatenate([x] * reps, axis=1) if reps > 1 else x

    key_t = lax.broadcasted_iota(jnp.int32, (tk, tq), 0)
    qry_t = qi * tq + lax.broadcasted_iota(jnp.int32, (tk, tq), 1)

    def fold8(x, op):
        x3 = x.reshape(tk // 8, 8, tq)
        return jnp.max(x3, axis=0) if op == "max" else (jnp.min(x3, axis=0) if op == "min" else jnp.sum(x3, axis=0))

    def a_body(c, carry):
        rmax, rmin = carry
        c0 = pl.multiple_of(c * tk, tk)
        k0 = kd0_ref[pl.ds(c0, tk), :]
        k1 = kd1_ref[pl.ds(c0, tk), :]
        acc = jnp.zeros((tq, tk), F32)
        for j in range(IDX_WIDTH // LANES):
            qp = qi_ref[:, j * LANES:(j + 1) * LANES]
            for par, kd in ((0, k0), (1, k1)):
                logits = _dot_t(qp, kd)
                acc = acc + tile_lanes(wb_ref[2 * j + par]) * jnp.maximum(logits, 0.0)
        causal = (c0 + col_in_chunk) <= row
        score_ref[c] = jnp.where(causal, acc, -jnp.inf)
        acc_t = acc.T
        causal_t = (c0 + key_t) <= qry_t
        sc_t = jnp.where(causal_t, acc_t, -jnp.inf)
        score_t_ref[pl.ds(c0, tk), :] = sc_t
        rmax = jnp.maximum(rmax, fold8(sc_t, "max"))
        rmin = jnp.minimum(rmin, fold8(jnp.where(causal_t, acc_t, jnp.inf), "min"))
        return rmax, rmin

    rmax, rmin = lax.fori_loop(0, n_chunks, a_body,
                               (jnp.full((8, tq), -jnp.inf, F32), jnp.full((8, tq), jnp.inf, F32)))

    vec = (1, tq)
    kf = float(topk)
    n_keys = qi * tq + lax.broadcasted_iota(jnp.int32, vec, 1) + 1

    lo0 = jnp.min(rmin, axis=0, keepdims=True)
    hi0 = jnp.max(rmax, axis=0, keepdims=True)
    key_f = key_t.astype(F32)

    def threshold_search(n_rows):
        def count_t(pred):
            accs = [jnp.zeros((8, tq), F32) for _ in range(4)]
            i = 0
            for r0 in range(0, n_rows, tk):
                ind = jnp.where(pred(r0, score_t_ref[r0:r0 + tk, :]), 1.0, 0.0)
                for g in range(tk // 8):
                    accs[i % 4] = accs[i % 4] + ind[g * 8:(g + 1) * 8, :]
                    i += 1
            return jnp.sum((accs[0] + accs[1]) + (accs[2] + accs[3]), axis=0, keepdims=True)

        def b_cond(carry):
            return carry[1] > 0.0

        def b_body(carry):
            state = (carry[0],) + tuple(carry[2:])
            flag = jnp.max(1.0 - carry[5])
            for _ in range(steps_per_check):
                state = b_step(*state)
            return (state[0], flag) + tuple(state[1:])

        def b_step(it, lo, hi, chi, done, th, tie):
            mid = jnp.where(it == 0, hi, 0.5 * lo + 0.5 * hi)
            cnt = count_t(lambda r0, sc: sc >= mid)
            active = done == 0.0
            adjacent = jnp.logical_and(it > 0, jnp.logical_or(mid <= lo, mid >= hi))
            tie_now = jnp.logical_and(active, adjacent)
            live = jnp.logical_and(active, jnp.logical_not(adjacent))
            hit = jnp.logical_and(live, cnt == kf)
            above = jnp.logical_and(live, cnt > kf)
            below = jnp.logical_and(live, cnt < kf)
            th = jnp.where(tie_now, lo, jnp.where(hit, mid, th))
            tie = jnp.where(tie_now, 1.0, tie)
            lo = jnp.where(above, mid, lo)
            hi = jnp.where(below, mid, hi)
            chi = jnp.where(below, cnt, chi)
            done = jnp.where(jnp.logical_or(tie_now, hit), 1.0, done)
            return it + 1, lo, hi, chi, done, th, tie

        steps_per_check = 2 if n_rows <= 4 * tk else 1

        init = (jnp.int32(0), jnp.float32(1.0), lo0, hi0,
                jnp.zeros(vec, F32), jnp.where(n_keys <= topk, 1.0, 0.0),
                jnp.full(vec, THETA_ALL, F32), jnp.zeros(vec, F32))
        _, _, _, _, chi, _, th, tie = lax.while_loop(b_cond, b_body, init)

        def tie_cut():
            need = kf - chi

            def body(_, carry):
                lo_i, hi_i = carry
                mid = jnp.floor(0.5 * (lo_i + hi_i))
                cnt = count_t(lambda r0, sc: jnp.logical_and(sc == th, key_f + float(r0) <= mid))
                ge = cnt >= need
                return jnp.where(ge, lo_i, mid), jnp.where(ge, mid, hi_i)

            n_iter = int(math.ceil(math.log2(S))) + 1
            _, hi_i = lax.fori_loop(0, n_iter, body, (jnp.full(vec, -1.0, F32), jnp.full(vec, float(S - 1), F32)))
            return jnp.where(tie > 0.0, hi_i, CUT_ALL)

        th_ref[...] = th
        cut_ref[...] = lax.cond(jnp.max(tie) > 0.0, tie_cut, lambda: jnp.full(vec, CUT_ALL, F32))

    pair_rows = 2 * tk
    variant = (n_chunks * tk + pair_rows - 1) // pair_rows - 1
    for v in range(-(-S // pair_rows)):
        pl.when(variant == v)(functools.partial(threshold_search, min((v + 1) * pair_rows, S)))

    th_c = jnp.broadcast_to(th_ref[...], (LANES, tq)).T
    cut_c = jnp.broadcast_to(cut_ref[...], (LANES, tq)).T
    th_t = tile_lanes(th_c)
    cut_t = tile_lanes(cut_c)

    m_ref[...] = jnp.full(m_ref.shape, MASK_NEG, F32)
    colf_in_chunk = col_in_chunk.astype(F32)

    def c1_body(c, _):
        c0 = pl.multiple_of(c * tk, tk)
        sc = score_ref[c]
        colf = colf_in_chunk + (c * tk).astype(F32)
        sel = jnp.logical_or(sc > th_t, jnp.logical_and(sc == th_t, colf <= cut_t))
        bias = jnp.where(sel, 0.0, MASK_NEG)
        s = _dot_t(qa_ref[...], kr_ref[pl.ds(c0, tk), :])
        s = (s.reshape(N_ATTN_HEADS, tq, tk) + bias[None]).reshape(rows_all, tk)
        s_ref[c] = s
        m_ref[...] = jnp.maximum(m_ref[...], lane_groups(s, jnp.maximum))
        return 0

    lax.fori_loop(0, n_chunks, c1_body, 0)
    m_ref[...] = jnp.broadcast_to(jnp.max(m_ref[...], axis=1, keepdims=True), m_ref.shape)

    acc_ref[...] = jnp.zeros(acc_ref.shape, F32)

    def c2_body(c, _):
        c0 = pl.multiple_of(c * tk, tk)
        p = jnp.exp(s_ref[c] - tile_lanes(m_ref[...]))
        acc_ref[...] += _dot(p.astype(BF16), vaug_ref[pl.ds(c0, tk), :])
        return 0

    lax.fori_loop(0, n_chunks, c2_body, 0)
    out = acc_ref[:, :LANES] / acc_ref[:, LANES:]
    for h in range(N_ATTN_HEADS):
        o_ref[0, :, h * LANES:(h + 1) * LANES] = out[h * tq:(h + 1) * tq, :].astype(o_ref.dtype)


def _attention(z3, tab_a, tab_i, topk):
    B, S, _ = z3.shape
    tq = min(128, S)
    tk = min(256, S)
    qblk = lambda w, col: pl.BlockSpec((1, tq, w), lambda b, q: (b, q, col // w))
    kblk = lambda w, col: pl.BlockSpec((1, S, w), lambda b, q: (b, 0, col // w))
    tab_q = pl.BlockSpec((1, tq, 2 * LANES), lambda b, q: (b, q, 0))
    tab_k = pl.BlockSpec((1, S, 2 * LANES), lambda b, q: (b, 0, 0))
    rows = N_ATTN_HEADS * tq
    const = lambda a: pl.BlockSpec(a.shape, lambda b, q: (0, 0))
    perm_a = _swap_matrix(ATTN_HEAD_DIM, ATTN_ROT_HALF)
    perm_i = _swap_matrix(IDX_HEAD_DIM, IDX_ROT_HALF)
    idx_w_scale = (N_IDX_HEADS ** -0.5) * (IDX_HEAD_DIM ** -0.5)
    assert math.frexp(idx_w_scale)[0] == 0.5, "the folded scale must be a power of two to stay exact in bf16"
    expand = jnp.where(jnp.arange(LANES)[:, None] == IDX_HEAD_DIM + jnp.arange(N_IDX_HEADS * LANES)[None, :] // LANES,
                       idx_w_scale, 0.0).astype(BF16)
    return pl.pallas_call(
        functools.partial(_attn_kernel, topk=topk, tq=tq, tk=tk),
        grid=(B, S // tq),
        in_specs=[qblk(ATTN_WIDTH, COL_AQ), qblk(IDX_WIDTH, COL_IQ), qblk(LANES, COL_IKW),
                  kblk(LANES, COL_AK), kblk(LANES, COL_AV), kblk(LANES, COL_IKW),
                  tab_q, tab_k, tab_q, tab_k, const(perm_a), const(perm_i), const(expand)],
        out_specs=pl.BlockSpec((1, tq, ATTN_WIDTH), lambda b, q: (b, q, 0)),
        out_shape=jax.ShapeDtypeStruct((B, S, ATTN_WIDTH), BF16),
        scratch_shapes=[pltpu.VMEM((S, LANES), BF16),
                        pltpu.VMEM((S, LANES), BF16),
                        pltpu.VMEM((S, LANES), BF16),
                        pltpu.VMEM((S, 2 * LANES), BF16),
                        pltpu.VMEM((rows, LANES), BF16),
                        pltpu.VMEM((tq, IDX_WIDTH), BF16),
                        pltpu.VMEM((N_IDX_HEADS, tq, LANES), F32),
                        pltpu.VMEM((S // tk, tq, tk), F32),
                        pltpu.VMEM((S, tq), F32),
                        pltpu.VMEM((1, tq), F32),
                        pltpu.VMEM((1, tq), F32),
                        pltpu.VMEM((S // tk, rows, tk), F32),
                        pltpu.VMEM((rows, LANES), F32),
                        pltpu.VMEM((rows, 2 * LANES), F32)],
        compiler_params=_cparams(2),
        name="dsa_attention",
    )(z3, z3, z3, z3, z3, z3, tab_a, tab_a, tab_i, tab_i, perm_a, perm_i, expand)


def _ret_kernel(rq_ref, rk_ref, rv_ref, rg_ref, tab_ref, decay_ref, zeta_ref, xi_ref, gch_ref, gn_ref,
                o_ref, state_ref, *, chunk):
    S = rq_ref.shape[1]
    C = chunk
    half = RET_KEY_DIM
    state_ref[...] = jnp.zeros(state_ref.shape, F32)
    lane = lax.broadcasted_iota(jnp.int32, (C, LANES), 1)
    first = lane < half

    def body(c, _):
        r0 = pl.multiple_of(c * C, C)
        sl = pl.ds(r0, C)
        cos, sin = tab_ref[0, sl, :LANES], tab_ref[0, sl, LANES:]
        cos_k, sin_k = cos * (RET_KEY_DIM ** -0.5), sin * (RET_KEY_DIM ** -0.5)
        first_half = lane % RET_KEY_DIM < RET_KEY_DIM // 2

        def rope(x, c, s):
            partner = jnp.where(first_half, pltpu.roll(x, LANES - RET_KEY_DIM // 2, 1),
                                pltpu.roll(x, RET_KEY_DIM // 2, 1))
            return x * c + partner * s

        for j in range(N_RET_HEADS // 2):
            q = rope(rq_ref[0, sl, j * LANES:(j + 1) * LANES].astype(F32), cos, sin)
            k = rope(rk_ref[0, sl, j * LANES:(j + 1) * LANES].astype(F32), cos_k, sin_k)
            kzt = (k * zeta_ref[j]).T.astype(BF16)
            state = state_ref[j]
            state_b = state.astype(BF16)
            new_rows = []
            for par in range(2):
                h = 2 * j + par
                keep = first if par == 0 else jnp.logical_not(first)
                qh = jnp.where(keep, q, 0.0).astype(BF16)
                kh = jnp.where(keep, k, 0.0).astype(BF16)
                v = rv_ref[0, sl, h * LANES:(h + 1) * LANES]
                inner = _dot_t(qh, kh) * decay_ref[h]
                o = _dot(inner.astype(BF16), v) + _dot(qh, state_b) * xi_ref[h]
                new_rows.append(_dot(kzt[par * half:(par + 1) * half, :], v))
                mu = jnp.mean(o, axis=-1, keepdims=True)
                d = o - mu
                var = jnp.mean(d * d, axis=-1, keepdims=True)
                y = d * lax.rsqrt(var + GN_EPS) * gn_ref[:, h * LANES:(h + 1) * LANES]
                g = rg_ref[0, sl, h * LANES:(h + 1) * LANES].astype(F32)
                o_ref[0, sl, h * LANES:(h + 1) * LANES] = (y * (g * jax.nn.sigmoid(g))).astype(o_ref.dtype)
            state_ref[j] = gch_ref[j] * state + jnp.concatenate(new_rows, axis=0)
        return 0

    lax.fori_loop(0, S // C, body, 0)


def _retention(z3, tab_r, consts, gn_w, chunk):
    B, S, _ = z3.shape
    decay, zeta, xi, gch = consts
    blk = lambda w, col: pl.BlockSpec((1, S, w), lambda b: (b, 0, col // w))
    const = lambda a: pl.BlockSpec(a.shape, lambda b: (0,) * a.ndim)
    return pl.pallas_call(
        functools.partial(_ret_kernel, chunk=chunk),
        grid=(B,),
        in_specs=[blk(RET_KEY_WIDTH, COL_RQ), blk(RET_KEY_WIDTH, COL_RK), blk(RET_WIDTH, COL_RV),
                  blk(RET_WIDTH, COL_RG), pl.BlockSpec((1, S, 2 * LANES), lambda b: (b, 0, 0)),
                  const(decay), const(zeta), const(xi), const(gch), const(gn_w)],
        out_specs=pl.BlockSpec((1, S, RET_WIDTH), lambda b: (b, 0, 0)),
        out_shape=jax.ShapeDtypeStruct((B, S, RET_WIDTH), BF16),
        scratch_shapes=[pltpu.VMEM((N_RET_HEADS // 2, LANES, RET_VAL_DIM), F32)],
        compiler_params=_cparams(1),
        name="retention",
    )(z3, z3, z3, z3, tab_r, decay, zeta, xi, gch, gn_w)


def _retention_consts(chunk):
    C = chunk
    log_g = jnp.log(1.0 - 2.0 ** (-5.0 - jnp.arange(N_RET_HEADS, dtype=F32)))
    i = jnp.arange(C, dtype=F32)
    diff = i[:, None] - i[None, :]
    decay = jnp.where(diff[None] >= 0, jnp.exp(jnp.maximum(diff, 0.0)[None] * log_g[:, None, None]), 0.0)
    zeta = jnp.exp((C - 1.0 - i)[None, :] * log_g[:, None])
    xi = jnp.exp((i + 1.0)[None, :] * log_g[:, None])
    g_chunk = jnp.exp(C * log_g)
    pair = lambda a: a.reshape(N_RET_HEADS // 2, 2, -1)
    zeta_p = jnp.repeat(pair(zeta).transpose(0, 2, 1), RET_KEY_DIM, axis=2)
    xi_b = jnp.broadcast_to(xi[:, :, None], (N_RET_HEADS, C, RET_VAL_DIM))
    gch = jnp.broadcast_to(jnp.repeat(pair(g_chunk), RET_KEY_DIM, axis=1).reshape(N_RET_HEADS // 2, LANES, 1),
                           (N_RET_HEADS // 2, LANES, RET_VAL_DIM))
    return decay.astype(F32), zeta_p.astype(F32), xi_b.astype(F32), gch.astype(F32)


def _out_proj_kernel(attn_ref, ret_ref, h_ref, wa_ref, wr_ref, g1_ref, g2_ref, h1_ref, m_ref, *, sub):
    for r0 in range(0, h_ref.shape[0], sub):
        sl = slice(r0, r0 + sub)
        mix = _dot(attn_ref[sl, :], wa_ref[...]) + _dot(ret_ref[sl, :], wr_ref[...])
        h1 = h_ref[sl, :] + _rms(mix, g1_ref[...])
        h1_ref[sl, :] = h1
        m_ref[sl, :] = _rms(h1, g2_ref[...]).astype(m_ref.dtype)


def _out_proj(attn, ret, h, wa, wr, g1, g2):
    T, D = h.shape
    tm = min(512, T)
    row = lambda w: pl.BlockSpec((tm, w), lambda i: (i, 0))
    const = lambda a: pl.BlockSpec(a.shape, lambda i: (0, 0))
    return pl.pallas_call(
        functools.partial(_out_proj_kernel, sub=min(256, tm)),
        grid=(T // tm,),
        in_specs=[row(attn.shape[1]), row(ret.shape[1]), row(D), const(wa), const(wr), const(g1), const(g2)],
        out_specs=[row(D), row(D)],
        out_shape=[jax.ShapeDtypeStruct((T, D), F32), jax.ShapeDtypeStruct((T, D), BF16)],
        compiler_params=_cparams(1),
        name="out_proj",
    )(attn, ret, h, wa, wr, g1, g2)


def _ffn_kernel(m_ref, w1_ref, w2_ref, h_ref, g_ref, o_ref, acc_ref):
    f = pl.program_id(1)

    @pl.when(f == 0)
    def _():
        acc_ref[...] = jnp.zeros(acc_ref.shape, F32)

    u = jnp.maximum(_dot(m_ref[...], w1_ref[...]), 0.0)
    acc_ref[...] += _dot((u * u).astype(BF16), w2_ref[...])

    @pl.when(f == pl.num_programs(1) - 1)
    def _():
        o_ref[...] = h_ref[...] + _rms(acc_ref[...], g_ref[...])


def _ffn(m, w1, w2, h, g):
    T, D = h.shape
    F = w1.shape[1]
    tm = min(512, T)
    tf = min(512, F)
    return pl.pallas_call(
        _ffn_kernel,
        grid=(T // tm, F // tf),
        in_specs=[pl.BlockSpec((tm, D), lambda i, f: (i, 0)),
                  pl.BlockSpec((D, tf), lambda i, f: (0, f)),
                  pl.BlockSpec((tf, D), lambda i, f: (f, 0)),
                  pl.BlockSpec((tm, D), lambda i, f: (i, 0)),
                  pl.BlockSpec((1, D), lambda i, f: (0, 0))],
        out_specs=pl.BlockSpec((tm, D), lambda i, f: (i, 0)),
        out_shape=jax.ShapeDtypeStruct((T, D), F32),
        scratch_shapes=[pltpu.VMEM((tm, D), F32)],
        compiler_params=_cparams(2),
        name="ffn",
    )(m, w1, w2, h, g)


def _ple_kernel(h_ref, p_ref, wg_ref, wp_ref, g_ref, o_ref, *, sub):
    for r0 in range(0, h_ref.shape[0], sub):
        sl = slice(r0, r0 + sub)
        h = h_ref[sl, :]
        gate = jax.nn.sigmoid(_dot(h.astype(BF16), wg_ref[...]))
        e = _dot(p_ref[sl, :].astype(BF16), wp_ref[...])
        o_ref[sl, :] = h + _rms(gate * e, g_ref[...])


def _ple(h, p, wg, wp, g):
    T, D = h.shape
    tm = min(512, T)
    row = lambda w: pl.BlockSpec((tm, w), lambda i: (i, 0))
    const = lambda a: pl.BlockSpec(a.shape, lambda i: (0, 0))
    return pl.pallas_call(
        functools.partial(_ple_kernel, sub=min(256, tm)),
        grid=(T // tm,),
        in_specs=[row(D), row(p.shape[1]), const(wg), const(wp), const(g)],
        out_specs=row(D),
        out_shape=jax.ShapeDtypeStruct((T, D), F32),
        compiler_params=_cparams(1),
        name="ple",
    )(h, p, wg, wp, g)


def _rope_table(positions, half, theta, group):
    inv = theta ** (-jnp.arange(half, dtype=F32) / half)
    ang = positions.astype(F32)[..., None] * inv
    compact = jnp.concatenate([jnp.cos(ang), jnp.sin(ang), jnp.ones_like(ang[..., :1])], axis=-1)
    spread = np.zeros((2 * half + 1, 2 * LANES), np.float32)
    for lane in range(LANES):
        d = lane % group
        if d < 2 * half:
            spread[d % half, lane] = 1.0
            spread[half + d % half, LANES + lane] = -1.0 if d < half else 1.0
        else:
            spread[2 * half, lane] = 1.0
    return jnp.einsum("bsk,kn->bsn", compact, jnp.asarray(spread), precision=lax.Precision.HIGHEST)


def _reorder_w_in(w):
    w = w.astype(BF16)
    a0 = ATTN_WIDTH
    a1 = a0 + 2 * ATTN_HEAD_DIM
    a2 = a1 + IDX_WIDTH
    a3 = a2 + IDX_HEAD_DIM + N_IDX_HEADS
    a4 = a3 + 2 * RET_KEY_WIDTH
    pad = jnp.zeros((w.shape[0], Z_WIDTH - Z_USED + LANES - IDX_HEAD_DIM - N_IDX_HEADS), BF16)
    return jnp.concatenate([w[:, :a0], w[:, a1:a2], w[:, a4:], w[:, a3:a4], w[:, a0:a1], w[:, a2:a3], pad], axis=1)


def kernel(x, p, positions, w_in, w_out, w_ff1, w_ff2, w_ple, w_ple_gate, pre_mix_norm, post_mix_norm,
           pre_ff_norm, post_ff_norm, ple_norm, ret_gn):
    B, S, D = x.shape
    depth = w_in.shape[0]
    T = B * S
    topk = min(TOPK_MAX, S // 4)
    ret_chunk = min(256, S)
    tab_a = _rope_table(positions, ATTN_ROT_HALF, ROPE_THETA, ATTN_HEAD_DIM)
    tab_i = _rope_table(positions, IDX_ROT_HALF, ROPE_THETA, IDX_HEAD_DIM)
    tab_r = _rope_table(positions, RET_KEY_DIM // 2, RET_THETA, RET_KEY_DIM)
    ret_consts = _retention_consts(ret_chunk)
    vec = lambda a: a.reshape(1, -1).astype(F32)

    h = x.reshape(T, D)
    for i in range(depth):
        z = _norm_proj(h, vec(pre_mix_norm[i]), _reorder_w_in(w_in[i]))
        z3 = z.reshape(B, S, Z_WIDTH)
        attn = _attention(z3, tab_a, tab_i, topk)
        ret = _retention(z3, tab_r, ret_consts, vec(ret_gn[i]), ret_chunk)
        wo = w_out[i].astype(BF16)
        h, m = _out_proj(attn.reshape(T, ATTN_WIDTH), ret.reshape(T, RET_WIDTH), h,
                         wo[:ATTN_WIDTH], wo[ATTN_WIDTH:], vec(post_mix_norm[i]), vec(pre_ff_norm[i]))
        h = _ffn(m, w_ff1[i].astype(BF16), w_ff2[i].astype(BF16), h, vec(post_ff_norm[i]))
        h = _ple(h, p[i].reshape(T, -1), w_ple_gate[i].astype(BF16), w_ple[i].astype(BF16), vec(ple_norm[i]))
    return h.reshape(B, S, D)
```

```python
import functools
import math

import jax
import jax.numpy as jnp
import numpy as np
from jax import lax
from jax.experimental import pallas as pl
from jax.experimental.pallas import tpu as pltpu

N_ATTN_HEADS = 8
ATTN_HEAD_DIM = 128
ATTN_ROT_HALF = 16
ROPE_THETA = 500000.0
TOPK_MAX = 256
N_IDX_HEADS = 16
IDX_HEAD_DIM = 64
IDX_ROT_HALF = 8
N_RET_HEADS = 8
RET_KEY_DIM = 64
RET_VAL_DIM = 128
RET_THETA = 10000.0
ATTN_WIDTH = N_ATTN_HEADS * ATTN_HEAD_DIM
IDX_WIDTH = N_IDX_HEADS * IDX_HEAD_DIM
RET_KEY_WIDTH = N_RET_HEADS * RET_KEY_DIM
RET_WIDTH = N_RET_HEADS * RET_VAL_DIM
RMS_EPS = 1e-6
GN_EPS = 1e-5

LANES = 128
COL_AQ = 0
COL_IQ = COL_AQ + ATTN_WIDTH
COL_RV = COL_IQ + IDX_WIDTH
COL_RG = COL_RV + RET_WIDTH
COL_RQ = COL_RG + RET_WIDTH
COL_RK = COL_RQ + RET_KEY_WIDTH
COL_AK = COL_RK + RET_KEY_WIDTH
COL_AV = COL_AK + ATTN_HEAD_DIM
COL_IKW = COL_AV + ATTN_HEAD_DIM
Z_USED = COL_IKW + LANES
Z_TILE_N = 512
Z_WIDTH = -(-Z_USED // Z_TILE_N) * Z_TILE_N

MASK_NEG = -1e30
THETA_ALL = -3e38
CUT_ALL = 1e9
VMEM_LIMIT = 56 * 1024 * 1024

BF16 = jnp.bfloat16
F32 = jnp.float32


def _cparams(n_grid):
    return pltpu.CompilerParams(dimension_semantics=("arbitrary",) * n_grid,
                                vmem_limit_bytes=VMEM_LIMIT)


def _rms(x, gain):
    ms = jnp.mean(x * x, axis=-1, keepdims=True)
    return x * lax.rsqrt(ms + RMS_EPS) * gain


def _rope(xb, cos, sin, swap):
    return xb.astype(F32) * cos + jnp.dot(xb, swap, preferred_element_type=F32) * sin


def _swap_matrix(group, half):
    l = jnp.arange(LANES)
    d = l % group
    partner = jnp.where(d < half, l + half, jnp.where(d < 2 * half, l - half, -1))
    return (jnp.arange(LANES)[:, None] == partner[None, :]).astype(BF16)


def _dot_t(a, b):
    return lax.dot_general(a, b, (((1,), (1,)), ((), ())), preferred_element_type=F32)


def _dot(a, b):
    return jnp.dot(a, b, preferred_element_type=F32)


def _norm_proj_kernel(h_ref, g_ref, w_ref, z_ref, a_ref, *, row_chunk):
    @pl.when(pl.program_id(1) == 0)
    def _():
        def body(r, _):
            r0 = pl.multiple_of(r * row_chunk, row_chunk)
            a_ref[pl.ds(r0, row_chunk), :] = _rms(h_ref[pl.ds(r0, row_chunk), :], g_ref[...]).astype(BF16)
            return 0
        lax.fori_loop(0, h_ref.shape[0] // row_chunk, body, 0)

    z_ref[...] = _dot(a_ref[...], w_ref[...]).astype(z_ref.dtype)


def _norm_proj(h, gain, w):
    T, D = h.shape
    N = w.shape[1]
    tm = min(1024, T)
    tn = Z_TILE_N
    return pl.pallas_call(
        functools.partial(_norm_proj_kernel, row_chunk=min(128, tm)),
        grid=(T // tm, N // tn),
        in_specs=[pl.BlockSpec((tm, D), lambda i, j: (i, 0)),
                  pl.BlockSpec((1, D), lambda i, j: (0, 0)),
                  pl.BlockSpec((D, tn), lambda i, j: (0, j))],
        out_specs=pl.BlockSpec((tm, tn), lambda i, j: (i, j)),
        out_shape=jax.ShapeDtypeStruct((T, N), BF16),
        scratch_shapes=[pltpu.VMEM((tm, D), BF16)],
        compiler_params=_cparams(2),
        name="norm_proj",
    )(h, gain, w)


def _attn_kernel(aq_ref, iq_ref, iwq_ref, ak_ref, av_ref, ikw_ref,
                 taq_ref, tak_ref, tiq_ref, tik_ref, pa_ref, pi_ref, ex_ref, o_ref,
                 kr_ref, kd0_ref, kd1_ref, vaug_ref, qa_ref, qi_ref, wb_ref, score_ref, score_t_ref,
                 th_ref, cut_ref, s_ref, m_ref, acc_ref, *, topk, tq, tk):
    S = ak_ref.shape[1]
    qi = pl.program_id(1)
    n_chunks = (qi * tq + tq + tk - 1) // tk
    reps = tk // LANES
    rows_all = N_ATTN_HEADS * tq

    @pl.when(qi == 0)
    def _():
        rows = min(256, S)

        def body(r, _):
            r0 = pl.multiple_of(r * rows, rows)
            sl = pl.ds(r0, rows)
            kr_ref[sl, :] = _rope(ak_ref[0, sl, :], tak_ref[0, sl, :LANES], tak_ref[0, sl, LANES:],
                                  pa_ref[...]).astype(BF16)
            lane_r = lax.broadcasted_iota(jnp.int32, (rows, LANES), 1)
            ik = _rope(ikw_ref[0, sl, :], tik_ref[0, sl, :LANES], tik_ref[0, sl, LANES:], pi_ref[...])
            ik = jnp.where(lane_r < IDX_HEAD_DIM, ik, 0.0)
            kd0_ref[sl, :] = ik.astype(BF16)
            kd1_ref[sl, :] = pltpu.roll(ik, IDX_HEAD_DIM, 1).astype(BF16)
            vaug_ref[sl, :] = jnp.concatenate([av_ref[0, sl, :], jnp.ones((rows, LANES), BF16)], axis=1)
            score_t_ref[sl, :] = jnp.full((rows, tq), -jnp.inf, F32)
            return 0
        lax.fori_loop(0, S // rows, body, 0)

    ca = taq_ref[0, :, :LANES] * (ATTN_HEAD_DIM ** -0.5)
    sa = taq_ref[0, :, LANES:] * (ATTN_HEAD_DIM ** -0.5)
    for h in range(N_ATTN_HEADS):
        x = aq_ref[0, :, h * LANES:(h + 1) * LANES]
        qa_ref[h * tq:(h + 1) * tq, :] = _rope(x, ca, sa, pa_ref[...]).astype(BF16)
    ci, si = tiq_ref[0, :, :LANES], tiq_ref[0, :, LANES:]
    for j in range(IDX_WIDTH // LANES):
        x = iq_ref[0, :, j * LANES:(j + 1) * LANES]
        qi_ref[:, j * LANES:(j + 1) * LANES] = _rope(x, ci, si, pi_ref[...]).astype(BF16)
    iw_b = jnp.dot(iwq_ref[0], ex_ref[...], preferred_element_type=F32)
    for h in range(N_IDX_HEADS):
        wb_ref[h] = iw_b[:, h * LANES:(h + 1) * LANES]

    col_in_chunk = lax.broadcasted_iota(jnp.int32, (tq, tk), 1)

    def lane_groups(x, op):
        r = x[:, :LANES]
        for g in range(1, reps):
            r = op(r, x[:, g * LANES:(g + 1) * LANES])
        return r

    def tile_lanes(x):
        return jnp.concatenate([x] * reps, axis=1) if reps > 1 else x

    key_t = lax.broadcasted_iota(jnp.int32, (tk, tq), 0)

    def fold8(x, op):
        x3 = x.reshape(tk // 8, 8, x.shape[1])
        return jnp.max(x3, axis=0) if op == "max" else (jnp.min(x3, axis=0) if op == "min" else jnp.sum(x3, axis=0))

    rb = min(LANES, tq)
    row_b = lax.broadcasted_iota(jnp.int32, (rb, tk), 0)
    col_b = lax.broadcasted_iota(jnp.int32, (rb, tk), 1)
    key_tb = lax.broadcasted_iota(jnp.int32, (tk, rb), 0)
    qry_tb = lax.broadcasted_iota(jnp.int32, (tk, rb), 1)

    def a_body(c, carry):
        rmax, rmin = carry
        c0 = pl.multiple_of(c * tk, tk)
        k0 = kd0_ref[pl.ds(c0, tk), :]
        k1 = kd1_ref[pl.ds(c0, tk), :]
        maxs, mins = [], []
        for r in range(tq // rb):
            rows = slice(r * rb, (r + 1) * rb)
            q0 = qi * tq + r * rb
            acc = jnp.zeros((rb, tk), F32)
            for j in range(IDX_WIDTH // LANES):
                qp = qi_ref[rows, j * LANES:(j + 1) * LANES]
                for par, kd in ((0, k0), (1, k1)):
                    logits = _dot_t(qp, kd)
                    acc = acc + tile_lanes(wb_ref[2 * j + par, rows, :]) * jnp.maximum(logits, 0.0)
            causal = (c0 + col_b) <= (q0 + row_b)
            score_ref[c, rows, :] = jnp.where(causal, acc, -jnp.inf)
            acc_t = acc.T
            causal_t = (c0 + key_tb) <= (q0 + qry_tb)
            sc_t = jnp.where(causal_t, acc_t, -jnp.inf)
            score_t_ref[pl.ds(c0, tk), rows] = sc_t
            maxs.append(fold8(sc_t, "max"))
            mins.append(fold8(jnp.where(causal_t, acc_t, jnp.inf), "min"))
        rmax = jnp.maximum(rmax, jnp.concatenate(maxs, axis=1))
        rmin = jnp.minimum(rmin, jnp.concatenate(mins, axis=1))
        return rmax, rmin

    rmax, rmin = lax.fori_loop(0, n_chunks, a_body,
                               (jnp.full((8, tq), -jnp.inf, F32), jnp.full((8, tq), jnp.inf, F32)))

    vec = (1, tq)
    kf = float(topk)
    n_keys = qi * tq + lax.broadcasted_iota(jnp.int32, vec, 1) + 1

    lo0 = jnp.min(rmin, axis=0, keepdims=True)
    hi0 = jnp.max(rmax, axis=0, keepdims=True)
    key_f = key_t.astype(F32)

    def threshold_search(n_rows):
        def count_t(pred):
            accs = [jnp.zeros((8, tq), F32) for _ in range(4)]
            i = 0
            for r0 in range(0, n_rows, tk):
                ind = jnp.where(pred(r0, score_t_ref[r0:r0 + tk, :]), 1.0, 0.0)
                for g in range(tk // 8):
                    accs[i % 4] = accs[i % 4] + ind[g * 8:(g + 1) * 8, :]
                    i += 1
            return jnp.sum((accs[0] + accs[1]) + (accs[2] + accs[3]), axis=0, keepdims=True)

        def b_cond(carry):
            return carry[1] > 0.0

        def b_body(carry):
            state = (carry[0],) + tuple(carry[2:])
            flag = jnp.max(1.0 - carry[5])
            for _ in range(steps_per_check):
                state = b_step(*state)
            return (state[0], flag) + tuple(state[1:])

        def b_step(it, lo, hi, chi, done, th, tie):
            mid = jnp.where(it == 0, hi, 0.5 * lo + 0.5 * hi)
            cnt = count_t(lambda r0, sc: sc >= mid)
            active = done == 0.0
            adjacent = jnp.logical_and(it > 0, jnp.logical_or(mid <= lo, mid >= hi))
            tie_now = jnp.logical_and(active, adjacent)
            live = jnp.logical_and(active, jnp.logical_not(adjacent))
            hit = jnp.logical_and(live, cnt == kf)
            above = jnp.logical_and(live, cnt > kf)
            below = jnp.logical_and(live, cnt < kf)
            th = jnp.where(tie_now, lo, jnp.where(hit, mid, th))
            tie = jnp.where(tie_now, 1.0, tie)
            lo = jnp.where(above, mid, lo)
            hi = jnp.where(below, mid, hi)
            chi = jnp.where(below, cnt, chi)
            done = jnp.where(jnp.logical_or(tie_now, hit), 1.0, done)
            return it + 1, lo, hi, chi, done, th, tie

        steps_per_check = 2 if n_rows <= 4 * tk else 1

        init = (jnp.int32(0), jnp.float32(1.0), lo0, hi0,
                jnp.zeros(vec, F32), jnp.where(n_keys <= topk, 1.0, 0.0),
                jnp.full(vec, THETA_ALL, F32), jnp.zeros(vec, F32))
        _, _, _, _, chi, _, th, tie = lax.while_loop(b_cond, b_body, init)

        def tie_cut():
            need = kf - chi

            def body(_, carry):
                lo_i, hi_i = carry
                mid = jnp.floor(0.5 * (lo_i + hi_i))
                cnt = count_t(lambda r0, sc: jnp.logical_and(sc == th, key_f + float(r0) <= mid))
                ge = cnt >= need
                return jnp.where(ge, lo_i, mid), jnp.where(ge, mid, hi_i)

            n_iter = int(math.ceil(math.log2(S))) + 1
            _, hi_i = lax.fori_loop(0, n_iter, body, (jnp.full(vec, -1.0, F32), jnp.full(vec, float(S - 1), F32)))
            return jnp.where(tie > 0.0, hi_i, CUT_ALL)

        th_ref[...] = th
        cut_ref[...] = lax.cond(jnp.max(tie) > 0.0, tie_cut, lambda: jnp.full(vec, CUT_ALL, F32))

    pair_rows = 2 * tk
    variant = (n_chunks * tk + pair_rows - 1) // pair_rows - 1
    for v in range(-(-S // pair_rows)):
        pl.when(variant == v)(functools.partial(threshold_search, min((v + 1) * pair_rows, S)))

    th_c = jnp.broadcast_to(th_ref[...], (LANES, tq)).T
    cut_c = jnp.broadcast_to(cut_ref[...], (LANES, tq)).T
    th_t = tile_lanes(th_c)
    cut_t = tile_lanes(cut_c)

    m_ref[...] = jnp.full(m_ref.shape, MASK_NEG, F32)
    colf_in_chunk = col_in_chunk.astype(F32)

    def c1_body(c, _):
        c0 = pl.multiple_of(c * tk, tk)
        sc = score_ref[c]
        colf = colf_in_chunk + (c * tk).astype(F32)
        sel = jnp.logical_or(sc > th_t, jnp.logical_and(sc == th_t, colf <= cut_t))
        bias = jnp.where(sel, 0.0, MASK_NEG)
        s = _dot_t(qa_ref[...], kr_ref[pl.ds(c0, tk), :])
        s = (s.reshape(N_ATTN_HEADS, tq, tk) + bias[None]).reshape(rows_all, tk)
        s_ref[c] = s
        m_ref[...] = jnp.maximum(m_ref[...], lane_groups(s, jnp.maximum))
        return 0

    lax.fori_loop(0, n_chunks, c1_body, 0)
    m_ref[...] = jnp.broadcast_to(jnp.max(m_ref[...], axis=1, keepdims=True), m_ref.shape)

    acc_ref[...] = jnp.zeros(acc_ref.shape, F32)

    def c2_body(c, _):
        c0 = pl.multiple_of(c * tk, tk)
        p = jnp.exp(s_ref[c] - tile_lanes(m_ref[...]))
        acc_ref[...] += _dot(p.astype(BF16), vaug_ref[pl.ds(c0, tk), :])
        return 0

    lax.fori_loop(0, n_chunks, c2_body, 0)
    out = acc_ref[:, :LANES] / acc_ref[:, LANES:]
    for h in range(N_ATTN_HEADS):
        o_ref[0, :, h * LANES:(h + 1) * LANES] = out[h * tq:(h + 1) * tq, :].astype(o_ref.dtype)


def _attention(z3, tab_a, tab_i, topk):
    B, S, _ = z3.shape
    tq = min(256, S)
    tk = min(256, S)
    qblk = lambda w, col: pl.BlockSpec((1, tq, w), lambda b, q: (b, q, col // w))
    kblk = lambda w, col: pl.BlockSpec((1, S, w), lambda b, q: (b, 0, col // w))
    tab_q = pl.BlockSpec((1, tq, 2 * LANES), lambda b, q: (b, q, 0))
    tab_k = pl.BlockSpec((1, S, 2 * LANES), lambda b, q: (b, 0, 0))
    rows = N_ATTN_HEADS * tq
    const = lambda a: pl.BlockSpec(a.shape, lambda b, q: (0, 0))
    perm_a = _swap_matrix(ATTN_HEAD_DIM, ATTN_ROT_HALF)
    perm_i = _swap_matrix(IDX_HEAD_DIM, IDX_ROT_HALF)
    idx_w_scale = (N_IDX_HEADS ** -0.5) * (IDX_HEAD_DIM ** -0.5)
    assert math.frexp(idx_w_scale)[0] == 0.5, "the folded scale must be a power of two to stay exact in bf16"
    expand = jnp.where(jnp.arange(LANES)[:, None] == IDX_HEAD_DIM + jnp.arange(N_IDX_HEADS * LANES)[None, :] // LANES,
                       idx_w_scale, 0.0).astype(BF16)
    return pl.pallas_call(
        functools.partial(_attn_kernel, topk=topk, tq=tq, tk=tk),
        grid=(B, S // tq),
        in_specs=[qblk(ATTN_WIDTH, COL_AQ), qblk(IDX_WIDTH, COL_IQ), qblk(LANES, COL_IKW),
                  kblk(LANES, COL_AK), kblk(LANES, COL_AV), kblk(LANES, COL_IKW),
                  tab_q, tab_k, tab_q, tab_k, const(perm_a), const(perm_i), const(expand)],
        out_specs=pl.BlockSpec((1, tq, ATTN_WIDTH), lambda b, q: (b, q, 0)),
        out_shape=jax.ShapeDtypeStruct((B, S, ATTN_WIDTH), BF16),
        scratch_shapes=[pltpu.VMEM((S, LANES), BF16),
                        pltpu.VMEM((S, LANES), BF16),
                        pltpu.VMEM((S, LANES), BF16),
                        pltpu.VMEM((S, 2 * LANES), BF16),
                        pltpu.VMEM((rows, LANES), BF16),
                        pltpu.VMEM((tq, IDX_WIDTH), BF16),
                        pltpu.VMEM((N_IDX_HEADS, tq, LANES), F32),
                        pltpu.VMEM((S // tk, tq, tk), F32),
                        pltpu.VMEM((S, tq), F32),
                        pltpu.VMEM((1, tq), F32),
                        pltpu.VMEM((1, tq), F32),
                        pltpu.VMEM((S // tk, rows, tk), F32),
                        pltpu.VMEM((rows, LANES), F32),
                        pltpu.VMEM((rows, 2 * LANES), F32)],
        compiler_params=_cparams(2),
        name="dsa_attention",
    )(z3, z3, z3, z3, z3, z3, tab_a, tab_a, tab_i, tab_i, perm_a, perm_i, expand)


def _ret_kernel(rq_ref, rk_ref, rv_ref, rg_ref, tab_ref, decay_ref, zeta_ref, xi_ref, gch_ref, gn_ref,
                o_ref, state_ref, *, chunk):
    S = rq_ref.shape[1]
    C = chunk
    half = RET_KEY_DIM
    state_ref[...] = jnp.zeros(state_ref.shape, F32)
    lane = lax.broadcasted_iota(jnp.int32, (C, LANES), 1)
    first = lane < half

    def body(c, _):
        r0 = pl.multiple_of(c * C, C)
        sl = pl.ds(r0, C)
        cos, sin = tab_ref[0, sl, :LANES], tab_ref[0, sl, LANES:]
        cos_k, sin_k = cos * (RET_KEY_DIM ** -0.5), sin * (RET_KEY_DIM ** -0.5)
        first_half = lane % RET_KEY_DIM < RET_KEY_DIM // 2

        def rope(x, c, s):
            partner = jnp.where(first_half, pltpu.roll(x, LANES - RET_KEY_DIM // 2, 1),
                                pltpu.roll(x, RET_KEY_DIM // 2, 1))
            return x * c + partner * s

        for j in range(N_RET_HEADS // 2):
            q = rope(rq_ref[0, sl, j * LANES:(j + 1) * LANES].astype(F32), cos, sin)
            k = rope(rk_ref[0, sl, j * LANES:(j + 1) * LANES].astype(F32), cos_k, sin_k)
            kzt = (k * zeta_ref[j]).T.astype(BF16)
            state = state_ref[j]
            state_b = state.astype(BF16)
            new_rows = []
            for par in range(2):
                h = 2 * j + par
                keep = first if par == 0 else jnp.logical_not(first)
                qh = jnp.where(keep, q, 0.0).astype(BF16)
                kh = jnp.where(keep, k, 0.0).astype(BF16)
                v = rv_ref[0, sl, h * LANES:(h + 1) * LANES]
                inner = _dot_t(qh, kh) * decay_ref[h]
                o = _dot(inner.astype(BF16), v) + _dot(qh, state_b) * xi_ref[h]
                new_rows.append(_dot(kzt[par * half:(par + 1) * half, :], v))
                mu = jnp.mean(o, axis=-1, keepdims=True)
                d = o - mu
                var = jnp.mean(d * d, axis=-1, keepdims=True)
                y = d * lax.rsqrt(var + GN_EPS) * gn_ref[:, h * LANES:(h + 1) * LANES]
                g = rg_ref[0, sl, h * LANES:(h + 1) * LANES].astype(F32)
                o_ref[0, sl, h * LANES:(h + 1) * LANES] = (y * (g * jax.nn.sigmoid(g))).astype(o_ref.dtype)
            state_ref[j] = gch_ref[j] * state + jnp.concatenate(new_rows, axis=0)
        return 0

    lax.fori_loop(0, S // C, body, 0)


def _retention(z3, tab_r, consts, gn_w, chunk):
    B, S, _ = z3.shape
    decay, zeta, xi, gch = consts
    blk = lambda w, col: pl.BlockSpec((1, S, w), lambda b: (b, 0, col // w))
    const = lambda a: pl.BlockSpec(a.shape, lambda b: (0,) * a.ndim)
    return pl.pallas_call(
        functools.partial(_ret_kernel, chunk=chunk),
        grid=(B,),
        in_specs=[blk(RET_KEY_WIDTH, COL_RQ), blk(RET_KEY_WIDTH, COL_RK), blk(RET_WIDTH, COL_RV),
                  blk(RET_WIDTH, COL_RG), pl.BlockSpec((1, S, 2 * LANES), lambda b: (b, 0, 0)),
                  const(decay), const(zeta), const(xi), const(gch), const(gn_w)],
        out_specs=pl.BlockSpec((1, S, RET_WIDTH), lambda b: (b, 0, 0)),
        out_shape=jax.ShapeDtypeStruct((B, S, RET_WIDTH), BF16),
        scratch_shapes=[pltpu.VMEM((N_RET_HEADS // 2, LANES, RET_VAL_DIM), F32)],
        compiler_params=_cparams(1),
        name="retention",
    )(z3, z3, z3, z3, tab_r, decay, zeta, xi, gch, gn_w)


def _retention_consts(chunk):
    C = chunk
    log_g = jnp.log(1.0 - 2.0 ** (-5.0 - jnp.arange(N_RET_HEADS, dtype=F32)))
    i = jnp.arange(C, dtype=F32)
    diff = i[:, None] - i[None, :]
    decay = jnp.where(diff[None] >= 0, jnp.exp(jnp.maximum(diff, 0.0)[None] * log_g[:, None, None]), 0.0)
    zeta = jnp.exp((C - 1.0 - i)[None, :] * log_g[:, None])
    xi = jnp.exp((i + 1.0)[None, :] * log_g[:, None])
    g_chunk = jnp.exp(C * log_g)
    pair = lambda a: a.reshape(N_RET_HEADS // 2, 2, -1)
    zeta_p = jnp.repeat(pair(zeta).transpose(0, 2, 1), RET_KEY_DIM, axis=2)
    xi_b = jnp.broadcast_to(xi[:, :, None], (N_RET_HEADS, C, RET_VAL_DIM))
    gch = jnp.broadcast_to(jnp.repeat(pair(g_chunk), RET_KEY_DIM, axis=1).reshape(N_RET_HEADS // 2, LANES, 1),
                           (N_RET_HEADS // 2, LANES, RET_VAL_DIM))
    return decay.astype(F32), zeta_p.astype(F32), xi_b.astype(F32), gch.astype(F32)


def _out_proj_kernel(attn_ref, ret_ref, h_ref, wa_ref, wr_ref, g1_ref, g2_ref, h1_ref, m_ref, *, sub):
    for r0 in range(0, h_ref.shape[0], sub):
        sl = slice(r0, r0 + sub)
        mix = _dot(attn_ref[sl, :], wa_ref[...]) + _dot(ret_ref[sl, :], wr_ref[...])
        h1 = h_ref[sl, :] + _rms(mix, g1_ref[...])
        h1_ref[sl, :] = h1
        m_ref[sl, :] = _rms(h1, g2_ref[...]).astype(m_ref.dtype)


def _out_proj(attn, ret, h, wa, wr, g1, g2):
    T, D = h.shape
    tm = min(512, T)
    row = lambda w: pl.BlockSpec((tm, w), lambda i: (i, 0))
    const = lambda a: pl.BlockSpec(a.shape, lambda i: (0, 0))
    return pl.pallas_call(
        functools.partial(_out_proj_kernel, sub=min(256, tm)),
        grid=(T // tm,),
        in_specs=[row(attn.shape[1]), row(ret.shape[1]), row(D), const(wa), const(wr), const(g1), const(g2)],
        out_specs=[row(D), row(D)],
        out_shape=[jax.ShapeDtypeStruct((T, D), F32), jax.ShapeDtypeStruct((T, D), BF16)],
        compiler_params=_cparams(1),
        name="out_proj",
    )(attn, ret, h, wa, wr, g1, g2)


def _ffn_kernel(m_ref, w1_ref, w2_ref, h_ref, g_ref, o_ref, acc_ref):
    f = pl.program_id(1)

    @pl.when(f == 0)
    def _():
        acc_ref[...] = jnp.zeros(acc_ref.shape, F32)

    u = jnp.maximum(_dot(m_ref[...], w1_ref[...]), 0.0)
    acc_ref[...] += _dot((u * u).astype(BF16), w2_ref[...])

    @pl.when(f == pl.num_programs(1) - 1)
    def _():
        o_ref[...] = h_ref[...] + _rms(acc_ref[...], g_ref[...])


def _ffn(m, w1, w2, h, g):
    T, D = h.shape
    F = w1.shape[1]
    tm = min(512, T)
    tf = min(512, F)
    return pl.pallas_call(
        _ffn_kernel,
        grid=(T // tm, F // tf),
        in_specs=[pl.BlockSpec((tm, D), lambda i, f: (i, 0)),
                  pl.BlockSpec((D, tf), lambda i, f: (0, f)),
                  pl.BlockSpec((tf, D), lambda i, f: (f, 0)),
                  pl.BlockSpec((tm, D), lambda i, f: (i, 0)),
                  pl.BlockSpec((1, D), lambda i, f: (0, 0))],
        out_specs=pl.BlockSpec((tm, D), lambda i, f: (i, 0)),
        out_shape=jax.ShapeDtypeStruct((T, D), F32),
        scratch_shapes=[pltpu.VMEM((tm, D), F32)],
        compiler_params=_cparams(2),
        name="ffn",
    )(m, w1, w2, h, g)


def _ple_kernel(h_ref, p_ref, wg_ref, wp_ref, g_ref, o_ref, *, sub):
    for r0 in range(0, h_ref.shape[0], sub):
        sl = slice(r0, r0 + sub)
        h = h_ref[sl, :]
        gate = jax.nn.sigmoid(_dot(h.astype(BF16), wg_ref[...]))
        e = _dot(p_ref[sl, :].astype(BF16), wp_ref[...])
        o_ref[sl, :] = h + _rms(gate * e, g_ref[...])


def _ple(h, p, wg, wp, g):
    T, D = h.shape
    tm = min(512, T)
    row = lambda w: pl.BlockSpec((tm, w), lambda i: (i, 0))
    const = lambda a: pl.BlockSpec(a.shape, lambda i: (0, 0))
    return pl.pallas_call(
        functools.partial(_ple_kernel, sub=min(256, tm)),
        grid=(T // tm,),
        in_specs=[row(D), row(p.shape[1]), const(wg), const(wp), const(g)],
        out_specs=row(D),
        out_shape=jax.ShapeDtypeStruct((T, D), F32),
        compiler_params=_cparams(1),
        name="ple",
    )(h, p, wg, wp, g)


def _rope_table(positions, half, theta, group):
    inv = theta ** (-jnp.arange(half, dtype=F32) / half)
    ang = positions.astype(F32)[..., None] * inv
    compact = jnp.concatenate([jnp.cos(ang), jnp.sin(ang), jnp.ones_like(ang[..., :1])], axis=-1)
    spread = np.zeros((2 * half + 1, 2 * LANES), np.float32)
    for lane in range(LANES):
        d = lane % group
        if d < 2 * half:
            spread[d % half, lane] = 1.0
            spread[half + d % half, LANES + lane] = -1.0 if d < half else 1.0
        else:
            spread[2 * half, lane] = 1.0
    return jnp.einsum("bsk,kn->bsn", compact, jnp.asarray(spread), precision=lax.Precision.HIGHEST)


def _reorder_w_in(w):
    w = w.astype(BF16)
    a0 = ATTN_WIDTH
    a1 = a0 + 2 * ATTN_HEAD_DIM
    a2 = a1 + IDX_WIDTH
    a3 = a2 + IDX_HEAD_DIM + N_IDX_HEADS
    a4 = a3 + 2 * RET_KEY_WIDTH
    pad = jnp.zeros((w.shape[0], Z_WIDTH - Z_USED + LANES - IDX_HEAD_DIM - N_IDX_HEADS), BF16)
    return jnp.concatenate([w[:, :a0], w[:, a1:a2], w[:, a4:], w[:, a3:a4], w[:, a0:a1], w[:, a2:a3], pad], axis=1)


def kernel(x, p, positions, w_in, w_out, w_ff1, w_ff2, w_ple, w_ple_gate, pre_mix_norm, post_mix_norm,
           pre_ff_norm, post_ff_norm, ple_norm, ret_gn):
    B, S, D = x.shape
    depth = w_in.shape[0]
    T = B * S
    topk = min(TOPK_MAX, S // 4)
    ret_chunk = min(256, S)
    tab_a = _rope_table(positions, ATTN_ROT_HALF, ROPE_THETA, ATTN_HEAD_DIM)
    tab_i = _rope_table(positions, IDX_ROT_HALF, ROPE_THETA, IDX_HEAD_DIM)
    tab_r = _rope_table(positions, RET_KEY_DIM // 2, RET_THETA, RET_KEY_DIM)
    ret_consts = _retention_consts(ret_chunk)
    vec = lambda a: a.reshape(1, -1).astype(F32)

    h = x.reshape(T, D)
    for i in range(depth):
        z = _norm_proj(h, vec(pre_mix_norm[i]), _reorder_w_in(w_in[i]))
        z3 = z.reshape(B, S, Z_WIDTH)
        attn = _attention(z3, tab_a, tab_i, topk)
        ret = _retention(z3, tab_r, ret_consts, vec(ret_gn[i]), ret_chunk)
        wo = w_out[i].astype(BF16)
        h, m = _out_proj(attn.reshape(T, ATTN_WIDTH), ret.reshape(T, RET_WIDTH), h,
                         wo[:ATTN_WIDTH], wo[ATTN_WIDTH:], vec(post_mix_norm[i]), vec(pre_ff_norm[i]))
        h = _ffn(m, w_ff1[i].astype(BF16), w_ff2[i].astype(BF16), h, vec(post_ff_norm[i]))
        h = _ple(h, p[i].reshape(T, -1), w_ple_gate[i].astype(BF16), w_ple[i].astype(BF16), vec(ple_norm[i]))
    return h.reshape(B, S, D)
```

```python
import functools
import math

import jax
import jax.numpy as jnp
import numpy as np
from jax import lax
from jax.experimental import pallas as pl
from jax.experimental.pallas import tpu as pltpu

N_ATTN_HEADS = 8
ATTN_HEAD_DIM = 128
ATTN_ROT_HALF = 16
ROPE_THETA = 500000.0
TOPK_MAX = 256
N_IDX_HEADS = 16
IDX_HEAD_DIM = 64
IDX_ROT_HALF = 8
N_RET_HEADS = 8
RET_KEY_DIM = 64
RET_VAL_DIM = 128
RET_THETA = 10000.0
ATTN_WIDTH = N_ATTN_HEADS * ATTN_HEAD_DIM
IDX_WIDTH = N_IDX_HEADS * IDX_HEAD_DIM
RET_KEY_WIDTH = N_RET_HEADS * RET_KEY_DIM
RET_WIDTH = N_RET_HEADS * RET_VAL_DIM
RMS_EPS = 1e-6
GN_EPS = 1e-5

LANES = 128
COL_AQ = 0
COL_IQ = COL_AQ + ATTN_WIDTH
COL_RV = COL_IQ + IDX_WIDTH
COL_RG = COL_RV + RET_WIDTH
COL_RQ = COL_RG + RET_WIDTH
COL_RK = COL_RQ + RET_KEY_WIDTH
COL_AK = COL_RK + RET_KEY_WIDTH
COL_AV = COL_AK + ATTN_HEAD_DIM
COL_IKW = COL_AV + ATTN_HEAD_DIM
Z_USED = COL_IKW + LANES
Z_TILE_N = 512
Z_WIDTH = -(-Z_USED // Z_TILE_N) * Z_TILE_N

MASK_NEG = -1e30
THETA_ALL = -3e38
CUT_ALL = 1e9
VMEM_LIMIT = 56 * 1024 * 1024

BF16 = jnp.bfloat16
F32 = jnp.float32


def _cparams(n_grid):
    return pltpu.CompilerParams(dimension_semantics=("arbitrary",) * n_grid,
                                vmem_limit_bytes=VMEM_LIMIT)


def _rms(x, gain):
    ms = jnp.mean(x * x, axis=-1, keepdims=True)
    return x * lax.rsqrt(ms + RMS_EPS) * gain


def _rope(xb, cos, sin, swap):
    return xb.astype(F32) * cos + jnp.dot(xb, swap, preferred_element_type=F32) * sin


def _swap_matrix(group, half):
    l = jnp.arange(LANES)
    d = l % group
    partner = jnp.where(d < half, l + half, jnp.where(d < 2 * half, l - half, -1))
    return (jnp.arange(LANES)[:, None] == partner[None, :]).astype(BF16)


def _dot_t(a, b):
    return lax.dot_general(a, b, (((1,), (1,)), ((), ())), preferred_element_type=F32)


def _dot(a, b):
    return jnp.dot(a, b, preferred_element_type=F32)


def _norm_proj_kernel(h_ref, g_ref, w_ref, z_ref, a_ref, *, row_chunk):
    @pl.when(pl.program_id(1) == 0)
    def _():
        def body(r, _):
            r0 = pl.multiple_of(r * row_chunk, row_chunk)
            a_ref[pl.ds(r0, row_chunk), :] = _rms(h_ref[pl.ds(r0, row_chunk), :], g_ref[...]).astype(BF16)
            return 0
        lax.fori_loop(0, h_ref.shape[0] // row_chunk, body, 0)

    z_ref[...] = _dot(a_ref[...], w_ref[...]).astype(z_ref.dtype)


def _norm_proj(h, gain, w):
    T, D = h.shape
    N = w.shape[1]
    tm = min(1024, T)
    tn = Z_TILE_N
    return pl.pallas_call(
        functools.partial(_norm_proj_kernel, row_chunk=min(128, tm)),
        grid=(T // tm, N // tn),
        in_specs=[pl.BlockSpec((tm, D), lambda i, j: (i, 0)),
                  pl.BlockSpec((1, D), lambda i, j: (0, 0)),
                  pl.BlockSpec((D, tn), lambda i, j: (0, j))],
        out_specs=pl.BlockSpec((tm, tn), lambda i, j: (i, j)),
        out_shape=jax.ShapeDtypeStruct((T, N), BF16),
        scratch_shapes=[pltpu.VMEM((tm, D), BF16)],
        compiler_params=_cparams(2),
        name="norm_proj",
    )(h, gain, w)


def _attn_kernel(aq_ref, iq_ref, iwq_ref, ak_ref, av_ref, ikw_ref,
                 taq_ref, tak_ref, tiq_ref, tik_ref, pa_ref, pi_ref, ex_ref, o_ref,
                 kr_ref, kd0_ref, kd1_ref, vaug_ref, qa_ref, qi_ref, wb_ref, score_ref, score_t_ref,
                 th_ref, cut_ref, s_ref, m_ref, acc_ref, *, topk, tq, tk):
    S = ak_ref.shape[1]
    qi = pl.program_id(1)
    n_chunks = (qi * tq + tq + tk - 1) // tk
    reps = tk // LANES
    rows_all = N_ATTN_HEADS * tq

    @pl.when(qi == 0)
    def _():
        rows = min(256, S)

        def body(r, _):
            r0 = pl.multiple_of(r * rows, rows)
            sl = pl.ds(r0, rows)
            kr_ref[sl, :] = _rope(ak_ref[0, sl, :], tak_ref[0, sl, :LANES], tak_ref[0, sl, LANES:],
                                  pa_ref[...]).astype(BF16)
            lane_r = lax.broadcasted_iota(jnp.int32, (rows, LANES), 1)
            ik = _rope(ikw_ref[0, sl, :], tik_ref[0, sl, :LANES], tik_ref[0, sl, LANES:], pi_ref[...])
            ik = jnp.where(lane_r < IDX_HEAD_DIM, ik, 0.0)
            kd0_ref[sl, :] = ik.astype(BF16)
            kd1_ref[sl, :] = pltpu.roll(ik, IDX_HEAD_DIM, 1).astype(BF16)
            vaug_ref[sl, :] = jnp.concatenate([av_ref[0, sl, :], jnp.ones((rows, LANES), BF16)], axis=1)
            score_t_ref[sl, :] = jnp.full((rows, tq), -jnp.inf, F32)
            return 0
        lax.fori_loop(0, S // rows, body, 0)

    ca = taq_ref[0, :, :LANES] * (ATTN_HEAD_DIM ** -0.5)
    sa = taq_ref[0, :, LANES:] * (ATTN_HEAD_DIM ** -0.5)
    for h in range(N_ATTN_HEADS):
        x = aq_ref[0, :, h * LANES:(h + 1) * LANES]
        qa_ref[h * tq:(h + 1) * tq, :] = _rope(x, ca, sa, pa_ref[...]).astype(BF16)
    ci, si = tiq_ref[0, :, :LANES], tiq_ref[0, :, LANES:]
    for j in range(IDX_WIDTH // LANES):
        x = iq_ref[0, :, j * LANES:(j + 1) * LANES]
        qi_ref[:, j * LANES:(j + 1) * LANES] = _rope(x, ci, si, pi_ref[...]).astype(BF16)
    iw_b = jnp.dot(iwq_ref[0], ex_ref[...], preferred_element_type=F32)
    for h in range(N_IDX_HEADS):
        wb_ref[h] = iw_b[:, h * LANES:(h + 1) * LANES]

    col_in_chunk = lax.broadcasted_iota(jnp.int32, (tq, tk), 1)

    def lane_groups(x, op):
        r = x[:, :LANES]
        for g in range(1, reps):
            r = op(r, x[:, g * LANES:(g + 1) * LANES])
        return r

    def tile_lanes(x):
        return jnp.concatenate([x] * reps, axis=1) if reps > 1 else x

    key_t = lax.broadcasted_iota(jnp.int32, (tk, tq), 0)

    def fold8(x, op):
        x3 = x.reshape(tk // 8, 8, x.shape[1])
        return jnp.max(x3, axis=0) if op == "max" else (jnp.min(x3, axis=0) if op == "min" else jnp.sum(x3, axis=0))

    rb = min(LANES, tq)
    row_b = lax.broadcasted_iota(jnp.int32, (rb, tk), 0)
    col_b = lax.broadcasted_iota(jnp.int32, (rb, tk), 1)
    key_tb = lax.broadcasted_iota(jnp.int32, (tk, rb), 0)
    qry_tb = lax.broadcasted_iota(jnp.int32, (tk, rb), 1)

    def a_body(c, carry):
        rmax, rmin = carry
        c0 = pl.multiple_of(c * tk, tk)
        k0 = kd0_ref[pl.ds(c0, tk), :]
        k1 = kd1_ref[pl.ds(c0, tk), :]
        maxs, mins = [], []
        for r in range(tq // rb):
            rows = slice(r * rb, (r + 1) * rb)
            q0 = qi * tq + r * rb
            acc = jnp.zeros((rb, tk), F32)
            for j in range(IDX_WIDTH // LANES):
                qp = qi_ref[rows, j * LANES:(j + 1) * LANES]
                for par, kd in ((0, k0), (1, k1)):
                    logits = _dot_t(qp, kd)
                    acc = acc + tile_lanes(wb_ref[2 * j + par, rows, :]) * jnp.maximum(logits, 0.0)
            causal = (c0 + col_b) <= (q0 + row_b)
            score_ref[c, rows, :] = jnp.where(causal, acc, -jnp.inf)
            acc_t = acc.T
            causal_t = (c0 + key_tb) <= (q0 + qry_tb)
            sc_t = jnp.where(causal_t, acc_t, -jnp.inf)
            score_t_ref[pl.ds(c0, tk), rows] = sc_t
            maxs.append(fold8(sc_t, "max"))
            mins.append(fold8(jnp.where(causal_t, acc_t, jnp.inf), "min"))
        rmax = jnp.maximum(rmax, jnp.concatenate(maxs, axis=1))
        rmin = jnp.minimum(rmin, jnp.concatenate(mins, axis=1))
        return rmax, rmin

    rmax, rmin = lax.fori_loop(0, n_chunks, a_body,
                               (jnp.full((8, tq), -jnp.inf, F32), jnp.full((8, tq), jnp.inf, F32)))

    vec = (1, tq)
    kf = float(topk)
    n_keys = qi * tq + lax.broadcasted_iota(jnp.int32, vec, 1) + 1

    lo0 = jnp.min(rmin, axis=0, keepdims=True)
    hi0 = jnp.max(rmax, axis=0, keepdims=True)
    key_f = key_t.astype(F32)

    def threshold_search(n_rows):
        def count_t(pred):
            accs = [jnp.zeros((8, tq), F32) for _ in range(4)]
            i = 0
            for r0 in range(0, n_rows, tk):
                ind = jnp.where(pred(r0, score_t_ref[r0:r0 + tk, :]), 1.0, 0.0)
                for g in range(tk // 8):
                    accs[i % 4] = accs[i % 4] + ind[g * 8:(g + 1) * 8, :]
                    i += 1
            return jnp.sum((accs[0] + accs[1]) + (accs[2] + accs[3]), axis=0, keepdims=True)

        def b_cond(carry):
            return carry[1] > 0.0

        def b_body(carry):
            state = (carry[0],) + tuple(carry[2:])
            flag = jnp.max(1.0 - carry[5])
            for _ in range(steps_per_check):
                state = b_step(*state)
            return (state[0], flag) + tuple(state[1:])

        def b_step(it, lo, hi, chi, done, th, tie):
            mid = jnp.where(it == 0, hi, 0.5 * lo + 0.5 * hi)
            cnt = count_t(lambda r0, sc: sc >= mid)
            active = done == 0.0
            adjacent = jnp.logical_and(it > 0, jnp.logical_or(mid <= lo, mid >= hi))
            tie_now = jnp.logical_and(active, adjacent)
            live = jnp.logical_and(active, jnp.logical_not(adjacent))
            hit = jnp.logical_and(live, cnt == kf)
            above = jnp.logical_and(live, cnt > kf)
            below = jnp.logical_and(live, cnt < kf)
            th = jnp.where(tie_now, lo, jnp.where(hit, mid, th))
            tie = jnp.where(tie_now, 1.0, tie)
            lo = jnp.where(above, mid, lo)
            hi = jnp.where(below, mid, hi)
            chi = jnp.where(below, cnt, chi)
            done = jnp.where(jnp.logical_or(tie_now, hit), 1.0, done)
            return it + 1, lo, hi, chi, done, th, tie

        steps_per_check = 2 if n_rows <= 4 * tk else 1

        init = (jnp.int32(0), jnp.float32(1.0), lo0, hi0,
                jnp.zeros(vec, F32), jnp.where(n_keys <= topk, 1.0, 0.0),
                jnp.full(vec, THETA_ALL, F32), jnp.zeros(vec, F32))
        _, _, _, _, chi, _, th, tie = lax.while_loop(b_cond, b_body, init)

        def tie_cut():
            need = kf - chi

            def body(_, carry):
                lo_i, hi_i = carry
                mid = jnp.floor(0.5 * (lo_i + hi_i))
                cnt = count_t(lambda r0, sc: jnp.logical_and(sc == th, key_f + float(r0) <= mid))
                ge = cnt >= need
                return jnp.where(ge, lo_i, mid), jnp.where(ge, mid, hi_i)

            n_iter = int(math.ceil(math.log2(S))) + 1
            _, hi_i = lax.fori_loop(0, n_iter, body, (jnp.full(vec, -1.0, F32), jnp.full(vec, float(S - 1), F32)))
            return jnp.where(tie > 0.0, hi_i, CUT_ALL)

        th_ref[...] = th
        cut_ref[...] = lax.cond(jnp.max(tie) > 0.0, tie_cut, lambda: jnp.full(vec, CUT_ALL, F32))

    pair_rows = 2 * tk
    variant = (n_chunks * tk + pair_rows - 1) // pair_rows - 1
    for v in range(-(-S // pair_rows)):
        pl.when(variant == v)(functools.partial(threshold_search, min((v + 1) * pair_rows, S)))

    th_c = jnp.broadcast_to(th_ref[...], (LANES, tq)).T
    cut_c = jnp.broadcast_to(cut_ref[...], (LANES, tq)).T
    th_t = tile_lanes(th_c)
    cut_t = tile_lanes(cut_c)

    m_ref[...] = jnp.full(m_ref.shape, MASK_NEG, F32)
    colf_in_chunk = col_in_chunk.astype(F32)

    def c1_body(c, _):
        c0 = pl.multiple_of(c * tk, tk)
        sc = score_ref[c]
        colf = colf_in_chunk + (c * tk).astype(F32)
        sel = jnp.logical_or(sc > th_t, jnp.logical_and(sc == th_t, colf <= cut_t))
        bias = jnp.where(sel, 0.0, MASK_NEG)
        s = _dot_t(qa_ref[...], kr_ref[pl.ds(c0, tk), :])
        s = (s.reshape(N_ATTN_HEADS, tq, tk) + bias[None]).reshape(rows_all, tk)
        s_ref[c] = s
        m_ref[...] = jnp.maximum(m_ref[...], lane_groups(s, jnp.maximum))
        return 0

    lax.fori_loop(0, n_chunks, c1_body, 0)
    m_ref[...] = jnp.broadcast_to(jnp.max(m_ref[...], axis=1, keepdims=True), m_ref.shape)

    acc_ref[...] = jnp.zeros(acc_ref.shape, F32)

    def c2_body(c, _):
        c0 = pl.multiple_of(c * tk, tk)
        p = jnp.exp(s_ref[c] - tile_lanes(m_ref[...]))
        acc_ref[...] += _dot(p.astype(BF16), vaug_ref[pl.ds(c0, tk), :])
        return 0

    lax.fori_loop(0, n_chunks, c2_body, 0)
    out = acc_ref[:, :LANES] / acc_ref[:, LANES:]
    for h in range(N_ATTN_HEADS):
        o_ref[0, :, h * LANES:(h + 1) * LANES] = out[h * tq:(h + 1) * tq, :].astype(o_ref.dtype)


def _attention(z3, tab_a, tab_i, topk):
    B, S, _ = z3.shape
    tq = min(256, S)
    tk = min(256, S)
    qblk = lambda w, col: pl.BlockSpec((1, tq, w), lambda b, q: (b, q, col // w))
    kblk = lambda w, col: pl.BlockSpec((1, S, w), lambda b, q: (b, 0, col // w))
    tab_q = pl.BlockSpec((1, tq, 2 * LANES), lambda b, q: (b, q, 0))
    tab_k = pl.BlockSpec((1, S, 2 * LANES), lambda b, q: (b, 0, 0))
    rows = N_ATTN_HEADS * tq
    const = lambda a: pl.BlockSpec(a.shape, lambda b, q: (0, 0))
    perm_a = _swap_matrix(ATTN_HEAD_DIM, ATTN_ROT_HALF)
    perm_i = _swap_matrix(IDX_HEAD_DIM, IDX_ROT_HALF)
    idx_w_scale = (N_IDX_HEADS ** -0.5) * (IDX_HEAD_DIM ** -0.5)
    assert math.frexp(idx_w_scale)[0] == 0.5, "the folded scale must be a power of two to stay exact in bf16"
    expand = jnp.where(jnp.arange(LANES)[:, None] == IDX_HEAD_DIM + jnp.arange(N_IDX_HEADS * LANES)[None, :] // LANES,
                       idx_w_scale, 0.0).astype(BF16)
    return pl.pallas_call(
        functools.partial(_attn_kernel, topk=topk, tq=tq, tk=tk),
        grid=(B, S // tq),
        in_specs=[qblk(ATTN_WIDTH, COL_AQ), qblk(IDX_WIDTH, COL_IQ), qblk(LANES, COL_IKW),
                  kblk(LANES, COL_AK), kblk(LANES, COL_AV), kblk(LANES, COL_IKW),
                  tab_q, tab_k, tab_q, tab_k, const(perm_a), const(perm_i), const(expand)],
        out_specs=pl.BlockSpec((1, tq, ATTN_WIDTH), lambda b, q: (b, q, 0)),
        out_shape=jax.ShapeDtypeStruct((B, S, ATTN_WIDTH), BF16),
        scratch_shapes=[pltpu.VMEM((S, LANES), BF16),
                        pltpu.VMEM((S, LANES), BF16),
                        pltpu.VMEM((S, LANES), BF16),
                        pltpu.VMEM((S, 2 * LANES), BF16),
                        pltpu.VMEM((rows, LANES), BF16),
                        pltpu.VMEM((tq, IDX_WIDTH), BF16),
                        pltpu.VMEM((N_IDX_HEADS, tq, LANES), F32),
                        pltpu.VMEM((S // tk, tq, tk), F32),
                        pltpu.VMEM((S, tq), F32),
                        pltpu.VMEM((1, tq), F32),
                        pltpu.VMEM((1, tq), F32),
                        pltpu.VMEM((S // tk, rows, tk), F32),
                        pltpu.VMEM((rows, LANES), F32),
                        pltpu.VMEM((rows, 2 * LANES), F32)],
        compiler_params=_cparams(2),
        name="dsa_attention",
    )(z3, z3, z3, z3, z3, z3, tab_a, tab_a, tab_i, tab_i, perm_a, perm_i, expand)


def _ret_kernel(rq_ref, rk_ref, rv_ref, rg_ref, tab_ref, decay_ref, zeta_ref, xi_ref, gch_ref, gn_ref,
                o_ref, state_ref, *, chunk):
    S = rq_ref.shape[1]
    C = chunk
    half = RET_KEY_DIM
    state_ref[...] = jnp.zeros(state_ref.shape, F32)
    lane = lax.broadcasted_iota(jnp.int32, (C, LANES), 1)
    first = lane < half

    def body(c, _):
        r0 = pl.multiple_of(c * C, C)
        sl = pl.ds(r0, C)
        cos, sin = tab_ref[0, sl, :LANES], tab_ref[0, sl, LANES:]
        cos_k, sin_k = cos * (RET_KEY_DIM ** -0.5), sin * (RET_KEY_DIM ** -0.5)
        first_half = lane % RET_KEY_DIM < RET_KEY_DIM // 2

        def rope(x, c, s):
            partner = jnp.where(first_half, pltpu.roll(x, LANES - RET_KEY_DIM // 2, 1),
                                pltpu.roll(x, RET_KEY_DIM // 2, 1))
            return x * c + partner * s

        for j in range(N_RET_HEADS // 2):
            q = rope(rq_ref[0, sl, j * LANES:(j + 1) * LANES].astype(F32), cos, sin)
            k = rope(rk_ref[0, sl, j * LANES:(j + 1) * LANES].astype(F32), cos_k, sin_k)
            kzt = (k * zeta_ref[j]).T.astype(BF16)
            state = state_ref[j]
            state_b = state.astype(BF16)
            new_rows = []
            for par in range(2):
                h = 2 * j + par
                keep = first if par == 0 else jnp.logical_not(first)
                qh = jnp.where(keep, q, 0.0).astype(BF16)
                kh = jnp.where(keep, k, 0.0).astype(BF16)
                v = rv_ref[0, sl, h * LANES:(h + 1) * LANES]
                inner = _dot_t(qh, kh) * decay_ref[h]
                o = _dot(inner.astype(BF16), v) + _dot(qh, state_b) * xi_ref[h]
                new_rows.append(_dot(kzt[par * half:(par + 1) * half, :], v))
                mu = jnp.mean(o, axis=-1, keepdims=True)
                d = o - mu
                var = jnp.mean(d * d, axis=-1, keepdims=True)
                y = d * lax.rsqrt(var + GN_EPS) * gn_ref[:, h * LANES:(h + 1) * LANES]
                g = rg_ref[0, sl, h * LANES:(h + 1) * LANES].astype(F32)
                o_ref[0, sl, h * LANES:(h + 1) * LANES] = (y * (g * jax.nn.sigmoid(g))).astype(o_ref.dtype)
            state_ref[j] = gch_ref[j] * state + jnp.concatenate(new_rows, axis=0)
        return 0

    lax.fori_loop(0, S // C, body, 0)


def _retention(z3, tab_r, consts, gn_w, chunk):
    B, S, _ = z3.shape
    decay, zeta, xi, gch = consts
    blk = lambda w, col: pl.BlockSpec((1, S, w), lambda b: (b, 0, col // w))
    const = lambda a: pl.BlockSpec(a.shape, lambda b: (0,) * a.ndim)
    return pl.pallas_call(
        functools.partial(_ret_kernel, chunk=chunk),
        grid=(B,),
        in_specs=[blk(RET_KEY_WIDTH, COL_RQ), blk(RET_KEY_WIDTH, COL_RK), blk(RET_WIDTH, COL_RV),
                  blk(RET_WIDTH, COL_RG), pl.BlockSpec((1, S, 2 * LANES), lambda b: (b, 0, 0)),
                  const(decay), const(zeta), const(xi), const(gch), const(gn_w)],
        out_specs=pl.BlockSpec((1, S, RET_WIDTH), lambda b: (b, 0, 0)),
        out_shape=jax.ShapeDtypeStruct((B, S, RET_WIDTH), BF16),
        scratch_shapes=[pltpu.VMEM((N_RET_HEADS // 2, LANES, RET_VAL_DIM), F32)],
        compiler_params=_cparams(1),
        name="retention",
    )(z3, z3, z3, z3, tab_r, decay, zeta, xi, gch, gn_w)


def _retention_consts(chunk):
    C = chunk
    log_g = jnp.log(1.0 - 2.0 ** (-5.0 - jnp.arange(N_RET_HEADS, dtype=F32)))
    i = jnp.arange(C, dtype=F32)
    diff = i[:, None] - i[None, :]
    decay = jnp.where(diff[None] >= 0, jnp.exp(jnp.maximum(diff, 0.0)[None] * log_g[:, None, None]), 0.0)
    zeta = jnp.exp((C - 1.0 - i)[None, :] * log_g[:, None])
    xi = jnp.exp((i + 1.0)[None, :] * log_g[:, None])
    g_chunk = jnp.exp(C * log_g)
    pair = lambda a: a.reshape(N_RET_HEADS // 2, 2, -1)
    zeta_p = jnp.repeat(pair(zeta).transpose(0, 2, 1), RET_KEY_DIM, axis=2)
    xi_b = jnp.broadcast_to(xi[:, :, None], (N_RET_HEADS, C, RET_VAL_DIM))
    gch = jnp.broadcast_to(jnp.repeat(pair(g_chunk), RET_KEY_DIM, axis=1).reshape(N_RET_HEADS // 2, LANES, 1),
                           (N_RET_HEADS // 2, LANES, RET_VAL_DIM))
    return decay.astype(F32), zeta_p.astype(F32), xi_b.astype(F32), gch.astype(F32)


def _out_proj_kernel(attn_ref, ret_ref, h_ref, wa_ref, wr_ref, g1_ref, g2_ref, h1_ref, m_ref, *, sub):
    for r0 in range(0, h_ref.shape[0], sub):
        sl = slice(r0, r0 + sub)
        mix = _dot(attn_ref[sl, :], wa_ref[...]) + _dot(ret_ref[sl, :], wr_ref[...])
        h1 = h_ref[sl, :] + _rms(mix, g1_ref[...])
        h1_ref[sl, :] = h1
        m_ref[sl, :] = _rms(h1, g2_ref[...]).astype(m_ref.dtype)


def _out_proj(attn, ret, h, wo, g1, g2):
    T, D = h.shape
    tm = min(512, T)
    wa_rows, wr_rows = attn.shape[1], ret.shape[1]
    assert wa_rows == wr_rows and wo.shape[0] == wa_rows + wr_rows
    row = lambda w: pl.BlockSpec((tm, w), lambda i: (i, 0))
    const = lambda a: pl.BlockSpec(a.shape, lambda i: (0, 0))
    return pl.pallas_call(
        functools.partial(_out_proj_kernel, sub=min(256, tm)),
        grid=(T // tm,),
        in_specs=[row(wa_rows), row(wr_rows), row(D),
                  pl.BlockSpec((wa_rows, D), lambda i: (0, 0)), pl.BlockSpec((wr_rows, D), lambda i: (1, 0)),
                  const(g1), const(g2)],
        out_specs=[row(D), row(D)],
        out_shape=[jax.ShapeDtypeStruct((T, D), F32), jax.ShapeDtypeStruct((T, D), BF16)],
        compiler_params=_cparams(1),
        name="out_proj",
    )(attn, ret, h, wo, wo, g1, g2)


def _ffn_kernel(m_ref, w1_ref, w2_ref, h_ref, g_ref, o_ref, acc_ref):
    f = pl.program_id(1)

    @pl.when(f == 0)
    def _():
        acc_ref[...] = jnp.zeros(acc_ref.shape, F32)

    u = jnp.maximum(_dot(m_ref[...], w1_ref[...]), 0.0)
    acc_ref[...] += _dot((u * u).astype(BF16), w2_ref[...])

    @pl.when(f == pl.num_programs(1) - 1)
    def _():
        o_ref[...] = h_ref[...] + _rms(acc_ref[...], g_ref[...])


def _ffn(m, w1, w2, h, g):
    T, D = h.shape
    F = w1.shape[1]
    tm = min(512, T)
    tf = min(1024, F)
    return pl.pallas_call(
        _ffn_kernel,
        grid=(T // tm, F // tf),
        in_specs=[pl.BlockSpec((tm, D), lambda i, f: (i, 0)),
                  pl.BlockSpec((D, tf), lambda i, f: (0, f)),
                  pl.BlockSpec((tf, D), lambda i, f: (f, 0)),
                  pl.BlockSpec((tm, D), lambda i, f: (i, 0)),
                  pl.BlockSpec((1, D), lambda i, f: (0, 0))],
        out_specs=pl.BlockSpec((tm, D), lambda i, f: (i, 0)),
        out_shape=jax.ShapeDtypeStruct((T, D), F32),
        scratch_shapes=[pltpu.VMEM((tm, D), F32)],
        compiler_params=_cparams(2),
        name="ffn",
    )(m, w1, w2, h, g)


def _ple_kernel(h_ref, p_ref, wg_ref, wp_ref, g_ref, o_ref, *, sub):
    for r0 in range(0, h_ref.shape[0], sub):
        sl = slice(r0, r0 + sub)
        h = h_ref[sl, :]
        gate = jax.nn.sigmoid(_dot(h.astype(BF16), wg_ref[...]))
        e = _dot(p_ref[sl, :].astype(BF16), wp_ref[...])
        o_ref[sl, :] = h + _rms(gate * e, g_ref[...])


def _ple(h, p, layer, wg, wp, g):
    T, D = h.shape
    tm = min(512, T)
    row = lambda w: pl.BlockSpec((tm, w), lambda i: (i, 0))
    const = lambda a: pl.BlockSpec(a.shape, lambda i: (0, 0))
    return pl.pallas_call(
        functools.partial(_ple_kernel, sub=min(256, tm)),
        grid=(T // tm,),
        in_specs=[row(D), pl.BlockSpec((None, tm, p.shape[2]), lambda i: (layer, i, 0)),
                  const(wg), const(wp), const(g)],
        out_specs=row(D),
        out_shape=jax.ShapeDtypeStruct((T, D), F32),
        compiler_params=_cparams(1),
        name="ple",
    )(h, p, wg, wp, g)


def _rope_table(positions, half, theta, group):
    inv = theta ** (-jnp.arange(half, dtype=F32) / half)
    ang = positions.astype(F32)[..., None] * inv
    compact = jnp.concatenate([jnp.cos(ang), jnp.sin(ang), jnp.ones_like(ang[..., :1])], axis=-1)
    spread = np.zeros((2 * half + 1, 2 * LANES), np.float32)
    for lane in range(LANES):
        d = lane % group
        if d < 2 * half:
            spread[d % half, lane] = 1.0
            spread[half + d % half, LANES + lane] = -1.0 if d < half else 1.0
        else:
            spread[2 * half, lane] = 1.0
    return jnp.einsum("bsk,kn->bsn", compact, jnp.asarray(spread), precision=lax.Precision.HIGHEST)


def _reorder_w_in(w):
    w = w.astype(BF16)
    a0 = ATTN_WIDTH
    a1 = a0 + 2 * ATTN_HEAD_DIM
    a2 = a1 + IDX_WIDTH
    a3 = a2 + IDX_HEAD_DIM + N_IDX_HEADS
    a4 = a3 + 2 * RET_KEY_WIDTH
    pad = jnp.zeros((w.shape[0], Z_WIDTH - Z_USED + LANES - IDX_HEAD_DIM - N_IDX_HEADS), BF16)
    return jnp.concatenate([w[:, :a0], w[:, a1:a2], w[:, a4:], w[:, a3:a4], w[:, a0:a1], w[:, a2:a3], pad], axis=1)


def kernel(x, p, positions, w_in, w_out, w_ff1, w_ff2, w_ple, w_ple_gate, pre_mix_norm, post_mix_norm,
           pre_ff_norm, post_ff_norm, ple_norm, ret_gn):
    B, S, D = x.shape
    depth = w_in.shape[0]
    T = B * S
    topk = min(TOPK_MAX, S // 4)
    ret_chunk = min(256, S)
    tab_a = _rope_table(positions, ATTN_ROT_HALF, ROPE_THETA, ATTN_HEAD_DIM)
    tab_i = _rope_table(positions, IDX_ROT_HALF, ROPE_THETA, IDX_HEAD_DIM)
    tab_r = _rope_table(positions, RET_KEY_DIM // 2, RET_THETA, RET_KEY_DIM)
    ret_consts = _retention_consts(ret_chunk)
    vec = lambda a: a.reshape(1, -1).astype(F32)

    h = x.reshape(T, D)
    for i in range(depth):
        z = _norm_proj(h, vec(pre_mix_norm[i]), _reorder_w_in(w_in[i]))
        z3 = z.reshape(B, S, Z_WIDTH)
        attn = _attention(z3, tab_a, tab_i, topk)
        ret = _retention(z3, tab_r, ret_consts, vec(ret_gn[i]), ret_chunk)
        h, m = _out_proj(attn.reshape(T, ATTN_WIDTH), ret.reshape(T, RET_WIDTH), h,
                         w_out[i].astype(BF16), vec(post_mix_norm[i]), vec(pre_ff_norm[i]))
        h = _ffn(m, w_ff1[i].astype(BF16), w_ff2[i].astype(BF16), h, vec(post_ff_norm[i]))
        h = _ple(h, p.reshape(depth, T, -1), i, w_ple_gate[i].astype(BF16), w_ple[i].astype(BF16),
                 vec(ple_norm[i]))
    return h.reshape(B, S, D)
```

```python
import functools
import math

import jax
import jax.numpy as jnp
import numpy as np
from jax import lax
from jax.experimental import pallas as pl
from jax.experimental.pallas import tpu as pltpu

N_ATTN_HEADS = 8
ATTN_HEAD_DIM = 128
ATTN_ROT_HALF = 16
ROPE_THETA = 500000.0
TOPK_MAX = 256
N_IDX_HEADS = 16
IDX_HEAD_DIM = 64
IDX_ROT_HALF = 8
N_RET_HEADS = 8
RET_KEY_DIM = 64
RET_VAL_DIM = 128
RET_THETA = 10000.0
ATTN_WIDTH = N_ATTN_HEADS * ATTN_HEAD_DIM
IDX_WIDTH = N_IDX_HEADS * IDX_HEAD_DIM
RET_KEY_WIDTH = N_RET_HEADS * RET_KEY_DIM
RET_WIDTH = N_RET_HEADS * RET_VAL_DIM
RMS_EPS = 1e-6
GN_EPS = 1e-5

LANES = 128
COL_AQ = 0
COL_IQ = COL_AQ + ATTN_WIDTH
COL_RV = COL_IQ + IDX_WIDTH
COL_RG = COL_RV + RET_WIDTH
COL_RQ = COL_RG + RET_WIDTH
COL_RK = COL_RQ + RET_KEY_WIDTH
COL_AK = COL_RK + RET_KEY_WIDTH
COL_AV = COL_AK + ATTN_HEAD_DIM
COL_IKW = COL_AV + ATTN_HEAD_DIM
Z_USED = COL_IKW + LANES
Z_TILE_N = 512
Z_WIDTH = -(-Z_USED // Z_TILE_N) * Z_TILE_N

MASK_NEG = -1e30
THETA_ALL = -3e38
CUT_ALL = 1e9
VMEM_LIMIT = 56 * 1024 * 1024

BF16 = jnp.bfloat16
F32 = jnp.float32


def _cparams(n_grid):
    return pltpu.CompilerParams(dimension_semantics=("arbitrary",) * n_grid,
                                vmem_limit_bytes=VMEM_LIMIT)


def _rms(x, gain):
    ms = jnp.mean(x * x, axis=-1, keepdims=True)
    return x * lax.rsqrt(ms + RMS_EPS) * gain


def _rope(xb, cos, sin, swap):
    return xb.astype(F32) * cos + jnp.dot(xb, swap, preferred_element_type=F32) * sin


def _swap_matrix(group, half):
    l = jnp.arange(LANES)
    d = l % group
    partner = jnp.where(d < half, l + half, jnp.where(d < 2 * half, l - half, -1))
    return (jnp.arange(LANES)[:, None] == partner[None, :]).astype(BF16)


def _dot_t(a, b):
    return lax.dot_general(a, b, (((1,), (1,)), ((), ())), preferred_element_type=F32)


def _dot(a, b):
    return jnp.dot(a, b, preferred_element_type=F32)


def _norm_proj_kernel(h_ref, g_ref, w_ref, z_ref, *, sub, tn):
    for r0 in range(0, h_ref.shape[0], sub):
        sl = slice(r0, r0 + sub)
        a = _rms(h_ref[sl, :], g_ref[...]).astype(BF16)
        for c0 in range(0, w_ref.shape[1], tn):
            z_ref[sl, c0:c0 + tn] = _dot(a, w_ref[:, c0:c0 + tn]).astype(z_ref.dtype)


def _norm_proj(h, gain, w):
    T, D = h.shape
    N = w.shape[1]
    tm = min(512, T)
    return pl.pallas_call(
        functools.partial(_norm_proj_kernel, sub=min(256, tm), tn=Z_TILE_N),
        grid=(T // tm,),
        in_specs=[pl.BlockSpec((tm, D), lambda i: (i, 0)),
                  pl.BlockSpec((1, D), lambda i: (0, 0)),
                  pl.BlockSpec((D, N), lambda i: (0, 0), pipeline_mode=pl.Buffered(1))],
        out_specs=pl.BlockSpec((tm, N), lambda i: (i, 0)),
        out_shape=jax.ShapeDtypeStruct((T, N), BF16),
        compiler_params=_cparams(1),
        name="norm_proj",
    )(h, gain, w)


def _attn_kernel(aq_ref, iq_ref, iwq_ref, ak_ref, av_ref, ikw_ref,
                 taq_ref, tak_ref, tiq_ref, tik_ref, pa_ref, pi_ref, ex_ref, o_ref,
                 kr_ref, kd0_ref, kd1_ref, vaug_ref, qa_ref, qi_ref, wb_ref, score_ref, score_t_ref,
                 th_ref, cut_ref, s_ref, m_ref, acc_ref, *, topk, tq, tk):
    S = ak_ref.shape[1]
    qi = pl.program_id(1)
    n_chunks = (qi * tq + tq + tk - 1) // tk
    reps = tk // LANES
    rows_all = N_ATTN_HEADS * tq

    @pl.when(qi == 0)
    def _():
        rows = min(256, S)

        def body(r, _):
            r0 = pl.multiple_of(r * rows, rows)
            sl = pl.ds(r0, rows)
            kr_ref[sl, :] = _rope(ak_ref[0, sl, :], tak_ref[0, sl, :LANES], tak_ref[0, sl, LANES:],
                                  pa_ref[...]).astype(BF16)
            lane_r = lax.broadcasted_iota(jnp.int32, (rows, LANES), 1)
            ik = _rope(ikw_ref[0, sl, :], tik_ref[0, sl, :LANES], tik_ref[0, sl, LANES:], pi_ref[...])
            ik = jnp.where(lane_r < IDX_HEAD_DIM, ik, 0.0)
            kd0_ref[sl, :] = ik.astype(BF16)
            kd1_ref[sl, :] = pltpu.roll(ik, IDX_HEAD_DIM, 1).astype(BF16)
            vaug_ref[sl, :] = jnp.concatenate([av_ref[0, sl, :], jnp.ones((rows, LANES), BF16)], axis=1)
            score_t_ref[sl, :] = jnp.full((rows, tq), -jnp.inf, F32)
            return 0
        lax.fori_loop(0, S // rows, body, 0)

    ca = taq_ref[0, :, :LANES] * (ATTN_HEAD_DIM ** -0.5)
    sa = taq_ref[0, :, LANES:] * (ATTN_HEAD_DIM ** -0.5)
    for h in range(N_ATTN_HEADS):
        x = aq_ref[0, :, h * LANES:(h + 1) * LANES]
        qa_ref[h * tq:(h + 1) * tq, :] = _rope(x, ca, sa, pa_ref[...]).astype(BF16)
    ci, si = tiq_ref[0, :, :LANES], tiq_ref[0, :, LANES:]
    for j in range(IDX_WIDTH // LANES):
        x = iq_ref[0, :, j * LANES:(j + 1) * LANES]
        qi_ref[:, j * LANES:(j + 1) * LANES] = _rope(x, ci, si, pi_ref[...]).astype(BF16)
    iw_b = jnp.dot(iwq_ref[0], ex_ref[...], preferred_element_type=F32)
    for h in range(N_IDX_HEADS):
        wb_ref[h] = iw_b[:, h * LANES:(h + 1) * LANES]

    col_in_chunk = lax.broadcasted_iota(jnp.int32, (tq, tk), 1)

    def lane_groups(x, op):
        r = x[:, :LANES]
        for g in range(1, reps):
            r = op(r, x[:, g * LANES:(g + 1) * LANES])
        return r

    def tile_lanes(x):
        return jnp.concatenate([x] * reps, axis=1) if reps > 1 else x

    key_t = lax.broadcasted_iota(jnp.int32, (tk, tq), 0)

    def fold8(x, op):
        x3 = x.reshape(tk // 8, 8, x.shape[1])
        return jnp.max(x3, axis=0) if op == "max" else (jnp.min(x3, axis=0) if op == "min" else jnp.sum(x3, axis=0))

    rb = min(LANES, tq)
    row_b = lax.broadcasted_iota(jnp.int32, (rb, tk), 0)
    col_b = lax.broadcasted_iota(jnp.int32, (rb, tk), 1)
    key_tb = lax.broadcasted_iota(jnp.int32, (tk, rb), 0)
    qry_tb = lax.broadcasted_iota(jnp.int32, (tk, rb), 1)

    def a_body(c, carry):
        rmax, rmin = carry
        c0 = pl.multiple_of(c * tk, tk)
        k0 = kd0_ref[pl.ds(c0, tk), :]
        k1 = kd1_ref[pl.ds(c0, tk), :]
        maxs, mins = [], []
        for r in range(tq // rb):
            rows = slice(r * rb, (r + 1) * rb)
            q0 = qi * tq + r * rb
            acc = jnp.zeros((rb, tk), F32)
            for j in range(IDX_WIDTH // LANES):
                qp = qi_ref[rows, j * LANES:(j + 1) * LANES]
                for par, kd in ((0, k0), (1, k1)):
                    logits = _dot_t(qp, kd)
                    acc = acc + tile_lanes(wb_ref[2 * j + par, rows, :]) * jnp.maximum(logits, 0.0)
            causal = (c0 + col_b) <= (q0 + row_b)
            score_ref[c, rows, :] = jnp.where(causal, acc, -jnp.inf)
            acc_t = acc.T
            causal_t = (c0 + key_tb) <= (q0 + qry_tb)
            sc_t = jnp.where(causal_t, acc_t, -jnp.inf)
            score_t_ref[pl.ds(c0, tk), rows] = sc_t
            maxs.append(fold8(sc_t, "max"))
            mins.append(fold8(jnp.where(causal_t, acc_t, jnp.inf), "min"))
        rmax = jnp.maximum(rmax, jnp.concatenate(maxs, axis=1))
        rmin = jnp.minimum(rmin, jnp.concatenate(mins, axis=1))
        return rmax, rmin

    rmax, rmin = lax.fori_loop(0, n_chunks, a_body,
                               (jnp.full((8, tq), -jnp.inf, F32), jnp.full((8, tq), jnp.inf, F32)))

    vec = (1, tq)
    kf = float(topk)
    n_keys = qi * tq + lax.broadcasted_iota(jnp.int32, vec, 1) + 1

    lo0 = jnp.min(rmin, axis=0, keepdims=True)
    hi0 = jnp.max(rmax, axis=0, keepdims=True)
    key_f = key_t.astype(F32)

    def threshold_search(n_rows):
        def count_t(pred):
            accs = [jnp.zeros((8, tq), F32) for _ in range(4)]
            i = 0
            for r0 in range(0, n_rows, tk):
                ind = jnp.where(pred(r0, score_t_ref[r0:r0 + tk, :]), 1.0, 0.0)
                for g in range(tk // 8):
                    accs[i % 4] = accs[i % 4] + ind[g * 8:(g + 1) * 8, :]
                    i += 1
            return jnp.sum((accs[0] + accs[1]) + (accs[2] + accs[3]), axis=0, keepdims=True)

        def b_cond(carry):
            return carry[1] > 0.0

        def b_body(carry):
            state = (carry[0],) + tuple(carry[2:])
            flag = jnp.max(1.0 - carry[5])
            for _ in range(steps_per_check):
                state = b_step(*state)
            return (state[0], flag) + tuple(state[1:])

        def b_step(it, lo, hi, chi, done, th, tie):
            mid = jnp.where(it == 0, hi, 0.5 * lo + 0.5 * hi)
            cnt = count_t(lambda r0, sc: sc >= mid)
            active = done == 0.0
            adjacent = jnp.logical_and(it > 0, jnp.logical_or(mid <= lo, mid >= hi))
            tie_now = jnp.logical_and(active, adjacent)
            live = jnp.logical_and(active, jnp.logical_not(adjacent))
            hit = jnp.logical_and(live, cnt == kf)
            above = jnp.logical_and(live, cnt > kf)
            below = jnp.logical_and(live, cnt < kf)
            th = jnp.where(tie_now, lo, jnp.where(hit, mid, th))
            tie = jnp.where(tie_now, 1.0, tie)
            lo = jnp.where(above, mid, lo)
            hi = jnp.where(below, mid, hi)
            chi = jnp.where(below, cnt, chi)
            done = jnp.where(jnp.logical_or(tie_now, hit), 1.0, done)
            return it + 1, lo, hi, chi, done, th, tie

        steps_per_check = 2 if n_rows <= 4 * tk else 1

        init = (jnp.int32(0), jnp.float32(1.0), lo0, hi0,
                jnp.zeros(vec, F32), jnp.where(n_keys <= topk, 1.0, 0.0),
                jnp.full(vec, THETA_ALL, F32), jnp.zeros(vec, F32))
        _, _, _, _, chi, _, th, tie = lax.while_loop(b_cond, b_body, init)

        def tie_cut():
            need = kf - chi

            def body(_, carry):
                lo_i, hi_i = carry
                mid = jnp.floor(0.5 * (lo_i + hi_i))
                cnt = count_t(lambda r0, sc: jnp.logical_and(sc == th, key_f + float(r0) <= mid))
                ge = cnt >= need
                return jnp.where(ge, lo_i, mid), jnp.where(ge, mid, hi_i)

            n_iter = int(math.ceil(math.log2(S))) + 1
            _, hi_i = lax.fori_loop(0, n_iter, body, (jnp.full(vec, -1.0, F32), jnp.full(vec, float(S - 1), F32)))
            return jnp.where(tie > 0.0, hi_i, CUT_ALL)

        th_ref[...] = th
        cut_ref[...] = lax.cond(jnp.max(tie) > 0.0, tie_cut, lambda: jnp.full(vec, CUT_ALL, F32))

    pair_rows = 2 * tk
    variant = (n_chunks * tk + pair_rows - 1) // pair_rows - 1
    for v in range(-(-S // pair_rows)):
        pl.when(variant == v)(functools.partial(threshold_search, min((v + 1) * pair_rows, S)))

    th_c = jnp.broadcast_to(th_ref[...], (LANES, tq)).T
    cut_c = jnp.broadcast_to(cut_ref[...], (LANES, tq)).T
    th_t = tile_lanes(th_c)
    cut_t = tile_lanes(cut_c)

    m_ref[...] = jnp.full(m_ref.shape, MASK_NEG, F32)
    colf_in_chunk = col_in_chunk.astype(F32)

    def c1_body(c, _):
        c0 = pl.multiple_of(c * tk, tk)
        sc = score_ref[c]
        colf = colf_in_chunk + (c * tk).astype(F32)
        sel = jnp.logical_or(sc > th_t, jnp.logical_and(sc == th_t, colf <= cut_t))
        bias = jnp.where(sel, 0.0, MASK_NEG)
        s = _dot_t(qa_ref[...], kr_ref[pl.ds(c0, tk), :])
        s = (s.reshape(N_ATTN_HEADS, tq, tk) + bias[None]).reshape(rows_all, tk)
        s_ref[c] = s
        m_ref[...] = jnp.maximum(m_ref[...], lane_groups(s, jnp.maximum))
        return 0

    lax.fori_loop(0, n_chunks, c1_body, 0)
    m_ref[...] = jnp.broadcast_to(jnp.max(m_ref[...], axis=1, keepdims=True), m_ref.shape)

    acc_ref[...] = jnp.zeros(acc_ref.shape, F32)

    def c2_body(c, _):
        c0 = pl.multiple_of(c * tk, tk)
        p = jnp.exp(s_ref[c] - tile_lanes(m_ref[...]))
        acc_ref[...] += _dot(p.astype(BF16), vaug_ref[pl.ds(c0, tk), :])
        return 0

    lax.fori_loop(0, n_chunks, c2_body, 0)
    out = acc_ref[:, :LANES] / acc_ref[:, LANES:]
    for h in range(N_ATTN_HEADS):
        o_ref[0, :, h * LANES:(h + 1) * LANES] = out[h * tq:(h + 1) * tq, :].astype(o_ref.dtype)


def _attention(z3, tab_a, tab_i, topk):
    B, S, _ = z3.shape
    tq = min(256, S)
    tk = min(256, S)
    qblk = lambda w, col: pl.BlockSpec((1, tq, w), lambda b, q: (b, q, col // w))
    kblk = lambda w, col: pl.BlockSpec((1, S, w), lambda b, q: (b, 0, col // w))
    tab_q = pl.BlockSpec((1, tq, 2 * LANES), lambda b, q: (b, q, 0))
    tab_k = pl.BlockSpec((1, S, 2 * LANES), lambda b, q: (b, 0, 0))
    rows = N_ATTN_HEADS * tq
    const = lambda a: pl.BlockSpec(a.shape, lambda b, q: (0, 0))
    perm_a = _swap_matrix(ATTN_HEAD_DIM, ATTN_ROT_HALF)
    perm_i = _swap_matrix(IDX_HEAD_DIM, IDX_ROT_HALF)
    idx_w_scale = (N_IDX_HEADS ** -0.5) * (IDX_HEAD_DIM ** -0.5)
    assert math.frexp(idx_w_scale)[0] == 0.5, "the folded scale must be a power of two to stay exact in bf16"
    expand = jnp.where(jnp.arange(LANES)[:, None] == IDX_HEAD_DIM + jnp.arange(N_IDX_HEADS * LANES)[None, :] // LANES,
                       idx_w_scale, 0.0).astype(BF16)
    return pl.pallas_call(
        functools.partial(_attn_kernel, topk=topk, tq=tq, tk=tk),
        grid=(B, S // tq),
        in_specs=[qblk(ATTN_WIDTH, COL_AQ), qblk(IDX_WIDTH, COL_IQ), qblk(LANES, COL_IKW),
                  kblk(LANES, COL_AK), kblk(LANES, COL_AV), kblk(LANES, COL_IKW),
                  tab_q, tab_k, tab_q, tab_k, const(perm_a), const(perm_i), const(expand)],
        out_specs=pl.BlockSpec((1, tq, ATTN_WIDTH), lambda b, q: (b, q, 0)),
        out_shape=jax.ShapeDtypeStruct((B, S, ATTN_WIDTH), BF16),
        scratch_shapes=[pltpu.VMEM((S, LANES), BF16),
                        pltpu.VMEM((S, LANES), BF16),
                        pltpu.VMEM((S, LANES), BF16),
                        pltpu.VMEM((S, 2 * LANES), BF16),
                        pltpu.VMEM((rows, LANES), BF16),
                        pltpu.VMEM((tq, IDX_WIDTH), BF16),
                        pltpu.VMEM((N_IDX_HEADS, tq, LANES), F32),
                        pltpu.VMEM((S // tk, tq, tk), F32),
                        pltpu.VMEM((S, tq), F32),
                        pltpu.VMEM((1, tq), F32),
                        pltpu.VMEM((1, tq), F32),
                        pltpu.VMEM((S // tk, rows, tk), F32),
                        pltpu.VMEM((rows, LANES), F32),
                        pltpu.VMEM((rows, 2 * LANES), F32)],
        compiler_params=_cparams(2),
        name="dsa_attention",
    )(z3, z3, z3, z3, z3, z3, tab_a, tab_a, tab_i, tab_i, perm_a, perm_i, expand)


def _ret_kernel(rq_ref, rk_ref, rv_ref, rg_ref, tab_ref, decay_ref, zeta_ref, xi_ref, gch_ref, gn_ref,
                o_ref, state_ref, *, chunk):
    S = rq_ref.shape[1]
    C = chunk
    half = RET_KEY_DIM
    state_ref[...] = jnp.zeros(state_ref.shape, F32)
    lane = lax.broadcasted_iota(jnp.int32, (C, LANES), 1)
    first = lane < half

    def body(c, _):
        r0 = pl.multiple_of(c * C, C)
        sl = pl.ds(r0, C)
        cos, sin = tab_ref[0, sl, :LANES], tab_ref[0, sl, LANES:]
        cos_k, sin_k = cos * (RET_KEY_DIM ** -0.5), sin * (RET_KEY_DIM ** -0.5)
        first_half = lane % RET_KEY_DIM < RET_KEY_DIM // 2

        def rope(x, c, s):
            partner = jnp.where(first_half, pltpu.roll(x, LANES - RET_KEY_DIM // 2, 1),
                                pltpu.roll(x, RET_KEY_DIM // 2, 1))
            return x * c + partner * s

        for j in range(N_RET_HEADS // 2):
            q = rope(rq_ref[0, sl, j * LANES:(j + 1) * LANES].astype(F32), cos, sin)
            k = rope(rk_ref[0, sl, j * LANES:(j + 1) * LANES].astype(F32), cos_k, sin_k)
            kzt = (k * zeta_ref[j]).T.astype(BF16)
            state = state_ref[j]
            state_b = state.astype(BF16)
            new_rows = []
            for par in range(2):
                h = 2 * j + par
                keep = first if par == 0 else jnp.logical_not(first)
                qh = jnp.where(keep, q, 0.0).astype(BF16)
                kh = jnp.where(keep, k, 0.0).astype(BF16)
                v = rv_ref[0, sl, h * LANES:(h + 1) * LANES]
                inner = _dot_t(qh, kh) * decay_ref[h]
                o = _dot(inner.astype(BF16), v) + _dot(qh, state_b) * xi_ref[h]
                new_rows.append(_dot(kzt[par * half:(par + 1) * half, :], v))
                mu = jnp.mean(o, axis=-1, keepdims=True)
                d = o - mu
                var = jnp.mean(d * d, axis=-1, keepdims=True)
                y = d * lax.rsqrt(var + GN_EPS) * gn_ref[:, h * LANES:(h + 1) * LANES]
                g = rg_ref[0, sl, h * LANES:(h + 1) * LANES].astype(F32)
                o_ref[0, sl, h * LANES:(h + 1) * LANES] = (y * (g * jax.nn.sigmoid(g))).astype(o_ref.dtype)
            state_ref[j] = gch_ref[j] * state + jnp.concatenate(new_rows, axis=0)
        return 0

    lax.fori_loop(0, S // C, body, 0)


def _retention(z3, tab_r, consts, gn_w, chunk):
    B, S, _ = z3.shape
    decay, zeta, xi, gch = consts
    blk = lambda w, col: pl.BlockSpec((1, S, w), lambda b: (b, 0, col // w))
    const = lambda a: pl.BlockSpec(a.shape, lambda b: (0,) * a.ndim)
    return pl.pallas_call(
        functools.partial(_ret_kernel, chunk=chunk),
        grid=(B,),
        in_specs=[blk(RET_KEY_WIDTH, COL_RQ), blk(RET_KEY_WIDTH, COL_RK), blk(RET_WIDTH, COL_RV),
                  blk(RET_WIDTH, COL_RG), pl.BlockSpec((1, S, 2 * LANES), lambda b: (b, 0, 0)),
                  const(decay), const(zeta), const(xi), const(gch), const(gn_w)],
        out_specs=pl.BlockSpec((1, S, RET_WIDTH), lambda b: (b, 0, 0)),
        out_shape=jax.ShapeDtypeStruct((B, S, RET_WIDTH), BF16),
        scratch_shapes=[pltpu.VMEM((N_RET_HEADS // 2, LANES, RET_VAL_DIM), F32)],
        compiler_params=_cparams(1),
        name="retention",
    )(z3, z3, z3, z3, tab_r, decay, zeta, xi, gch, gn_w)


def _retention_consts(chunk):
    C = chunk
    log_g = jnp.log(1.0 - 2.0 ** (-5.0 - jnp.arange(N_RET_HEADS, dtype=F32)))
    i = jnp.arange(C, dtype=F32)
    diff = i[:, None] - i[None, :]
    decay = jnp.where(diff[None] >= 0, jnp.exp(jnp.maximum(diff, 0.0)[None] * log_g[:, None, None]), 0.0)
    zeta = jnp.exp((C - 1.0 - i)[None, :] * log_g[:, None])
    xi = jnp.exp((i + 1.0)[None, :] * log_g[:, None])
    g_chunk = jnp.exp(C * log_g)
    pair = lambda a: a.reshape(N_RET_HEADS // 2, 2, -1)
    zeta_p = jnp.repeat(pair(zeta).transpose(0, 2, 1), RET_KEY_DIM, axis=2)
    xi_b = jnp.broadcast_to(xi[:, :, None], (N_RET_HEADS, C, RET_VAL_DIM))
    gch = jnp.broadcast_to(jnp.repeat(pair(g_chunk), RET_KEY_DIM, axis=1).reshape(N_RET_HEADS // 2, LANES, 1),
                           (N_RET_HEADS // 2, LANES, RET_VAL_DIM))
    return decay.astype(F32), zeta_p.astype(F32), xi_b.astype(F32), gch.astype(F32)


def _out_proj_kernel(attn_ref, ret_ref, h_ref, wa_ref, wr_ref, g1_ref, g2_ref, h1_ref, m_ref, *, sub):
    for r0 in range(0, h_ref.shape[0], sub):
        sl = slice(r0, r0 + sub)
        mix = _dot(attn_ref[sl, :], wa_ref[...]) + _dot(ret_ref[sl, :], wr_ref[...])
        h1 = h_ref[sl, :] + _rms(mix, g1_ref[...])
        h1_ref[sl, :] = h1
        m_ref[sl, :] = _rms(h1, g2_ref[...]).astype(m_ref.dtype)


def _out_proj(attn, ret, h, wo, g1, g2):
    T, D = h.shape
    tm = min(512, T)
    wa_rows, wr_rows = attn.shape[1], ret.shape[1]
    assert wa_rows == wr_rows and wo.shape[0] == wa_rows + wr_rows
    row = lambda w: pl.BlockSpec((tm, w), lambda i: (i, 0))
    const = lambda a: pl.BlockSpec(a.shape, lambda i: (0, 0))
    return pl.pallas_call(
        functools.partial(_out_proj_kernel, sub=min(256, tm)),
        grid=(T // tm,),
        in_specs=[row(wa_rows), row(wr_rows), row(D),
                  pl.BlockSpec((wa_rows, D), lambda i: (0, 0)), pl.BlockSpec((wr_rows, D), lambda i: (1, 0)),
                  const(g1), const(g2)],
        out_specs=[row(D), row(D)],
        out_shape=[jax.ShapeDtypeStruct((T, D), F32), jax.ShapeDtypeStruct((T, D), BF16)],
        compiler_params=_cparams(1),
        name="out_proj",
    )(attn, ret, h, wo, wo, g1, g2)


def _ffn_kernel(m_ref, w1_ref, w2_ref, h_ref, g_ref, o_ref, acc_ref):
    f = pl.program_id(1)

    @pl.when(f == 0)
    def _():
        acc_ref[...] = jnp.zeros(acc_ref.shape, F32)

    u = jnp.maximum(_dot(m_ref[...], w1_ref[...]), 0.0)
    acc_ref[...] += _dot((u * u).astype(BF16), w2_ref[...])

    @pl.when(f == pl.num_programs(1) - 1)
    def _():
        o_ref[...] = h_ref[...] + _rms(acc_ref[...], g_ref[...])


def _ffn(m, w1, w2, h, g):
    T, D = h.shape
    F = w1.shape[1]
    tm = min(512, T)
    tf = min(1024, F)
    return pl.pallas_call(
        _ffn_kernel,
        grid=(T // tm, F // tf),
        in_specs=[pl.BlockSpec((tm, D), lambda i, f: (i, 0)),
                  pl.BlockSpec((D, tf), lambda i, f: (0, f)),
                  pl.BlockSpec((tf, D), lambda i, f: (f, 0)),
                  pl.BlockSpec((tm, D), lambda i, f: (i, 0)),
                  pl.BlockSpec((1, D), lambda i, f: (0, 0))],
        out_specs=pl.BlockSpec((tm, D), lambda i, f: (i, 0)),
        out_shape=jax.ShapeDtypeStruct((T, D), F32),
        scratch_shapes=[pltpu.VMEM((tm, D), F32)],
        compiler_params=_cparams(2),
        name="ffn",
    )(m, w1, w2, h, g)


def _ple_kernel(h_ref, p_ref, wg_ref, wp_ref, g_ref, o_ref, *, sub):
    for r0 in range(0, h_ref.shape[0], sub):
        sl = slice(r0, r0 + sub)
        h = h_ref[sl, :]
        gate = jax.nn.sigmoid(_dot(h.astype(BF16), wg_ref[...]))
        e = _dot(p_ref[sl, :].astype(BF16), wp_ref[...])
        o_ref[sl, :] = h + _rms(gate * e, g_ref[...])


def _ple(h, p, layer, wg, wp, g):
    T, D = h.shape
    tm = min(512, T)
    row = lambda w: pl.BlockSpec((tm, w), lambda i: (i, 0))
    const = lambda a: pl.BlockSpec(a.shape, lambda i: (0, 0))
    return pl.pallas_call(
        functools.partial(_ple_kernel, sub=min(256, tm)),
        grid=(T // tm,),
        in_specs=[row(D), pl.BlockSpec((None, tm, p.shape[2]), lambda i: (layer, i, 0)),
                  const(wg), const(wp), const(g)],
        out_specs=row(D),
        out_shape=jax.ShapeDtypeStruct((T, D), F32),
        compiler_params=_cparams(1),
        name="ple",
    )(h, p, wg, wp, g)


def _rope_table(positions, half, theta, group):
    inv = theta ** (-jnp.arange(half, dtype=F32) / half)
    ang = positions.astype(F32)[..., None] * inv
    compact = jnp.concatenate([jnp.cos(ang), jnp.sin(ang), jnp.ones_like(ang[..., :1])], axis=-1)
    spread = np.zeros((2 * half + 1, 2 * LANES), np.float32)
    for lane in range(LANES):
        d = lane % group
        if d < 2 * half:
            spread[d % half, lane] = 1.0
            spread[half + d % half, LANES + lane] = -1.0 if d < half else 1.0
        else:
            spread[2 * half, lane] = 1.0
    return jnp.einsum("bsk,kn->bsn", compact, jnp.asarray(spread), precision=lax.Precision.HIGHEST)


def _reorder_w_in(w):
    w = w.astype(BF16)
    a0 = ATTN_WIDTH
    a1 = a0 + 2 * ATTN_HEAD_DIM
    a2 = a1 + IDX_WIDTH
    a3 = a2 + IDX_HEAD_DIM + N_IDX_HEADS
    a4 = a3 + 2 * RET_KEY_WIDTH
    pad = jnp.zeros((w.shape[0], Z_WIDTH - Z_USED + LANES - IDX_HEAD_DIM - N_IDX_HEADS), BF16)
    return jnp.concatenate([w[:, :a0], w[:, a1:a2], w[:, a4:], w[:, a3:a4], w[:, a0:a1], w[:, a2:a3], pad], axis=1)


def kernel(x, p, positions, w_in, w_out, w_ff1, w_ff2, w_ple, w_ple_gate, pre_mix_norm, post_mix_norm,
           pre_ff_norm, post_ff_norm, ple_norm, ret_gn):
    B, S, D = x.shape
    depth = w_in.shape[0]
    T = B * S
    topk = min(TOPK_MAX, S // 4)
    ret_chunk = min(256, S)
    tab_a = _rope_table(positions, ATTN_ROT_HALF, ROPE_THETA, ATTN_HEAD_DIM)
    tab_i = _rope_table(positions, IDX_ROT_HALF, ROPE_THETA, IDX_HEAD_DIM)
    tab_r = _rope_table(positions, RET_KEY_DIM // 2, RET_THETA, RET_KEY_DIM)
    ret_consts = _retention_consts(ret_chunk)
    vec = lambda a: a.reshape(1, -1).astype(F32)

    h = x.reshape(T, D)
    for i in range(depth):
        z = _norm_proj(h, vec(pre_mix_norm[i]), _reorder_w_in(w_in[i]))
        z3 = z.reshape(B, S, Z_WIDTH)
        attn = _attention(z3, tab_a, tab_i, topk)
        ret = _retention(z3, tab_r, ret_consts, vec(ret_gn[i]), ret_chunk)
        h, m = _out_proj(attn.reshape(T, ATTN_WIDTH), ret.reshape(T, RET_WIDTH), h,
                         w_out[i].astype(BF16), vec(post_mix_norm[i]), vec(pre_ff_norm[i]))
        h = _ffn(m, w_ff1[i].astype(BF16), w_ff2[i].astype(BF16), h, vec(post_ff_norm[i]))
        h = _ple(h, p.reshape(depth, T, -1), i, w_ple_gate[i].astype(BF16), w_ple[i].astype(BF16),
                 vec(ple_norm[i]))
    return h.reshape(B, S, D)
```

```python
import functools
import math

import jax
import jax.numpy as jnp
import numpy as np
from jax import lax
from jax.experimental import pallas as pl
from jax.experimental.pallas import tpu as pltpu

N_ATTN_HEADS = 8
ATTN_HEAD_DIM = 128
ATTN_ROT_HALF = 16
ROPE_THETA = 500000.0
TOPK_MAX = 256
N_IDX_HEADS = 16
IDX_HEAD_DIM = 64
IDX_ROT_HALF = 8
N_RET_HEADS = 8
RET_KEY_DIM = 64
RET_VAL_DIM = 128
RET_THETA = 10000.0
ATTN_WIDTH = N_ATTN_HEADS * ATTN_HEAD_DIM
IDX_WIDTH = N_IDX_HEADS * IDX_HEAD_DIM
RET_KEY_WIDTH = N_RET_HEADS * RET_KEY_DIM
RET_WIDTH = N_RET_HEADS * RET_VAL_DIM
RMS_EPS = 1e-6
GN_EPS = 1e-5

LANES = 128
COL_AQ = 0
COL_IQ = COL_AQ + ATTN_WIDTH
COL_RV = COL_IQ + IDX_WIDTH
COL_RG = COL_RV + RET_WIDTH
COL_RQ = COL_RG + RET_WIDTH
COL_RK = COL_RQ + RET_KEY_WIDTH
COL_AK = COL_RK + RET_KEY_WIDTH
COL_AV = COL_AK + ATTN_HEAD_DIM
COL_IKW = COL_AV + ATTN_HEAD_DIM
Z_USED = COL_IKW + LANES
Z_TILE_N = 512
Z_WIDTH = -(-Z_USED // Z_TILE_N) * Z_TILE_N

MASK_NEG = -1e30
THETA_ALL = -3e38
CUT_ALL = 1e9
BOUND_SLACK = 1.05
DENOM_SAFE = 1e-20
SEARCH_MAX_STEPS = 320
VMEM_LIMIT = 56 * 1024 * 1024

BF16 = jnp.bfloat16
F32 = jnp.float32


def _cparams(n_grid):
    return pltpu.CompilerParams(dimension_semantics=("arbitrary",) * n_grid,
                                vmem_limit_bytes=VMEM_LIMIT)


def _rms(x, gain):
    ms = jnp.mean(x * x, axis=-1, keepdims=True)
    return x * lax.rsqrt(ms + RMS_EPS) * gain


def _rope(xb, cos, sin, swap):
    return xb.astype(F32) * cos + jnp.dot(xb, swap, preferred_element_type=F32) * sin


def _swap_matrix(group, half):
    l = jnp.arange(LANES)
    d = l % group
    partner = jnp.where(d < half, l + half, jnp.where(d < 2 * half, l - half, -1))
    return (jnp.arange(LANES)[:, None] == partner[None, :]).astype(BF16)


def _dot_t(a, b):
    return lax.dot_general(a, b, (((1,), (1,)), ((), ())), preferred_element_type=F32)


def _dot(a, b):
    return jnp.dot(a, b, preferred_element_type=F32)


def _norm_proj_kernel(h_ref, g_ref, w_ref, z_ref, *, sub, tn):
    for r0 in range(0, h_ref.shape[0], sub):
        sl = slice(r0, r0 + sub)
        a = _rms(h_ref[sl, :], g_ref[...]).astype(BF16)
        for c0 in range(0, w_ref.shape[1], tn):
            z_ref[sl, c0:c0 + tn] = _dot(a, w_ref[:, c0:c0 + tn]).astype(z_ref.dtype)


def _norm_proj(h, gain, w):
    T, D = h.shape
    N = w.shape[1]
    tm = min(512, T)
    return pl.pallas_call(
        functools.partial(_norm_proj_kernel, sub=min(256, tm), tn=Z_TILE_N),
        grid=(T // tm,),
        in_specs=[pl.BlockSpec((tm, D), lambda i: (i, 0)),
                  pl.BlockSpec((1, D), lambda i: (0, 0)),
                  pl.BlockSpec((D, N), lambda i: (0, 0), pipeline_mode=pl.Buffered(1))],
        out_specs=pl.BlockSpec((tm, N), lambda i: (i, 0)),
        out_shape=jax.ShapeDtypeStruct((T, N), BF16),
        compiler_params=_cparams(1),
        name="norm_proj",
    )(h, gain, w)


def _attn_kernel(aq_ref, iq_ref, iwq_ref, ak_ref, av_ref, ikw_ref,
                 taq_ref, tak_ref, tiq_ref, tik_ref, pa_ref, pi_ref, ex_ref, o_ref,
                 kr_ref, kd0_ref, kd1_ref, vaug_ref, qa_ref, qi_ref, wb_ref, score_ref, score_t_ref,
                 th_ref, cut_ref, s_ref, m_ref, acc_ref, bound_ref, ksq_ref, tie_flag_ref, *, topk, tq, tk):
    S = ak_ref.shape[1]
    qi = pl.program_id(1)
    n_chunks = (qi * tq + tq + tk - 1) // tk
    reps = tk // LANES
    rows_all = N_ATTN_HEADS * tq
    ones_sq = jnp.ones((LANES, LANES), BF16)

    @pl.when(qi == 0)
    def _():
        rows = min(256, S)

        def body(r, _):
            r0 = pl.multiple_of(r * rows, rows)
            sl = pl.ds(r0, rows)
            kr = _rope(ak_ref[0, sl, :], tak_ref[0, sl, :LANES], tak_ref[0, sl, LANES:], pa_ref[...])
            kr_ref[sl, :] = kr.astype(BF16)
            ksq = _dot((kr * kr).astype(BF16), ones_sq).reshape(rows // 8, 8, LANES)
            ksq_ref[...] = jnp.maximum(jnp.where(r == 0, 0.0, ksq_ref[...]), jnp.max(ksq, axis=0))
            lane_r = lax.broadcasted_iota(jnp.int32, (rows, LANES), 1)
            ik = _rope(ikw_ref[0, sl, :], tik_ref[0, sl, :LANES], tik_ref[0, sl, LANES:], pi_ref[...])
            ik = jnp.where(lane_r < IDX_HEAD_DIM, ik, 0.0)
            kd0_ref[sl, :] = ik.astype(BF16)
            kd1_ref[sl, :] = pltpu.roll(ik, IDX_HEAD_DIM, 1).astype(BF16)
            vaug_ref[sl, :] = jnp.concatenate([av_ref[0, sl, :], jnp.ones((rows, LANES), BF16)], axis=1)
            score_t_ref[sl, :] = jnp.full((rows, tq), -jnp.inf, F32)
            return 0
        lax.fori_loop(0, S // rows, body, 0)

        def fill(c, _):
            score_ref[c] = jnp.full((tq, tk), -jnp.inf, F32)
            return 0
        lax.fori_loop(0, S // tk, fill, 0)

    ca = taq_ref[0, :, :LANES] * (ATTN_HEAD_DIM ** -0.5)
    sa = taq_ref[0, :, LANES:] * (ATTN_HEAD_DIM ** -0.5)
    k_norm = jnp.sqrt(jnp.max(ksq_ref[...], axis=0, keepdims=True))
    for h in range(N_ATTN_HEADS):
        x = aq_ref[0, :, h * LANES:(h + 1) * LANES]
        qf = _rope(x, ca, sa, pa_ref[...])
        qa_ref[h * tq:(h + 1) * tq, :] = qf.astype(BF16)
        q_norm = jnp.sqrt(_dot((qf * qf).astype(BF16), ones_sq))
        bound_ref[h * tq:(h + 1) * tq, :] = q_norm * k_norm * BOUND_SLACK
    ci, si = tiq_ref[0, :, :LANES], tiq_ref[0, :, LANES:]
    for j in range(IDX_WIDTH // LANES):
        x = iq_ref[0, :, j * LANES:(j + 1) * LANES]
        qi_ref[:, j * LANES:(j + 1) * LANES] = _rope(x, ci, si, pi_ref[...]).astype(BF16)
    iw_b = jnp.dot(iwq_ref[0], ex_ref[...], preferred_element_type=F32)
    for h in range(N_IDX_HEADS):
        wb_ref[h] = iw_b[:, h * LANES:(h + 1) * LANES]

    col_in_chunk = lax.broadcasted_iota(jnp.int32, (tq, tk), 1)

    def lane_groups(x, op):
        r = x[:, :LANES]
        for g in range(1, reps):
            r = op(r, x[:, g * LANES:(g + 1) * LANES])
        return r

    def tile_lanes(x):
        return jnp.concatenate([x] * reps, axis=1) if reps > 1 else x

    key_t = lax.broadcasted_iota(jnp.int32, (tk, tq), 0)

    def fold8(x, op):
        x3 = x.reshape(tk // 8, 8, x.shape[1])
        return jnp.max(x3, axis=0) if op == "max" else (jnp.min(x3, axis=0) if op == "min" else jnp.sum(x3, axis=0))

    rb = min(LANES, tq)
    row_b = lax.broadcasted_iota(jnp.int32, (rb, tk), 0)
    col_b = lax.broadcasted_iota(jnp.int32, (rb, tk), 1)
    key_tb = lax.broadcasted_iota(jnp.int32, (tk, rb), 0)
    qry_tb = lax.broadcasted_iota(jnp.int32, (tk, rb), 1)

    def a_body(c, carry):
        rmax, rmin = carry
        c0 = pl.multiple_of(c * tk, tk)
        k0 = kd0_ref[pl.ds(c0, tk), :]
        k1 = kd1_ref[pl.ds(c0, tk), :]
        maxs, mins = [], []
        for r in range(tq // rb):
            rows = slice(r * rb, (r + 1) * rb)
            q0 = qi * tq + r * rb
            acc = jnp.zeros((rb, tk), F32)
            for j in range(IDX_WIDTH // LANES):
                qp = qi_ref[rows, j * LANES:(j + 1) * LANES]
                for par, kd in ((0, k0), (1, k1)):
                    logits = _dot_t(qp, kd)
                    acc = acc + tile_lanes(wb_ref[2 * j + par, rows, :]) * jnp.maximum(logits, 0.0)
            causal = (c0 + col_b) <= (q0 + row_b)
            score_ref[c, rows, :] = jnp.where(causal, acc, -jnp.inf)
            acc_t = acc.T
            causal_t = (c0 + key_tb) <= (q0 + qry_tb)
            sc_t = jnp.where(causal_t, acc_t, -jnp.inf)
            score_t_ref[pl.ds(c0, tk), rows] = sc_t
            maxs.append(fold8(sc_t, "max"))
            mins.append(fold8(jnp.where(causal_t, acc_t, jnp.inf), "min"))
        rmax = jnp.maximum(rmax, jnp.concatenate(maxs, axis=1))
        rmin = jnp.minimum(rmin, jnp.concatenate(mins, axis=1))
        return rmax, rmin

    rmax, rmin = lax.fori_loop(0, n_chunks, a_body,
                               (jnp.full((8, tq), -jnp.inf, F32), jnp.full((8, tq), jnp.inf, F32)))

    vec = (1, tq)
    kf = float(topk)
    n_keys = qi * tq + lax.broadcasted_iota(jnp.int32, vec, 1) + 1

    lo0 = jnp.min(rmin, axis=0, keepdims=True)
    hi0 = jnp.max(rmax, axis=0, keepdims=True)
    key_f = key_t.astype(F32)

    def threshold_search(n_rows):
        def count_t(pred):
            accs = [jnp.zeros((8, tq), F32) for _ in range(4)]
            i = 0
            for r0 in range(0, n_rows, tk):
                ind = jnp.where(pred(r0, score_t_ref[r0:r0 + tk, :]), 1.0, 0.0)
                for g in range(tk // 8):
                    accs[i % 4] = accs[i % 4] + ind[g * 8:(g + 1) * 8, :]
                    i += 1
            return jnp.sum((accs[0] + accs[1]) + (accs[2] + accs[3]), axis=0, keepdims=True)

        def b_cond(carry):
            return jnp.logical_and(carry[1] > 0.0, carry[0] < SEARCH_MAX_STEPS)

        def b_body(carry):
            state = (carry[0],) + tuple(carry[2:])
            flag = jnp.max(1.0 - carry[5])
            for _ in range(steps_per_check):
                state = b_step(*state)
            return (state[0], flag) + tuple(state[1:])

        def b_step(it, lo, hi, chi, done, th, tie):
            mid = jnp.where(it == 0, hi, 0.5 * lo + 0.5 * hi)
            cnt = count_t(lambda r0, sc: sc >= mid)
            active = done == 0.0
            adjacent = jnp.logical_and(it > 0, jnp.logical_or(mid <= lo, mid >= hi))
            tie_now = jnp.logical_and(active, adjacent)
            live = jnp.logical_and(active, jnp.logical_not(adjacent))
            hit = jnp.logical_and(live, cnt == kf)
            above = jnp.logical_and(live, cnt > kf)
            below = jnp.logical_and(live, cnt < kf)
            th = jnp.where(tie_now, lo, jnp.where(hit, mid, th))
            tie = jnp.where(tie_now, 1.0, tie)
            lo = jnp.where(above, mid, lo)
            hi = jnp.where(below, mid, hi)
            chi = jnp.where(below, cnt, chi)
            done = jnp.where(jnp.logical_or(tie_now, hit), 1.0, done)
            return it + 1, lo, hi, chi, done, th, tie

        steps_per_check = 2 if n_rows <= 4 * tk else 1

        init = (jnp.int32(0), jnp.float32(1.0), lo0, hi0,
                jnp.zeros(vec, F32), jnp.where(n_keys <= topk, 1.0, 0.0),
                jnp.full(vec, THETA_ALL, F32), jnp.zeros(vec, F32))
        _, _, _, _, chi, _, th, tie = lax.while_loop(b_cond, b_body, init)

        def tie_cut():
            need = kf - chi

            def body(_, carry):
                lo_i, hi_i = carry
                mid = jnp.floor(0.5 * (lo_i + hi_i))
                cnt = count_t(lambda r0, sc: jnp.logical_and(sc == th, key_f + float(r0) <= mid))
                ge = cnt >= need
                return jnp.where(ge, lo_i, mid), jnp.where(ge, mid, hi_i)

            n_iter = int(math.ceil(math.log2(S))) + 1
            _, hi_i = lax.fori_loop(0, n_iter, body, (jnp.full(vec, -1.0, F32), jnp.full(vec, float(S - 1), F32)))
            return jnp.where(tie > 0.0, hi_i, CUT_ALL)

        th_ref[...] = th
        has_tie = jnp.max(tie) > 0.0
        tie_flag_ref[0] = has_tie.astype(jnp.int32)
        cut_ref[...] = lax.cond(has_tie, tie_cut, lambda: jnp.full(vec, CUT_ALL, F32))

    pair_rows = 2 * tk
    variant = (n_chunks * tk + pair_rows - 1) // pair_rows - 1
    for v in range(-(-S // pair_rows)):
        pl.when(variant == v)(functools.partial(threshold_search, min((v + 1) * pair_rows, S)))

    th_c = jnp.broadcast_to(th_ref[...], (LANES, tq)).T
    cut_c = jnp.broadcast_to(cut_ref[...], (LANES, tq)).T
    th_t = tile_lanes(th_c)
    cut_t = tile_lanes(cut_c)

    colf_in_chunk = col_in_chunk.astype(F32)

    def masked_logits(c, with_ties):
        c0 = pl.multiple_of(c * tk, tk)
        sc = score_ref[c]
        if with_ties:
            colf = colf_in_chunk + (c * tk).astype(F32)
            sel = jnp.logical_or(sc > th_t, jnp.logical_and(sc == th_t, colf <= cut_t))
        else:
            sel = sc >= th_t
        bias = jnp.where(sel, 0.0, MASK_NEG)
        s = _dot_t(qa_ref[...], kr_ref[pl.ds(c0, tk), :])
        return (s.reshape(N_ATTN_HEADS, tq, tk) + bias[None]).reshape(rows_all, tk)

    def accumulate(c, p):
        c0 = pl.multiple_of(c * tk, tk)
        acc_ref[...] += _dot(p.astype(BF16), vaug_ref[pl.ds(c0, tk), :])

    def one_pass(with_ties):
        acc_ref[...] = jnp.zeros(acc_ref.shape, F32)
        shift = tile_lanes(bound_ref[...])

        def body(cp, _):
            c = 2 * cp
            p0 = jnp.exp(masked_logits(c, with_ties) - shift).astype(BF16)
            p1 = jnp.exp(masked_logits(c + 1, with_ties) - shift).astype(BF16)
            r0 = pl.multiple_of(c * tk, 2 * tk)
            acc_ref[...] += _dot(jnp.concatenate([p0, p1], axis=1), vaug_ref[pl.ds(r0, 2 * tk), :])
            return 0
        lax.fori_loop(0, (n_chunks + 1) // 2, body, 0)

    def two_pass():
        m_ref[...] = jnp.full(m_ref.shape, MASK_NEG, F32)

        def body1(c, _):
            s = masked_logits(c, True)
            s_ref[c] = s
            m_ref[...] = jnp.maximum(m_ref[...], lane_groups(s, jnp.maximum))
            return 0
        lax.fori_loop(0, n_chunks, body1, 0)
        m_ref[...] = jnp.broadcast_to(jnp.max(m_ref[...], axis=1, keepdims=True), m_ref.shape)
        acc_ref[...] = jnp.zeros(acc_ref.shape, F32)

        def body2(c, _):
            accumulate(c, jnp.exp(s_ref[c] - tile_lanes(m_ref[...])))
            return 0
        lax.fori_loop(0, n_chunks, body2, 0)

    has_tie = tie_flag_ref[0] > 0
    pl.when(has_tie)(functools.partial(one_pass, True))
    pl.when(jnp.logical_not(has_tie))(functools.partial(one_pass, False))
    l_min = jnp.min(acc_ref[:, LANES:])
    pl.when(jnp.logical_not(l_min > DENOM_SAFE))(two_pass)
    out = acc_ref[:, :LANES] / acc_ref[:, LANES:]
    for h in range(N_ATTN_HEADS):
        o_ref[0, :, h * LANES:(h + 1) * LANES] = out[h * tq:(h + 1) * tq, :].astype(o_ref.dtype)


def _attention(z3, tab_a, tab_i, topk):
    B, S, _ = z3.shape
    tq = min(256, S)
    tk = min(256, S)
    assert S % (2 * tk) == 0 and S % tq == 0 and tk % tq == 0, "key chunks are consumed in pairs"
    qblk = lambda w, col: pl.BlockSpec((1, tq, w), lambda b, q: (b, q, col // w))
    kblk = lambda w, col: pl.BlockSpec((1, S, w), lambda b, q: (b, 0, col // w))
    tab_q = pl.BlockSpec((1, tq, 2 * LANES), lambda b, q: (b, q, 0))
    tab_k = pl.BlockSpec((1, S, 2 * LANES), lambda b, q: (b, 0, 0))
    rows = N_ATTN_HEADS * tq
    const = lambda a: pl.BlockSpec(a.shape, lambda b, q: (0, 0))
    perm_a = _swap_matrix(ATTN_HEAD_DIM, ATTN_ROT_HALF)
    perm_i = _swap_matrix(IDX_HEAD_DIM, IDX_ROT_HALF)
    idx_w_scale = (N_IDX_HEADS ** -0.5) * (IDX_HEAD_DIM ** -0.5)
    assert math.frexp(idx_w_scale)[0] == 0.5, "the folded scale must be a power of two to stay exact in bf16"
    expand = jnp.where(jnp.arange(LANES)[:, None] == IDX_HEAD_DIM + jnp.arange(N_IDX_HEADS * LANES)[None, :] // LANES,
                       idx_w_scale, 0.0).astype(BF16)
    return pl.pallas_call(
        functools.partial(_attn_kernel, topk=topk, tq=tq, tk=tk),
        grid=(B, S // tq),
        in_specs=[qblk(ATTN_WIDTH, COL_AQ), qblk(IDX_WIDTH, COL_IQ), qblk(LANES, COL_IKW),
                  kblk(LANES, COL_AK), kblk(LANES, COL_AV), kblk(LANES, COL_IKW),
                  tab_q, tab_k, tab_q, tab_k, const(perm_a), const(perm_i), const(expand)],
        out_specs=pl.BlockSpec((1, tq, ATTN_WIDTH), lambda b, q: (b, q, 0)),
        out_shape=jax.ShapeDtypeStruct((B, S, ATTN_WIDTH), BF16),
        scratch_shapes=[pltpu.VMEM((S, LANES), BF16),
                        pltpu.VMEM((S, LANES), BF16),
                        pltpu.VMEM((S, LANES), BF16),
                        pltpu.VMEM((S, 2 * LANES), BF16),
                        pltpu.VMEM((rows, LANES), BF16),
                        pltpu.VMEM((tq, IDX_WIDTH), BF16),
                        pltpu.VMEM((N_IDX_HEADS, tq, LANES), F32),
                        pltpu.VMEM((S // tk, tq, tk), F32),
                        pltpu.VMEM((S, tq), F32),
                        pltpu.VMEM((1, tq), F32),
                        pltpu.VMEM((1, tq), F32),
                        pltpu.VMEM((S // tk, rows, tk), F32),
                        pltpu.VMEM((rows, LANES), F32),
                        pltpu.VMEM((rows, 2 * LANES), F32),
                        pltpu.VMEM((rows, LANES), F32),
                        pltpu.VMEM((8, LANES), F32),
                        pltpu.SMEM((1,), jnp.int32)],
        compiler_params=_cparams(2),
        name="dsa_attention",
    )(z3, z3, z3, z3, z3, z3, tab_a, tab_a, tab_i, tab_i, perm_a, perm_i, expand)


def _ret_kernel(rq_ref, rk_ref, rv_ref, rg_ref, tab_ref, decay_ref, zeta_ref, xi_ref, gch_ref, gn_ref,
                o_ref, state_ref, *, chunk):
    S = rq_ref.shape[1]
    C = chunk
    half = RET_KEY_DIM
    state_ref[...] = jnp.zeros(state_ref.shape, F32)
    lane = lax.broadcasted_iota(jnp.int32, (C, LANES), 1)
    first = lane < half

    def body(c, _):
        r0 = pl.multiple_of(c * C, C)
        sl = pl.ds(r0, C)
        cos, sin = tab_ref[0, sl, :LANES], tab_ref[0, sl, LANES:]
        cos_k, sin_k = cos * (RET_KEY_DIM ** -0.5), sin * (RET_KEY_DIM ** -0.5)
        first_half = lane % RET_KEY_DIM < RET_KEY_DIM // 2

        def rope(x, c, s):
            partner = jnp.where(first_half, pltpu.roll(x, LANES - RET_KEY_DIM // 2, 1),
                                pltpu.roll(x, RET_KEY_DIM // 2, 1))
            return x * c + partner * s

        for j in range(N_RET_HEADS // 2):
            q = rope(rq_ref[0, sl, j * LANES:(j + 1) * LANES].astype(F32), cos, sin)
            k = rope(rk_ref[0, sl, j * LANES:(j + 1) * LANES].astype(F32), cos_k, sin_k)
            kzt = (k * zeta_ref[j]).T.astype(BF16)
            state = state_ref[j]
            state_b = state.astype(BF16)
            new_rows = []
            for par in range(2):
                h = 2 * j + par
                keep = first if par == 0 else jnp.logical_not(first)
                qh = jnp.where(keep, q, 0.0).astype(BF16)
                kh = jnp.where(keep, k, 0.0).astype(BF16)
                v = rv_ref[0, sl, h * LANES:(h + 1) * LANES]
                inner = _dot_t(qh, kh) * decay_ref[h]
                o = _dot(inner.astype(BF16), v) + _dot(qh, state_b) * xi_ref[h]
                new_rows.append(_dot(kzt[par * half:(par + 1) * half, :], v))
                mu = jnp.mean(o, axis=-1, keepdims=True)
                d = o - mu
                var = jnp.mean(d * d, axis=-1, keepdims=True)
                y = d * lax.rsqrt(var + GN_EPS) * gn_ref[:, h * LANES:(h + 1) * LANES]
                g = rg_ref[0, sl, h * LANES:(h + 1) * LANES].astype(F32)
                o_ref[0, sl, h * LANES:(h + 1) * LANES] = (y * (g * jax.nn.sigmoid(g))).astype(o_ref.dtype)
            state_ref[j] = gch_ref[j] * state + jnp.concatenate(new_rows, axis=0)
        return 0

    lax.fori_loop(0, S // C, body, 0)


def _retention(z3, tab_r, consts, gn_w, chunk):
    B, S, _ = z3.shape
    decay, zeta, xi, gch = consts
    blk = lambda w, col: pl.BlockSpec((1, S, w), lambda b: (b, 0, col // w))
    const = lambda a: pl.BlockSpec(a.shape, lambda b: (0,) * a.ndim)
    return pl.pallas_call(
        functools.partial(_ret_kernel, chunk=chunk),
        grid=(B,),
        in_specs=[blk(RET_KEY_WIDTH, COL_RQ), blk(RET_KEY_WIDTH, COL_RK), blk(RET_WIDTH, COL_RV),
                  blk(RET_WIDTH, COL_RG), pl.BlockSpec((1, S, 2 * LANES), lambda b: (b, 0, 0)),
                  const(decay), const(zeta), const(xi), const(gch), const(gn_w)],
        out_specs=pl.BlockSpec((1, S, RET_WIDTH), lambda b: (b, 0, 0)),
        out_shape=jax.ShapeDtypeStruct((B, S, RET_WIDTH), BF16),
        scratch_shapes=[pltpu.VMEM((N_RET_HEADS // 2, LANES, RET_VAL_DIM), F32)],
        compiler_params=_cparams(1),
        name="retention",
    )(z3, z3, z3, z3, tab_r, decay, zeta, xi, gch, gn_w)


def _retention_consts(chunk):
    C = chunk
    log_g = jnp.log(1.0 - 2.0 ** (-5.0 - jnp.arange(N_RET_HEADS, dtype=F32)))
    i = jnp.arange(C, dtype=F32)
    diff = i[:, None] - i[None, :]
    decay = jnp.where(diff[None] >= 0, jnp.exp(jnp.maximum(diff, 0.0)[None] * log_g[:, None, None]), 0.0)
    zeta = jnp.exp((C - 1.0 - i)[None, :] * log_g[:, None])
    xi = jnp.exp((i + 1.0)[None, :] * log_g[:, None])
    g_chunk = jnp.exp(C * log_g)
    pair = lambda a: a.reshape(N_RET_HEADS // 2, 2, -1)
    zeta_p = jnp.repeat(pair(zeta).transpose(0, 2, 1), RET_KEY_DIM, axis=2)
    xi_b = jnp.broadcast_to(xi[:, :, None], (N_RET_HEADS, C, RET_VAL_DIM))
    gch = jnp.broadcast_to(jnp.repeat(pair(g_chunk), RET_KEY_DIM, axis=1).reshape(N_RET_HEADS // 2, LANES, 1),
                           (N_RET_HEADS // 2, LANES, RET_VAL_DIM))
    return decay.astype(F32), zeta_p.astype(F32), xi_b.astype(F32), gch.astype(F32)


def _out_proj_kernel(attn_ref, ret_ref, h_ref, wa_ref, wr_ref, g1_ref, g2_ref, h1_ref, m_ref, *, sub):
    for r0 in range(0, h_ref.shape[0], sub):
        sl = slice(r0, r0 + sub)
        mix = _dot(attn_ref[sl, :], wa_ref[...]) + _dot(ret_ref[sl, :], wr_ref[...])
        h1 = h_ref[sl, :] + _rms(mix, g1_ref[...])
        h1_ref[sl, :] = h1
        m_ref[sl, :] = _rms(h1, g2_ref[...]).astype(m_ref.dtype)


def _out_proj(attn, ret, h, wo, g1, g2):
    T, D = h.shape
    tm = min(512, T)
    wa_rows, wr_rows = attn.shape[1], ret.shape[1]
    assert wa_rows == wr_rows and wo.shape[0] == wa_rows + wr_rows
    row = lambda w: pl.BlockSpec((tm, w), lambda i: (i, 0))
    const = lambda a: pl.BlockSpec(a.shape, lambda i: (0, 0))
    return pl.pallas_call(
        functools.partial(_out_proj_kernel, sub=min(256, tm)),
        grid=(T // tm,),
        in_specs=[row(wa_rows), row(wr_rows), row(D),
                  pl.BlockSpec((wa_rows, D), lambda i: (0, 0)), pl.BlockSpec((wr_rows, D), lambda i: (1, 0)),
                  const(g1), const(g2)],
        out_specs=[row(D), row(D)],
        out_shape=[jax.ShapeDtypeStruct((T, D), F32), jax.ShapeDtypeStruct((T, D), BF16)],
        compiler_params=_cparams(1),
        name="out_proj",
    )(attn, ret, h, wo, wo, g1, g2)


def _ffn_kernel(m_ref, w1_ref, w2_ref, h_ref, g_ref, o_ref, acc_ref):
    f = pl.program_id(1)

    @pl.when(f == 0)
    def _():
        acc_ref[...] = jnp.zeros(acc_ref.shape, F32)

    u = jnp.maximum(_dot(m_ref[...], w1_ref[...]), 0.0)
    acc_ref[...] += _dot((u * u).astype(BF16), w2_ref[...])

    @pl.when(f == pl.num_programs(1) - 1)
    def _():
        o_ref[...] = h_ref[...] + _rms(acc_ref[...], g_ref[...])


def _ffn(m, w1, w2, h, g):
    T, D = h.shape
    F = w1.shape[1]
    tm = min(512, T)
    tf = min(1024, F)
    return pl.pallas_call(
        _ffn_kernel,
        grid=(T // tm, F // tf),
        in_specs=[pl.BlockSpec((tm, D), lambda i, f: (i, 0)),
                  pl.BlockSpec((D, tf), lambda i, f: (0, f)),
                  pl.BlockSpec((tf, D), lambda i, f: (f, 0)),
                  pl.BlockSpec((tm, D), lambda i, f: (i, 0)),
                  pl.BlockSpec((1, D), lambda i, f: (0, 0))],
        out_specs=pl.BlockSpec((tm, D), lambda i, f: (i, 0)),
        out_shape=jax.ShapeDtypeStruct((T, D), F32),
        scratch_shapes=[pltpu.VMEM((tm, D), F32)],
        compiler_params=_cparams(2),
        name="ffn",
    )(m, w1, w2, h, g)


def _ple_kernel(h_ref, p_ref, wg_ref, wp_ref, g_ref, o_ref, *, sub):
    for r0 in range(0, h_ref.shape[0], sub):
        sl = slice(r0, r0 + sub)
        h = h_ref[sl, :]
        gate = jax.nn.sigmoid(_dot(h.astype(BF16), wg_ref[...]))
        e = _dot(p_ref[sl, :].astype(BF16), wp_ref[...])
        o_ref[sl, :] = h + _rms(gate * e, g_ref[...])


def _ple(h, p, layer, wg, wp, g):
    T, D = h.shape
    tm = min(512, T)
    row = lambda w: pl.BlockSpec((tm, w), lambda i: (i, 0))
    const = lambda a: pl.BlockSpec(a.shape, lambda i: (0, 0))
    return pl.pallas_call(
        functools.partial(_ple_kernel, sub=min(256, tm)),
        grid=(T // tm,),
        in_specs=[row(D), pl.BlockSpec((None, tm, p.shape[2]), lambda i: (layer, i, 0)),
                  const(wg), const(wp), const(g)],
        out_specs=row(D),
        out_shape=jax.ShapeDtypeStruct((T, D), F32),
        compiler_params=_cparams(1),
        name="ple",
    )(h, p, wg, wp, g)


def _rope_table(positions, half, theta, group):
    inv = theta ** (-jnp.arange(half, dtype=F32) / half)
    ang = positions.astype(F32)[..., None] * inv
    compact = jnp.concatenate([jnp.cos(ang), jnp.sin(ang), jnp.ones_like(ang[..., :1])], axis=-1)
    spread = np.zeros((2 * half + 1, 2 * LANES), np.float32)
    for lane in range(LANES):
        d = lane % group
        if d < 2 * half:
            spread[d % half, lane] = 1.0
            spread[half + d % half, LANES + lane] = -1.0 if d < half else 1.0
        else:
            spread[2 * half, lane] = 1.0
    return jnp.einsum("bsk,kn->bsn", compact, jnp.asarray(spread), precision=lax.Precision.HIGHEST)


def _reorder_w_in(w):
    w = w.astype(BF16)
    a0 = ATTN_WIDTH
    a1 = a0 + 2 * ATTN_HEAD_DIM
    a2 = a1 + IDX_WIDTH
    a3 = a2 + IDX_HEAD_DIM + N_IDX_HEADS
    a4 = a3 + 2 * RET_KEY_WIDTH
    pad = jnp.zeros((w.shape[0], Z_WIDTH - Z_USED + LANES - IDX_HEAD_DIM - N_IDX_HEADS), BF16)
    return jnp.concatenate([w[:, :a0], w[:, a1:a2], w[:, a4:], w[:, a3:a4], w[:, a0:a1], w[:, a2:a3], pad], axis=1)


def kernel(x, p, positions, w_in, w_out, w_ff1, w_ff2, w_ple, w_ple_gate, pre_mix_norm, post_mix_norm,
           pre_ff_norm, post_ff_norm, ple_norm, ret_gn):
    B, S, D = x.shape
    depth = w_in.shape[0]
    T = B * S
    topk = min(TOPK_MAX, S // 4)
    ret_chunk = min(256, S)
    tab_a = _rope_table(positions, ATTN_ROT_HALF, ROPE_THETA, ATTN_HEAD_DIM)
    tab_i = _rope_table(positions, IDX_ROT_HALF, ROPE_THETA, IDX_HEAD_DIM)
    tab_r = _rope_table(positions, RET_KEY_DIM // 2, RET_THETA, RET_KEY_DIM)
    ret_consts = _retention_consts(ret_chunk)
    vec = lambda a: a.reshape(1, -1).astype(F32)

    h = x.reshape(T, D)
    for i in range(depth):
        z = _norm_proj(h, vec(pre_mix_norm[i]), _reorder_w_in(w_in[i]))
        z3 = z.reshape(B, S, Z_WIDTH)
        attn = _attention(z3, tab_a, tab_i, topk)
        ret = _retention(z3, tab_r, ret_consts, vec(ret_gn[i]), ret_chunk)
        h, m = _out_proj(attn.reshape(T, ATTN_WIDTH), ret.reshape(T, RET_WIDTH), h,
                         w_out[i].astype(BF16), vec(post_mix_norm[i]), vec(pre_ff_norm[i]))
        h = _ffn(m, w_ff1[i].astype(BF16), w_ff2[i].astype(BF16), h, vec(post_ff_norm[i]))
        h = _ple(h, p.reshape(depth, T, -1), i, w_ple_gate[i].astype(BF16), w_ple[i].astype(BF16),
                 vec(ple_norm[i]))
    return h.reshape(B, S, D)
```

```python
import functools
import math

import jax
import jax.numpy as jnp
import numpy as np
from jax import lax
from jax.experimental import pallas as pl
from jax.experimental.pallas import tpu as pltpu

N_ATTN_HEADS = 8
ATTN_HEAD_DIM = 128
ATTN_ROT_HALF = 16
ROPE_THETA = 500000.0
TOPK_MAX = 256
N_IDX_HEADS = 16
IDX_HEAD_DIM = 64
IDX_ROT_HALF = 8
N_RET_HEADS = 8
RET_KEY_DIM = 64
RET_VAL_DIM = 128
RET_THETA = 10000.0
ATTN_WIDTH = N_ATTN_HEADS * ATTN_HEAD_DIM
IDX_WIDTH = N_IDX_HEADS * IDX_HEAD_DIM
RET_KEY_WIDTH = N_RET_HEADS * RET_KEY_DIM
RET_WIDTH = N_RET_HEADS * RET_VAL_DIM
RMS_EPS = 1e-6
GN_EPS = 1e-5

LANES = 128
COL_AQ = 0
COL_IQ = COL_AQ + ATTN_WIDTH
COL_RV = COL_IQ + IDX_WIDTH
COL_RG = COL_RV + RET_WIDTH
COL_RQ = COL_RG + RET_WIDTH
COL_RK = COL_RQ + RET_KEY_WIDTH
COL_AK = COL_RK + RET_KEY_WIDTH
COL_AV = COL_AK + ATTN_HEAD_DIM
COL_IKW = COL_AV + ATTN_HEAD_DIM
Z_USED = COL_IKW + LANES
Z_TILE_N = 512
Z_WIDTH = -(-Z_USED // Z_TILE_N) * Z_TILE_N

MASK_NEG = -1e30
THETA_ALL = -3e38
CUT_ALL = 1e9
BOUND_SLACK = 1.05
DENOM_SAFE = 1e-20
SEARCH_MAX_STEPS = 320
V7X_VMEM_BYTES = 64 * 1024 * 1024
VMEM_LIMIT = V7X_VMEM_BYTES - 8 * 1024 * 1024

BF16 = jnp.bfloat16
F32 = jnp.float32


def _cparams(n_grid):
    return pltpu.CompilerParams(dimension_semantics=("arbitrary",) * n_grid,
                                vmem_limit_bytes=VMEM_LIMIT)


def _rms(x, gain):
    ms = jnp.mean(x * x, axis=-1, keepdims=True)
    return x * lax.rsqrt(ms + RMS_EPS) * gain


def _rope(xb, cos, sin, swap):
    return xb.astype(F32) * cos + jnp.dot(xb, swap, preferred_element_type=F32) * sin


def _swap_matrix(group, half):
    l = jnp.arange(LANES)
    d = l % group
    partner = jnp.where(d < half, l + half, jnp.where(d < 2 * half, l - half, -1))
    return (jnp.arange(LANES)[:, None] == partner[None, :]).astype(BF16)


def _dot_t(a, b):
    return lax.dot_general(a, b, (((1,), (1,)), ((), ())), preferred_element_type=F32)


def _dot(a, b):
    return jnp.dot(a, b, preferred_element_type=F32)


def _norm_proj_kernel(h_ref, g_ref, w_ref, z_ref, *, sub, tn):
    for r0 in range(0, h_ref.shape[0], sub):
        sl = slice(r0, r0 + sub)
        a = _rms(h_ref[sl, :], g_ref[...]).astype(BF16)
        for c0 in range(0, w_ref.shape[1], tn):
            z_ref[sl, c0:c0 + tn] = _dot(a, w_ref[:, c0:c0 + tn]).astype(z_ref.dtype)


def _norm_proj(h, gain, w):
    T, D = h.shape
    N = w.shape[1]
    tm = min(512, T)
    return pl.pallas_call(
        functools.partial(_norm_proj_kernel, sub=min(256, tm), tn=Z_TILE_N),
        grid=(T // tm,),
        in_specs=[pl.BlockSpec((tm, D), lambda i: (i, 0)),
                  pl.BlockSpec((1, D), lambda i: (0, 0)),
                  pl.BlockSpec((D, N), lambda i: (0, 0), pipeline_mode=pl.Buffered(1))],
        out_specs=pl.BlockSpec((tm, N), lambda i: (i, 0)),
        out_shape=jax.ShapeDtypeStruct((T, N), BF16),
        compiler_params=_cparams(1),
        name="norm_proj",
    )(h, gain, w)


def _attn_kernel(aq_ref, iq_ref, iwq_ref, ak_ref, av_ref, ikw_ref,
                 taq_ref, tak_ref, tiq_ref, tik_ref, pa_ref, pi_ref, ex_ref, o_ref,
                 kr_ref, kd0_ref, kd1_ref, vaug_ref, qa_ref, qi_ref, wb_ref, score_ref, score_t_ref,
                 th_ref, cut_ref, s_ref, m_ref, acc_ref, bound_ref, ksq_ref, tie_flag_ref, *, topk, tq, tk):
    S = ak_ref.shape[1]
    qi = pl.program_id(1)
    n_chunks = (qi * tq + tq + tk - 1) // tk
    reps = tk // LANES
    rows_all = N_ATTN_HEADS * tq
    ones_sq = jnp.ones((LANES, LANES), BF16)

    @pl.when(qi == 0)
    def _():
        rows = min(256, S)

        def body(r, _):
            r0 = pl.multiple_of(r * rows, rows)
            sl = pl.ds(r0, rows)
            kr = _rope(ak_ref[0, sl, :], tak_ref[0, sl, :LANES], tak_ref[0, sl, LANES:], pa_ref[...])
            kr_ref[sl, :] = kr.astype(BF16)
            ksq = _dot((kr * kr).astype(BF16), ones_sq).reshape(rows // 8, 8, LANES)
            ksq_ref[...] = jnp.maximum(jnp.where(r == 0, 0.0, ksq_ref[...]), jnp.max(ksq, axis=0))
            lane_r = lax.broadcasted_iota(jnp.int32, (rows, LANES), 1)
            ik = _rope(ikw_ref[0, sl, :], tik_ref[0, sl, :LANES], tik_ref[0, sl, LANES:], pi_ref[...])
            ik = jnp.where(lane_r < IDX_HEAD_DIM, ik, 0.0)
            kd0_ref[sl, :] = ik.astype(BF16)
            kd1_ref[sl, :] = pltpu.roll(ik, IDX_HEAD_DIM, 1).astype(BF16)
            vaug_ref[sl, :] = jnp.concatenate([av_ref[0, sl, :], jnp.ones((rows, LANES), BF16)], axis=1)
            score_t_ref[sl, :] = jnp.full((rows, tq), -jnp.inf, F32)
            return 0
        lax.fori_loop(0, S // rows, body, 0)

        def fill(c, _):
            score_ref[c] = jnp.full((tq, tk), -jnp.inf, F32)
            return 0
        lax.fori_loop(0, S // tk, fill, 0)

    ca = taq_ref[0, :, :LANES] * (ATTN_HEAD_DIM ** -0.5)
    sa = taq_ref[0, :, LANES:] * (ATTN_HEAD_DIM ** -0.5)
    k_norm = jnp.sqrt(jnp.max(ksq_ref[...], axis=0, keepdims=True))
    for h in range(N_ATTN_HEADS):
        x = aq_ref[0, :, h * LANES:(h + 1) * LANES]
        qf = _rope(x, ca, sa, pa_ref[...])
        qa_ref[h * tq:(h + 1) * tq, :] = qf.astype(BF16)
        q_norm = jnp.sqrt(_dot((qf * qf).astype(BF16), ones_sq))
        bound_ref[h * tq:(h + 1) * tq, :] = q_norm * k_norm * BOUND_SLACK
    ci, si = tiq_ref[0, :, :LANES], tiq_ref[0, :, LANES:]
    for j in range(IDX_WIDTH // LANES):
        x = iq_ref[0, :, j * LANES:(j + 1) * LANES]
        qi_ref[:, j * LANES:(j + 1) * LANES] = _rope(x, ci, si, pi_ref[...]).astype(BF16)
    iw_b = jnp.dot(iwq_ref[0], ex_ref[...], preferred_element_type=F32)
    for h in range(N_IDX_HEADS):
        wb_ref[h] = iw_b[:, h * LANES:(h + 1) * LANES]

    col_in_chunk = lax.broadcasted_iota(jnp.int32, (tq, tk), 1)

    def lane_groups(x, op):
        r = x[:, :LANES]
        for g in range(1, reps):
            r = op(r, x[:, g * LANES:(g + 1) * LANES])
        return r

    def tile_lanes(x):
        return jnp.concatenate([x] * reps, axis=1) if reps > 1 else x

    key_t = lax.broadcasted_iota(jnp.int32, (tk, tq), 0)

    def fold8(x, op):
        x3 = x.reshape(tk // 8, 8, x.shape[1])
        return jnp.max(x3, axis=0) if op == "max" else (jnp.min(x3, axis=0) if op == "min" else jnp.sum(x3, axis=0))

    rb = min(LANES, tq)
    row_b = lax.broadcasted_iota(jnp.int32, (rb, tk), 0)
    col_b = lax.broadcasted_iota(jnp.int32, (rb, tk), 1)
    key_tb = lax.broadcasted_iota(jnp.int32, (tk, rb), 0)
    qry_tb = lax.broadcasted_iota(jnp.int32, (tk, rb), 1)

    def a_body(c, carry):
        rmax, rmin = carry
        c0 = pl.multiple_of(c * tk, tk)
        k0 = kd0_ref[pl.ds(c0, tk), :]
        k1 = kd1_ref[pl.ds(c0, tk), :]
        maxs, mins = [], []
        for r in range(tq // rb):
            rows = slice(r * rb, (r + 1) * rb)
            q0 = qi * tq + r * rb
            acc = jnp.zeros((rb, tk), F32)
            for j in range(IDX_WIDTH // LANES):
                qp = qi_ref[rows, j * LANES:(j + 1) * LANES]
                for par, kd in ((0, k0), (1, k1)):
                    logits = _dot_t(qp, kd)
                    acc = acc + tile_lanes(wb_ref[2 * j + par, rows, :]) * jnp.maximum(logits, 0.0)
            causal = (c0 + col_b) <= (q0 + row_b)
            score_ref[c, rows, :] = jnp.where(causal, acc, -jnp.inf)
            acc_t = acc.T
            causal_t = (c0 + key_tb) <= (q0 + qry_tb)
            sc_t = jnp.where(causal_t, acc_t, -jnp.inf)
            score_t_ref[pl.ds(c0, tk), rows] = sc_t
            maxs.append(fold8(sc_t, "max"))
            mins.append(fold8(jnp.where(causal_t, acc_t, jnp.inf), "min"))
        rmax = jnp.maximum(rmax, jnp.concatenate(maxs, axis=1))
        rmin = jnp.minimum(rmin, jnp.concatenate(mins, axis=1))
        return rmax, rmin

    rmax, rmin = lax.fori_loop(0, n_chunks, a_body,
                               (jnp.full((8, tq), -jnp.inf, F32), jnp.full((8, tq), jnp.inf, F32)))

    vec = (1, tq)
    kf = float(topk)
    n_keys = qi * tq + lax.broadcasted_iota(jnp.int32, vec, 1) + 1

    lo0 = jnp.min(rmin, axis=0, keepdims=True)
    hi0 = jnp.max(rmax, axis=0, keepdims=True)
    key_f = key_t.astype(F32)

    def threshold_search(n_rows):
        def count_t(pred):
            accs = [jnp.zeros((8, tq), F32) for _ in range(4)]
            i = 0
            for r0 in range(0, n_rows, tk):
                ind = jnp.where(pred(r0, score_t_ref[r0:r0 + tk, :]), 1.0, 0.0)
                for g in range(tk // 8):
                    accs[i % 4] = accs[i % 4] + ind[g * 8:(g + 1) * 8, :]
                    i += 1
            return jnp.sum((accs[0] + accs[1]) + (accs[2] + accs[3]), axis=0, keepdims=True)

        def b_cond(carry):
            return jnp.logical_and(carry[1] > 0.0, carry[0] < SEARCH_MAX_STEPS)

        def b_body(carry):
            state = (carry[0],) + tuple(carry[2:])
            flag = jnp.max(1.0 - carry[5])
            for _ in range(steps_per_check):
                state = b_step(*state)
            return (state[0], flag) + tuple(state[1:])

        def b_step(it, lo, hi, chi, done, th, tie):
            mid = jnp.where(it == 0, hi, 0.5 * lo + 0.5 * hi)
            cnt = count_t(lambda r0, sc: sc >= mid)
            active = done == 0.0
            adjacent = jnp.logical_and(it > 0, jnp.logical_or(mid <= lo, mid >= hi))
            tie_now = jnp.logical_and(active, adjacent)
            live = jnp.logical_and(active, jnp.logical_not(adjacent))
            hit = jnp.logical_and(live, cnt == kf)
            above = jnp.logical_and(live, cnt > kf)
            below = jnp.logical_and(live, cnt < kf)
            th = jnp.where(tie_now, lo, jnp.where(hit, mid, th))
            tie = jnp.where(tie_now, 1.0, tie)
            lo = jnp.where(above, mid, lo)
            hi = jnp.where(below, mid, hi)
            chi = jnp.where(below, cnt, chi)
            done = jnp.where(jnp.logical_or(tie_now, hit), 1.0, done)
            return it + 1, lo, hi, chi, done, th, tie

        steps_per_check = 2 if n_rows <= 4 * tk else 1

        init = (jnp.int32(0), jnp.float32(1.0), lo0, hi0,
                jnp.zeros(vec, F32), jnp.where(n_keys <= topk, 1.0, 0.0),
                jnp.full(vec, THETA_ALL, F32), jnp.zeros(vec, F32))
        _, _, _, _, chi, _, th, tie = lax.while_loop(b_cond, b_body, init)

        def tie_cut():
            need = kf - chi

            def body(_, carry):
                lo_i, hi_i = carry
                mid = jnp.floor(0.5 * (lo_i + hi_i))
                cnt = count_t(lambda r0, sc: jnp.logical_and(sc == th, key_f + float(r0) <= mid))
                ge = cnt >= need
                return jnp.where(ge, lo_i, mid), jnp.where(ge, mid, hi_i)

            n_iter = int(math.ceil(math.log2(S))) + 1
            _, hi_i = lax.fori_loop(0, n_iter, body, (jnp.full(vec, -1.0, F32), jnp.full(vec, float(S - 1), F32)))
            return jnp.where(tie > 0.0, hi_i, CUT_ALL)

        th_ref[...] = th
        has_tie = jnp.max(tie) > 0.0
        tie_flag_ref[0] = has_tie.astype(jnp.int32)
        cut_ref[...] = lax.cond(has_tie, tie_cut, lambda: jnp.full(vec, CUT_ALL, F32))

    pair_rows = 2 * tk
    variant = (n_chunks * tk + pair_rows - 1) // pair_rows - 1
    for v in range(-(-S // pair_rows)):
        pl.when(variant == v)(functools.partial(threshold_search, min((v + 1) * pair_rows, S)))

    th_c = jnp.broadcast_to(th_ref[...], (LANES, tq)).T
    cut_c = jnp.broadcast_to(cut_ref[...], (LANES, tq)).T
    th_t = tile_lanes(th_c)
    cut_t = tile_lanes(cut_c)

    colf_in_chunk = col_in_chunk.astype(F32)

    def masked_logits(c, with_ties):
        c0 = pl.multiple_of(c * tk, tk)
        sc = score_ref[c]
        if with_ties:
            colf = colf_in_chunk + (c * tk).astype(F32)
            sel = jnp.logical_or(sc > th_t, jnp.logical_and(sc == th_t, colf <= cut_t))
        else:
            sel = sc >= th_t
        bias = jnp.where(sel, 0.0, MASK_NEG)
        s = _dot_t(qa_ref[...], kr_ref[pl.ds(c0, tk), :])
        return (s.reshape(N_ATTN_HEADS, tq, tk) + bias[None]).reshape(rows_all, tk)

    def accumulate(c, p):
        c0 = pl.multiple_of(c * tk, tk)
        acc_ref[...] += _dot(p.astype(BF16), vaug_ref[pl.ds(c0, tk), :])

    def one_pass(with_ties):
        acc_ref[...] = jnp.zeros(acc_ref.shape, F32)
        shift = tile_lanes(bound_ref[...])

        def body(cp, _):
            c = 2 * cp
            p0 = jnp.exp(masked_logits(c, with_ties) - shift).astype(BF16)
            p1 = jnp.exp(masked_logits(c + 1, with_ties) - shift).astype(BF16)
            r0 = pl.multiple_of(c * tk, 2 * tk)
            acc_ref[...] += _dot(jnp.concatenate([p0, p1], axis=1), vaug_ref[pl.ds(r0, 2 * tk), :])
            return 0
        lax.fori_loop(0, (n_chunks + 1) // 2, body, 0)

    def two_pass():
        m_ref[...] = jnp.full(m_ref.shape, MASK_NEG, F32)

        def body1(c, _):
            s = masked_logits(c, True)
            s_ref[c] = s
            m_ref[...] = jnp.maximum(m_ref[...], lane_groups(s, jnp.maximum))
            return 0
        lax.fori_loop(0, n_chunks, body1, 0)
        m_ref[...] = jnp.broadcast_to(jnp.max(m_ref[...], axis=1, keepdims=True), m_ref.shape)
        acc_ref[...] = jnp.zeros(acc_ref.shape, F32)

        def body2(c, _):
            accumulate(c, jnp.exp(s_ref[c] - tile_lanes(m_ref[...])))
            return 0
        lax.fori_loop(0, n_chunks, body2, 0)

    has_tie = tie_flag_ref[0] > 0
    pl.when(has_tie)(functools.partial(one_pass, True))
    pl.when(jnp.logical_not(has_tie))(functools.partial(one_pass, False))
    l_min = jnp.min(acc_ref[:, LANES:])
    pl.when(jnp.logical_not(l_min > DENOM_SAFE))(two_pass)
    out = acc_ref[:, :LANES] / acc_ref[:, LANES:]
    for h in range(N_ATTN_HEADS):
        o_ref[0, :, h * LANES:(h + 1) * LANES] = out[h * tq:(h + 1) * tq, :].astype(o_ref.dtype)


def _attention(z3, tab_a, tab_i, topk):
    B, S, _ = z3.shape
    tq = min(256, S)
    tk = min(256, S)
    assert S % (2 * tk) == 0 and S % tq == 0 and tk % tq == 0, "key chunks are consumed in pairs"
    qblk = lambda w, col: pl.BlockSpec((1, tq, w), lambda b, q: (b, q, col // w))
    kblk = lambda w, col: pl.BlockSpec((1, S, w), lambda b, q: (b, 0, col // w))
    tab_q = pl.BlockSpec((1, tq, 2 * LANES), lambda b, q: (b, q, 0))
    tab_k = pl.BlockSpec((1, S, 2 * LANES), lambda b, q: (b, 0, 0))
    rows = N_ATTN_HEADS * tq
    const = lambda a: pl.BlockSpec(a.shape, lambda b, q: (0, 0))
    perm_a = _swap_matrix(ATTN_HEAD_DIM, ATTN_ROT_HALF)
    perm_i = _swap_matrix(IDX_HEAD_DIM, IDX_ROT_HALF)
    idx_w_scale = (N_IDX_HEADS ** -0.5) * (IDX_HEAD_DIM ** -0.5)
    assert math.frexp(idx_w_scale)[0] == 0.5, "the folded scale must be a power of two to stay exact in bf16"
    expand = jnp.where(jnp.arange(LANES)[:, None] == IDX_HEAD_DIM + jnp.arange(N_IDX_HEADS * LANES)[None, :] // LANES,
                       idx_w_scale, 0.0).astype(BF16)
    return pl.pallas_call(
        functools.partial(_attn_kernel, topk=topk, tq=tq, tk=tk),
        grid=(B, S // tq),
        in_specs=[qblk(ATTN_WIDTH, COL_AQ), qblk(IDX_WIDTH, COL_IQ), qblk(LANES, COL_IKW),
                  kblk(LANES, COL_AK), kblk(LANES, COL_AV), kblk(LANES, COL_IKW),
                  tab_q, tab_k, tab_q, tab_k, const(perm_a), const(perm_i), const(expand)],
        out_specs=pl.BlockSpec((1, tq, ATTN_WIDTH), lambda b, q: (b, q, 0)),
        out_shape=jax.ShapeDtypeStruct((B, S, ATTN_WIDTH), BF16),
        scratch_shapes=[pltpu.VMEM((S, LANES), BF16),
                        pltpu.VMEM((S, LANES), BF16),
                        pltpu.VMEM((S, LANES), BF16),
                        pltpu.VMEM((S, 2 * LANES), BF16),
                        pltpu.VMEM((rows, LANES), BF16),
                        pltpu.VMEM((tq, IDX_WIDTH), BF16),
                        pltpu.VMEM((N_IDX_HEADS, tq, LANES), F32),
                        pltpu.VMEM((S // tk, tq, tk), F32),
                        pltpu.VMEM((S, tq), F32),
                        pltpu.VMEM((1, tq), F32),
                        pltpu.VMEM((1, tq), F32),
                        pltpu.VMEM((S // tk, rows, tk), F32),
                        pltpu.VMEM((rows, LANES), F32),
                        pltpu.VMEM((rows, 2 * LANES), F32),
                        pltpu.VMEM((rows, LANES), F32),
                        pltpu.VMEM((8, LANES), F32),
                        pltpu.SMEM((1,), jnp.int32)],
        compiler_params=_cparams(2),
        name="dsa_attention",
    )(z3, z3, z3, z3, z3, z3, tab_a, tab_a, tab_i, tab_i, perm_a, perm_i, expand)


def _ret_kernel(rq_ref, rk_ref, rv_ref, rg_ref, tab_ref, decay_ref, zeta_ref, xi_ref, gch_ref, gn_ref,
                o_ref, state_ref, *, chunk):
    S = rq_ref.shape[1]
    C = chunk
    half = RET_KEY_DIM
    state_ref[...] = jnp.zeros(state_ref.shape, F32)
    lane = lax.broadcasted_iota(jnp.int32, (C, LANES), 1)
    first = lane < half

    def body(c, _):
        r0 = pl.multiple_of(c * C, C)
        sl = pl.ds(r0, C)
        cos, sin = tab_ref[0, sl, :LANES], tab_ref[0, sl, LANES:]
        cos_k, sin_k = cos * (RET_KEY_DIM ** -0.5), sin * (RET_KEY_DIM ** -0.5)
        first_half = lane % RET_KEY_DIM < RET_KEY_DIM // 2

        def rope(x, c, s):
            partner = jnp.where(first_half, pltpu.roll(x, LANES - RET_KEY_DIM // 2, 1),
                                pltpu.roll(x, RET_KEY_DIM // 2, 1))
            return x * c + partner * s

        for j in range(N_RET_HEADS // 2):
            q = rope(rq_ref[0, sl, j * LANES:(j + 1) * LANES].astype(F32), cos, sin)
            k = rope(rk_ref[0, sl, j * LANES:(j + 1) * LANES].astype(F32), cos_k, sin_k)
            kzt = (k * zeta_ref[j]).T.astype(BF16)
            state = state_ref[j]
            state_b = state.astype(BF16)
            new_rows = []
            for par in range(2):
                h = 2 * j + par
                keep = first if par == 0 else jnp.logical_not(first)
                qh = jnp.where(keep, q, 0.0).astype(BF16)
                kh = jnp.where(keep, k, 0.0).astype(BF16)
                v = rv_ref[0, sl, h * LANES:(h + 1) * LANES]
                inner = _dot_t(qh, kh) * decay_ref[h]
                o = _dot(inner.astype(BF16), v) + _dot(qh, state_b) * xi_ref[h]
                new_rows.append(_dot(kzt[par * half:(par + 1) * half, :], v))
                mu = jnp.mean(o, axis=-1, keepdims=True)
                d = o - mu
                var = jnp.mean(d * d, axis=-1, keepdims=True)
                y = d * lax.rsqrt(var + GN_EPS) * gn_ref[:, h * LANES:(h + 1) * LANES]
                g = rg_ref[0, sl, h * LANES:(h + 1) * LANES].astype(F32)
                o_ref[0, sl, h * LANES:(h + 1) * LANES] = (y * (g * jax.nn.sigmoid(g))).astype(o_ref.dtype)
            state_ref[j] = gch_ref[j] * state + jnp.concatenate(new_rows, axis=0)
        return 0

    lax.fori_loop(0, S // C, body, 0)


def _retention(z3, tab_r, consts, gn_w, chunk):
    B, S, _ = z3.shape
    decay, zeta, xi, gch = consts
    blk = lambda w, col: pl.BlockSpec((1, S, w), lambda b: (b, 0, col // w))
    const = lambda a: pl.BlockSpec(a.shape, lambda b: (0,) * a.ndim)
    return pl.pallas_call(
        functools.partial(_ret_kernel, chunk=chunk),
        grid=(B,),
        in_specs=[blk(RET_KEY_WIDTH, COL_RQ), blk(RET_KEY_WIDTH, COL_RK), blk(RET_WIDTH, COL_RV),
                  blk(RET_WIDTH, COL_RG), pl.BlockSpec((1, S, 2 * LANES), lambda b: (b, 0, 0)),
                  const(decay), const(zeta), const(xi), const(gch), const(gn_w)],
        out_specs=pl.BlockSpec((1, S, RET_WIDTH), lambda b: (b, 0, 0)),
        out_shape=jax.ShapeDtypeStruct((B, S, RET_WIDTH), BF16),
        scratch_shapes=[pltpu.VMEM((N_RET_HEADS // 2, LANES, RET_VAL_DIM), F32)],
        compiler_params=_cparams(1),
        name="retention",
    )(z3, z3, z3, z3, tab_r, decay, zeta, xi, gch, gn_w)


def _retention_consts(chunk):
    C = chunk
    log_g = jnp.log(1.0 - 2.0 ** (-5.0 - jnp.arange(N_RET_HEADS, dtype=F32)))
    i = jnp.arange(C, dtype=F32)
    diff = i[:, None] - i[None, :]
    decay = jnp.where(diff[None] >= 0, jnp.exp(jnp.maximum(diff, 0.0)[None] * log_g[:, None, None]), 0.0)
    zeta = jnp.exp((C - 1.0 - i)[None, :] * log_g[:, None])
    xi = jnp.exp((i + 1.0)[None, :] * log_g[:, None])
    g_chunk = jnp.exp(C * log_g)
    pair = lambda a: a.reshape(N_RET_HEADS // 2, 2, -1)
    zeta_p = jnp.repeat(pair(zeta).transpose(0, 2, 1), RET_KEY_DIM, axis=2)
    xi_b = jnp.broadcast_to(xi[:, :, None], (N_RET_HEADS, C, RET_VAL_DIM))
    gch = jnp.broadcast_to(jnp.repeat(pair(g_chunk), RET_KEY_DIM, axis=1).reshape(N_RET_HEADS // 2, LANES, 1),
                           (N_RET_HEADS // 2, LANES, RET_VAL_DIM))
    return decay.astype(F32), zeta_p.astype(F32), xi_b.astype(F32), gch.astype(F32)


def _out_proj_kernel(attn_ref, ret_ref, h_ref, wa_ref, wr_ref, g1_ref, g2_ref, h1_ref, m_ref, *, sub):
    for r0 in range(0, h_ref.shape[0], sub):
        sl = slice(r0, r0 + sub)
        mix = _dot(attn_ref[sl, :], wa_ref[...]) + _dot(ret_ref[sl, :], wr_ref[...])
        h1 = h_ref[sl, :] + _rms(mix, g1_ref[...])
        h1_ref[sl, :] = h1
        m_ref[sl, :] = _rms(h1, g2_ref[...]).astype(m_ref.dtype)


def _out_proj(attn, ret, h, wo, g1, g2):
    T, D = h.shape
    tm = min(512, T)
    wa_rows, wr_rows = attn.shape[1], ret.shape[1]
    assert wa_rows == wr_rows and wo.shape[0] == wa_rows + wr_rows
    row = lambda w: pl.BlockSpec((tm, w), lambda i: (i, 0))
    const = lambda a: pl.BlockSpec(a.shape, lambda i: (0, 0))
    return pl.pallas_call(
        functools.partial(_out_proj_kernel, sub=min(256, tm)),
        grid=(T // tm,),
        in_specs=[row(wa_rows), row(wr_rows), row(D),
                  pl.BlockSpec((wa_rows, D), lambda i: (0, 0)), pl.BlockSpec((wr_rows, D), lambda i: (1, 0)),
                  const(g1), const(g2)],
        out_specs=[row(D), row(D)],
        out_shape=[jax.ShapeDtypeStruct((T, D), F32), jax.ShapeDtypeStruct((T, D), BF16)],
        compiler_params=_cparams(1),
        name="out_proj",
    )(attn, ret, h, wo, wo, g1, g2)


def _ffn_kernel(m_ref, w1_ref, w2_ref, h_ref, g_ref, o_ref, acc_ref):
    f = pl.program_id(1)

    @pl.when(f == 0)
    def _():
        acc_ref[...] = jnp.zeros(acc_ref.shape, F32)

    u = jnp.maximum(_dot(m_ref[...], w1_ref[...]), 0.0)
    acc_ref[...] += _dot((u * u).astype(BF16), w2_ref[...])

    @pl.when(f == pl.num_programs(1) - 1)
    def _():
        o_ref[...] = h_ref[...] + _rms(acc_ref[...], g_ref[...])


def _ffn(m, w1, w2, h, g):
    T, D = h.shape
    F = w1.shape[1]
    tm = min(512, T)
    tf = min(1024, F)
    return pl.pallas_call(
        _ffn_kernel,
        grid=(T // tm, F // tf),
        in_specs=[pl.BlockSpec((tm, D), lambda i, f: (i, 0)),
                  pl.BlockSpec((D, tf), lambda i, f: (0, f)),
                  pl.BlockSpec((tf, D), lambda i, f: (f, 0)),
                  pl.BlockSpec((tm, D), lambda i, f: (i, 0)),
                  pl.BlockSpec((1, D), lambda i, f: (0, 0))],
        out_specs=pl.BlockSpec((tm, D), lambda i, f: (i, 0)),
        out_shape=jax.ShapeDtypeStruct((T, D), F32),
        scratch_shapes=[pltpu.VMEM((tm, D), F32)],
        compiler_params=_cparams(2),
        name="ffn",
    )(m, w1, w2, h, g)


def _ple_kernel(h_ref, p_ref, wg_ref, wp_ref, g_ref, o_ref, *, sub):
    for r0 in range(0, h_ref.shape[0], sub):
        sl = slice(r0, r0 + sub)
        h = h_ref[sl, :]
        gate = jax.nn.sigmoid(_dot(h.astype(BF16), wg_ref[...]))
        e = _dot(p_ref[sl, :].astype(BF16), wp_ref[...])
        o_ref[sl, :] = h + _rms(gate * e, g_ref[...])


def _ple(h, p, layer, wg, wp, g):
    T, D = h.shape
    tm = min(512, T)
    row = lambda w: pl.BlockSpec((tm, w), lambda i: (i, 0))
    const = lambda a: pl.BlockSpec(a.shape, lambda i: (0, 0))
    return pl.pallas_call(
        functools.partial(_ple_kernel, sub=min(256, tm)),
        grid=(T // tm,),
        in_specs=[row(D), pl.BlockSpec((None, tm, p.shape[2]), lambda i: (layer, i, 0)),
                  const(wg), const(wp), const(g)],
        out_specs=row(D),
        out_shape=jax.ShapeDtypeStruct((T, D), F32),
        compiler_params=_cparams(1),
        name="ple",
    )(h, p, wg, wp, g)


def _rope_table(positions, half, theta, group):
    inv = theta ** (-jnp.arange(half, dtype=F32) / half)
    ang = positions.astype(F32)[..., None] * inv
    compact = jnp.concatenate([jnp.cos(ang), jnp.sin(ang), jnp.ones_like(ang[..., :1])], axis=-1)
    spread = np.zeros((2 * half + 1, 2 * LANES), np.float32)
    for lane in range(LANES):
        d = lane % group
        if d < 2 * half:
            spread[d % half, lane] = 1.0
            spread[half + d % half, LANES + lane] = -1.0 if d < half else 1.0
        else:
            spread[2 * half, lane] = 1.0
    return jnp.einsum("bsk,kn->bsn", compact, jnp.asarray(spread), precision=lax.Precision.HIGHEST)


def _reorder_w_in(w):
    w = w.astype(BF16)
    a0 = ATTN_WIDTH
    a1 = a0 + 2 * ATTN_HEAD_DIM
    a2 = a1 + IDX_WIDTH
    a3 = a2 + IDX_HEAD_DIM + N_IDX_HEADS
    a4 = a3 + 2 * RET_KEY_WIDTH
    pad = jnp.zeros((w.shape[0], Z_WIDTH - Z_USED + LANES - IDX_HEAD_DIM - N_IDX_HEADS), BF16)
    return jnp.concatenate([w[:, :a0], w[:, a1:a2], w[:, a4:], w[:, a3:a4], w[:, a0:a1], w[:, a2:a3], pad], axis=1)


def kernel(x, p, positions, w_in, w_out, w_ff1, w_ff2, w_ple, w_ple_gate, pre_mix_norm, post_mix_norm,
           pre_ff_norm, post_ff_norm, ple_norm, ret_gn):
    B, S, D = x.shape
    depth = w_in.shape[0]
    T = B * S
    topk = min(TOPK_MAX, S // 4)
    ret_chunk = min(256, S)
    tab_a = _rope_table(positions, ATTN_ROT_HALF, ROPE_THETA, ATTN_HEAD_DIM)
    tab_i = _rope_table(positions, IDX_ROT_HALF, ROPE_THETA, IDX_HEAD_DIM)
    tab_r = _rope_table(positions, RET_KEY_DIM // 2, RET_THETA, RET_KEY_DIM)
    ret_consts = _retention_consts(ret_chunk)
    vec = lambda a: a.reshape(1, -1).astype(F32)

    h = x.reshape(T, D)
    for i in range(depth):
        z = _norm_proj(h, vec(pre_mix_norm[i]), _reorder_w_in(w_in[i]))
        z3 = z.reshape(B, S, Z_WIDTH)
        attn = _attention(z3, tab_a, tab_i, topk)
        ret = _retention(z3, tab_r, ret_consts, vec(ret_gn[i]), ret_chunk)
        h, m = _out_proj(attn.reshape(T, ATTN_WIDTH), ret.reshape(T, RET_WIDTH), h,
                         w_out[i].astype(BF16), vec(post_mix_norm[i]), vec(pre_ff_norm[i]))
        h = _ffn(m, w_ff1[i].astype(BF16), w_ff2[i].astype(BF16), h, vec(post_ff_norm[i]))
        h = _ple(h, p.reshape(depth, T, -1), i, w_ple_gate[i].astype(BF16), w_ple[i].astype(BF16),
                 vec(ple_norm[i]))
    return h.reshape(B, S, D)
```

```python
import functools
import math

import jax
import jax.numpy as jnp
import numpy as np
from jax import lax
from jax.experimental import pallas as pl
from jax.experimental.pallas import tpu as pltpu

N_ATTN_HEADS = 8
ATTN_HEAD_DIM = 128
ATTN_ROT_HALF = 16
ROPE_THETA = 500000.0
TOPK_MAX = 256
N_IDX_HEADS = 16
IDX_HEAD_DIM = 64
IDX_ROT_HALF = 8
N_RET_HEADS = 8
RET_KEY_DIM = 64
RET_VAL_DIM = 128
RET_THETA = 10000.0
ATTN_WIDTH = N_ATTN_HEADS * ATTN_HEAD_DIM
IDX_WIDTH = N_IDX_HEADS * IDX_HEAD_DIM
RET_KEY_WIDTH = N_RET_HEADS * RET_KEY_DIM
RET_WIDTH = N_RET_HEADS * RET_VAL_DIM
RMS_EPS = 1e-6
GN_EPS = 1e-5

LANES = 128
COL_AQ = 0
COL_IQ = COL_AQ + ATTN_WIDTH
COL_RV = COL_IQ + IDX_WIDTH
COL_RG = COL_RV + RET_WIDTH
COL_RQ = COL_RG + RET_WIDTH
COL_RK = COL_RQ + RET_KEY_WIDTH
COL_AK = COL_RK + RET_KEY_WIDTH
COL_AV = COL_AK + ATTN_HEAD_DIM
COL_IKW = COL_AV + ATTN_HEAD_DIM
Z_USED = COL_IKW + LANES
Z_TILE_N = 512
Z_WIDTH = -(-Z_USED // Z_TILE_N) * Z_TILE_N

MASK_NEG = -1e30
THETA_ALL = -3e38
CUT_ALL = 1e9
BOUND_SLACK = 1.05
DENOM_SAFE = 1e-20
SEARCH_MAX_STEPS = 320
CAST_BLOCK_BYTES = 4 * 1024 * 1024
V7X_VMEM_BYTES = 64 * 1024 * 1024
VMEM_LIMIT = V7X_VMEM_BYTES - 8 * 1024 * 1024

BF16 = jnp.bfloat16
F32 = jnp.float32


def _cparams(n_grid):
    return pltpu.CompilerParams(dimension_semantics=("arbitrary",) * n_grid,
                                vmem_limit_bytes=VMEM_LIMIT)


def _rms(x, gain):
    ms = jnp.mean(x * x, axis=-1, keepdims=True)
    return x * lax.rsqrt(ms + RMS_EPS) * gain


def _rope(xb, cos, sin, swap):
    return xb.astype(F32) * cos + jnp.dot(xb, swap, preferred_element_type=F32) * sin


def _swap_matrix(group, half):
    l = jnp.arange(LANES)
    d = l % group
    partner = jnp.where(d < half, l + half, jnp.where(d < 2 * half, l - half, -1))
    return (jnp.arange(LANES)[:, None] == partner[None, :]).astype(BF16)


def _dot_t(a, b):
    return lax.dot_general(a, b, (((1,), (1,)), ((), ())), preferred_element_type=F32)


def _dot(a, b):
    return jnp.dot(a, b, preferred_element_type=F32)


def _norm_proj_kernel(h_ref, g_ref, w_ref, z_ref, *, sub, tn):
    for r0 in range(0, h_ref.shape[0], sub):
        sl = slice(r0, r0 + sub)
        a = _rms(h_ref[sl, :], g_ref[...]).astype(BF16)
        for c0 in range(0, w_ref.shape[1], tn):
            z_ref[sl, c0:c0 + tn] = _dot(a, w_ref[:, c0:c0 + tn]).astype(z_ref.dtype)


def _norm_proj(h, gain, w):
    T, D = h.shape
    N = w.shape[1]
    tm = min(512, T)
    return pl.pallas_call(
        functools.partial(_norm_proj_kernel, sub=min(256, tm), tn=Z_TILE_N),
        grid=(T // tm,),
        in_specs=[pl.BlockSpec((tm, D), lambda i: (i, 0)),
                  pl.BlockSpec((1, D), lambda i: (0, 0)),
                  pl.BlockSpec((D, N), lambda i: (0, 0), pipeline_mode=pl.Buffered(1))],
        out_specs=pl.BlockSpec((tm, N), lambda i: (i, 0)),
        out_shape=jax.ShapeDtypeStruct((T, N), BF16),
        compiler_params=_cparams(1),
        name="norm_proj",
    )(h, gain, w)


def _attn_kernel(aq_ref, iq_ref, iwq_ref, ak_ref, av_ref, ikw_ref,
                 taq_ref, tak_ref, tiq_ref, tik_ref, pa_ref, pi_ref, ex_ref, o_ref,
                 kr_ref, kd0_ref, kd1_ref, vaug_ref, qa_ref, qi_ref, wb_ref, score_ref, score_t_ref,
                 th_ref, cut_ref, s_ref, m_ref, acc_ref, bound_ref, ksq_ref, tie_flag_ref, *, topk, tq, tk):
    S = ak_ref.shape[1]
    qi = pl.program_id(1)
    n_chunks = (qi * tq + tq + tk - 1) // tk
    reps = tk // LANES
    rows_all = N_ATTN_HEADS * tq
    ones_sq = jnp.ones((LANES, LANES), BF16)

    @pl.when(qi == 0)
    def _():
        rows = min(256, S)

        def body(r, _):
            r0 = pl.multiple_of(r * rows, rows)
            sl = pl.ds(r0, rows)
            kr = _rope(ak_ref[0, sl, :], tak_ref[0, sl, :LANES], tak_ref[0, sl, LANES:], pa_ref[...])
            kr_ref[sl, :] = kr.astype(BF16)
            ksq = _dot((kr * kr).astype(BF16), ones_sq).reshape(rows // 8, 8, LANES)
            ksq_ref[...] = jnp.maximum(jnp.where(r == 0, 0.0, ksq_ref[...]), jnp.max(ksq, axis=0))
            lane_r = lax.broadcasted_iota(jnp.int32, (rows, LANES), 1)
            ik = _rope(ikw_ref[0, sl, :], tik_ref[0, sl, :LANES], tik_ref[0, sl, LANES:], pi_ref[...])
            ik = jnp.where(lane_r < IDX_HEAD_DIM, ik, 0.0)
            kd0_ref[sl, :] = ik.astype(BF16)
            kd1_ref[sl, :] = pltpu.roll(ik, IDX_HEAD_DIM, 1).astype(BF16)
            vaug_ref[sl, :] = jnp.concatenate([av_ref[0, sl, :], jnp.ones((rows, LANES), BF16)], axis=1)
            score_t_ref[sl, :] = jnp.full((rows, tq), -jnp.inf, F32)
            return 0
        lax.fori_loop(0, S // rows, body, 0)

        def fill(c, _):
            score_ref[c] = jnp.full((tq, tk), -jnp.inf, F32)
            return 0
        lax.fori_loop(0, S // tk, fill, 0)

    ca = taq_ref[0, :, :LANES] * (ATTN_HEAD_DIM ** -0.5)
    sa = taq_ref[0, :, LANES:] * (ATTN_HEAD_DIM ** -0.5)
    k_norm = jnp.sqrt(jnp.max(ksq_ref[...], axis=0, keepdims=True))
    for h in range(N_ATTN_HEADS):
        x = aq_ref[0, :, h * LANES:(h + 1) * LANES]
        qf = _rope(x, ca, sa, pa_ref[...])
        qa_ref[h * tq:(h + 1) * tq, :] = qf.astype(BF16)
        q_norm = jnp.sqrt(_dot((qf * qf).astype(BF16), ones_sq))
        bound_ref[h * tq:(h + 1) * tq, :] = q_norm * k_norm * BOUND_SLACK
    ci, si = tiq_ref[0, :, :LANES], tiq_ref[0, :, LANES:]
    for j in range(IDX_WIDTH // LANES):
        x = iq_ref[0, :, j * LANES:(j + 1) * LANES]
        qi_ref[:, j * LANES:(j + 1) * LANES] = _rope(x, ci, si, pi_ref[...]).astype(BF16)
    iw_b = jnp.dot(iwq_ref[0], ex_ref[...], preferred_element_type=F32)
    for h in range(N_IDX_HEADS):
        wb_ref[h] = iw_b[:, h * LANES:(h + 1) * LANES]

    col_in_chunk = lax.broadcasted_iota(jnp.int32, (tq, tk), 1)

    def lane_groups(x, op):
        r = x[:, :LANES]
        for g in range(1, reps):
            r = op(r, x[:, g * LANES:(g + 1) * LANES])
        return r

    def tile_lanes(x):
        return jnp.concatenate([x] * reps, axis=1) if reps > 1 else x

    key_t = lax.broadcasted_iota(jnp.int32, (tk, tq), 0)

    def fold8(x, op):
        x3 = x.reshape(tk // 8, 8, x.shape[1])
        return jnp.max(x3, axis=0) if op == "max" else (jnp.min(x3, axis=0) if op == "min" else jnp.sum(x3, axis=0))

    rb = min(LANES, tq)
    row_b = lax.broadcasted_iota(jnp.int32, (rb, tk), 0)
    col_b = lax.broadcasted_iota(jnp.int32, (rb, tk), 1)
    key_tb = lax.broadcasted_iota(jnp.int32, (tk, rb), 0)
    qry_tb = lax.broadcasted_iota(jnp.int32, (tk, rb), 1)

    def a_body(c, carry):
        rmax, rmin = carry
        c0 = pl.multiple_of(c * tk, tk)
        k0 = kd0_ref[pl.ds(c0, tk), :]
        k1 = kd1_ref[pl.ds(c0, tk), :]
        maxs, mins = [], []
        for r in range(tq // rb):
            rows = slice(r * rb, (r + 1) * rb)
            q0 = qi * tq + r * rb
            acc = jnp.zeros((rb, tk), F32)
            for j in range(IDX_WIDTH // LANES):
                qp = qi_ref[rows, j * LANES:(j + 1) * LANES]
                for par, kd in ((0, k0), (1, k1)):
                    logits = _dot_t(qp, kd)
                    acc = acc + tile_lanes(wb_ref[2 * j + par, rows, :]) * jnp.maximum(logits, 0.0)
            causal = (c0 + col_b) <= (q0 + row_b)
            score_ref[c, rows, :] = jnp.where(causal, acc, -jnp.inf)
            acc_t = acc.T
            causal_t = (c0 + key_tb) <= (q0 + qry_tb)
            sc_t = jnp.where(causal_t, acc_t, -jnp.inf)
            score_t_ref[pl.ds(c0, tk), rows] = sc_t
            maxs.append(fold8(sc_t, "max"))
            mins.append(fold8(jnp.where(causal_t, acc_t, jnp.inf), "min"))
        rmax = jnp.maximum(rmax, jnp.concatenate(maxs, axis=1))
        rmin = jnp.minimum(rmin, jnp.concatenate(mins, axis=1))
        return rmax, rmin

    rmax, rmin = lax.fori_loop(0, n_chunks, a_body,
                               (jnp.full((8, tq), -jnp.inf, F32), jnp.full((8, tq), jnp.inf, F32)))

    vec = (1, tq)
    kf = float(topk)
    n_keys = qi * tq + lax.broadcasted_iota(jnp.int32, vec, 1) + 1

    lo0 = jnp.min(rmin, axis=0, keepdims=True)
    hi0 = jnp.max(rmax, axis=0, keepdims=True)
    key_f = key_t.astype(F32)

    def threshold_search(n_rows):
        def count_t(pred):
            accs = [jnp.zeros((8, tq), F32) for _ in range(4)]
            i = 0
            for r0 in range(0, n_rows, tk):
                ind = jnp.where(pred(r0, score_t_ref[r0:r0 + tk, :]), 1.0, 0.0)
                for g in range(tk // 8):
                    accs[i % 4] = accs[i % 4] + ind[g * 8:(g + 1) * 8, :]
                    i += 1
            return jnp.sum((accs[0] + accs[1]) + (accs[2] + accs[3]), axis=0, keepdims=True)

        def b_cond(carry):
            return jnp.logical_and(carry[1] > 0.0, carry[0] < SEARCH_MAX_STEPS)

        def b_body(carry):
            state = (carry[0],) + tuple(carry[2:])
            flag = jnp.max(1.0 - carry[5])
            for _ in range(steps_per_check):
                state = b_step(*state)
            return (state[0], flag) + tuple(state[1:])

        def b_step(it, lo, hi, chi, done, th, tie):
            mid = jnp.where(it == 0, hi, 0.5 * lo + 0.5 * hi)
            cnt = count_t(lambda r0, sc: sc >= mid)
            active = done == 0.0
            adjacent = jnp.logical_and(it > 0, jnp.logical_or(mid <= lo, mid >= hi))
            tie_now = jnp.logical_and(active, adjacent)
            live = jnp.logical_and(active, jnp.logical_not(adjacent))
            hit = jnp.logical_and(live, cnt == kf)
            above = jnp.logical_and(live, cnt > kf)
            below = jnp.logical_and(live, cnt < kf)
            th = jnp.where(tie_now, lo, jnp.where(hit, mid, th))
            tie = jnp.where(tie_now, 1.0, tie)
            lo = jnp.where(above, mid, lo)
            hi = jnp.where(below, mid, hi)
            chi = jnp.where(below, cnt, chi)
            done = jnp.where(jnp.logical_or(tie_now, hit), 1.0, done)
            return it + 1, lo, hi, chi, done, th, tie

        steps_per_check = 2 if n_rows <= 4 * tk else 1

        init = (jnp.int32(0), jnp.float32(1.0), lo0, hi0,
                jnp.zeros(vec, F32), jnp.where(n_keys <= topk, 1.0, 0.0),
                jnp.full(vec, THETA_ALL, F32), jnp.zeros(vec, F32))
        _, _, _, _, chi, _, th, tie = lax.while_loop(b_cond, b_body, init)

        def tie_cut():
            need = kf - chi

            def body(_, carry):
                lo_i, hi_i = carry
                mid = jnp.floor(0.5 * (lo_i + hi_i))
                cnt = count_t(lambda r0, sc: jnp.logical_and(sc == th, key_f + float(r0) <= mid))
                ge = cnt >= need
                return jnp.where(ge, lo_i, mid), jnp.where(ge, mid, hi_i)

            n_iter = int(math.ceil(math.log2(S))) + 1
            _, hi_i = lax.fori_loop(0, n_iter, body, (jnp.full(vec, -1.0, F32), jnp.full(vec, float(S - 1), F32)))
            return jnp.where(tie > 0.0, hi_i, CUT_ALL)

        th_ref[...] = th
        has_tie = jnp.max(tie) > 0.0
        tie_flag_ref[0] = has_tie.astype(jnp.int32)
        cut_ref[...] = lax.cond(has_tie, tie_cut, lambda: jnp.full(vec, CUT_ALL, F32))

    pair_rows = 2 * tk
    variant = (n_chunks * tk + pair_rows - 1) // pair_rows - 1
    for v in range(-(-S // pair_rows)):
        pl.when(variant == v)(functools.partial(threshold_search, min((v + 1) * pair_rows, S)))

    th_c = jnp.broadcast_to(th_ref[...], (LANES, tq)).T
    cut_c = jnp.broadcast_to(cut_ref[...], (LANES, tq)).T
    th_t = tile_lanes(th_c)
    cut_t = tile_lanes(cut_c)

    colf_in_chunk = col_in_chunk.astype(F32)

    def masked_logits(c, with_ties):
        c0 = pl.multiple_of(c * tk, tk)
        sc = score_ref[c]
        if with_ties:
            colf = colf_in_chunk + (c * tk).astype(F32)
            sel = jnp.logical_or(sc > th_t, jnp.logical_and(sc == th_t, colf <= cut_t))
        else:
            sel = sc >= th_t
        bias = jnp.where(sel, 0.0, MASK_NEG)
        s = _dot_t(qa_ref[...], kr_ref[pl.ds(c0, tk), :])
        return (s.reshape(N_ATTN_HEADS, tq, tk) + bias[None]).reshape(rows_all, tk)

    def accumulate(c, p):
        c0 = pl.multiple_of(c * tk, tk)
        acc_ref[...] += _dot(p.astype(BF16), vaug_ref[pl.ds(c0, tk), :])

    def one_pass(with_ties):
        acc_ref[...] = jnp.zeros(acc_ref.shape, F32)
        shift = tile_lanes(bound_ref[...])

        def body(cp, _):
            c = 2 * cp
            p0 = jnp.exp(masked_logits(c, with_ties) - shift).astype(BF16)
            p1 = jnp.exp(masked_logits(c + 1, with_ties) - shift).astype(BF16)
            r0 = pl.multiple_of(c * tk, 2 * tk)
            acc_ref[...] += _dot(jnp.concatenate([p0, p1], axis=1), vaug_ref[pl.ds(r0, 2 * tk), :])
            return 0
        lax.fori_loop(0, (n_chunks + 1) // 2, body, 0)

    def two_pass():
        m_ref[...] = jnp.full(m_ref.shape, MASK_NEG, F32)

        def body1(c, _):
            s = masked_logits(c, True)
            s_ref[c] = s
            m_ref[...] = jnp.maximum(m_ref[...], lane_groups(s, jnp.maximum))
            return 0
        lax.fori_loop(0, n_chunks, body1, 0)
        m_ref[...] = jnp.broadcast_to(jnp.max(m_ref[...], axis=1, keepdims=True), m_ref.shape)
        acc_ref[...] = jnp.zeros(acc_ref.shape, F32)

        def body2(c, _):
            accumulate(c, jnp.exp(s_ref[c] - tile_lanes(m_ref[...])))
            return 0
        lax.fori_loop(0, n_chunks, body2, 0)

    has_tie = tie_flag_ref[0] > 0
    pl.when(has_tie)(functools.partial(one_pass, True))
    pl.when(jnp.logical_not(has_tie))(functools.partial(one_pass, False))
    l_min = jnp.min(acc_ref[:, LANES:])
    pl.when(jnp.logical_not(l_min > DENOM_SAFE))(two_pass)
    out = acc_ref[:, :LANES] / acc_ref[:, LANES:]
    for h in range(N_ATTN_HEADS):
        o_ref[0, :, h * LANES:(h + 1) * LANES] = out[h * tq:(h + 1) * tq, :].astype(o_ref.dtype)


def _attention(z3, tab_a, tab_i, topk):
    B, S, _ = z3.shape
    tq = min(256, S)
    tk = min(256, S)
    assert S % (2 * tk) == 0 and S % tq == 0 and tk % tq == 0, "key chunks are consumed in pairs"
    qblk = lambda w, col: pl.BlockSpec((1, tq, w), lambda b, q: (b, q, col // w))
    kblk = lambda w, col: pl.BlockSpec((1, S, w), lambda b, q: (b, 0, col // w))
    tab_q = pl.BlockSpec((1, tq, 2 * LANES), lambda b, q: (b, q, 0))
    tab_k = pl.BlockSpec((1, S, 2 * LANES), lambda b, q: (b, 0, 0))
    rows = N_ATTN_HEADS * tq
    const = lambda a: pl.BlockSpec(a.shape, lambda b, q: (0, 0))
    perm_a = _swap_matrix(ATTN_HEAD_DIM, ATTN_ROT_HALF)
    perm_i = _swap_matrix(IDX_HEAD_DIM, IDX_ROT_HALF)
    idx_w_scale = (N_IDX_HEADS ** -0.5) * (IDX_HEAD_DIM ** -0.5)
    assert math.frexp(idx_w_scale)[0] == 0.5, "the folded scale must be a power of two to stay exact in bf16"
    expand = jnp.where(jnp.arange(LANES)[:, None] == IDX_HEAD_DIM + jnp.arange(N_IDX_HEADS * LANES)[None, :] // LANES,
                       idx_w_scale, 0.0).astype(BF16)
    return pl.pallas_call(
        functools.partial(_attn_kernel, topk=topk, tq=tq, tk=tk),
        grid=(B, S // tq),
        in_specs=[qblk(ATTN_WIDTH, COL_AQ), qblk(IDX_WIDTH, COL_IQ), qblk(LANES, COL_IKW),
                  kblk(LANES, COL_AK), kblk(LANES, COL_AV), kblk(LANES, COL_IKW),
                  tab_q, tab_k, tab_q, tab_k, const(perm_a), const(perm_i), const(expand)],
        out_specs=pl.BlockSpec((1, tq, ATTN_WIDTH), lambda b, q: (b, q, 0)),
        out_shape=jax.ShapeDtypeStruct((B, S, ATTN_WIDTH), BF16),
        scratch_shapes=[pltpu.VMEM((S, LANES), BF16),
                        pltpu.VMEM((S, LANES), BF16),
                        pltpu.VMEM((S, LANES), BF16),
                        pltpu.VMEM((S, 2 * LANES), BF16),
                        pltpu.VMEM((rows, LANES), BF16),
                        pltpu.VMEM((tq, IDX_WIDTH), BF16),
                        pltpu.VMEM((N_IDX_HEADS, tq, LANES), F32),
                        pltpu.VMEM((S // tk, tq, tk), F32),
                        pltpu.VMEM((S, tq), F32),
                        pltpu.VMEM((1, tq), F32),
                        pltpu.VMEM((1, tq), F32),
                        pltpu.VMEM((S // tk, rows, tk), F32),
                        pltpu.VMEM((rows, LANES), F32),
                        pltpu.VMEM((rows, 2 * LANES), F32),
                        pltpu.VMEM((rows, LANES), F32),
                        pltpu.VMEM((8, LANES), F32),
                        pltpu.SMEM((1,), jnp.int32)],
        compiler_params=_cparams(2),
        name="dsa_attention",
    )(z3, z3, z3, z3, z3, z3, tab_a, tab_a, tab_i, tab_i, perm_a, perm_i, expand)


def _ret_kernel(rq_ref, rk_ref, rv_ref, rg_ref, tab_ref, decay_ref, zeta_ref, xi_ref, gch_ref, gn_ref,
                o_ref, state_ref, *, chunk):
    S = rq_ref.shape[1]
    C = chunk
    half = RET_KEY_DIM
    state_ref[...] = jnp.zeros(state_ref.shape, F32)
    lane = lax.broadcasted_iota(jnp.int32, (C, LANES), 1)
    first = lane < half

    def body(c, _):
        r0 = pl.multiple_of(c * C, C)
        sl = pl.ds(r0, C)
        cos, sin = tab_ref[0, sl, :LANES], tab_ref[0, sl, LANES:]
        cos_k, sin_k = cos * (RET_KEY_DIM ** -0.5), sin * (RET_KEY_DIM ** -0.5)
        first_half = lane % RET_KEY_DIM < RET_KEY_DIM // 2

        def rope(x, c, s):
            partner = jnp.where(first_half, pltpu.roll(x, LANES - RET_KEY_DIM // 2, 1),
                                pltpu.roll(x, RET_KEY_DIM // 2, 1))
            return x * c + partner * s

        for j in range(N_RET_HEADS // 2):
            q = rope(rq_ref[0, sl, j * LANES:(j + 1) * LANES].astype(F32), cos, sin)
            k = rope(rk_ref[0, sl, j * LANES:(j + 1) * LANES].astype(F32), cos_k, sin_k)
            kzt = (k * zeta_ref[j]).T.astype(BF16)
            state = state_ref[j]
            state_b = state.astype(BF16)
            new_rows = []
            for par in range(2):
                h = 2 * j + par
                keep = first if par == 0 else jnp.logical_not(first)
                qh = jnp.where(keep, q, 0.0).astype(BF16)
                kh = jnp.where(keep, k, 0.0).astype(BF16)
                v = rv_ref[0, sl, h * LANES:(h + 1) * LANES]
                inner = _dot_t(qh, kh) * decay_ref[h]
                o = _dot(inner.astype(BF16), v) + _dot(qh, state_b) * xi_ref[h]
                new_rows.append(_dot(kzt[par * half:(par + 1) * half, :], v))
                mu = jnp.mean(o, axis=-1, keepdims=True)
                d = o - mu
                var = jnp.mean(d * d, axis=-1, keepdims=True)
                y = d * lax.rsqrt(var + GN_EPS) * gn_ref[:, h * LANES:(h + 1) * LANES]
                g = rg_ref[0, sl, h * LANES:(h + 1) * LANES].astype(F32)
                o_ref[0, sl, h * LANES:(h + 1) * LANES] = (y * (g * jax.nn.sigmoid(g))).astype(o_ref.dtype)
            state_ref[j] = gch_ref[j] * state + jnp.concatenate(new_rows, axis=0)
        return 0

    lax.fori_loop(0, S // C, body, 0)


def _retention(z3, tab_r, consts, gn_w, chunk):
    B, S, _ = z3.shape
    decay, zeta, xi, gch = consts
    blk = lambda w, col: pl.BlockSpec((1, S, w), lambda b: (b, 0, col // w))
    const = lambda a: pl.BlockSpec(a.shape, lambda b: (0,) * a.ndim)
    return pl.pallas_call(
        functools.partial(_ret_kernel, chunk=chunk),
        grid=(B,),
        in_specs=[blk(RET_KEY_WIDTH, COL_RQ), blk(RET_KEY_WIDTH, COL_RK), blk(RET_WIDTH, COL_RV),
                  blk(RET_WIDTH, COL_RG), pl.BlockSpec((1, S, 2 * LANES), lambda b: (b, 0, 0)),
                  const(decay), const(zeta), const(xi), const(gch), const(gn_w)],
        out_specs=pl.BlockSpec((1, S, RET_WIDTH), lambda b: (b, 0, 0)),
        out_shape=jax.ShapeDtypeStruct((B, S, RET_WIDTH), BF16),
        scratch_shapes=[pltpu.VMEM((N_RET_HEADS // 2, LANES, RET_VAL_DIM), F32)],
        compiler_params=_cparams(1),
        name="retention",
    )(z3, z3, z3, z3, tab_r, decay, zeta, xi, gch, gn_w)


def _retention_consts(chunk):
    C = chunk
    log_g = jnp.log(1.0 - 2.0 ** (-5.0 - jnp.arange(N_RET_HEADS, dtype=F32)))
    i = jnp.arange(C, dtype=F32)
    diff = i[:, None] - i[None, :]
    decay = jnp.where(diff[None] >= 0, jnp.exp(jnp.maximum(diff, 0.0)[None] * log_g[:, None, None]), 0.0)
    zeta = jnp.exp((C - 1.0 - i)[None, :] * log_g[:, None])
    xi = jnp.exp((i + 1.0)[None, :] * log_g[:, None])
    g_chunk = jnp.exp(C * log_g)
    pair = lambda a: a.reshape(N_RET_HEADS // 2, 2, -1)
    zeta_p = jnp.repeat(pair(zeta).transpose(0, 2, 1), RET_KEY_DIM, axis=2)
    xi_b = jnp.broadcast_to(xi[:, :, None], (N_RET_HEADS, C, RET_VAL_DIM))
    gch = jnp.broadcast_to(jnp.repeat(pair(g_chunk), RET_KEY_DIM, axis=1).reshape(N_RET_HEADS // 2, LANES, 1),
                           (N_RET_HEADS // 2, LANES, RET_VAL_DIM))
    return decay.astype(F32), zeta_p.astype(F32), xi_b.astype(F32), gch.astype(F32)


def _out_proj_kernel(attn_ref, ret_ref, h_ref, wa_ref, wr_ref, g1_ref, g2_ref, h1_ref, m_ref, *, sub):
    for r0 in range(0, h_ref.shape[0], sub):
        sl = slice(r0, r0 + sub)
        mix = _dot(attn_ref[sl, :], wa_ref[...]) + _dot(ret_ref[sl, :], wr_ref[...])
        h1 = h_ref[sl, :] + _rms(mix, g1_ref[...])
        h1_ref[sl, :] = h1
        m_ref[sl, :] = _rms(h1, g2_ref[...]).astype(m_ref.dtype)


def _out_proj(attn, ret, h, wo, g1, g2):
    T, D = h.shape
    tm = min(512, T)
    wa_rows, wr_rows = attn.shape[1], ret.shape[1]
    assert wa_rows == wr_rows and wo.shape[0] == wa_rows + wr_rows
    row = lambda w: pl.BlockSpec((tm, w), lambda i: (i, 0))
    const = lambda a: pl.BlockSpec(a.shape, lambda i: (0, 0))
    return pl.pallas_call(
        functools.partial(_out_proj_kernel, sub=min(256, tm)),
        grid=(T // tm,),
        in_specs=[row(wa_rows), row(wr_rows), row(D),
                  pl.BlockSpec((wa_rows, D), lambda i: (0, 0)), pl.BlockSpec((wr_rows, D), lambda i: (1, 0)),
                  const(g1), const(g2)],
        out_specs=[row(D), row(D)],
        out_shape=[jax.ShapeDtypeStruct((T, D), F32), jax.ShapeDtypeStruct((T, D), BF16)],
        compiler_params=_cparams(1),
        name="out_proj",
    )(attn, ret, h, wo, wo, g1, g2)


def _ffn_kernel(m_ref, w1_ref, w2_ref, h_ref, g_ref, o_ref, acc_ref):
    f = pl.program_id(1)

    @pl.when(f == 0)
    def _():
        acc_ref[...] = jnp.zeros(acc_ref.shape, F32)

    u = jnp.maximum(_dot(m_ref[...], w1_ref[...]), 0.0)
    acc_ref[...] += _dot((u * u).astype(BF16), w2_ref[...])

    @pl.when(f == pl.num_programs(1) - 1)
    def _():
        o_ref[...] = h_ref[...] + _rms(acc_ref[...], g_ref[...])


def _ffn(m, w1, w2, h, g):
    T, D = h.shape
    F = w1.shape[1]
    tm = min(512, T)
    tf = min(1024, F)
    return pl.pallas_call(
        _ffn_kernel,
        grid=(T // tm, F // tf),
        in_specs=[pl.BlockSpec((tm, D), lambda i, f: (i, 0)),
                  pl.BlockSpec((D, tf), lambda i, f: (0, f)),
                  pl.BlockSpec((tf, D), lambda i, f: (f, 0)),
                  pl.BlockSpec((tm, D), lambda i, f: (i, 0)),
                  pl.BlockSpec((1, D), lambda i, f: (0, 0))],
        out_specs=pl.BlockSpec((tm, D), lambda i, f: (i, 0)),
        out_shape=jax.ShapeDtypeStruct((T, D), F32),
        scratch_shapes=[pltpu.VMEM((tm, D), F32)],
        compiler_params=_cparams(2),
        name="ffn",
    )(m, w1, w2, h, g)


def _ple_kernel(h_ref, p_ref, wg_ref, wp_ref, g_ref, o_ref, *, sub):
    for r0 in range(0, h_ref.shape[0], sub):
        sl = slice(r0, r0 + sub)
        h = h_ref[sl, :]
        gate = jax.nn.sigmoid(_dot(h.astype(BF16), wg_ref[...]))
        e = _dot(p_ref[sl, :].astype(BF16), wp_ref[...])
        o_ref[sl, :] = h + _rms(gate * e, g_ref[...])


def _ple(h, p, layer, wg, wp, g):
    T, D = h.shape
    tm = min(512, T)
    row = lambda w: pl.BlockSpec((tm, w), lambda i: (i, 0))
    const = lambda a: pl.BlockSpec(a.shape, lambda i: (0, 0))
    return pl.pallas_call(
        functools.partial(_ple_kernel, sub=min(256, tm)),
        grid=(T // tm,),
        in_specs=[row(D), pl.BlockSpec((None, tm, p.shape[2]), lambda i: (layer, i, 0)),
                  const(wg), const(wp), const(g)],
        out_specs=row(D),
        out_shape=jax.ShapeDtypeStruct((T, D), F32),
        compiler_params=_cparams(1),
        name="ple",
    )(h, p, wg, wp, g)


def _rope_table(positions, half, theta, group):
    inv = theta ** (-jnp.arange(half, dtype=F32) / half)
    ang = positions.astype(F32)[..., None] * inv
    compact = jnp.concatenate([jnp.cos(ang), jnp.sin(ang), jnp.ones_like(ang[..., :1])], axis=-1)
    spread = np.zeros((2 * half + 1, 2 * LANES), np.float32)
    for lane in range(LANES):
        d = lane % group
        if d < 2 * half:
            spread[d % half, lane] = 1.0
            spread[half + d % half, LANES + lane] = -1.0 if d < half else 1.0
        else:
            spread[2 * half, lane] = 1.0
    return jnp.einsum("bsk,kn->bsn", compact, jnp.asarray(spread), precision=lax.Precision.HIGHEST)


def _cast_kernel(x_ref, o_ref):
    o_ref[...] = x_ref[...].astype(o_ref.dtype)


def _to_bf16(w, layer):
    _, R, C = w.shape
    tr = max(16, min(R, CAST_BLOCK_BYTES // (4 * C) // 16 * 16))
    while R % tr:
        tr -= 16
    return pl.pallas_call(
        _cast_kernel,
        grid=(R // tr,),
        in_specs=[pl.BlockSpec((None, tr, C), lambda r: (layer, r, 0))],
        out_specs=pl.BlockSpec((tr, C), lambda r: (r, 0)),
        out_shape=jax.ShapeDtypeStruct((R, C), BF16),
        compiler_params=_cparams(1),
        name="to_bf16",
    )(w)


def _reorder_w_in(w):
    a0 = ATTN_WIDTH
    a1 = a0 + 2 * ATTN_HEAD_DIM
    a2 = a1 + IDX_WIDTH
    a3 = a2 + IDX_HEAD_DIM + N_IDX_HEADS
    a4 = a3 + 2 * RET_KEY_WIDTH
    pad = jnp.zeros((w.shape[0], Z_WIDTH - Z_USED + LANES - IDX_HEAD_DIM - N_IDX_HEADS), BF16)
    return jnp.concatenate([w[:, :a0], w[:, a1:a2], w[:, a4:], w[:, a3:a4], w[:, a0:a1], w[:, a2:a3], pad], axis=1)


def kernel(x, p, positions, w_in, w_out, w_ff1, w_ff2, w_ple, w_ple_gate, pre_mix_norm, post_mix_norm,
           pre_ff_norm, post_ff_norm, ple_norm, ret_gn):
    B, S, D = x.shape
    depth = w_in.shape[0]
    T = B * S
    topk = min(TOPK_MAX, S // 4)
    ret_chunk = min(256, S)
    tab_a = _rope_table(positions, ATTN_ROT_HALF, ROPE_THETA, ATTN_HEAD_DIM)
    tab_i = _rope_table(positions, IDX_ROT_HALF, ROPE_THETA, IDX_HEAD_DIM)
    tab_r = _rope_table(positions, RET_KEY_DIM // 2, RET_THETA, RET_KEY_DIM)
    ret_consts = _retention_consts(ret_chunk)
    vec = lambda a: a.reshape(1, -1).astype(F32)

    h = x.reshape(T, D)
    for i in range(depth):
        z = _norm_proj(h, vec(pre_mix_norm[i]), _reorder_w_in(_to_bf16(w_in, i)))
        z3 = z.reshape(B, S, Z_WIDTH)
        attn = _attention(z3, tab_a, tab_i, topk)
        ret = _retention(z3, tab_r, ret_consts, vec(ret_gn[i]), ret_chunk)
        h, m = _out_proj(attn.reshape(T, ATTN_WIDTH), ret.reshape(T, RET_WIDTH), h,
                         _to_bf16(w_out, i), vec(post_mix_norm[i]), vec(pre_ff_norm[i]))
        h = _ffn(m, _to_bf16(w_ff1, i), _to_bf16(w_ff2, i), h, vec(post_ff_norm[i]))
        h = _ple(h, p.reshape(depth, T, -1), i, _to_bf16(w_ple_gate, i), _to_bf16(w_ple, i), vec(ple_norm[i]))
    return h.reshape(B, S, D)
```

```python
import functools
import math

import jax
import jax.numpy as jnp
import numpy as np
from jax import lax
from jax.experimental import pallas as pl
from jax.experimental.pallas import tpu as pltpu

N_ATTN_HEADS = 8
ATTN_HEAD_DIM = 128
ATTN_ROT_HALF = 16
ROPE_THETA = 500000.0
TOPK_MAX = 256
N_IDX_HEADS = 16
IDX_HEAD_DIM = 64
IDX_ROT_HALF = 8
N_RET_HEADS = 8
RET_KEY_DIM = 64
RET_VAL_DIM = 128
RET_THETA = 10000.0
ATTN_WIDTH = N_ATTN_HEADS * ATTN_HEAD_DIM
IDX_WIDTH = N_IDX_HEADS * IDX_HEAD_DIM
RET_KEY_WIDTH = N_RET_HEADS * RET_KEY_DIM
RET_WIDTH = N_RET_HEADS * RET_VAL_DIM
RMS_EPS = 1e-6
GN_EPS = 1e-5

LANES = 128
COL_AQ = 0
COL_IQ = COL_AQ + ATTN_WIDTH
COL_RV = COL_IQ + IDX_WIDTH
COL_RG = COL_RV + RET_WIDTH
COL_RQ = COL_RG + RET_WIDTH
COL_RK = COL_RQ + RET_KEY_WIDTH
COL_AK = COL_RK + RET_KEY_WIDTH
COL_AV = COL_AK + ATTN_HEAD_DIM
COL_IKW = COL_AV + ATTN_HEAD_DIM
Z_USED = COL_IKW + LANES
Z_TILE_N = 512
Z_WIDTH = -(-Z_USED // Z_TILE_N) * Z_TILE_N

MASK_NEG = -1e30
THETA_ALL = -3e38
CUT_ALL = 1e9
BOUND_SLACK = 1.05
DENOM_SAFE = 1e-20
SEARCH_MAX_STEPS = 320
CAST_BLOCK_BYTES = 4 * 1024 * 1024
V7X_VMEM_BYTES = 64 * 1024 * 1024
VMEM_LIMIT = V7X_VMEM_BYTES - 8 * 1024 * 1024

BF16 = jnp.bfloat16
F32 = jnp.float32


def _cparams(n_grid):
    return pltpu.CompilerParams(dimension_semantics=("arbitrary",) * n_grid,
                                vmem_limit_bytes=VMEM_LIMIT)


def _rms(x, gain):
    ms = jnp.mean(x * x, axis=-1, keepdims=True)
    return x * lax.rsqrt(ms + RMS_EPS) * gain


def _rope(xb, cos, sin, swap):
    return xb.astype(F32) * cos + jnp.dot(xb, swap, preferred_element_type=F32) * sin


def _swap_matrix(group, half):
    l = jnp.arange(LANES)
    d = l % group
    partner = jnp.where(d < half, l + half, jnp.where(d < 2 * half, l - half, -1))
    return (jnp.arange(LANES)[:, None] == partner[None, :]).astype(BF16)


def _dot_t(a, b):
    return lax.dot_general(a, b, (((1,), (1,)), ((), ())), preferred_element_type=F32)


def _dot(a, b):
    return jnp.dot(a, b, preferred_element_type=F32)


def _norm_proj_kernel(h_ref, g_ref, w_ref, z_ref, *, sub, tn):
    for r0 in range(0, h_ref.shape[0], sub):
        sl = slice(r0, r0 + sub)
        a = _rms(h_ref[sl, :], g_ref[...]).astype(BF16)
        for c0 in range(0, w_ref.shape[1], tn):
            z_ref[sl, c0:c0 + tn] = _dot(a, w_ref[:, c0:c0 + tn]).astype(z_ref.dtype)


def _norm_proj(h, gain, w):
    T, D = h.shape
    N = w.shape[1]
    tm = min(512, T)
    return pl.pallas_call(
        functools.partial(_norm_proj_kernel, sub=min(256, tm), tn=Z_TILE_N),
        grid=(T // tm,),
        in_specs=[pl.BlockSpec((tm, D), lambda i: (i, 0)),
                  pl.BlockSpec((1, D), lambda i: (0, 0)),
                  pl.BlockSpec((D, N), lambda i: (0, 0), pipeline_mode=pl.Buffered(1))],
        out_specs=pl.BlockSpec((tm, N), lambda i: (i, 0)),
        out_shape=jax.ShapeDtypeStruct((T, N), BF16),
        compiler_params=_cparams(1),
        name="norm_proj",
    )(h, gain, w)


def _attn_kernel(aq_ref, iq_ref, iwq_ref, ak_ref, av_ref, ikw_ref,
                 taq_ref, tak_ref, tiq_ref, tik_ref, pa_ref, pi_ref, ex_ref, o_ref,
                 kr_ref, kd0_ref, kd1_ref, vaug_ref, qa_ref, qi_ref, wb_ref, score_ref, score_t_ref,
                 th_ref, cut_ref, s_ref, m_ref, acc_ref, bound_ref, ksq_ref, tie_flag_ref, *, topk, tq, tk):
    S = ak_ref.shape[1]
    qi = pl.program_id(1)
    n_chunks = (qi * tq + tq + tk - 1) // tk
    reps = tk // LANES
    rows_all = N_ATTN_HEADS * tq
    ones_sq = jnp.ones((LANES, LANES), BF16)

    @pl.when(qi == 0)
    def _():
        rows = min(256, S)

        def body(r, _):
            r0 = pl.multiple_of(r * rows, rows)
            sl = pl.ds(r0, rows)
            kr = _rope(ak_ref[0, sl, :], tak_ref[0, sl, :LANES], tak_ref[0, sl, LANES:], pa_ref[...])
            kr_ref[sl, :] = kr.astype(BF16)
            ksq = _dot((kr * kr).astype(BF16), ones_sq).reshape(rows // 8, 8, LANES)
            ksq_ref[...] = jnp.maximum(jnp.where(r == 0, 0.0, ksq_ref[...]), jnp.max(ksq, axis=0))
            lane_r = lax.broadcasted_iota(jnp.int32, (rows, LANES), 1)
            ik = _rope(ikw_ref[0, sl, :], tik_ref[0, sl, :LANES], tik_ref[0, sl, LANES:], pi_ref[...])
            ik = jnp.where(lane_r < IDX_HEAD_DIM, ik, 0.0)
            kd0_ref[sl, :] = ik.astype(BF16)
            kd1_ref[sl, :] = pltpu.roll(ik, IDX_HEAD_DIM, 1).astype(BF16)
            vaug_ref[sl, :] = jnp.concatenate([av_ref[0, sl, :], jnp.ones((rows, LANES), BF16)], axis=1)
            score_t_ref[sl, :] = jnp.full((rows, tq), -jnp.inf, F32)
            return 0
        lax.fori_loop(0, S // rows, body, 0)

        def fill(c, _):
            score_ref[c] = jnp.full((tq, tk), -jnp.inf, F32)
            return 0
        lax.fori_loop(0, S // tk, fill, 0)

    ca = taq_ref[0, :, :LANES] * (ATTN_HEAD_DIM ** -0.5)
    sa = taq_ref[0, :, LANES:] * (ATTN_HEAD_DIM ** -0.5)
    k_norm = jnp.sqrt(jnp.max(ksq_ref[...], axis=0, keepdims=True))
    for h in range(N_ATTN_HEADS):
        x = aq_ref[0, :, h * LANES:(h + 1) * LANES]
        qf = _rope(x, ca, sa, pa_ref[...])
        qa_ref[h * tq:(h + 1) * tq, :] = qf.astype(BF16)
        q_norm = jnp.sqrt(_dot((qf * qf).astype(BF16), ones_sq))
        bound_ref[h * tq:(h + 1) * tq, :] = q_norm * k_norm * BOUND_SLACK
    ci, si = tiq_ref[0, :, :LANES], tiq_ref[0, :, LANES:]
    for j in range(IDX_WIDTH // LANES):
        x = iq_ref[0, :, j * LANES:(j + 1) * LANES]
        qi_ref[:, j * LANES:(j + 1) * LANES] = _rope(x, ci, si, pi_ref[...]).astype(BF16)
    iw_b = jnp.dot(iwq_ref[0], ex_ref[...], preferred_element_type=F32)
    for h in range(N_IDX_HEADS):
        wb_ref[h] = iw_b[:, h * LANES:(h + 1) * LANES]

    col_in_chunk = lax.broadcasted_iota(jnp.int32, (tq, tk), 1)

    def lane_groups(x, op):
        r = x[:, :LANES]
        for g in range(1, reps):
            r = op(r, x[:, g * LANES:(g + 1) * LANES])
        return r

    def tile_lanes(x):
        return jnp.concatenate([x] * reps, axis=1) if reps > 1 else x

    key_t = lax.broadcasted_iota(jnp.int32, (tk, tq), 0)

    def fold8(x, op):
        x3 = x.reshape(tk // 8, 8, x.shape[1])
        return jnp.max(x3, axis=0) if op == "max" else (jnp.min(x3, axis=0) if op == "min" else jnp.sum(x3, axis=0))

    rb = min(LANES, tq)
    row_b = lax.broadcasted_iota(jnp.int32, (rb, tk), 0)
    col_b = lax.broadcasted_iota(jnp.int32, (rb, tk), 1)
    key_tb = lax.broadcasted_iota(jnp.int32, (tk, rb), 0)
    qry_tb = lax.broadcasted_iota(jnp.int32, (tk, rb), 1)

    def a_body(c, carry):
        rmax, rmin = carry
        c0 = pl.multiple_of(c * tk, tk)
        k0 = kd0_ref[pl.ds(c0, tk), :]
        k1 = kd1_ref[pl.ds(c0, tk), :]
        maxs, mins = [], []
        for r in range(tq // rb):
            rows = slice(r * rb, (r + 1) * rb)
            q0 = qi * tq + r * rb
            acc = jnp.zeros((rb, tk), F32)
            for j in range(IDX_WIDTH // LANES):
                qp = qi_ref[rows, j * LANES:(j + 1) * LANES]
                for par, kd in ((0, k0), (1, k1)):
                    logits = _dot_t(qp, kd)
                    acc = acc + tile_lanes(wb_ref[2 * j + par, rows, :]) * jnp.maximum(logits, 0.0)
            causal = (c0 + col_b) <= (q0 + row_b)
            score_ref[c, rows, :] = jnp.where(causal, acc, -jnp.inf)
            acc_t = acc.T
            causal_t = (c0 + key_tb) <= (q0 + qry_tb)
            sc_t = jnp.where(causal_t, acc_t, -jnp.inf)
            score_t_ref[pl.ds(c0, tk), rows] = sc_t
            maxs.append(fold8(sc_t, "max"))
            mins.append(fold8(jnp.where(causal_t, acc_t, jnp.inf), "min"))
        rmax = jnp.maximum(rmax, jnp.concatenate(maxs, axis=1))
        rmin = jnp.minimum(rmin, jnp.concatenate(mins, axis=1))
        return rmax, rmin

    rmax, rmin = lax.fori_loop(0, n_chunks, a_body,
                               (jnp.full((8, tq), -jnp.inf, F32), jnp.full((8, tq), jnp.inf, F32)))

    vec = (1, tq)
    kf = float(topk)
    n_keys = qi * tq + lax.broadcasted_iota(jnp.int32, vec, 1) + 1

    lo0 = jnp.min(rmin, axis=0, keepdims=True)
    hi0 = jnp.max(rmax, axis=0, keepdims=True)
    key_f = key_t.astype(F32)

    def threshold_search(n_rows):
        def count_t(pred):
            accs = [jnp.zeros((8, tq), F32) for _ in range(4)]
            i = 0
            for r0 in range(0, n_rows, tk):
                ind = jnp.where(pred(r0, score_t_ref[r0:r0 + tk, :]), 1.0, 0.0)
                for g in range(tk // 8):
                    accs[i % 4] = accs[i % 4] + ind[g * 8:(g + 1) * 8, :]
                    i += 1
            return jnp.sum((accs[0] + accs[1]) + (accs[2] + accs[3]), axis=0, keepdims=True)

        def b_cond(carry):
            return jnp.logical_and(carry[1] > 0.0, carry[0] < SEARCH_MAX_STEPS)

        def b_body(carry):
            state = (carry[0],) + tuple(carry[2:])
            flag = jnp.max(1.0 - carry[5])
            for _ in range(steps_per_check):
                state = b_step(*state)
            return (state[0], flag) + tuple(state[1:])

        def b_step(it, lo, hi, chi, done, th, tie):
            mid = jnp.where(it == 0, hi, 0.5 * lo + 0.5 * hi)
            cnt = count_t(lambda r0, sc: sc >= mid)
            active = done == 0.0
            adjacent = jnp.logical_and(it > 0, jnp.logical_or(mid <= lo, mid >= hi))
            tie_now = jnp.logical_and(active, adjacent)
            live = jnp.logical_and(active, jnp.logical_not(adjacent))
            hit = jnp.logical_and(live, cnt == kf)
            above = jnp.logical_and(live, cnt > kf)
            below = jnp.logical_and(live, cnt < kf)
            th = jnp.where(tie_now, lo, jnp.where(hit, mid, th))
            tie = jnp.where(tie_now, 1.0, tie)
            lo = jnp.where(above, mid, lo)
            hi = jnp.where(below, mid, hi)
            chi = jnp.where(below, cnt, chi)
            done = jnp.where(jnp.logical_or(tie_now, hit), 1.0, done)
            return it + 1, lo, hi, chi, done, th, tie

        steps_per_check = 2 if n_rows <= 4 * tk else 1

        init = (jnp.int32(0), jnp.float32(1.0), lo0, hi0,
                jnp.zeros(vec, F32), jnp.where(n_keys <= topk, 1.0, 0.0),
                jnp.full(vec, THETA_ALL, F32), jnp.zeros(vec, F32))
        _, _, _, _, chi, _, th, tie = lax.while_loop(b_cond, b_body, init)

        def tie_cut():
            need = kf - chi

            def body(_, carry):
                lo_i, hi_i = carry
                mid = jnp.floor(0.5 * (lo_i + hi_i))
                cnt = count_t(lambda r0, sc: jnp.logical_and(sc == th, key_f + float(r0) <= mid))
                ge = cnt >= need
                return jnp.where(ge, lo_i, mid), jnp.where(ge, mid, hi_i)

            n_iter = int(math.ceil(math.log2(S))) + 1
            _, hi_i = lax.fori_loop(0, n_iter, body, (jnp.full(vec, -1.0, F32), jnp.full(vec, float(S - 1), F32)))
            return jnp.where(tie > 0.0, hi_i, CUT_ALL)

        th_ref[...] = th
        has_tie = jnp.max(tie) > 0.0
        tie_flag_ref[0] = has_tie.astype(jnp.int32)
        cut_ref[...] = lax.cond(has_tie, tie_cut, lambda: jnp.full(vec, CUT_ALL, F32))

    pair_rows = 2 * tk
    variant = (n_chunks * tk + pair_rows - 1) // pair_rows - 1
    for v in range(-(-S // pair_rows)):
        pl.when(variant == v)(functools.partial(threshold_search, min((v + 1) * pair_rows, S)))

    th_c = jnp.broadcast_to(th_ref[...], (LANES, tq)).T
    cut_c = jnp.broadcast_to(cut_ref[...], (LANES, tq)).T
    th_t = tile_lanes(th_c)
    cut_t = tile_lanes(cut_c)

    colf_in_chunk = col_in_chunk.astype(F32)

    def masked_logits(c, with_ties):
        c0 = pl.multiple_of(c * tk, tk)
        sc = score_ref[c]
        if with_ties:
            colf = colf_in_chunk + (c * tk).astype(F32)
            sel = jnp.logical_or(sc > th_t, jnp.logical_and(sc == th_t, colf <= cut_t))
        else:
            sel = sc >= th_t
        bias = jnp.where(sel, 0.0, MASK_NEG)
        s = _dot_t(qa_ref[...], kr_ref[pl.ds(c0, tk), :])
        return (s.reshape(N_ATTN_HEADS, tq, tk) + bias[None]).reshape(rows_all, tk)

    def accumulate(c, p):
        c0 = pl.multiple_of(c * tk, tk)
        acc_ref[...] += _dot(p.astype(BF16), vaug_ref[pl.ds(c0, tk), :])

    def one_pass(with_ties):
        acc_ref[...] = jnp.zeros(acc_ref.shape, F32)
        shift = tile_lanes(bound_ref[...])

        def body(cp, _):
            c = 2 * cp
            p0 = jnp.exp(masked_logits(c, with_ties) - shift).astype(BF16)
            p1 = jnp.exp(masked_logits(c + 1, with_ties) - shift).astype(BF16)
            r0 = pl.multiple_of(c * tk, 2 * tk)
            acc_ref[...] += _dot(jnp.concatenate([p0, p1], axis=1), vaug_ref[pl.ds(r0, 2 * tk), :])
            return 0
        lax.fori_loop(0, (n_chunks + 1) // 2, body, 0)

    def two_pass():
        m_ref[...] = jnp.full(m_ref.shape, MASK_NEG, F32)

        def body1(c, _):
            s = masked_logits(c, True)
            s_ref[c] = s
            m_ref[...] = jnp.maximum(m_ref[...], lane_groups(s, jnp.maximum))
            return 0
        lax.fori_loop(0, n_chunks, body1, 0)
        m_ref[...] = jnp.broadcast_to(jnp.max(m_ref[...], axis=1, keepdims=True), m_ref.shape)
        acc_ref[...] = jnp.zeros(acc_ref.shape, F32)

        def body2(c, _):
            accumulate(c, jnp.exp(s_ref[c] - tile_lanes(m_ref[...])))
            return 0
        lax.fori_loop(0, n_chunks, body2, 0)

    has_tie = tie_flag_ref[0] > 0
    pl.when(has_tie)(functools.partial(one_pass, True))
    pl.when(jnp.logical_not(has_tie))(functools.partial(one_pass, False))
    l_min = jnp.min(acc_ref[:, LANES:])
    pl.when(jnp.logical_not(l_min > DENOM_SAFE))(two_pass)
    out = acc_ref[:, :LANES] / acc_ref[:, LANES:]
    for h in range(N_ATTN_HEADS):
        o_ref[0, :, h * LANES:(h + 1) * LANES] = out[h * tq:(h + 1) * tq, :].astype(o_ref.dtype)


def _attention(z3, tab_a, tab_i, topk):
    B, S, _ = z3.shape
    tq = min(256, S)
    tk = min(256, S)
    assert S % (2 * tk) == 0 and S % tq == 0 and tk % tq == 0, "key chunks are consumed in pairs"
    qblk = lambda w, col: pl.BlockSpec((1, tq, w), lambda b, q: (b, q, col // w))
    kblk = lambda w, col: pl.BlockSpec((1, S, w), lambda b, q: (b, 0, col // w))
    tab_q = pl.BlockSpec((1, tq, 2 * LANES), lambda b, q: (b, q, 0))
    tab_k = pl.BlockSpec((1, S, 2 * LANES), lambda b, q: (b, 0, 0))
    rows = N_ATTN_HEADS * tq
    const = lambda a: pl.BlockSpec(a.shape, lambda b, q: (0, 0))
    perm_a = _swap_matrix(ATTN_HEAD_DIM, ATTN_ROT_HALF)
    perm_i = _swap_matrix(IDX_HEAD_DIM, IDX_ROT_HALF)
    idx_w_scale = (N_IDX_HEADS ** -0.5) * (IDX_HEAD_DIM ** -0.5)
    assert math.frexp(idx_w_scale)[0] == 0.5, "the folded scale must be a power of two to stay exact in bf16"
    expand = jnp.where(jnp.arange(LANES)[:, None] == IDX_HEAD_DIM + jnp.arange(N_IDX_HEADS * LANES)[None, :] // LANES,
                       idx_w_scale, 0.0).astype(BF16)
    return pl.pallas_call(
        functools.partial(_attn_kernel, topk=topk, tq=tq, tk=tk),
        grid=(B, S // tq),
        in_specs=[qblk(ATTN_WIDTH, COL_AQ), qblk(IDX_WIDTH, COL_IQ), qblk(LANES, COL_IKW),
                  kblk(LANES, COL_AK), kblk(LANES, COL_AV), kblk(LANES, COL_IKW),
                  tab_q, tab_k, tab_q, tab_k, const(perm_a), const(perm_i), const(expand)],
        out_specs=pl.BlockSpec((1, tq, ATTN_WIDTH), lambda b, q: (b, q, 0)),
        out_shape=jax.ShapeDtypeStruct((B, S, ATTN_WIDTH), BF16),
        scratch_shapes=[pltpu.VMEM((S, LANES), BF16),
                        pltpu.VMEM((S, LANES), BF16),
                        pltpu.VMEM((S, LANES), BF16),
                        pltpu.VMEM((S, 2 * LANES), BF16),
                        pltpu.VMEM((rows, LANES), BF16),
                        pltpu.VMEM((tq, IDX_WIDTH), BF16),
                        pltpu.VMEM((N_IDX_HEADS, tq, LANES), F32),
                        pltpu.VMEM((S // tk, tq, tk), F32),
                        pltpu.VMEM((S, tq), F32),
                        pltpu.VMEM((1, tq), F32),
                        pltpu.VMEM((1, tq), F32),
                        pltpu.VMEM((S // tk, rows, tk), F32),
                        pltpu.VMEM((rows, LANES), F32),
                        pltpu.VMEM((rows, 2 * LANES), F32),
                        pltpu.VMEM((rows, LANES), F32),
                        pltpu.VMEM((8, LANES), F32),
                        pltpu.SMEM((1,), jnp.int32)],
        compiler_params=_cparams(2),
        name="dsa_attention",
    )(z3, z3, z3, z3, z3, z3, tab_a, tab_a, tab_i, tab_i, perm_a, perm_i, expand)


def _ret_kernel(rq_ref, rk_ref, rv_ref, rg_ref, tab_ref, decay_ref, zeta_ref, xi_ref, gch_ref, gn_ref,
                o_ref, state_ref, *, chunk):
    S = rq_ref.shape[1]
    C = chunk
    half = RET_KEY_DIM
    state_ref[...] = jnp.zeros(state_ref.shape, F32)
    lane = lax.broadcasted_iota(jnp.int32, (C, LANES), 1)
    first = lane < half

    def body(c, _):
        r0 = pl.multiple_of(c * C, C)
        sl = pl.ds(r0, C)
        cos, sin = tab_ref[0, sl, :LANES], tab_ref[0, sl, LANES:]
        cos_k, sin_k = cos * (RET_KEY_DIM ** -0.5), sin * (RET_KEY_DIM ** -0.5)
        first_half = lane % RET_KEY_DIM < RET_KEY_DIM // 2

        def rope(x, c, s):
            partner = jnp.where(first_half, pltpu.roll(x, LANES - RET_KEY_DIM // 2, 1),
                                pltpu.roll(x, RET_KEY_DIM // 2, 1))
            return x * c + partner * s

        for j in range(N_RET_HEADS // 2):
            q = rope(rq_ref[0, sl, j * LANES:(j + 1) * LANES].astype(F32), cos, sin)
            k = rope(rk_ref[0, sl, j * LANES:(j + 1) * LANES].astype(F32), cos_k, sin_k)
            kzt = (k * zeta_ref[j]).T.astype(BF16)
            state = state_ref[j]
            state_b = state.astype(BF16)
            new_rows = []
            for par in range(2):
                h = 2 * j + par
                keep = first if par == 0 else jnp.logical_not(first)
                qh = jnp.where(keep, q, 0.0).astype(BF16)
                kh = jnp.where(keep, k, 0.0).astype(BF16)
                v = rv_ref[0, sl, h * LANES:(h + 1) * LANES]
                inner = _dot_t(qh, kh) * decay_ref[h]
                o = _dot(inner.astype(BF16), v) + _dot(qh, state_b) * xi_ref[h]
                new_rows.append(_dot(kzt[par * half:(par + 1) * half, :], v))
                mu = jnp.mean(o, axis=-1, keepdims=True)
                d = o - mu
                var = jnp.mean(d * d, axis=-1, keepdims=True)
                y = d * lax.rsqrt(var + GN_EPS) * gn_ref[:, h * LANES:(h + 1) * LANES]
                g = rg_ref[0, sl, h * LANES:(h + 1) * LANES].astype(F32)
                o_ref[0, sl, h * LANES:(h + 1) * LANES] = (y * (g * jax.nn.sigmoid(g))).astype(o_ref.dtype)
            state_ref[j] = gch_ref[j] * state + jnp.concatenate(new_rows, axis=0)
        return 0

    lax.fori_loop(0, S // C, body, 0)


def _retention(z3, tab_r, consts, gn_w, chunk):
    B, S, _ = z3.shape
    decay, zeta, xi, gch = consts
    blk = lambda w, col: pl.BlockSpec((1, S, w), lambda b: (b, 0, col // w))
    const = lambda a: pl.BlockSpec(a.shape, lambda b: (0,) * a.ndim)
    return pl.pallas_call(
        functools.partial(_ret_kernel, chunk=chunk),
        grid=(B,),
        in_specs=[blk(RET_KEY_WIDTH, COL_RQ), blk(RET_KEY_WIDTH, COL_RK), blk(RET_WIDTH, COL_RV),
                  blk(RET_WIDTH, COL_RG), pl.BlockSpec((1, S, 2 * LANES), lambda b: (b, 0, 0)),
                  const(decay), const(zeta), const(xi), const(gch), const(gn_w)],
        out_specs=pl.BlockSpec((1, S, RET_WIDTH), lambda b: (b, 0, 0)),
        out_shape=jax.ShapeDtypeStruct((B, S, RET_WIDTH), BF16),
        scratch_shapes=[pltpu.VMEM((N_RET_HEADS // 2, LANES, RET_VAL_DIM), F32)],
        compiler_params=_cparams(1),
        name="retention",
    )(z3, z3, z3, z3, tab_r, decay, zeta, xi, gch, gn_w)


def _retention_consts(chunk):
    C = chunk
    log_g = jnp.log(1.0 - 2.0 ** (-5.0 - jnp.arange(N_RET_HEADS, dtype=F32)))
    i = jnp.arange(C, dtype=F32)
    diff = i[:, None] - i[None, :]
    decay = jnp.where(diff[None] >= 0, jnp.exp(jnp.maximum(diff, 0.0)[None] * log_g[:, None, None]), 0.0)
    zeta = jnp.exp((C - 1.0 - i)[None, :] * log_g[:, None])
    xi = jnp.exp((i + 1.0)[None, :] * log_g[:, None])
    g_chunk = jnp.exp(C * log_g)
    pair = lambda a: a.reshape(N_RET_HEADS // 2, 2, -1)
    zeta_p = jnp.repeat(pair(zeta).transpose(0, 2, 1), RET_KEY_DIM, axis=2)
    xi_b = jnp.broadcast_to(xi[:, :, None], (N_RET_HEADS, C, RET_VAL_DIM))
    gch = jnp.broadcast_to(jnp.repeat(pair(g_chunk), RET_KEY_DIM, axis=1).reshape(N_RET_HEADS // 2, LANES, 1),
                           (N_RET_HEADS // 2, LANES, RET_VAL_DIM))
    return decay.astype(F32), zeta_p.astype(F32), xi_b.astype(F32), gch.astype(F32)


def _out_proj_kernel(attn_ref, ret_ref, h_ref, wa_ref, wr_ref, g1_ref, g2_ref, h1_ref, m_ref, *, sub):
    for r0 in range(0, h_ref.shape[0], sub):
        sl = slice(r0, r0 + sub)
        mix = _dot(attn_ref[sl, :], wa_ref[...]) + _dot(ret_ref[sl, :], wr_ref[...])
        h1 = h_ref[sl, :] + _rms(mix, g1_ref[...])
        h1_ref[sl, :] = h1
        m_ref[sl, :] = _rms(h1, g2_ref[...]).astype(m_ref.dtype)


def _out_proj(attn, ret, h, wo, g1, g2):
    T, D = h.shape
    tm = min(512, T)
    wa_rows, wr_rows = attn.shape[1], ret.shape[1]
    assert wa_rows == wr_rows and wo.shape[0] == wa_rows + wr_rows
    row = lambda w: pl.BlockSpec((tm, w), lambda i: (i, 0))
    const = lambda a: pl.BlockSpec(a.shape, lambda i: (0, 0))
    return pl.pallas_call(
        functools.partial(_out_proj_kernel, sub=min(256, tm)),
        grid=(T // tm,),
        in_specs=[row(wa_rows), row(wr_rows), row(D),
                  pl.BlockSpec((wa_rows, D), lambda i: (0, 0)), pl.BlockSpec((wr_rows, D), lambda i: (1, 0)),
                  const(g1), const(g2)],
        out_specs=[row(D), row(D)],
        out_shape=[jax.ShapeDtypeStruct((T, D), F32), jax.ShapeDtypeStruct((T, D), BF16)],
        compiler_params=_cparams(1),
        name="out_proj",
    )(attn, ret, h, wo, wo, g1, g2)


def _ffn_kernel(m_ref, w1_ref, w2_ref, h_ref, g_ref, o_ref, acc_ref):
    f = pl.program_id(1)

    @pl.when(f == 0)
    def _():
        acc_ref[...] = jnp.zeros(acc_ref.shape, F32)

    u = jnp.maximum(_dot(m_ref[...], w1_ref[...]), 0.0)
    acc_ref[...] += _dot((u * u).astype(BF16), w2_ref[...])

    @pl.when(f == pl.num_programs(1) - 1)
    def _():
        o_ref[...] = h_ref[...] + _rms(acc_ref[...], g_ref[...])


def _ffn(m, w1, w2, h, g):
    T, D = h.shape
    F = w1.shape[1]
    tm = min(512, T)
    tf = min(1024, F)
    return pl.pallas_call(
        _ffn_kernel,
        grid=(T // tm, F // tf),
        in_specs=[pl.BlockSpec((tm, D), lambda i, f: (i, 0)),
                  pl.BlockSpec((D, tf), lambda i, f: (0, f)),
                  pl.BlockSpec((tf, D), lambda i, f: (f, 0)),
                  pl.BlockSpec((tm, D), lambda i, f: (i, 0)),
                  pl.BlockSpec((1, D), lambda i, f: (0, 0))],
        out_specs=pl.BlockSpec((tm, D), lambda i, f: (i, 0)),
        out_shape=jax.ShapeDtypeStruct((T, D), F32),
        scratch_shapes=[pltpu.VMEM((tm, D), F32)],
        compiler_params=_cparams(2),
        name="ffn",
    )(m, w1, w2, h, g)


def _ple_kernel(h_ref, p_ref, wg_ref, wp_ref, g_ref, o_ref, *, sub):
    for r0 in range(0, h_ref.shape[0], sub):
        sl = slice(r0, r0 + sub)
        h = h_ref[sl, :]
        gate = jax.nn.sigmoid(_dot(h.astype(BF16), wg_ref[...]))
        e = _dot(p_ref[sl, :].astype(BF16), wp_ref[...])
        o_ref[sl, :] = h + _rms(gate * e, g_ref[...])


def _ple(h, p, layer, wg, wp, g):
    T, D = h.shape
    tm = min(512, T)
    row = lambda w: pl.BlockSpec((tm, w), lambda i: (i, 0))
    const = lambda a: pl.BlockSpec(a.shape, lambda i: (0, 0))
    return pl.pallas_call(
        functools.partial(_ple_kernel, sub=min(256, tm)),
        grid=(T // tm,),
        in_specs=[row(D), pl.BlockSpec((None, tm, p.shape[2]), lambda i: (layer, i, 0)),
                  const(wg), const(wp), const(g)],
        out_specs=row(D),
        out_shape=jax.ShapeDtypeStruct((T, D), F32),
        compiler_params=_cparams(1),
        name="ple",
    )(h, p, wg, wp, g)


def _rope_table(positions, half, theta, group):
    inv = theta ** (-jnp.arange(half, dtype=F32) / half)
    ang = positions.astype(F32)[..., None] * inv
    compact = jnp.concatenate([jnp.cos(ang), jnp.sin(ang), jnp.ones_like(ang[..., :1])], axis=-1)
    spread = np.zeros((2 * half + 1, 2 * LANES), np.float32)
    for lane in range(LANES):
        d = lane % group
        if d < 2 * half:
            spread[d % half, lane] = 1.0
            spread[half + d % half, LANES + lane] = -1.0 if d < half else 1.0
        else:
            spread[2 * half, lane] = 1.0
    return jnp.einsum("bsk,kn->bsn", compact, jnp.asarray(spread), precision=lax.Precision.HIGHEST)


def _cast_kernel(x_ref, o_ref):
    o_ref[...] = x_ref[...].astype(o_ref.dtype)


def _to_bf16(w, layer):
    _, R, C = w.shape
    tr = max(16, min(R, CAST_BLOCK_BYTES // (4 * C) // 16 * 16))
    while R % tr:
        tr -= 16
    return pl.pallas_call(
        _cast_kernel,
        grid=(R // tr,),
        in_specs=[pl.BlockSpec((None, tr, C), lambda r: (layer, r, 0))],
        out_specs=pl.BlockSpec((tr, C), lambda r: (r, 0)),
        out_shape=jax.ShapeDtypeStruct((R, C), BF16),
        compiler_params=_cparams(1),
        name="to_bf16",
    )(w)


def _reorder_w_in(w):
    a0 = ATTN_WIDTH
    a1 = a0 + 2 * ATTN_HEAD_DIM
    a2 = a1 + IDX_WIDTH
    a3 = a2 + IDX_HEAD_DIM + N_IDX_HEADS
    a4 = a3 + 2 * RET_KEY_WIDTH
    pad = jnp.zeros((w.shape[0], Z_WIDTH - Z_USED + LANES - IDX_HEAD_DIM - N_IDX_HEADS), BF16)
    return jnp.concatenate([w[:, :a0], w[:, a1:a2], w[:, a4:], w[:, a3:a4], w[:, a0:a1], w[:, a2:a3], pad], axis=1)


def kernel(x, p, positions, w_in, w_out, w_ff1, w_ff2, w_ple, w_ple_gate, pre_mix_norm, post_mix_norm,
           pre_ff_norm, post_ff_norm, ple_norm, ret_gn):
    B, S, D = x.shape
    depth = w_in.shape[0]
    T = B * S
    topk = min(TOPK_MAX, S // 4)
    ret_chunk = min(256, S)
    tab_a = _rope_table(positions, ATTN_ROT_HALF, ROPE_THETA, ATTN_HEAD_DIM)
    tab_i = _rope_table(positions, IDX_ROT_HALF, ROPE_THETA, IDX_HEAD_DIM)
    tab_r = _rope_table(positions, RET_KEY_DIM // 2, RET_THETA, RET_KEY_DIM)
    ret_consts = _retention_consts(ret_chunk)
    vec = lambda a: a.reshape(1, -1).astype(F32)

    h = x.reshape(T, D)
    for i in range(depth):
        z = _norm_proj(h, vec(pre_mix_norm[i]), _reorder_w_in(w_in[i].astype(BF16)))
        z3 = z.reshape(B, S, Z_WIDTH)
        attn = _attention(z3, tab_a, tab_i, topk)
        ret = _retention(z3, tab_r, ret_consts, vec(ret_gn[i]), ret_chunk)
        h, m = _out_proj(attn.reshape(T, ATTN_WIDTH), ret.reshape(T, RET_WIDTH), h,
                         _to_bf16(w_out, i), vec(post_mix_norm[i]), vec(pre_ff_norm[i]))
        h = _ffn(m, _to_bf16(w_ff1, i), _to_bf16(w_ff2, i), h, vec(post_ff_norm[i]))
        h = _ple(h, p.reshape(depth, T, -1), i, _to_bf16(w_ple_gate, i), _to_bf16(w_ple, i), vec(ple_norm[i]))
    return h.reshape(B, S, D)
```

```python
import functools
import math

import jax
import jax.numpy as jnp
import numpy as np
from jax import lax
from jax.experimental import pallas as pl
from jax.experimental.pallas import tpu as pltpu

N_ATTN_HEADS = 8
ATTN_HEAD_DIM = 128
ATTN_ROT_HALF = 16
ROPE_THETA = 500000.0
TOPK_MAX = 256
N_IDX_HEADS = 16
IDX_HEAD_DIM = 64
IDX_ROT_HALF = 8
N_RET_HEADS = 8
RET_KEY_DIM = 64
RET_VAL_DIM = 128
RET_THETA = 10000.0
ATTN_WIDTH = N_ATTN_HEADS * ATTN_HEAD_DIM
IDX_WIDTH = N_IDX_HEADS * IDX_HEAD_DIM
RET_KEY_WIDTH = N_RET_HEADS * RET_KEY_DIM
RET_WIDTH = N_RET_HEADS * RET_VAL_DIM
RMS_EPS = 1e-6
GN_EPS = 1e-5

LANES = 128
COL_AQ = 0
COL_IQ = COL_AQ + ATTN_WIDTH
COL_RV = COL_IQ + IDX_WIDTH
COL_RG = COL_RV + RET_WIDTH
COL_RQ = COL_RG + RET_WIDTH
COL_RK = COL_RQ + RET_KEY_WIDTH
COL_AK = COL_RK + RET_KEY_WIDTH
COL_AV = COL_AK + ATTN_HEAD_DIM
COL_IKW = COL_AV + ATTN_HEAD_DIM
Z_USED = COL_IKW + LANES
Z_TILE_N = 512
Z_WIDTH = -(-Z_USED // Z_TILE_N) * Z_TILE_N

MASK_NEG = -1e30
THETA_ALL = -3e38
CUT_ALL = 1e9
BOUND_SLACK = 1.05
DENOM_SAFE = 1e-20
SEARCH_MAX_STEPS = 320
CAST_BLOCK_BYTES = 4 * 1024 * 1024
V7X_VMEM_BYTES = 64 * 1024 * 1024
VMEM_LIMIT = V7X_VMEM_BYTES - 8 * 1024 * 1024

BF16 = jnp.bfloat16
F32 = jnp.float32


def _cparams(n_grid):
    return pltpu.CompilerParams(dimension_semantics=("arbitrary",) * n_grid,
                                vmem_limit_bytes=VMEM_LIMIT)


def _rms(x, gain):
    ms = jnp.mean(x * x, axis=-1, keepdims=True)
    return x * lax.rsqrt(ms + RMS_EPS) * gain


def _rope(xb, cos, sin, swap):
    return xb.astype(F32) * cos + jnp.dot(xb, swap, preferred_element_type=F32) * sin


def _swap_matrix(group, half):
    l = jnp.arange(LANES)
    d = l % group
    partner = jnp.where(d < half, l + half, jnp.where(d < 2 * half, l - half, -1))
    return (jnp.arange(LANES)[:, None] == partner[None, :]).astype(BF16)


def _dot_t(a, b):
    return lax.dot_general(a, b, (((1,), (1,)), ((), ())), preferred_element_type=F32)


def _dot(a, b):
    return jnp.dot(a, b, preferred_element_type=F32)


def _norm_proj_kernel(h_ref, g_ref, w_ref, z_ref, *, sub, tn):
    for r0 in range(0, h_ref.shape[0], sub):
        sl = slice(r0, r0 + sub)
        a = _rms(h_ref[sl, :], g_ref[...]).astype(BF16)
        for c0 in range(0, w_ref.shape[1], tn):
            z_ref[sl, c0:c0 + tn] = _dot(a, w_ref[:, c0:c0 + tn]).astype(z_ref.dtype)


def _norm_proj(h, gain, w):
    T, D = h.shape
    N = w.shape[1]
    tm = min(512, T)
    return pl.pallas_call(
        functools.partial(_norm_proj_kernel, sub=min(256, tm), tn=Z_TILE_N),
        grid=(T // tm,),
        in_specs=[pl.BlockSpec((tm, D), lambda i: (i, 0)),
                  pl.BlockSpec((1, D), lambda i: (0, 0)),
                  pl.BlockSpec((D, N), lambda i: (0, 0), pipeline_mode=pl.Buffered(1))],
        out_specs=pl.BlockSpec((tm, N), lambda i: (i, 0)),
        out_shape=jax.ShapeDtypeStruct((T, N), BF16),
        compiler_params=_cparams(1),
        name="norm_proj",
    )(h, gain, w)


def _attn_kernel(aq_ref, iq_ref, iwq_ref, ak_ref, av_ref, ikw_ref,
                 taq_ref, tak_ref, tiq_ref, tik_ref, pa_ref, pi_ref, ex_ref, o_ref,
                 kr_ref, kd0_ref, kd1_ref, vaug_ref, qa_ref, qi_ref, wb_ref, score_ref, score_t_ref,
                 th_ref, cut_ref, s_ref, m_ref, acc_ref, bound_ref, ksq_ref, tie_flag_ref, *, topk, tq, tk):
    S = ak_ref.shape[1]
    qi = pl.program_id(1)
    n_chunks = (qi * tq + tq + tk - 1) // tk
    reps = tk // LANES
    rows_all = N_ATTN_HEADS * tq
    ones_sq = jnp.ones((LANES, LANES), BF16)

    @pl.when(qi == 0)
    def _():
        rows = min(256, S)

        def body(r, _):
            r0 = pl.multiple_of(r * rows, rows)
            sl = pl.ds(r0, rows)
            kr = _rope(ak_ref[0, sl, :], tak_ref[0, sl, :LANES], tak_ref[0, sl, LANES:], pa_ref[...])
            kr_ref[sl, :] = kr.astype(BF16)
            ksq = _dot((kr * kr).astype(BF16), ones_sq).reshape(rows // 8, 8, LANES)
            ksq_ref[...] = jnp.maximum(jnp.where(r == 0, 0.0, ksq_ref[...]), jnp.max(ksq, axis=0))
            lane_r = lax.broadcasted_iota(jnp.int32, (rows, LANES), 1)
            ik = _rope(ikw_ref[0, sl, :], tik_ref[0, sl, :LANES], tik_ref[0, sl, LANES:], pi_ref[...])
            ik = jnp.where(lane_r < IDX_HEAD_DIM, ik, 0.0)
            kd0_ref[sl, :] = ik.astype(BF16)
            kd1_ref[sl, :] = pltpu.roll(ik, IDX_HEAD_DIM, 1).astype(BF16)
            vaug_ref[sl, :] = jnp.concatenate([av_ref[0, sl, :], jnp.ones((rows, LANES), BF16)], axis=1)
            score_t_ref[sl, :] = jnp.full((rows, tq), -jnp.inf, F32)
            return 0
        lax.fori_loop(0, S // rows, body, 0)

        def fill(c, _):
            score_ref[c] = jnp.full((tq, tk), -jnp.inf, F32)
            return 0
        lax.fori_loop(0, S // tk, fill, 0)

    ca = taq_ref[0, :, :LANES] * (ATTN_HEAD_DIM ** -0.5)
    sa = taq_ref[0, :, LANES:] * (ATTN_HEAD_DIM ** -0.5)
    k_norm = jnp.sqrt(jnp.max(ksq_ref[...], axis=0, keepdims=True))
    for h in range(N_ATTN_HEADS):
        x = aq_ref[0, :, h * LANES:(h + 1) * LANES]
        qf = _rope(x, ca, sa, pa_ref[...])
        qa_ref[h * tq:(h + 1) * tq, :] = qf.astype(BF16)
        q_norm = jnp.sqrt(_dot((qf * qf).astype(BF16), ones_sq))
        bound_ref[h * tq:(h + 1) * tq, :] = q_norm * k_norm * BOUND_SLACK
    ci, si = tiq_ref[0, :, :LANES], tiq_ref[0, :, LANES:]
    for j in range(IDX_WIDTH // LANES):
        x = iq_ref[0, :, j * LANES:(j + 1) * LANES]
        qi_ref[:, j * LANES:(j + 1) * LANES] = _rope(x, ci, si, pi_ref[...]).astype(BF16)
    iw_b = jnp.dot(iwq_ref[0], ex_ref[...], preferred_element_type=F32)
    for h in range(N_IDX_HEADS):
        wb_ref[h] = iw_b[:, h * LANES:(h + 1) * LANES]

    col_in_chunk = lax.broadcasted_iota(jnp.int32, (tq, tk), 1)

    def lane_groups(x, op):
        r = x[:, :LANES]
        for g in range(1, reps):
            r = op(r, x[:, g * LANES:(g + 1) * LANES])
        return r

    def tile_lanes(x):
        return jnp.concatenate([x] * reps, axis=1) if reps > 1 else x

    key_t = lax.broadcasted_iota(jnp.int32, (tk, tq), 0)

    def fold8(x, op):
        x3 = x.reshape(tk // 8, 8, x.shape[1])
        return jnp.max(x3, axis=0) if op == "max" else (jnp.min(x3, axis=0) if op == "min" else jnp.sum(x3, axis=0))

    rb = min(LANES, tq)
    row_b = lax.broadcasted_iota(jnp.int32, (rb, tk), 0)
    col_b = lax.broadcasted_iota(jnp.int32, (rb, tk), 1)
    key_tb = lax.broadcasted_iota(jnp.int32, (tk, rb), 0)
    qry_tb = lax.broadcasted_iota(jnp.int32, (tk, rb), 1)

    def a_body(c, carry):
        rmax, rmin = carry
        c0 = pl.multiple_of(c * tk, tk)
        k0 = kd0_ref[pl.ds(c0, tk), :]
        k1 = kd1_ref[pl.ds(c0, tk), :]
        maxs, mins = [], []
        for r in range(tq // rb):
            rows = slice(r * rb, (r + 1) * rb)
            q0 = qi * tq + r * rb
            acc = jnp.zeros((rb, tk), F32)
            for j in range(IDX_WIDTH // LANES):
                qp = qi_ref[rows, j * LANES:(j + 1) * LANES]
                for par, kd in ((0, k0), (1, k1)):
                    logits = _dot_t(qp, kd)
                    acc = acc + tile_lanes(wb_ref[2 * j + par, rows, :]) * jnp.maximum(logits, 0.0)
            causal = (c0 + col_b) <= (q0 + row_b)
            score_ref[c, rows, :] = jnp.where(causal, acc, -jnp.inf)
            acc_t = acc.T
            causal_t = (c0 + key_tb) <= (q0 + qry_tb)
            sc_t = jnp.where(causal_t, acc_t, -jnp.inf)
            score_t_ref[pl.ds(c0, tk), rows] = sc_t
            maxs.append(fold8(sc_t, "max"))
            mins.append(fold8(jnp.where(causal_t, acc_t, jnp.inf), "min"))
        rmax = jnp.maximum(rmax, jnp.concatenate(maxs, axis=1))
        rmin = jnp.minimum(rmin, jnp.concatenate(mins, axis=1))
        return rmax, rmin

    rmax, rmin = lax.fori_loop(0, n_chunks, a_body,
                               (jnp.full((8, tq), -jnp.inf, F32), jnp.full((8, tq), jnp.inf, F32)))

    vec = (1, tq)
    kf = float(topk)
    n_keys = qi * tq + lax.broadcasted_iota(jnp.int32, vec, 1) + 1

    lo0 = jnp.min(rmin, axis=0, keepdims=True)
    hi0 = jnp.max(rmax, axis=0, keepdims=True)
    key_f = key_t.astype(F32)

    def threshold_search(n_rows):
        def count_t(pred):
            accs = [jnp.zeros((8, tq), F32) for _ in range(4)]
            i = 0
            for r0 in range(0, n_rows, tk):
                ind = jnp.where(pred(r0, score_t_ref[r0:r0 + tk, :]), 1.0, 0.0)
                for g in range(tk // 8):
                    accs[i % 4] = accs[i % 4] + ind[g * 8:(g + 1) * 8, :]
                    i += 1
            return jnp.sum((accs[0] + accs[1]) + (accs[2] + accs[3]), axis=0, keepdims=True)

        def b_cond(carry):
            return jnp.logical_and(carry[1] > 0.0, carry[0] < SEARCH_MAX_STEPS)

        def b_body(carry):
            state = (carry[0],) + tuple(carry[2:])
            flag = jnp.max(1.0 - carry[5])
            for _ in range(steps_per_check):
                state = b_step(*state)
            return (state[0], flag) + tuple(state[1:])

        def b_step(it, lo, hi, chi, done, th, tie):
            mid = jnp.where(it == 0, hi, 0.5 * lo + 0.5 * hi)
            cnt = count_t(lambda r0, sc: sc >= mid)
            active = done == 0.0
            adjacent = jnp.logical_and(it > 0, jnp.logical_or(mid <= lo, mid >= hi))
            tie_now = jnp.logical_and(active, adjacent)
            live = jnp.logical_and(active, jnp.logical_not(adjacent))
            hit = jnp.logical_and(live, cnt == kf)
            above = jnp.logical_and(live, cnt > kf)
            below = jnp.logical_and(live, cnt < kf)
            th = jnp.where(tie_now, lo, jnp.where(hit, mid, th))
            tie = jnp.where(tie_now, 1.0, tie)
            lo = jnp.where(above, mid, lo)
            hi = jnp.where(below, mid, hi)
            chi = jnp.where(below, cnt, chi)
            done = jnp.where(jnp.logical_or(tie_now, hit), 1.0, done)
            return it + 1, lo, hi, chi, done, th, tie

        steps_per_check = 2 if n_rows <= 4 * tk else 1

        init = (jnp.int32(0), jnp.float32(1.0), lo0, hi0,
                jnp.zeros(vec, F32), jnp.where(n_keys <= topk, 1.0, 0.0),
                jnp.full(vec, THETA_ALL, F32), jnp.zeros(vec, F32))
        _, _, _, _, chi, _, th, tie = lax.while_loop(b_cond, b_body, init)

        def tie_cut():
            need = kf - chi

            def body(_, carry):
                lo_i, hi_i = carry
                mid = jnp.floor(0.5 * (lo_i + hi_i))
                cnt = count_t(lambda r0, sc: jnp.logical_and(sc == th, key_f + float(r0) <= mid))
                ge = cnt >= need
                return jnp.where(ge, lo_i, mid), jnp.where(ge, mid, hi_i)

            n_iter = int(math.ceil(math.log2(S))) + 1
            _, hi_i = lax.fori_loop(0, n_iter, body, (jnp.full(vec, -1.0, F32), jnp.full(vec, float(S - 1), F32)))
            return jnp.where(tie > 0.0, hi_i, CUT_ALL)

        th_ref[...] = th
        has_tie = jnp.max(tie) > 0.0
        tie_flag_ref[0] = has_tie.astype(jnp.int32)
        cut_ref[...] = lax.cond(has_tie, tie_cut, lambda: jnp.full(vec, CUT_ALL, F32))

    pair_rows = 2 * tk
    variant = (n_chunks * tk + pair_rows - 1) // pair_rows - 1
    for v in range(-(-S // pair_rows)):
        pl.when(variant == v)(functools.partial(threshold_search, min((v + 1) * pair_rows, S)))

    th_c = jnp.broadcast_to(th_ref[...], (LANES, tq)).T
    cut_c = jnp.broadcast_to(cut_ref[...], (LANES, tq)).T
    th_t = tile_lanes(th_c)
    cut_t = tile_lanes(cut_c)

    colf_in_chunk = col_in_chunk.astype(F32)

    def masked_logits(c, with_ties):
        c0 = pl.multiple_of(c * tk, tk)
        sc = score_ref[c]
        if with_ties:
            colf = colf_in_chunk + (c * tk).astype(F32)
            sel = jnp.logical_or(sc > th_t, jnp.logical_and(sc == th_t, colf <= cut_t))
        else:
            sel = sc >= th_t
        bias = jnp.where(sel, 0.0, MASK_NEG)
        s = _dot_t(qa_ref[...], kr_ref[pl.ds(c0, tk), :])
        return (s.reshape(N_ATTN_HEADS, tq, tk) + bias[None]).reshape(rows_all, tk)

    def accumulate(c, p):
        c0 = pl.multiple_of(c * tk, tk)
        acc_ref[...] += _dot(p.astype(BF16), vaug_ref[pl.ds(c0, tk), :])

    def one_pass(with_ties):
        acc_ref[...] = jnp.zeros(acc_ref.shape, F32)
        shift = tile_lanes(bound_ref[...])

        def body(cp, _):
            c = 2 * cp
            p0 = jnp.exp(masked_logits(c, with_ties) - shift).astype(BF16)
            p1 = jnp.exp(masked_logits(c + 1, with_ties) - shift).astype(BF16)
            r0 = pl.multiple_of(c * tk, 2 * tk)
            acc_ref[...] += _dot(jnp.concatenate([p0, p1], axis=1), vaug_ref[pl.ds(r0, 2 * tk), :])
            return 0
        lax.fori_loop(0, (n_chunks + 1) // 2, body, 0)

    def two_pass():
        m_ref[...] = jnp.full(m_ref.shape, MASK_NEG, F32)

        def body1(c, _):
            s = masked_logits(c, True)
            s_ref[c] = s
            m_ref[...] = jnp.maximum(m_ref[...], lane_groups(s, jnp.maximum))
            return 0
        lax.fori_loop(0, n_chunks, body1, 0)
        m_ref[...] = jnp.broadcast_to(jnp.max(m_ref[...], axis=1, keepdims=True), m_ref.shape)
        acc_ref[...] = jnp.zeros(acc_ref.shape, F32)

        def body2(c, _):
            accumulate(c, jnp.exp(s_ref[c] - tile_lanes(m_ref[...])))
            return 0
        lax.fori_loop(0, n_chunks, body2, 0)

    has_tie = tie_flag_ref[0] > 0
    pl.when(has_tie)(functools.partial(one_pass, True))
    pl.when(jnp.logical_not(has_tie))(functools.partial(one_pass, False))
    l_min = jnp.min(acc_ref[:, LANES:])
    pl.when(jnp.logical_not(l_min > DENOM_SAFE))(two_pass)
    out = acc_ref[:, :LANES] / acc_ref[:, LANES:]
    for h in range(N_ATTN_HEADS):
        o_ref[0, :, h * LANES:(h + 1) * LANES] = out[h * tq:(h + 1) * tq, :].astype(o_ref.dtype)


def _attention(z3, tab_a, tab_i, topk):
    B, S, _ = z3.shape
    tq = min(256, S)
    tk = min(256, S)
    assert S % (2 * tk) == 0 and S % tq == 0 and tk % tq == 0, "key chunks are consumed in pairs"
    qblk = lambda w, col: pl.BlockSpec((1, tq, w), lambda b, q: (b, q, col // w))
    kblk = lambda w, col: pl.BlockSpec((1, S, w), lambda b, q: (b, 0, col // w))
    tab_q = pl.BlockSpec((1, tq, 2 * LANES), lambda b, q: (b, q, 0))
    tab_k = pl.BlockSpec((1, S, 2 * LANES), lambda b, q: (b, 0, 0))
    rows = N_ATTN_HEADS * tq
    const = lambda a: pl.BlockSpec(a.shape, lambda b, q: (0, 0))
    perm_a = _swap_matrix(ATTN_HEAD_DIM, ATTN_ROT_HALF)
    perm_i = _swap_matrix(IDX_HEAD_DIM, IDX_ROT_HALF)
    idx_w_scale = (N_IDX_HEADS ** -0.5) * (IDX_HEAD_DIM ** -0.5)
    assert math.frexp(idx_w_scale)[0] == 0.5, "the folded scale must be a power of two to stay exact in bf16"
    expand = jnp.where(jnp.arange(LANES)[:, None] == IDX_HEAD_DIM + jnp.arange(N_IDX_HEADS * LANES)[None, :] // LANES,
                       idx_w_scale, 0.0).astype(BF16)
    return pl.pallas_call(
        functools.partial(_attn_kernel, topk=topk, tq=tq, tk=tk),
        grid=(B, S // tq),
        in_specs=[qblk(ATTN_WIDTH, COL_AQ), qblk(IDX_WIDTH, COL_IQ), qblk(LANES, COL_IKW),
                  kblk(LANES, COL_AK), kblk(LANES, COL_AV), kblk(LANES, COL_IKW),
                  tab_q, tab_k, tab_q, tab_k, const(perm_a), const(perm_i), const(expand)],
        out_specs=pl.BlockSpec((1, tq, ATTN_WIDTH), lambda b, q: (b, q, 0)),
        out_shape=jax.ShapeDtypeStruct((B, S, ATTN_WIDTH), BF16),
        scratch_shapes=[pltpu.VMEM((S, LANES), BF16),
                        pltpu.VMEM((S, LANES), BF16),
                        pltpu.VMEM((S, LANES), BF16),
                        pltpu.VMEM((S, 2 * LANES), BF16),
                        pltpu.VMEM((rows, LANES), BF16),
                        pltpu.VMEM((tq, IDX_WIDTH), BF16),
                        pltpu.VMEM((N_IDX_HEADS, tq, LANES), F32),
                        pltpu.VMEM((S // tk, tq, tk), F32),
                        pltpu.VMEM((S, tq), F32),
                        pltpu.VMEM((1, tq), F32),
                        pltpu.VMEM((1, tq), F32),
                        pltpu.VMEM((S // tk, rows, tk), F32),
                        pltpu.VMEM((rows, LANES), F32),
                        pltpu.VMEM((rows, 2 * LANES), F32),
                        pltpu.VMEM((rows, LANES), F32),
                        pltpu.VMEM((8, LANES), F32),
                        pltpu.SMEM((1,), jnp.int32)],
        compiler_params=_cparams(2),
        name="dsa_attention",
    )(z3, z3, z3, z3, z3, z3, tab_a, tab_a, tab_i, tab_i, perm_a, perm_i, expand)


def _ret_kernel(rq_ref, rk_ref, rv_ref, rg_ref, tab_ref, decay_ref, zeta_ref, xi_ref, gch_ref, gn_ref,
                o_ref, state_ref, *, chunk):
    S = rq_ref.shape[1]
    C = chunk
    half = RET_KEY_DIM
    state_ref[...] = jnp.zeros(state_ref.shape, F32)
    lane = lax.broadcasted_iota(jnp.int32, (C, LANES), 1)
    first = lane < half

    def body(c, _):
        r0 = pl.multiple_of(c * C, C)
        sl = pl.ds(r0, C)
        cos, sin = tab_ref[0, sl, :LANES], tab_ref[0, sl, LANES:]
        cos_k, sin_k = cos * (RET_KEY_DIM ** -0.5), sin * (RET_KEY_DIM ** -0.5)
        first_half = lane % RET_KEY_DIM < RET_KEY_DIM // 2

        def rope(x, c, s):
            partner = jnp.where(first_half, pltpu.roll(x, LANES - RET_KEY_DIM // 2, 1),
                                pltpu.roll(x, RET_KEY_DIM // 2, 1))
            return x * c + partner * s

        for j in range(N_RET_HEADS // 2):
            q = rope(rq_ref[0, sl, j * LANES:(j + 1) * LANES].astype(F32), cos, sin)
            k = rope(rk_ref[0, sl, j * LANES:(j + 1) * LANES].astype(F32), cos_k, sin_k)
            kzt = (k * zeta_ref[j]).T.astype(BF16)
            state = state_ref[j]
            state_b = state.astype(BF16)
            new_rows = []
            for par in range(2):
                h = 2 * j + par
                keep = first if par == 0 else jnp.logical_not(first)
                qh = jnp.where(keep, q, 0.0).astype(BF16)
                kh = jnp.where(keep, k, 0.0).astype(BF16)
                v = rv_ref[0, sl, h * LANES:(h + 1) * LANES]
                inner = _dot_t(qh, kh) * decay_ref[h]
                o = _dot(inner.astype(BF16), v) + _dot(qh, state_b) * xi_ref[h]
                new_rows.append(_dot(kzt[par * half:(par + 1) * half, :], v))
                mu = jnp.mean(o, axis=-1, keepdims=True)
                d = o - mu
                var = jnp.mean(d * d, axis=-1, keepdims=True)
                y = d * lax.rsqrt(var + GN_EPS) * gn_ref[:, h * LANES:(h + 1) * LANES]
                g = rg_ref[0, sl, h * LANES:(h + 1) * LANES].astype(F32)
                o_ref[0, sl, h * LANES:(h + 1) * LANES] = (y * (g * jax.nn.sigmoid(g))).astype(o_ref.dtype)
            state_ref[j] = gch_ref[j] * state + jnp.concatenate(new_rows, axis=0)
        return 0

    lax.fori_loop(0, S // C, body, 0)


def _retention(z3, tab_r, consts, gn_w, chunk):
    B, S, _ = z3.shape
    decay, zeta, xi, gch = consts
    blk = lambda w, col: pl.BlockSpec((1, S, w), lambda b: (b, 0, col // w))
    const = lambda a: pl.BlockSpec(a.shape, lambda b: (0,) * a.ndim)
    return pl.pallas_call(
        functools.partial(_ret_kernel, chunk=chunk),
        grid=(B,),
        in_specs=[blk(RET_KEY_WIDTH, COL_RQ), blk(RET_KEY_WIDTH, COL_RK), blk(RET_WIDTH, COL_RV),
                  blk(RET_WIDTH, COL_RG), pl.BlockSpec((1, S, 2 * LANES), lambda b: (b, 0, 0)),
                  const(decay), const(zeta), const(xi), const(gch), const(gn_w)],
        out_specs=pl.BlockSpec((1, S, RET_WIDTH), lambda b: (b, 0, 0)),
        out_shape=jax.ShapeDtypeStruct((B, S, RET_WIDTH), BF16),
        scratch_shapes=[pltpu.VMEM((N_RET_HEADS // 2, LANES, RET_VAL_DIM), F32)],
        compiler_params=_cparams(1),
        name="retention",
    )(z3, z3, z3, z3, tab_r, decay, zeta, xi, gch, gn_w)


def _retention_consts(chunk):
    C = chunk
    log_g = jnp.log(1.0 - 2.0 ** (-5.0 - jnp.arange(N_RET_HEADS, dtype=F32)))
    i = jnp.arange(C, dtype=F32)
    diff = i[:, None] - i[None, :]
    decay = jnp.where(diff[None] >= 0, jnp.exp(jnp.maximum(diff, 0.0)[None] * log_g[:, None, None]), 0.0)
    zeta = jnp.exp((C - 1.0 - i)[None, :] * log_g[:, None])
    xi = jnp.exp((i + 1.0)[None, :] * log_g[:, None])
    g_chunk = jnp.exp(C * log_g)
    pair = lambda a: a.reshape(N_RET_HEADS // 2, 2, -1)
    zeta_p = jnp.repeat(pair(zeta).transpose(0, 2, 1), RET_KEY_DIM, axis=2)
    xi_b = jnp.broadcast_to(xi[:, :, None], (N_RET_HEADS, C, RET_VAL_DIM))
    gch = jnp.broadcast_to(jnp.repeat(pair(g_chunk), RET_KEY_DIM, axis=1).reshape(N_RET_HEADS // 2, LANES, 1),
                           (N_RET_HEADS // 2, LANES, RET_VAL_DIM))
    return decay.astype(F32), zeta_p.astype(F32), xi_b.astype(F32), gch.astype(F32)


def _out_proj_kernel(attn_ref, ret_ref, h_ref, wa_ref, wr_ref, g1_ref, g2_ref, h1_ref, m_ref, *, sub):
    for r0 in range(0, h_ref.shape[0], sub):
        sl = slice(r0, r0 + sub)
        mix = _dot(attn_ref[sl, :], wa_ref[...]) + _dot(ret_ref[sl, :], wr_ref[...])
        h1 = h_ref[sl, :] + _rms(mix, g1_ref[...])
        h1_ref[sl, :] = h1
        m_ref[sl, :] = _rms(h1, g2_ref[...]).astype(m_ref.dtype)


def _out_proj(attn, ret, h, wo, g1, g2):
    T, D = h.shape
    tm = min(512, T)
    wa_rows, wr_rows = attn.shape[1], ret.shape[1]
    assert wa_rows == wr_rows and wo.shape[0] == wa_rows + wr_rows
    row = lambda w: pl.BlockSpec((tm, w), lambda i: (i, 0))
    const = lambda a: pl.BlockSpec(a.shape, lambda i: (0, 0))
    return pl.pallas_call(
        functools.partial(_out_proj_kernel, sub=min(256, tm)),
        grid=(T // tm,),
        in_specs=[row(wa_rows), row(wr_rows), row(D),
                  pl.BlockSpec((wa_rows, D), lambda i: (0, 0)), pl.BlockSpec((wr_rows, D), lambda i: (1, 0)),
                  const(g1), const(g2)],
        out_specs=[row(D), row(D)],
        out_shape=[jax.ShapeDtypeStruct((T, D), F32), jax.ShapeDtypeStruct((T, D), BF16)],
        compiler_params=_cparams(1),
        name="out_proj",
    )(attn, ret, h, wo, wo, g1, g2)


def _ffn_kernel(m_ref, w1_ref, w2_ref, y_ref):
    @pl.when(pl.program_id(1) == 0)
    def _():
        y_ref[...] = jnp.zeros(y_ref.shape, F32)

    u = jnp.maximum(_dot(m_ref[...], w1_ref[...]), 0.0)
    y_ref[...] += _dot((u * u).astype(BF16), w2_ref[...])


def _ffn(m, w1, w2):
    T, D = m.shape
    F = w1.shape[1]
    tm = min(1024, T)
    tf = min(1024, F)
    return pl.pallas_call(
        _ffn_kernel,
        grid=(T // tm, F // tf),
        in_specs=[pl.BlockSpec((tm, D), lambda i, f: (i, 0)),
                  pl.BlockSpec((D, tf), lambda i, f: (0, f)),
                  pl.BlockSpec((tf, D), lambda i, f: (f, 0))],
        out_specs=pl.BlockSpec((tm, D), lambda i, f: (i, 0)),
        out_shape=jax.ShapeDtypeStruct((T, D), F32),
        compiler_params=_cparams(2),
        name="ffn",
    )(m, w1, w2)


def _ple_kernel(h_ref, y_ref, p_ref, wg_ref, wp_ref, gff_ref, g_ref, o_ref, *, sub):
    for r0 in range(0, h_ref.shape[0], sub):
        sl = slice(r0, r0 + sub)
        h = h_ref[sl, :] + _rms(y_ref[sl, :], gff_ref[...])
        gate = jax.nn.sigmoid(_dot(h.astype(BF16), wg_ref[...]))
        e = _dot(p_ref[sl, :].astype(BF16), wp_ref[...])
        o_ref[sl, :] = h + _rms(gate * e, g_ref[...])


def _ple(h, y, p, layer, wg, wp, g_ff, g):
    T, D = h.shape
    tm = min(512, T)
    row = lambda w: pl.BlockSpec((tm, w), lambda i: (i, 0))
    const = lambda a: pl.BlockSpec(a.shape, lambda i: (0, 0))
    return pl.pallas_call(
        functools.partial(_ple_kernel, sub=min(256, tm)),
        grid=(T // tm,),
        in_specs=[row(D), row(D), pl.BlockSpec((None, tm, p.shape[2]), lambda i: (layer, i, 0)),
                  const(wg), const(wp), const(g_ff), const(g)],
        out_specs=row(D),
        out_shape=jax.ShapeDtypeStruct((T, D), F32),
        compiler_params=_cparams(1),
        name="ple",
    )(h, y, p, wg, wp, g_ff, g)


def _rope_table(positions, half, theta, group):
    inv = theta ** (-jnp.arange(half, dtype=F32) / half)
    ang = positions.astype(F32)[..., None] * inv
    compact = jnp.concatenate([jnp.cos(ang), jnp.sin(ang), jnp.ones_like(ang[..., :1])], axis=-1)
    spread = np.zeros((2 * half + 1, 2 * LANES), np.float32)
    for lane in range(LANES):
        d = lane % group
        if d < 2 * half:
            spread[d % half, lane] = 1.0
            spread[half + d % half, LANES + lane] = -1.0 if d < half else 1.0
        else:
            spread[2 * half, lane] = 1.0
    return jnp.einsum("bsk,kn->bsn", compact, jnp.asarray(spread), precision=lax.Precision.HIGHEST)


def _cast_kernel(x_ref, o_ref):
    o_ref[...] = x_ref[...].astype(o_ref.dtype)


def _to_bf16(w, layer):
    _, R, C = w.shape
    tr = max(16, min(R, CAST_BLOCK_BYTES // (4 * C) // 16 * 16))
    while R % tr:
        tr -= 16
    return pl.pallas_call(
        _cast_kernel,
        grid=(R // tr,),
        in_specs=[pl.BlockSpec((None, tr, C), lambda r: (layer, r, 0))],
        out_specs=pl.BlockSpec((tr, C), lambda r: (r, 0)),
        out_shape=jax.ShapeDtypeStruct((R, C), BF16),
        compiler_params=_cparams(1),
        name="to_bf16",
    )(w)


def _reorder_w_in(w):
    a0 = ATTN_WIDTH
    a1 = a0 + 2 * ATTN_HEAD_DIM
    a2 = a1 + IDX_WIDTH
    a3 = a2 + IDX_HEAD_DIM + N_IDX_HEADS
    a4 = a3 + 2 * RET_KEY_WIDTH
    pad = jnp.zeros((w.shape[0], Z_WIDTH - Z_USED + LANES - IDX_HEAD_DIM - N_IDX_HEADS), BF16)
    return jnp.concatenate([w[:, :a0], w[:, a1:a2], w[:, a4:], w[:, a3:a4], w[:, a0:a1], w[:, a2:a3], pad], axis=1)


def kernel(x, p, positions, w_in, w_out, w_ff1, w_ff2, w_ple, w_ple_gate, pre_mix_norm, post_mix_norm,
           pre_ff_norm, post_ff_norm, ple_norm, ret_gn):
    B, S, D = x.shape
    depth = w_in.shape[0]
    T = B * S
    topk = min(TOPK_MAX, S // 4)
    ret_chunk = min(256, S)
    tab_a = _rope_table(positions, ATTN_ROT_HALF, ROPE_THETA, ATTN_HEAD_DIM)
    tab_i = _rope_table(positions, IDX_ROT_HALF, ROPE_THETA, IDX_HEAD_DIM)
    tab_r = _rope_table(positions, RET_KEY_DIM // 2, RET_THETA, RET_KEY_DIM)
    ret_consts = _retention_consts(ret_chunk)
    vec = lambda a: a.reshape(1, -1).astype(F32)

    h = x.reshape(T, D)
    for i in range(depth):
        z = _norm_proj(h, vec(pre_mix_norm[i]), _reorder_w_in(w_in[i].astype(BF16)))
        z3 = z.reshape(B, S, Z_WIDTH)
        attn = _attention(z3, tab_a, tab_i, topk)
        ret = _retention(z3, tab_r, ret_consts, vec(ret_gn[i]), ret_chunk)
        h, m = _out_proj(attn.reshape(T, ATTN_WIDTH), ret.reshape(T, RET_WIDTH), h,
                         _to_bf16(w_out, i), vec(post_mix_norm[i]), vec(pre_ff_norm[i]))
        y = _ffn(m, _to_bf16(w_ff1, i), _to_bf16(w_ff2, i))
        h = _ple(h, y, p.reshape(depth, T, -1), i, _to_bf16(w_ple_gate, i), _to_bf16(w_ple, i),
                 vec(post_ff_norm[i]), vec(ple_norm[i]))
    return h.reshape(B, S, D)
```

```python
import functools
import math

import jax
import jax.numpy as jnp
import numpy as np
from jax import lax
from jax.experimental import pallas as pl
from jax.experimental.pallas import tpu as pltpu

N_ATTN_HEADS = 8
ATTN_HEAD_DIM = 128
ATTN_ROT_HALF = 16
ROPE_THETA = 500000.0
TOPK_MAX = 256
N_IDX_HEADS = 16
IDX_HEAD_DIM = 64
IDX_ROT_HALF = 8
N_RET_HEADS = 8
RET_KEY_DIM = 64
RET_VAL_DIM = 128
RET_THETA = 10000.0
ATTN_WIDTH = N_ATTN_HEADS * ATTN_HEAD_DIM
IDX_WIDTH = N_IDX_HEADS * IDX_HEAD_DIM
RET_KEY_WIDTH = N_RET_HEADS * RET_KEY_DIM
RET_WIDTH = N_RET_HEADS * RET_VAL_DIM
RMS_EPS = 1e-6
GN_EPS = 1e-5

LANES = 128
COL_AQ = 0
COL_IQ = COL_AQ + ATTN_WIDTH
COL_RV = COL_IQ + IDX_WIDTH
COL_RG = COL_RV + RET_WIDTH
COL_RQ = COL_RG + RET_WIDTH
COL_RK = COL_RQ + RET_KEY_WIDTH
COL_AK = COL_RK + RET_KEY_WIDTH
COL_AV = COL_AK + ATTN_HEAD_DIM
COL_IKW = COL_AV + ATTN_HEAD_DIM
Z_USED = COL_IKW + LANES
Z_TILE_N = 512
Z_WIDTH = -(-Z_USED // Z_TILE_N) * Z_TILE_N

MASK_NEG = -1e30
THETA_ALL = -3e38
CUT_ALL = 1e9
BOUND_SLACK = 1.05
DENOM_SAFE = 1e-20
SEARCH_MAX_STEPS = 320
CAST_BLOCK_BYTES = 4 * 1024 * 1024
V7X_VMEM_BYTES = 64 * 1024 * 1024
VMEM_LIMIT = V7X_VMEM_BYTES - 8 * 1024 * 1024

BF16 = jnp.bfloat16
F32 = jnp.float32


def _cparams(n_grid):
    return pltpu.CompilerParams(dimension_semantics=("arbitrary",) * n_grid,
                                vmem_limit_bytes=VMEM_LIMIT)


def _rms(x, gain):
    ms = jnp.mean(x * x, axis=-1, keepdims=True)
    return x * lax.rsqrt(ms + RMS_EPS) * gain


def _rope(xb, cos, sin, swap):
    return xb.astype(F32) * cos + jnp.dot(xb, swap, preferred_element_type=F32) * sin


def _swap_matrix(group, half):
    l = jnp.arange(LANES)
    d = l % group
    partner = jnp.where(d < half, l + half, jnp.where(d < 2 * half, l - half, -1))
    return (jnp.arange(LANES)[:, None] == partner[None, :]).astype(BF16)


def _dot_t(a, b):
    return lax.dot_general(a, b, (((1,), (1,)), ((), ())), preferred_element_type=F32)


def _dot(a, b):
    return jnp.dot(a, b, preferred_element_type=F32)


def _norm_proj_kernel(h_ref, g_ref, w_ref, z_ref, *, sub, tn):
    for r0 in range(0, h_ref.shape[0], sub):
        sl = slice(r0, r0 + sub)
        a = _rms(h_ref[sl, :], g_ref[...]).astype(BF16)
        for c0 in range(0, w_ref.shape[1], tn):
            z_ref[sl, c0:c0 + tn] = _dot(a, w_ref[:, c0:c0 + tn]).astype(z_ref.dtype)


def _norm_proj(h, gain, w):
    T, D = h.shape
    N = w.shape[1]
    tm = min(512, T)
    return pl.pallas_call(
        functools.partial(_norm_proj_kernel, sub=min(256, tm), tn=Z_TILE_N),
        grid=(T // tm,),
        in_specs=[pl.BlockSpec((tm, D), lambda i: (i, 0)),
                  pl.BlockSpec((1, D), lambda i: (0, 0)),
                  pl.BlockSpec((D, N), lambda i: (0, 0), pipeline_mode=pl.Buffered(1))],
        out_specs=pl.BlockSpec((tm, N), lambda i: (i, 0)),
        out_shape=jax.ShapeDtypeStruct((T, N), BF16),
        compiler_params=_cparams(1),
        name="norm_proj",
    )(h, gain, w)


def _attn_kernel(aq_ref, iq_ref, iwq_ref, ak_ref, av_ref, ikw_ref,
                 taq_ref, tak_ref, tiq_ref, tik_ref, pa_ref, pi_ref, ex_ref, o_ref,
                 kr_ref, kd0_ref, kd1_ref, vaug_ref, qa_ref, qi_ref, wb_ref, score_ref, score_t_ref,
                 th_ref, cut_ref, s_ref, m_ref, acc_ref, bound_ref, ksq_ref, tie_flag_ref, *, topk, tq, tk):
    S = ak_ref.shape[1]
    qi = pl.program_id(1)
    n_chunks = (qi * tq + tq + tk - 1) // tk
    reps = tk // LANES
    rows_all = N_ATTN_HEADS * tq
    ones_sq = jnp.ones((LANES, LANES), BF16)

    @pl.when(qi == 0)
    def _():
        rows = min(256, S)

        def body(r, _):
            r0 = pl.multiple_of(r * rows, rows)
            sl = pl.ds(r0, rows)
            kr = _rope(ak_ref[0, sl, :], tak_ref[0, sl, :LANES], tak_ref[0, sl, LANES:], pa_ref[...])
            kr_ref[sl, :] = kr.astype(BF16)
            ksq = _dot((kr * kr).astype(BF16), ones_sq).reshape(rows // 8, 8, LANES)
            ksq_ref[...] = jnp.maximum(jnp.where(r == 0, 0.0, ksq_ref[...]), jnp.max(ksq, axis=0))
            lane_r = lax.broadcasted_iota(jnp.int32, (rows, LANES), 1)
            ik = _rope(ikw_ref[0, sl, :], tik_ref[0, sl, :LANES], tik_ref[0, sl, LANES:], pi_ref[...])
            ik = jnp.where(lane_r < IDX_HEAD_DIM, ik, 0.0)
            kd0_ref[sl, :] = ik.astype(BF16)
            kd1_ref[sl, :] = pltpu.roll(ik, IDX_HEAD_DIM, 1).astype(BF16)
            vaug_ref[sl, :] = jnp.concatenate([av_ref[0, sl, :], jnp.ones((rows, LANES), BF16)], axis=1)
            score_t_ref[sl, :] = jnp.full((rows, tq), -jnp.inf, F32)
            return 0
        lax.fori_loop(0, S // rows, body, 0)

    ca = taq_ref[0, :, :LANES] * (ATTN_HEAD_DIM ** -0.5)
    sa = taq_ref[0, :, LANES:] * (ATTN_HEAD_DIM ** -0.5)
    k_norm = jnp.sqrt(jnp.max(ksq_ref[...], axis=0, keepdims=True))
    for h in range(N_ATTN_HEADS):
        x = aq_ref[0, :, h * LANES:(h + 1) * LANES]
        qf = _rope(x, ca, sa, pa_ref[...])
        qa_ref[h * tq:(h + 1) * tq, :] = qf.astype(BF16)
        q_norm = jnp.sqrt(_dot((qf * qf).astype(BF16), ones_sq))
        bound_ref[h * tq:(h + 1) * tq, :] = q_norm * k_norm * BOUND_SLACK
    ci, si = tiq_ref[0, :, :LANES], tiq_ref[0, :, LANES:]
    for j in range(IDX_WIDTH // LANES):
        x = iq_ref[0, :, j * LANES:(j + 1) * LANES]
        qi_ref[:, j * LANES:(j + 1) * LANES] = _rope(x, ci, si, pi_ref[...]).astype(BF16)
    iw_b = jnp.dot(iwq_ref[0], ex_ref[...], preferred_element_type=F32)
    for h in range(N_IDX_HEADS):
        wb_ref[h] = iw_b[:, h * LANES:(h + 1) * LANES]

    col_in_chunk = lax.broadcasted_iota(jnp.int32, (tq, tk), 1)

    def lane_groups(x, op):
        r = x[:, :LANES]
        for g in range(1, reps):
            r = op(r, x[:, g * LANES:(g + 1) * LANES])
        return r

    def tile_lanes(x):
        return jnp.concatenate([x] * reps, axis=1) if reps > 1 else x

    key_t = lax.broadcasted_iota(jnp.int32, (tk, tq), 0)

    def fold8(x, op):
        x3 = x.reshape(tk // 8, 8, x.shape[1])
        return jnp.max(x3, axis=0) if op == "max" else (jnp.min(x3, axis=0) if op == "min" else jnp.sum(x3, axis=0))

    rb = min(LANES, tq)
    row_b = lax.broadcasted_iota(jnp.int32, (rb, tk), 0)
    col_b = lax.broadcasted_iota(jnp.int32, (rb, tk), 1)
    key_tb = lax.broadcasted_iota(jnp.int32, (tk, rb), 0)
    qry_tb = lax.broadcasted_iota(jnp.int32, (tk, rb), 1)

    def a_body(c, carry):
        rmax, rmin = carry
        c0 = pl.multiple_of(c * tk, tk)
        k0 = kd0_ref[pl.ds(c0, tk), :]
        k1 = kd1_ref[pl.ds(c0, tk), :]
        maxs, mins = [], []
        for r in range(tq // rb):
            rows = slice(r * rb, (r + 1) * rb)
            q0 = qi * tq + r * rb
            acc = jnp.zeros((rb, tk), F32)
            for j in range(IDX_WIDTH // LANES):
                qp = qi_ref[rows, j * LANES:(j + 1) * LANES]
                for par, kd in ((0, k0), (1, k1)):
                    logits = _dot_t(qp, kd)
                    acc = acc + tile_lanes(wb_ref[2 * j + par, rows, :]) * jnp.maximum(logits, 0.0)
            causal = (c0 + col_b) <= (q0 + row_b)
            score_ref[c, rows, :] = jnp.where(causal, acc, -jnp.inf)
            acc_t = acc.T
            causal_t = (c0 + key_tb) <= (q0 + qry_tb)
            sc_t = jnp.where(causal_t, acc_t, -jnp.inf)
            score_t_ref[pl.ds(c0, tk), rows] = sc_t
            maxs.append(fold8(sc_t, "max"))
            mins.append(fold8(jnp.where(causal_t, acc_t, jnp.inf), "min"))
        rmax = jnp.maximum(rmax, jnp.concatenate(maxs, axis=1))
        rmin = jnp.minimum(rmin, jnp.concatenate(mins, axis=1))
        return rmax, rmin

    extremes = (jnp.full((8, tq), -jnp.inf, F32), jnp.full((8, tq), jnp.inf, F32))
    extremes = lax.fori_loop(0, n_chunks // 2, lambda cp, e: a_body(2 * cp + 1, a_body(2 * cp, e)), extremes)
    rmax, rmin = lax.fori_loop(0, n_chunks % 2, lambda _, e: a_body(n_chunks - 1, e), extremes)

    vec = (1, tq)
    kf = float(topk)
    n_keys = qi * tq + lax.broadcasted_iota(jnp.int32, vec, 1) + 1

    lo0 = jnp.min(rmin, axis=0, keepdims=True)
    hi0 = jnp.max(rmax, axis=0, keepdims=True)
    key_f = key_t.astype(F32)

    def threshold_search(n_rows):
        def count_t(pred):
            accs = [jnp.zeros((8, tq), F32) for _ in range(4)]
            i = 0
            for r0 in range(0, n_rows, tk):
                ind = jnp.where(pred(r0, score_t_ref[r0:r0 + tk, :]), 1.0, 0.0)
                for g in range(tk // 8):
                    accs[i % 4] = accs[i % 4] + ind[g * 8:(g + 1) * 8, :]
                    i += 1
            return jnp.sum((accs[0] + accs[1]) + (accs[2] + accs[3]), axis=0, keepdims=True)

        def b_cond(carry):
            return jnp.logical_and(carry[1] > 0.0, carry[0] < SEARCH_MAX_STEPS)

        def b_body(carry):
            state = (carry[0],) + tuple(carry[2:])
            flag = jnp.max(1.0 - carry[5])
            for _ in range(steps_per_check):
                state = b_step(*state)
            return (state[0], flag) + tuple(state[1:])

        def b_step(it, lo, hi, chi, done, th, tie):
            mid = jnp.where(it == 0, hi, 0.5 * lo + 0.5 * hi)
            cnt = count_t(lambda r0, sc: sc >= mid)
            active = done == 0.0
            adjacent = jnp.logical_and(it > 0, jnp.logical_or(mid <= lo, mid >= hi))
            tie_now = jnp.logical_and(active, adjacent)
            live = jnp.logical_and(active, jnp.logical_not(adjacent))
            hit = jnp.logical_and(live, cnt == kf)
            above = jnp.logical_and(live, cnt > kf)
            below = jnp.logical_and(live, cnt < kf)
            th = jnp.where(tie_now, lo, jnp.where(hit, mid, th))
            tie = jnp.where(tie_now, 1.0, tie)
            lo = jnp.where(above, mid, lo)
            hi = jnp.where(below, mid, hi)
            chi = jnp.where(below, cnt, chi)
            done = jnp.where(jnp.logical_or(tie_now, hit), 1.0, done)
            return it + 1, lo, hi, chi, done, th, tie

        steps_per_check = 2 if n_rows <= 4 * tk else 1

        init = (jnp.int32(0), jnp.float32(1.0), lo0, hi0,
                jnp.zeros(vec, F32), jnp.where(n_keys <= topk, 1.0, 0.0),
                jnp.full(vec, THETA_ALL, F32), jnp.zeros(vec, F32))
        _, _, _, _, chi, _, th, tie = lax.while_loop(b_cond, b_body, init)

        def tie_cut():
            need = kf - chi

            def body(_, carry):
                lo_i, hi_i = carry
                mid = jnp.floor(0.5 * (lo_i + hi_i))
                cnt = count_t(lambda r0, sc: jnp.logical_and(sc == th, key_f + float(r0) <= mid))
                ge = cnt >= need
                return jnp.where(ge, lo_i, mid), jnp.where(ge, mid, hi_i)

            n_iter = int(math.ceil(math.log2(S))) + 1
            _, hi_i = lax.fori_loop(0, n_iter, body, (jnp.full(vec, -1.0, F32), jnp.full(vec, float(S - 1), F32)))
            return jnp.where(tie > 0.0, hi_i, CUT_ALL)

        th_ref[...] = th
        has_tie = jnp.max(tie) > 0.0
        tie_flag_ref[0] = has_tie.astype(jnp.int32)
        cut_ref[...] = lax.cond(has_tie, tie_cut, lambda: jnp.full(vec, CUT_ALL, F32))

    pair_rows = 2 * tk
    variant = (n_chunks * tk + pair_rows - 1) // pair_rows - 1
    for v in range(-(-S // pair_rows)):
        pl.when(variant == v)(functools.partial(threshold_search, min((v + 1) * pair_rows, S)))

    th_c = jnp.broadcast_to(th_ref[...], (LANES, tq)).T
    cut_c = jnp.broadcast_to(cut_ref[...], (LANES, tq)).T
    th_t = tile_lanes(th_c)
    cut_t = tile_lanes(cut_c)

    colf_in_chunk = col_in_chunk.astype(F32)

    def masked_logits(c, with_ties):
        c0 = pl.multiple_of(c * tk, tk)
        sc = score_ref[c]
        if with_ties:
            colf = colf_in_chunk + (c * tk).astype(F32)
            sel = jnp.logical_or(sc > th_t, jnp.logical_and(sc == th_t, colf <= cut_t))
        else:
            sel = sc >= th_t
        bias = jnp.where(sel, 0.0, MASK_NEG)
        s = _dot_t(qa_ref[...], kr_ref[pl.ds(c0, tk), :])
        return (s.reshape(N_ATTN_HEADS, tq, tk) + bias[None]).reshape(rows_all, tk)

    def accumulate(c, p):
        c0 = pl.multiple_of(c * tk, tk)
        acc_ref[...] += _dot(p.astype(BF16), vaug_ref[pl.ds(c0, tk), :])

    def one_pass(with_ties):
        acc_ref[...] = jnp.zeros(acc_ref.shape, F32)
        shift = tile_lanes(bound_ref[...])

        def pair(cp, _):
            c = 2 * cp
            p0 = jnp.exp(masked_logits(c, with_ties) - shift).astype(BF16)
            p1 = jnp.exp(masked_logits(c + 1, with_ties) - shift).astype(BF16)
            r0 = pl.multiple_of(c * tk, 2 * tk)
            acc_ref[...] += _dot(jnp.concatenate([p0, p1], axis=1), vaug_ref[pl.ds(r0, 2 * tk), :])
            return 0

        def last(_, __):
            accumulate(n_chunks - 1, jnp.exp(masked_logits(n_chunks - 1, with_ties) - shift))
            return 0
        lax.fori_loop(0, n_chunks // 2, pair, 0)
        lax.fori_loop(0, n_chunks % 2, last, 0)

    def two_pass():
        m_ref[...] = jnp.full(m_ref.shape, MASK_NEG, F32)

        def body1(c, _):
            s = masked_logits(c, True)
            s_ref[c] = s
            m_ref[...] = jnp.maximum(m_ref[...], lane_groups(s, jnp.maximum))
            return 0
        lax.fori_loop(0, n_chunks, body1, 0)
        m_ref[...] = jnp.broadcast_to(jnp.max(m_ref[...], axis=1, keepdims=True), m_ref.shape)
        acc_ref[...] = jnp.zeros(acc_ref.shape, F32)

        def body2(c, _):
            accumulate(c, jnp.exp(s_ref[c] - tile_lanes(m_ref[...])))
            return 0
        lax.fori_loop(0, n_chunks, body2, 0)

    has_tie = tie_flag_ref[0] > 0
    pl.when(has_tie)(functools.partial(one_pass, True))
    pl.when(jnp.logical_not(has_tie))(functools.partial(one_pass, False))
    l_min = jnp.min(acc_ref[:, LANES:])
    pl.when(jnp.logical_not(l_min > DENOM_SAFE))(two_pass)
    out = acc_ref[:, :LANES] / acc_ref[:, LANES:]
    for h in range(N_ATTN_HEADS):
        o_ref[0, :, h * LANES:(h + 1) * LANES] = out[h * tq:(h + 1) * tq, :].astype(o_ref.dtype)


def _attention(z3, tab_a, tab_i, topk):
    B, S, _ = z3.shape
    tq = min(256, S)
    tk = min(256, S)
    assert S % (2 * tk) == 0 and S % tq == 0 and tk % tq == 0, "key chunks are consumed in pairs"
    qblk = lambda w, col: pl.BlockSpec((1, tq, w), lambda b, q: (b, q, col // w))
    kblk = lambda w, col: pl.BlockSpec((1, S, w), lambda b, q: (b, 0, col // w))
    tab_q = pl.BlockSpec((1, tq, 2 * LANES), lambda b, q: (b, q, 0))
    tab_k = pl.BlockSpec((1, S, 2 * LANES), lambda b, q: (b, 0, 0))
    rows = N_ATTN_HEADS * tq
    const = lambda a: pl.BlockSpec(a.shape, lambda b, q: (0, 0))
    perm_a = _swap_matrix(ATTN_HEAD_DIM, ATTN_ROT_HALF)
    perm_i = _swap_matrix(IDX_HEAD_DIM, IDX_ROT_HALF)
    idx_w_scale = (N_IDX_HEADS ** -0.5) * (IDX_HEAD_DIM ** -0.5)
    assert math.frexp(idx_w_scale)[0] == 0.5, "the folded scale must be a power of two to stay exact in bf16"
    expand = jnp.where(jnp.arange(LANES)[:, None] == IDX_HEAD_DIM + jnp.arange(N_IDX_HEADS * LANES)[None, :] // LANES,
                       idx_w_scale, 0.0).astype(BF16)
    return pl.pallas_call(
        functools.partial(_attn_kernel, topk=topk, tq=tq, tk=tk),
        grid=(B, S // tq),
        in_specs=[qblk(ATTN_WIDTH, COL_AQ), qblk(IDX_WIDTH, COL_IQ), qblk(LANES, COL_IKW),
                  kblk(LANES, COL_AK), kblk(LANES, COL_AV), kblk(LANES, COL_IKW),
                  tab_q, tab_k, tab_q, tab_k, const(perm_a), const(perm_i), const(expand)],
        out_specs=pl.BlockSpec((1, tq, ATTN_WIDTH), lambda b, q: (b, q, 0)),
        out_shape=jax.ShapeDtypeStruct((B, S, ATTN_WIDTH), BF16),
        scratch_shapes=[pltpu.VMEM((S, LANES), BF16),
                        pltpu.VMEM((S, LANES), BF16),
                        pltpu.VMEM((S, LANES), BF16),
                        pltpu.VMEM((S, 2 * LANES), BF16),
                        pltpu.VMEM((rows, LANES), BF16),
                        pltpu.VMEM((tq, IDX_WIDTH), BF16),
                        pltpu.VMEM((N_IDX_HEADS, tq, LANES), F32),
                        pltpu.VMEM((S // tk, tq, tk), F32),
                        pltpu.VMEM((S, tq), F32),
                        pltpu.VMEM((1, tq), F32),
                        pltpu.VMEM((1, tq), F32),
                        pltpu.VMEM((S // tk, rows, tk), F32),
                        pltpu.VMEM((rows, LANES), F32),
                        pltpu.VMEM((rows, 2 * LANES), F32),
                        pltpu.VMEM((rows, LANES), F32),
                        pltpu.VMEM((8, LANES), F32),
                        pltpu.SMEM((1,), jnp.int32)],
        compiler_params=_cparams(2),
        name="dsa_attention",
    )(z3, z3, z3, z3, z3, z3, tab_a, tab_a, tab_i, tab_i, perm_a, perm_i, expand)


def _ret_kernel(rq_ref, rk_ref, rv_ref, rg_ref, tab_ref, decay_ref, zeta_ref, xi_ref, gch_ref, gn_ref,
                o_ref, state_ref, *, chunk):
    S = rq_ref.shape[1]
    C = chunk
    half = RET_KEY_DIM
    state_ref[...] = jnp.zeros(state_ref.shape, F32)
    lane = lax.broadcasted_iota(jnp.int32, (C, LANES), 1)
    first = lane < half

    def body(c, _):
        r0 = pl.multiple_of(c * C, C)
        sl = pl.ds(r0, C)
        cos, sin = tab_ref[0, sl, :LANES], tab_ref[0, sl, LANES:]
        cos_k, sin_k = cos * (RET_KEY_DIM ** -0.5), sin * (RET_KEY_DIM ** -0.5)
        first_half = lane % RET_KEY_DIM < RET_KEY_DIM // 2

        def rope(x, c, s):
            partner = jnp.where(first_half, pltpu.roll(x, LANES - RET_KEY_DIM // 2, 1),
                                pltpu.roll(x, RET_KEY_DIM // 2, 1))
            return x * c + partner * s

        for j in range(N_RET_HEADS // 2):
            q = rope(rq_ref[0, sl, j * LANES:(j + 1) * LANES].astype(F32), cos, sin)
            k = rope(rk_ref[0, sl, j * LANES:(j + 1) * LANES].astype(F32), cos_k, sin_k)
            kzt = (k * zeta_ref[j]).T.astype(BF16)
            state = state_ref[j]
            state_b = state.astype(BF16)
            new_rows = []
            for par in range(2):
                h = 2 * j + par
                keep = first if par == 0 else jnp.logical_not(first)
                qh = jnp.where(keep, q, 0.0).astype(BF16)
                kh = jnp.where(keep, k, 0.0).astype(BF16)
                v = rv_ref[0, sl, h * LANES:(h + 1) * LANES]
                inner = _dot_t(qh, kh) * decay_ref[h]
                o = _dot(inner.astype(BF16), v) + _dot(qh, state_b) * xi_ref[h]
                new_rows.append(_dot(kzt[par * half:(par + 1) * half, :], v))
                mu = jnp.mean(o, axis=-1, keepdims=True)
                d = o - mu
                var = jnp.mean(d * d, axis=-1, keepdims=True)
                y = d * lax.rsqrt(var + GN_EPS) * gn_ref[:, h * LANES:(h + 1) * LANES]
                g = rg_ref[0, sl, h * LANES:(h + 1) * LANES].astype(F32)
                o_ref[0, sl, h * LANES:(h + 1) * LANES] = (y * (g * jax.nn.sigmoid(g))).astype(o_ref.dtype)
            state_ref[j] = gch_ref[j] * state + jnp.concatenate(new_rows, axis=0)
        return 0

    lax.fori_loop(0, S // C, body, 0)


def _retention(z3, tab_r, consts, gn_w, chunk):
    B, S, _ = z3.shape
    decay, zeta, xi, gch = consts
    blk = lambda w, col: pl.BlockSpec((1, S, w), lambda b: (b, 0, col // w))
    const = lambda a: pl.BlockSpec(a.shape, lambda b: (0,) * a.ndim)
    return pl.pallas_call(
        functools.partial(_ret_kernel, chunk=chunk),
        grid=(B,),
        in_specs=[blk(RET_KEY_WIDTH, COL_RQ), blk(RET_KEY_WIDTH, COL_RK), blk(RET_WIDTH, COL_RV),
                  blk(RET_WIDTH, COL_RG), pl.BlockSpec((1, S, 2 * LANES), lambda b: (b, 0, 0)),
                  const(decay), const(zeta), const(xi), const(gch), const(gn_w)],
        out_specs=pl.BlockSpec((1, S, RET_WIDTH), lambda b: (b, 0, 0)),
        out_shape=jax.ShapeDtypeStruct((B, S, RET_WIDTH), BF16),
        scratch_shapes=[pltpu.VMEM((N_RET_HEADS // 2, LANES, RET_VAL_DIM), F32)],
        compiler_params=_cparams(1),
        name="retention",
    )(z3, z3, z3, z3, tab_r, decay, zeta, xi, gch, gn_w)


def _retention_consts(chunk):
    C = chunk
    log_g = jnp.log(1.0 - 2.0 ** (-5.0 - jnp.arange(N_RET_HEADS, dtype=F32)))
    i = jnp.arange(C, dtype=F32)
    diff = i[:, None] - i[None, :]
    decay = jnp.where(diff[None] >= 0, jnp.exp(jnp.maximum(diff, 0.0)[None] * log_g[:, None, None]), 0.0)
    zeta = jnp.exp((C - 1.0 - i)[None, :] * log_g[:, None])
    xi = jnp.exp((i + 1.0)[None, :] * log_g[:, None])
    g_chunk = jnp.exp(C * log_g)
    pair = lambda a: a.reshape(N_RET_HEADS // 2, 2, -1)
    zeta_p = jnp.repeat(pair(zeta).transpose(0, 2, 1), RET_KEY_DIM, axis=2)
    xi_b = jnp.broadcast_to(xi[:, :, None], (N_RET_HEADS, C, RET_VAL_DIM))
    gch = jnp.broadcast_to(jnp.repeat(pair(g_chunk), RET_KEY_DIM, axis=1).reshape(N_RET_HEADS // 2, LANES, 1),
                           (N_RET_HEADS // 2, LANES, RET_VAL_DIM))
    return decay.astype(F32), zeta_p.astype(F32), xi_b.astype(F32), gch.astype(F32)


def _out_proj_kernel(attn_ref, ret_ref, h_ref, wa_ref, wr_ref, g1_ref, g2_ref, h1_ref, m_ref, *, sub):
    for r0 in range(0, h_ref.shape[0], sub):
        sl = slice(r0, r0 + sub)
        mix = _dot(attn_ref[sl, :], wa_ref[...]) + _dot(ret_ref[sl, :], wr_ref[...])
        h1 = h_ref[sl, :] + _rms(mix, g1_ref[...])
        h1_ref[sl, :] = h1
        m_ref[sl, :] = _rms(h1, g2_ref[...]).astype(m_ref.dtype)


def _out_proj(attn, ret, h, wo, g1, g2):
    T, D = h.shape
    tm = min(512, T)
    wa_rows, wr_rows = attn.shape[1], ret.shape[1]
    assert wa_rows == wr_rows and wo.shape[0] == wa_rows + wr_rows
    row = lambda w: pl.BlockSpec((tm, w), lambda i: (i, 0))
    const = lambda a: pl.BlockSpec(a.shape, lambda i: (0, 0))
    return pl.pallas_call(
        functools.partial(_out_proj_kernel, sub=min(256, tm)),
        grid=(T // tm,),
        in_specs=[row(wa_rows), row(wr_rows), row(D),
                  pl.BlockSpec((wa_rows, D), lambda i: (0, 0)), pl.BlockSpec((wr_rows, D), lambda i: (1, 0)),
                  const(g1), const(g2)],
        out_specs=[row(D), row(D)],
        out_shape=[jax.ShapeDtypeStruct((T, D), F32), jax.ShapeDtypeStruct((T, D), BF16)],
        compiler_params=_cparams(1),
        name="out_proj",
    )(attn, ret, h, wo, wo, g1, g2)


def _ffn_kernel(m_ref, w1_ref, w2_ref, y_ref):
    @pl.when(pl.program_id(1) == 0)
    def _():
        y_ref[...] = jnp.zeros(y_ref.shape, F32)

    u = jnp.maximum(_dot(m_ref[...], w1_ref[...]), 0.0)
    y_ref[...] += _dot((u * u).astype(BF16), w2_ref[...])


def _ffn(m, w1, w2):
    T, D = m.shape
    F = w1.shape[1]
    tm = min(1024, T)
    tf = min(1024, F)
    return pl.pallas_call(
        _ffn_kernel,
        grid=(T // tm, F // tf),
        in_specs=[pl.BlockSpec((tm, D), lambda i, f: (i, 0)),
                  pl.BlockSpec((D, tf), lambda i, f: (0, f)),
                  pl.BlockSpec((tf, D), lambda i, f: (f, 0))],
        out_specs=pl.BlockSpec((tm, D), lambda i, f: (i, 0)),
        out_shape=jax.ShapeDtypeStruct((T, D), F32),
        compiler_params=_cparams(2),
        name="ffn",
    )(m, w1, w2)


def _ple_kernel(h_ref, y_ref, p_ref, wg_ref, wp_ref, gff_ref, g_ref, o_ref, *, sub):
    for r0 in range(0, h_ref.shape[0], sub):
        sl = slice(r0, r0 + sub)
        h = h_ref[sl, :] + _rms(y_ref[sl, :], gff_ref[...])
        gate = jax.nn.sigmoid(_dot(h.astype(BF16), wg_ref[...]))
        e = _dot(p_ref[sl, :].astype(BF16), wp_ref[...])
        o_ref[sl, :] = h + _rms(gate * e, g_ref[...])


def _ple(h, y, p, layer, wg, wp, g_ff, g):
    T, D = h.shape
    tm = min(512, T)
    row = lambda w: pl.BlockSpec((tm, w), lambda i: (i, 0))
    const = lambda a: pl.BlockSpec(a.shape, lambda i: (0, 0))
    return pl.pallas_call(
        functools.partial(_ple_kernel, sub=min(256, tm)),
        grid=(T // tm,),
        in_specs=[row(D), row(D), pl.BlockSpec((None, tm, p.shape[2]), lambda i: (layer, i, 0)),
                  const(wg), const(wp), const(g_ff), const(g)],
        out_specs=row(D),
        out_shape=jax.ShapeDtypeStruct((T, D), F32),
        compiler_params=_cparams(1),
        name="ple",
    )(h, y, p, wg, wp, g_ff, g)


def _rope_table(positions, half, theta, group):
    inv = theta ** (-jnp.arange(half, dtype=F32) / half)
    ang = positions.astype(F32)[..., None] * inv
    compact = jnp.concatenate([jnp.cos(ang), jnp.sin(ang), jnp.ones_like(ang[..., :1])], axis=-1)
    spread = np.zeros((2 * half + 1, 2 * LANES), np.float32)
    for lane in range(LANES):
        d = lane % group
        if d < 2 * half:
            spread[d % half, lane] = 1.0
            spread[half + d % half, LANES + lane] = -1.0 if d < half else 1.0
        else:
            spread[2 * half, lane] = 1.0
    return jnp.einsum("bsk,kn->bsn", compact, jnp.asarray(spread), precision=lax.Precision.HIGHEST)


def _cast_kernel(x_ref, o_ref):
    o_ref[...] = x_ref[...].astype(o_ref.dtype)


def _to_bf16(w, layer):
    _, R, C = w.shape
    tr = max(16, min(R, CAST_BLOCK_BYTES // (4 * C) // 16 * 16))
    while R % tr:
        tr -= 16
    return pl.pallas_call(
        _cast_kernel,
        grid=(R // tr,),
        in_specs=[pl.BlockSpec((None, tr, C), lambda r: (layer, r, 0))],
        out_specs=pl.BlockSpec((tr, C), lambda r: (r, 0)),
        out_shape=jax.ShapeDtypeStruct((R, C), BF16),
        compiler_params=_cparams(1),
        name="to_bf16",
    )(w)


def _reorder_w_in(w):
    a0 = ATTN_WIDTH
    a1 = a0 + 2 * ATTN_HEAD_DIM
    a2 = a1 + IDX_WIDTH
    a3 = a2 + IDX_HEAD_DIM + N_IDX_HEADS
    a4 = a3 + 2 * RET_KEY_WIDTH
    pad = jnp.zeros((w.shape[0], Z_WIDTH - Z_USED + LANES - IDX_HEAD_DIM - N_IDX_HEADS), BF16)
    return jnp.concatenate([w[:, :a0], w[:, a1:a2], w[:, a4:], w[:, a3:a4], w[:, a0:a1], w[:, a2:a3], pad], axis=1)


def kernel(x, p, positions, w_in, w_out, w_ff1, w_ff2, w_ple, w_ple_gate, pre_mix_norm, post_mix_norm,
           pre_ff_norm, post_ff_norm, ple_norm, ret_gn):
    B, S, D = x.shape
    depth = w_in.shape[0]
    T = B * S
    topk = min(TOPK_MAX, S // 4)
    ret_chunk = min(256, S)
    tab_a = _rope_table(positions, ATTN_ROT_HALF, ROPE_THETA, ATTN_HEAD_DIM)
    tab_i = _rope_table(positions, IDX_ROT_HALF, ROPE_THETA, IDX_HEAD_DIM)
    tab_r = _rope_table(positions, RET_KEY_DIM // 2, RET_THETA, RET_KEY_DIM)
    ret_consts = _retention_consts(ret_chunk)
    vec = lambda a: a.reshape(1, -1).astype(F32)

    h = x.reshape(T, D)
    for i in range(depth):
        z = _norm_proj(h, vec(pre_mix_norm[i]), _reorder_w_in(w_in[i].astype(BF16)))
        z3 = z.reshape(B, S, Z_WIDTH)
        attn = _attention(z3, tab_a, tab_i, topk)
        ret = _retention(z3, tab_r, ret_consts, vec(ret_gn[i]), ret_chunk)
        h, m = _out_proj(attn.reshape(T, ATTN_WIDTH), ret.reshape(T, RET_WIDTH), h,
                         _to_bf16(w_out, i), vec(post_mix_norm[i]), vec(pre_ff_norm[i]))
        y = _ffn(m, _to_bf16(w_ff1, i), _to_bf16(w_ff2, i))
        h = _ple(h, y, p.reshape(depth, T, -1), i, _to_bf16(w_ple_gate, i), _to_bf16(w_ple, i),
                 vec(post_ff_norm[i]), vec(ple_norm[i]))
    return h.reshape(B, S, D)
```

```python
import functools
import math

import jax
import jax.numpy as jnp
import numpy as np
from jax import lax
from jax.experimental import pallas as pl
from jax.experimental.pallas import tpu as pltpu

N_ATTN_HEADS = 8
ATTN_HEAD_DIM = 128
ATTN_ROT_HALF = 16
ROPE_THETA = 500000.0
TOPK_MAX = 256
N_IDX_HEADS = 16
IDX_HEAD_DIM = 64
IDX_ROT_HALF = 8
N_RET_HEADS = 8
RET_KEY_DIM = 64
RET_VAL_DIM = 128
RET_THETA = 10000.0
ATTN_WIDTH = N_ATTN_HEADS * ATTN_HEAD_DIM
IDX_WIDTH = N_IDX_HEADS * IDX_HEAD_DIM
RET_KEY_WIDTH = N_RET_HEADS * RET_KEY_DIM
RET_WIDTH = N_RET_HEADS * RET_VAL_DIM
RMS_EPS = 1e-6
GN_EPS = 1e-5

LANES = 128
COL_AQ = 0
COL_IQ = COL_AQ + ATTN_WIDTH
COL_RV = COL_IQ + IDX_WIDTH
COL_RG = COL_RV + RET_WIDTH
COL_RQ = COL_RG + RET_WIDTH
COL_RK = COL_RQ + RET_KEY_WIDTH
COL_AK = COL_RK + RET_KEY_WIDTH
COL_AV = COL_AK + ATTN_HEAD_DIM
COL_IKW = COL_AV + ATTN_HEAD_DIM
Z_USED = COL_IKW + LANES
Z_TILE_N = 512
Z_WIDTH = -(-Z_USED // Z_TILE_N) * Z_TILE_N

MASK_NEG = -1e30
THETA_ALL = -3e38
CUT_ALL = 1e9
BOUND_SLACK = 1.05
DENOM_SAFE = 1e-20
SEARCH_MAX_STEPS = 320
CAST_BLOCK_BYTES = 4 * 1024 * 1024
V7X_VMEM_BYTES = 64 * 1024 * 1024
VMEM_LIMIT = V7X_VMEM_BYTES - 8 * 1024 * 1024

BF16 = jnp.bfloat16
F32 = jnp.float32


def _cparams(n_grid):
    return pltpu.CompilerParams(dimension_semantics=("arbitrary",) * n_grid,
                                vmem_limit_bytes=VMEM_LIMIT)


def _rms(x, gain):
    ms = jnp.mean(x * x, axis=-1, keepdims=True)
    return x * lax.rsqrt(ms + RMS_EPS) * gain


def _rope(xb, cos, sin, swap):
    return xb.astype(F32) * cos + jnp.dot(xb, swap, preferred_element_type=F32) * sin


def _swap_matrix(group, half):
    l = jnp.arange(LANES)
    d = l % group
    partner = jnp.where(d < half, l + half, jnp.where(d < 2 * half, l - half, -1))
    return (jnp.arange(LANES)[:, None] == partner[None, :]).astype(BF16)


def _dot_t(a, b):
    return lax.dot_general(a, b, (((1,), (1,)), ((), ())), preferred_element_type=F32)


def _dot(a, b):
    return jnp.dot(a, b, preferred_element_type=F32)


def _norm_proj_kernel(h_ref, g_ref, w_ref, z_ref, *, sub, tn):
    for r0 in range(0, h_ref.shape[0], sub):
        sl = slice(r0, r0 + sub)
        a = _rms(h_ref[sl, :], g_ref[...]).astype(BF16)
        for c0 in range(0, w_ref.shape[1], tn):
            z_ref[sl, c0:c0 + tn] = _dot(a, w_ref[:, c0:c0 + tn]).astype(z_ref.dtype)


def _norm_proj(h, gain, w):
    T, D = h.shape
    N = w.shape[1]
    tm = min(512, T)
    return pl.pallas_call(
        functools.partial(_norm_proj_kernel, sub=min(256, tm), tn=Z_TILE_N),
        grid=(T // tm,),
        in_specs=[pl.BlockSpec((tm, D), lambda i: (i, 0)),
                  pl.BlockSpec((1, D), lambda i: (0, 0)),
                  pl.BlockSpec((D, N), lambda i: (0, 0), pipeline_mode=pl.Buffered(1))],
        out_specs=pl.BlockSpec((tm, N), lambda i: (i, 0)),
        out_shape=jax.ShapeDtypeStruct((T, N), BF16),
        compiler_params=_cparams(1),
        name="norm_proj",
    )(h, gain, w)


def _attn_kernel(aq_ref, iq_ref, iwq_ref, ak_ref, av_ref, ikw_ref,
                 taq_ref, tak_ref, tiq_ref, tik_ref, pa_ref, pi_ref, ex_ref, o_ref,
                 kr_ref, kd0_ref, kd1_ref, vaug_ref, qa_ref, qi_ref, wb_ref, score_ref, score_t_ref,
                 th_ref, cut_ref, s_ref, m_ref, acc_ref, bound_ref, ksq_ref, tie_flag_ref, *, topk, tq, tk):
    S = ak_ref.shape[1]
    qi = pl.program_id(1)
    n_chunks = (qi * tq + tq + tk - 1) // tk
    reps = tk // LANES
    rows_all = N_ATTN_HEADS * tq
    ones_sq = jnp.ones((LANES, LANES), BF16)

    @pl.when(qi == 0)
    def _():
        rows = min(256, S)

        def body(r, _):
            r0 = pl.multiple_of(r * rows, rows)
            sl = pl.ds(r0, rows)
            kr = _rope(ak_ref[0, sl, :], tak_ref[0, sl, :LANES], tak_ref[0, sl, LANES:], pa_ref[...])
            kr_ref[sl, :] = kr.astype(BF16)
            ksq = _dot((kr * kr).astype(BF16), ones_sq).reshape(rows // 8, 8, LANES)
            ksq_ref[...] = jnp.maximum(jnp.where(r == 0, 0.0, ksq_ref[...]), jnp.max(ksq, axis=0))
            lane_r = lax.broadcasted_iota(jnp.int32, (rows, LANES), 1)
            ik = _rope(ikw_ref[0, sl, :], tik_ref[0, sl, :LANES], tik_ref[0, sl, LANES:], pi_ref[...])
            ik = jnp.where(lane_r < IDX_HEAD_DIM, ik, 0.0)
            kd0_ref[sl, :] = ik.astype(BF16)
            kd1_ref[sl, :] = pltpu.roll(ik, IDX_HEAD_DIM, 1).astype(BF16)
            vaug_ref[sl, :] = jnp.concatenate([av_ref[0, sl, :], jnp.ones((rows, LANES), BF16)], axis=1)
            score_t_ref[sl, :] = jnp.full((rows, tq), -jnp.inf, F32)
            return 0
        lax.fori_loop(0, S // rows, body, 0)

    ca = taq_ref[0, :, :LANES] * (ATTN_HEAD_DIM ** -0.5)
    sa = taq_ref[0, :, LANES:] * (ATTN_HEAD_DIM ** -0.5)
    k_norm = jnp.sqrt(jnp.max(ksq_ref[...], axis=0, keepdims=True))
    for h in range(N_ATTN_HEADS):
        x = aq_ref[0, :, h * LANES:(h + 1) * LANES]
        qf = _rope(x, ca, sa, pa_ref[...])
        qa_ref[h * tq:(h + 1) * tq, :] = qf.astype(BF16)
        q_norm = jnp.sqrt(_dot((qf * qf).astype(BF16), ones_sq))
        bound_ref[h * tq:(h + 1) * tq, :] = q_norm * k_norm * BOUND_SLACK
    ci, si = tiq_ref[0, :, :LANES], tiq_ref[0, :, LANES:]
    for j in range(IDX_WIDTH // LANES):
        x = iq_ref[0, :, j * LANES:(j + 1) * LANES]
        qi_ref[:, j * LANES:(j + 1) * LANES] = _rope(x, ci, si, pi_ref[...]).astype(BF16)
    iw_b = jnp.dot(iwq_ref[0], ex_ref[...], preferred_element_type=F32)
    for h in range(N_IDX_HEADS):
        wb_ref[h] = iw_b[:, h * LANES:(h + 1) * LANES]

    col_in_chunk = lax.broadcasted_iota(jnp.int32, (tq, tk), 1)

    def lane_groups(x, op):
        r = x[:, :LANES]
        for g in range(1, reps):
            r = op(r, x[:, g * LANES:(g + 1) * LANES])
        return r

    def tile_lanes(x):
        return jnp.concatenate([x] * reps, axis=1) if reps > 1 else x

    key_t = lax.broadcasted_iota(jnp.int32, (tk, tq), 0)

    def fold8(x, op):
        x3 = x.reshape(tk // 8, 8, x.shape[1])
        return jnp.max(x3, axis=0) if op == "max" else (jnp.min(x3, axis=0) if op == "min" else jnp.sum(x3, axis=0))

    rb = min(LANES, tq)
    row_b = lax.broadcasted_iota(jnp.int32, (rb, tk), 0)
    col_b = lax.broadcasted_iota(jnp.int32, (rb, tk), 1)
    key_tb = lax.broadcasted_iota(jnp.int32, (tk, rb), 0)
    qry_tb = lax.broadcasted_iota(jnp.int32, (tk, rb), 1)

    def a_body(c, carry):
        rmax, rmin = carry
        c0 = pl.multiple_of(c * tk, tk)
        k0 = kd0_ref[pl.ds(c0, tk), :]
        k1 = kd1_ref[pl.ds(c0, tk), :]
        maxs, mins = [], []
        for r in range(tq // rb):
            rows = slice(r * rb, (r + 1) * rb)
            q0 = qi * tq + r * rb
            acc = jnp.zeros((rb, tk), F32)
            for j in range(IDX_WIDTH // LANES):
                qp = qi_ref[rows, j * LANES:(j + 1) * LANES]
                for par, kd in ((0, k0), (1, k1)):
                    logits = _dot_t(qp, kd)
                    acc = acc + tile_lanes(wb_ref[2 * j + par, rows, :]) * jnp.maximum(logits, 0.0)
            causal = (c0 + col_b) <= (q0 + row_b)
            score_ref[c, rows, :] = jnp.where(causal, acc, -jnp.inf)
            acc_t = acc.T
            causal_t = (c0 + key_tb) <= (q0 + qry_tb)
            sc_t = jnp.where(causal_t, acc_t, -jnp.inf)
            score_t_ref[pl.ds(c0, tk), rows] = sc_t
            maxs.append(fold8(sc_t, "max"))
            mins.append(fold8(jnp.where(causal_t, acc_t, jnp.inf), "min"))
        rmax = jnp.maximum(rmax, jnp.concatenate(maxs, axis=1))
        rmin = jnp.minimum(rmin, jnp.concatenate(mins, axis=1))
        return rmax, rmin

    extremes = (jnp.full((8, tq), -jnp.inf, F32), jnp.full((8, tq), jnp.inf, F32))
    extremes = lax.fori_loop(0, n_chunks // 2, lambda cp, e: a_body(2 * cp + 1, a_body(2 * cp, e)), extremes)
    rmax, rmin = lax.fori_loop(0, n_chunks % 2, lambda _, e: a_body(n_chunks - 1, e), extremes)

    vec = (1, tq)
    kf = float(topk)
    n_keys = qi * tq + lax.broadcasted_iota(jnp.int32, vec, 1) + 1

    lo0 = jnp.min(rmin, axis=0, keepdims=True)
    top = jnp.max(rmax, axis=0, keepdims=True)
    hi0 = top + jnp.maximum(jnp.abs(top) * 1e-6, 1e-30)
    key_f = key_t.astype(F32)

    def threshold_search(n_rows):
        def count_t(pred):
            accs = [jnp.zeros((8, tq), F32) for _ in range(4)]
            i = 0
            for r0 in range(0, n_rows, tk):
                ind = jnp.where(pred(r0, score_t_ref[r0:r0 + tk, :]), 1.0, 0.0)
                for g in range(tk // 8):
                    accs[i % 4] = accs[i % 4] + ind[g * 8:(g + 1) * 8, :]
                    i += 1
            return jnp.sum((accs[0] + accs[1]) + (accs[2] + accs[3]), axis=0, keepdims=True)

        def b_cond(carry):
            return jnp.logical_and(carry[1] > 0.0, carry[0] < SEARCH_MAX_STEPS)

        def b_body(carry):
            state = (carry[0],) + tuple(carry[2:])
            flag = jnp.max(1.0 - carry[5])
            for _ in range(steps_per_check):
                state = b_step(*state)
            return (state[0], flag) + tuple(state[1:])

        def b_step(it, lo, hi, chi, done, th, tie):
            mid = 0.5 * lo + 0.5 * hi
            cnt = count_t(lambda r0, sc: sc >= mid)
            active = done == 0.0
            adjacent = jnp.logical_or(mid <= lo, mid >= hi)
            tie_now = jnp.logical_and(active, adjacent)
            live = jnp.logical_and(active, jnp.logical_not(adjacent))
            hit = jnp.logical_and(live, cnt == kf)
            above = jnp.logical_and(live, cnt > kf)
            below = jnp.logical_and(live, cnt < kf)
            th = jnp.where(tie_now, lo, jnp.where(hit, mid, th))
            tie = jnp.where(tie_now, 1.0, tie)
            lo = jnp.where(above, mid, lo)
            hi = jnp.where(below, mid, hi)
            chi = jnp.where(below, cnt, chi)
            done = jnp.where(jnp.logical_or(tie_now, hit), 1.0, done)
            return it + 1, lo, hi, chi, done, th, tie

        steps_per_check = 2 if n_rows <= 4 * tk else 1

        init = (jnp.int32(0), jnp.float32(1.0), lo0, hi0,
                jnp.zeros(vec, F32), jnp.where(n_keys <= topk, 1.0, 0.0),
                jnp.full(vec, THETA_ALL, F32), jnp.zeros(vec, F32))
        _, _, _, _, chi, _, th, tie = lax.while_loop(b_cond, b_body, init)

        def tie_cut():
            need = kf - chi

            def body(_, carry):
                lo_i, hi_i = carry
                mid = jnp.floor(0.5 * (lo_i + hi_i))
                cnt = count_t(lambda r0, sc: jnp.logical_and(sc == th, key_f + float(r0) <= mid))
                ge = cnt >= need
                return jnp.where(ge, lo_i, mid), jnp.where(ge, mid, hi_i)

            n_iter = int(math.ceil(math.log2(S))) + 1
            _, hi_i = lax.fori_loop(0, n_iter, body, (jnp.full(vec, -1.0, F32), jnp.full(vec, float(S - 1), F32)))
            return jnp.where(tie > 0.0, hi_i, CUT_ALL)

        th_ref[...] = th
        has_tie = jnp.max(tie) > 0.0
        tie_flag_ref[0] = has_tie.astype(jnp.int32)
        cut_ref[...] = lax.cond(has_tie, tie_cut, lambda: jnp.full(vec, CUT_ALL, F32))

    pair_rows = 2 * tk
    variant = (n_chunks * tk + pair_rows - 1) // pair_rows - 1
    for v in range(-(-S // pair_rows)):
        pl.when(variant == v)(functools.partial(threshold_search, min((v + 1) * pair_rows, S)))

    th_c = jnp.broadcast_to(th_ref[...], (LANES, tq)).T
    cut_c = jnp.broadcast_to(cut_ref[...], (LANES, tq)).T
    th_t = tile_lanes(th_c)
    cut_t = tile_lanes(cut_c)

    colf_in_chunk = col_in_chunk.astype(F32)

    def masked_logits(c, with_ties):
        c0 = pl.multiple_of(c * tk, tk)
        sc = score_ref[c]
        if with_ties:
            colf = colf_in_chunk + (c * tk).astype(F32)
            sel = jnp.logical_or(sc > th_t, jnp.logical_and(sc == th_t, colf <= cut_t))
        else:
            sel = sc >= th_t
        bias = jnp.where(sel, 0.0, MASK_NEG)
        s = _dot_t(qa_ref[...], kr_ref[pl.ds(c0, tk), :])
        return (s.reshape(N_ATTN_HEADS, tq, tk) + bias[None]).reshape(rows_all, tk)

    def accumulate(c, p):
        c0 = pl.multiple_of(c * tk, tk)
        acc_ref[...] += _dot(p.astype(BF16), vaug_ref[pl.ds(c0, tk), :])

    def one_pass(with_ties):
        acc_ref[...] = jnp.zeros(acc_ref.shape, F32)
        shift = tile_lanes(bound_ref[...])

        def pair(cp, _):
            c = 2 * cp
            p0 = jnp.exp(masked_logits(c, with_ties) - shift).astype(BF16)
            p1 = jnp.exp(masked_logits(c + 1, with_ties) - shift).astype(BF16)
            r0 = pl.multiple_of(c * tk, 2 * tk)
            acc_ref[...] += _dot(jnp.concatenate([p0, p1], axis=1), vaug_ref[pl.ds(r0, 2 * tk), :])
            return 0

        def last(_, __):
            accumulate(n_chunks - 1, jnp.exp(masked_logits(n_chunks - 1, with_ties) - shift))
            return 0
        lax.fori_loop(0, n_chunks // 2, pair, 0)
        lax.fori_loop(0, n_chunks % 2, last, 0)

    def two_pass():
        m_ref[...] = jnp.full(m_ref.shape, MASK_NEG, F32)

        def body1(c, _):
            s = masked_logits(c, True)
            s_ref[c] = s
            m_ref[...] = jnp.maximum(m_ref[...], lane_groups(s, jnp.maximum))
            return 0
        lax.fori_loop(0, n_chunks, body1, 0)
        m_ref[...] = jnp.broadcast_to(jnp.max(m_ref[...], axis=1, keepdims=True), m_ref.shape)
        acc_ref[...] = jnp.zeros(acc_ref.shape, F32)

        def body2(c, _):
            accumulate(c, jnp.exp(s_ref[c] - tile_lanes(m_ref[...])))
            return 0
        lax.fori_loop(0, n_chunks, body2, 0)

    has_tie = tie_flag_ref[0] > 0
    pl.when(has_tie)(functools.partial(one_pass, True))
    pl.when(jnp.logical_not(has_tie))(functools.partial(one_pass, False))
    l_min = jnp.min(acc_ref[:, LANES:])
    pl.when(jnp.logical_not(l_min > DENOM_SAFE))(two_pass)
    out = acc_ref[:, :LANES] / acc_ref[:, LANES:]
    for h in range(N_ATTN_HEADS):
        o_ref[0, :, h * LANES:(h + 1) * LANES] = out[h * tq:(h + 1) * tq, :].astype(o_ref.dtype)


def _attention(z3, tab_a, tab_i, topk):
    B, S, _ = z3.shape
    tq = min(256, S)
    tk = min(256, S)
    assert S % (2 * tk) == 0 and S % tq == 0 and tk % tq == 0, "key chunks are consumed in pairs"
    qblk = lambda w, col: pl.BlockSpec((1, tq, w), lambda b, q: (b, q, col // w))
    kblk = lambda w, col: pl.BlockSpec((1, S, w), lambda b, q: (b, 0, col // w))
    tab_q = pl.BlockSpec((1, tq, 2 * LANES), lambda b, q: (b, q, 0))
    tab_k = pl.BlockSpec((1, S, 2 * LANES), lambda b, q: (b, 0, 0))
    rows = N_ATTN_HEADS * tq
    const = lambda a: pl.BlockSpec(a.shape, lambda b, q: (0, 0))
    perm_a = _swap_matrix(ATTN_HEAD_DIM, ATTN_ROT_HALF)
    perm_i = _swap_matrix(IDX_HEAD_DIM, IDX_ROT_HALF)
    idx_w_scale = (N_IDX_HEADS ** -0.5) * (IDX_HEAD_DIM ** -0.5)
    assert math.frexp(idx_w_scale)[0] == 0.5, "the folded scale must be a power of two to stay exact in bf16"
    expand = jnp.where(jnp.arange(LANES)[:, None] == IDX_HEAD_DIM + jnp.arange(N_IDX_HEADS * LANES)[None, :] // LANES,
                       idx_w_scale, 0.0).astype(BF16)
    return pl.pallas_call(
        functools.partial(_attn_kernel, topk=topk, tq=tq, tk=tk),
        grid=(B, S // tq),
        in_specs=[qblk(ATTN_WIDTH, COL_AQ), qblk(IDX_WIDTH, COL_IQ), qblk(LANES, COL_IKW),
                  kblk(LANES, COL_AK), kblk(LANES, COL_AV), kblk(LANES, COL_IKW),
                  tab_q, tab_k, tab_q, tab_k, const(perm_a), const(perm_i), const(expand)],
        out_specs=pl.BlockSpec((1, tq, ATTN_WIDTH), lambda b, q: (b, q, 0)),
        out_shape=jax.ShapeDtypeStruct((B, S, ATTN_WIDTH), BF16),
        scratch_shapes=[pltpu.VMEM((S, LANES), BF16),
                        pltpu.VMEM((S, LANES), BF16),
                        pltpu.VMEM((S, LANES), BF16),
                        pltpu.VMEM((S, 2 * LANES), BF16),
                        pltpu.VMEM((rows, LANES), BF16),
                        pltpu.VMEM((tq, IDX_WIDTH), BF16),
                        pltpu.VMEM((N_IDX_HEADS, tq, LANES), F32),
                        pltpu.VMEM((S // tk, tq, tk), F32),
                        pltpu.VMEM((S, tq), F32),
                        pltpu.VMEM((1, tq), F32),
                        pltpu.VMEM((1, tq), F32),
                        pltpu.VMEM((S // tk, rows, tk), F32),
                        pltpu.VMEM((rows, LANES), F32),
                        pltpu.VMEM((rows, 2 * LANES), F32),
                        pltpu.VMEM((rows, LANES), F32),
                        pltpu.VMEM((8, LANES), F32),
                        pltpu.SMEM((1,), jnp.int32)],
        compiler_params=_cparams(2),
        name="dsa_attention",
    )(z3, z3, z3, z3, z3, z3, tab_a, tab_a, tab_i, tab_i, perm_a, perm_i, expand)


def _ret_kernel(rq_ref, rk_ref, rv_ref, rg_ref, tab_ref, decay_ref, zeta_ref, xi_ref, gch_ref, gn_ref,
                o_ref, state_ref, *, chunk):
    S = rq_ref.shape[1]
    C = chunk
    half = RET_KEY_DIM
    state_ref[...] = jnp.zeros(state_ref.shape, F32)
    lane = lax.broadcasted_iota(jnp.int32, (C, LANES), 1)
    first = lane < half

    def body(c, _):
        r0 = pl.multiple_of(c * C, C)
        sl = pl.ds(r0, C)
        cos, sin = tab_ref[0, sl, :LANES], tab_ref[0, sl, LANES:]
        cos_k, sin_k = cos * (RET_KEY_DIM ** -0.5), sin * (RET_KEY_DIM ** -0.5)
        first_half = lane % RET_KEY_DIM < RET_KEY_DIM // 2

        def rope(x, c, s):
            partner = jnp.where(first_half, pltpu.roll(x, LANES - RET_KEY_DIM // 2, 1),
                                pltpu.roll(x, RET_KEY_DIM // 2, 1))
            return x * c + partner * s

        for j in range(N_RET_HEADS // 2):
            q = rope(rq_ref[0, sl, j * LANES:(j + 1) * LANES].astype(F32), cos, sin)
            k = rope(rk_ref[0, sl, j * LANES:(j + 1) * LANES].astype(F32), cos_k, sin_k)
            kzt = (k * zeta_ref[j]).T.astype(BF16)
            state = state_ref[j]
            state_b = state.astype(BF16)
            new_rows = []
            for par in range(2):
                h = 2 * j + par
                keep = first if par == 0 else jnp.logical_not(first)
                qh = jnp.where(keep, q, 0.0).astype(BF16)
                kh = jnp.where(keep, k, 0.0).astype(BF16)
                v = rv_ref[0, sl, h * LANES:(h + 1) * LANES]
                inner = _dot_t(qh, kh) * decay_ref[h]
                o = _dot(inner.astype(BF16), v) + _dot(qh, state_b) * xi_ref[h]
                new_rows.append(_dot(kzt[par * half:(par + 1) * half, :], v))
                mu = jnp.mean(o, axis=-1, keepdims=True)
                d = o - mu
                var = jnp.mean(d * d, axis=-1, keepdims=True)
                y = d * lax.rsqrt(var + GN_EPS) * gn_ref[:, h * LANES:(h + 1) * LANES]
                g = rg_ref[0, sl, h * LANES:(h + 1) * LANES].astype(F32)
                o_ref[0, sl, h * LANES:(h + 1) * LANES] = (y * (g * jax.nn.sigmoid(g))).astype(o_ref.dtype)
            state_ref[j] = gch_ref[j] * state + jnp.concatenate(new_rows, axis=0)
        return 0

    lax.fori_loop(0, S // C, body, 0)


def _retention(z3, tab_r, consts, gn_w, chunk):
    B, S, _ = z3.shape
    decay, zeta, xi, gch = consts
    blk = lambda w, col: pl.BlockSpec((1, S, w), lambda b: (b, 0, col // w))
    const = lambda a: pl.BlockSpec(a.shape, lambda b: (0,) * a.ndim)
    return pl.pallas_call(
        functools.partial(_ret_kernel, chunk=chunk),
        grid=(B,),
        in_specs=[blk(RET_KEY_WIDTH, COL_RQ), blk(RET_KEY_WIDTH, COL_RK), blk(RET_WIDTH, COL_RV),
                  blk(RET_WIDTH, COL_RG), pl.BlockSpec((1, S, 2 * LANES), lambda b: (b, 0, 0)),
                  const(decay), const(zeta), const(xi), const(gch), const(gn_w)],
        out_specs=pl.BlockSpec((1, S, RET_WIDTH), lambda b: (b, 0, 0)),
        out_shape=jax.ShapeDtypeStruct((B, S, RET_WIDTH), BF16),
        scratch_shapes=[pltpu.VMEM((N_RET_HEADS // 2, LANES, RET_VAL_DIM), F32)],
        compiler_params=_cparams(1),
        name="retention",
    )(z3, z3, z3, z3, tab_r, decay, zeta, xi, gch, gn_w)


def _retention_consts(chunk):
    C = chunk
    log_g = jnp.log(1.0 - 2.0 ** (-5.0 - jnp.arange(N_RET_HEADS, dtype=F32)))
    i = jnp.arange(C, dtype=F32)
    diff = i[:, None] - i[None, :]
    decay = jnp.where(diff[None] >= 0, jnp.exp(jnp.maximum(diff, 0.0)[None] * log_g[:, None, None]), 0.0)
    zeta = jnp.exp((C - 1.0 - i)[None, :] * log_g[:, None])
    xi = jnp.exp((i + 1.0)[None, :] * log_g[:, None])
    g_chunk = jnp.exp(C * log_g)
    pair = lambda a: a.reshape(N_RET_HEADS // 2, 2, -1)
    zeta_p = jnp.repeat(pair(zeta).transpose(0, 2, 1), RET_KEY_DIM, axis=2)
    xi_b = jnp.broadcast_to(xi[:, :, None], (N_RET_HEADS, C, RET_VAL_DIM))
    gch = jnp.broadcast_to(jnp.repeat(pair(g_chunk), RET_KEY_DIM, axis=1).reshape(N_RET_HEADS // 2, LANES, 1),
                           (N_RET_HEADS // 2, LANES, RET_VAL_DIM))
    return decay.astype(F32), zeta_p.astype(F32), xi_b.astype(F32), gch.astype(F32)


def _out_proj_kernel(attn_ref, ret_ref, h_ref, wa_ref, wr_ref, g1_ref, g2_ref, h1_ref, m_ref, *, sub):
    for r0 in range(0, h_ref.shape[0], sub):
        sl = slice(r0, r0 + sub)
        mix = _dot(attn_ref[sl, :], wa_ref[...]) + _dot(ret_ref[sl, :], wr_ref[...])
        h1 = h_ref[sl, :] + _rms(mix, g1_ref[...])
        h1_ref[sl, :] = h1
        m_ref[sl, :] = _rms(h1, g2_ref[...]).astype(m_ref.dtype)


def _out_proj(attn, ret, h, wo, g1, g2):
    T, D = h.shape
    tm = min(512, T)
    wa_rows, wr_rows = attn.shape[1], ret.shape[1]
    assert wa_rows == wr_rows and wo.shape[0] == wa_rows + wr_rows
    row = lambda w: pl.BlockSpec((tm, w), lambda i: (i, 0))
    const = lambda a: pl.BlockSpec(a.shape, lambda i: (0, 0))
    return pl.pallas_call(
        functools.partial(_out_proj_kernel, sub=min(256, tm)),
        grid=(T // tm,),
        in_specs=[row(wa_rows), row(wr_rows), row(D),
                  pl.BlockSpec((wa_rows, D), lambda i: (0, 0)), pl.BlockSpec((wr_rows, D), lambda i: (1, 0)),
                  const(g1), const(g2)],
        out_specs=[row(D), row(D)],
        out_shape=[jax.ShapeDtypeStruct((T, D), F32), jax.ShapeDtypeStruct((T, D), BF16)],
        compiler_params=_cparams(1),
        name="out_proj",
    )(attn, ret, h, wo, wo, g1, g2)


def _ffn_kernel(m_ref, w1_ref, w2_ref, y_ref):
    def step(first):
        u = jnp.maximum(_dot(m_ref[...], w1_ref[...]), 0.0)
        part = _dot((u * u).astype(BF16), w2_ref[...])
        if first:
            y_ref[...] = part
        else:
            y_ref[...] += part

    pl.when(pl.program_id(1) == 0)(functools.partial(step, True))
    pl.when(pl.program_id(1) > 0)(functools.partial(step, False))


def _ffn(m, w1, w2):
    T, D = m.shape
    F = w1.shape[1]
    tm = min(1024, T)
    tf = min(1024, F)
    return pl.pallas_call(
        _ffn_kernel,
        grid=(T // tm, F // tf),
        in_specs=[pl.BlockSpec((tm, D), lambda i, f: (i, 0)),
                  pl.BlockSpec((D, tf), lambda i, f: (0, f)),
                  pl.BlockSpec((tf, D), lambda i, f: (f, 0))],
        out_specs=pl.BlockSpec((tm, D), lambda i, f: (i, 0)),
        out_shape=jax.ShapeDtypeStruct((T, D), F32),
        compiler_params=_cparams(2),
        name="ffn",
    )(m, w1, w2)


def _ple_kernel(h_ref, y_ref, p_ref, wg_ref, wp_ref, gff_ref, g_ref, o_ref, *, sub):
    for r0 in range(0, h_ref.shape[0], sub):
        sl = slice(r0, r0 + sub)
        h = h_ref[sl, :] + _rms(y_ref[sl, :], gff_ref[...])
        gate = jax.nn.sigmoid(_dot(h.astype(BF16), wg_ref[...]))
        e = _dot(p_ref[sl, :].astype(BF16), wp_ref[...])
        o_ref[sl, :] = h + _rms(gate * e, g_ref[...])


def _ple(h, y, p, layer, wg, wp, g_ff, g):
    T, D = h.shape
    tm = min(512, T)
    row = lambda w: pl.BlockSpec((tm, w), lambda i: (i, 0))
    const = lambda a: pl.BlockSpec(a.shape, lambda i: (0, 0))
    return pl.pallas_call(
        functools.partial(_ple_kernel, sub=min(256, tm)),
        grid=(T // tm,),
        in_specs=[row(D), row(D), pl.BlockSpec((None, tm, p.shape[2]), lambda i: (layer, i, 0)),
                  const(wg), const(wp), const(g_ff), const(g)],
        out_specs=row(D),
        out_shape=jax.ShapeDtypeStruct((T, D), F32),
        compiler_params=_cparams(1),
        name="ple",
    )(h, y, p, wg, wp, g_ff, g)


def _rope_table(positions, half, theta, group):
    inv = theta ** (-jnp.arange(half, dtype=F32) / half)
    ang = positions.astype(F32)[..., None] * inv
    compact = jnp.concatenate([jnp.cos(ang), jnp.sin(ang), jnp.ones_like(ang[..., :1])], axis=-1)
    spread = np.zeros((2 * half + 1, 2 * LANES), np.float32)
    for lane in range(LANES):
        d = lane % group
        if d < 2 * half:
            spread[d % half, lane] = 1.0
            spread[half + d % half, LANES + lane] = -1.0 if d < half else 1.0
        else:
            spread[2 * half, lane] = 1.0
    return jnp.einsum("bsk,kn->bsn", compact, jnp.asarray(spread), precision=lax.Precision.HIGHEST)


def _cast_kernel(x_ref, o_ref):
    o_ref[...] = x_ref[...].astype(o_ref.dtype)


def _to_bf16(w, layer):
    _, R, C = w.shape
    tr = max(16, min(R, CAST_BLOCK_BYTES // (4 * C) // 16 * 16))
    while R % tr:
        tr -= 16
    return pl.pallas_call(
        _cast_kernel,
        grid=(R // tr,),
        in_specs=[pl.BlockSpec((None, tr, C), lambda r: (layer, r, 0))],
        out_specs=pl.BlockSpec((tr, C), lambda r: (r, 0)),
        out_shape=jax.ShapeDtypeStruct((R, C), BF16),
        compiler_params=_cparams(1),
        name="to_bf16",
    )(w)


def _reorder_w_in(w):
    a0 = ATTN_WIDTH
    a1 = a0 + 2 * ATTN_HEAD_DIM
    a2 = a1 + IDX_WIDTH
    a3 = a2 + IDX_HEAD_DIM + N_IDX_HEADS
    a4 = a3 + 2 * RET_KEY_WIDTH
    pad = jnp.zeros((w.shape[0], Z_WIDTH - Z_USED + LANES - IDX_HEAD_DIM - N_IDX_HEADS), BF16)
    return jnp.concatenate([w[:, :a0], w[:, a1:a2], w[:, a4:], w[:, a3:a4], w[:, a0:a1], w[:, a2:a3], pad], axis=1)


def kernel(x, p, positions, w_in, w_out, w_ff1, w_ff2, w_ple, w_ple_gate, pre_mix_norm, post_mix_norm,
           pre_ff_norm, post_ff_norm, ple_norm, ret_gn):
    B, S, D = x.shape
    depth = w_in.shape[0]
    T = B * S
    topk = min(TOPK_MAX, S // 4)
    ret_chunk = min(256, S)
    tab_a = _rope_table(positions, ATTN_ROT_HALF, ROPE_THETA, ATTN_HEAD_DIM)
    tab_i = _rope_table(positions, IDX_ROT_HALF, ROPE_THETA, IDX_HEAD_DIM)
    tab_r = _rope_table(positions, RET_KEY_DIM // 2, RET_THETA, RET_KEY_DIM)
    ret_consts = _retention_consts(ret_chunk)
    vec = lambda a: a.reshape(1, -1).astype(F32)

    h = x.reshape(T, D)
    for i in range(depth):
        z = _norm_proj(h, vec(pre_mix_norm[i]), _reorder_w_in(w_in[i].astype(BF16)))
        z3 = z.reshape(B, S, Z_WIDTH)
        attn = _attention(z3, tab_a, tab_i, topk)
        ret = _retention(z3, tab_r, ret_consts, vec(ret_gn[i]), ret_chunk)
        h, m = _out_proj(attn.reshape(T, ATTN_WIDTH), ret.reshape(T, RET_WIDTH), h,
                         _to_bf16(w_out, i), vec(post_mix_norm[i]), vec(pre_ff_norm[i]))
        y = _ffn(m, _to_bf16(w_ff1, i), _to_bf16(w_ff2, i))
        h = _ple(h, y, p.reshape(depth, T, -1), i, _to_bf16(w_ple_gate, i), _to_bf16(w_ple, i),
                 vec(post_ff_norm[i]), vec(ple_norm[i]))
    return h.reshape(B, S, D)
```

```python
import functools
import math

import jax
import jax.numpy as jnp
import numpy as np
from jax import lax
from jax.experimental import pallas as pl
from jax.experimental.pallas import tpu as pltpu

N_ATTN_HEADS = 8
ATTN_HEAD_DIM = 128
ATTN_ROT_HALF = 16
ROPE_THETA = 500000.0
TOPK_MAX = 256
N_IDX_HEADS = 16
IDX_HEAD_DIM = 64
IDX_ROT_HALF = 8
N_RET_HEADS = 8
RET_KEY_DIM = 64
RET_VAL_DIM = 128
RET_THETA = 10000.0
ATTN_WIDTH = N_ATTN_HEADS * ATTN_HEAD_DIM
IDX_WIDTH = N_IDX_HEADS * IDX_HEAD_DIM
RET_KEY_WIDTH = N_RET_HEADS * RET_KEY_DIM
RET_WIDTH = N_RET_HEADS * RET_VAL_DIM
RMS_EPS = 1e-6
GN_EPS = 1e-5

LANES = 128
COL_AQ = 0
COL_IQ = COL_AQ + ATTN_WIDTH
COL_RV = COL_IQ + IDX_WIDTH
COL_RG = COL_RV + RET_WIDTH
COL_RQ = COL_RG + RET_WIDTH
COL_RK = COL_RQ + RET_KEY_WIDTH
COL_AK = COL_RK + RET_KEY_WIDTH
COL_AV = COL_AK + ATTN_HEAD_DIM
COL_IKW = COL_AV + ATTN_HEAD_DIM
Z_USED = COL_IKW + LANES
Z_TILE_N = 512
Z_WIDTH = -(-Z_USED // Z_TILE_N) * Z_TILE_N

MASK_NEG = -1e30
THETA_ALL = -3e38
CUT_ALL = 1e9
BOUND_SLACK = 1.05
DENOM_SAFE = 1e-20
SEARCH_MAX_STEPS = 320
CAST_BLOCK_BYTES = 4 * 1024 * 1024
V7X_VMEM_BYTES = 64 * 1024 * 1024
VMEM_LIMIT = V7X_VMEM_BYTES - 8 * 1024 * 1024

BF16 = jnp.bfloat16
F32 = jnp.float32


def _cparams(n_grid):
    return pltpu.CompilerParams(dimension_semantics=("arbitrary",) * n_grid,
                                vmem_limit_bytes=VMEM_LIMIT)


def _rms(x, gain):
    ms = jnp.mean(x * x, axis=-1, keepdims=True)
    return x * lax.rsqrt(ms + RMS_EPS) * gain


def _rope(xb, cos, sin, swap):
    return xb.astype(F32) * cos + jnp.dot(xb, swap, preferred_element_type=F32) * sin


def _swap_matrix(group, half):
    l = jnp.arange(LANES)
    d = l % group
    partner = jnp.where(d < half, l + half, jnp.where(d < 2 * half, l - half, -1))
    return (jnp.arange(LANES)[:, None] == partner[None, :]).astype(BF16)


def _dot_t(a, b):
    return lax.dot_general(a, b, (((1,), (1,)), ((), ())), preferred_element_type=F32)


def _dot(a, b):
    return jnp.dot(a, b, preferred_element_type=F32)


def _norm_proj_kernel(h_ref, g_ref, w_ref, z_ref, *, sub, tn):
    for r0 in range(0, h_ref.shape[0], sub):
        sl = slice(r0, r0 + sub)
        a = _rms(h_ref[sl, :], g_ref[...]).astype(BF16)
        for c0 in range(0, w_ref.shape[1], tn):
            z_ref[sl, c0:c0 + tn] = _dot(a, w_ref[:, c0:c0 + tn]).astype(z_ref.dtype)


def _norm_proj(h, gain, w):
    T, D = h.shape
    N = w.shape[1]
    tm = min(512, T)
    return pl.pallas_call(
        functools.partial(_norm_proj_kernel, sub=min(256, tm), tn=Z_TILE_N),
        grid=(T // tm,),
        in_specs=[pl.BlockSpec((tm, D), lambda i: (i, 0)),
                  pl.BlockSpec((1, D), lambda i: (0, 0)),
                  pl.BlockSpec((D, N), lambda i: (0, 0), pipeline_mode=pl.Buffered(1))],
        out_specs=pl.BlockSpec((tm, N), lambda i: (i, 0)),
        out_shape=jax.ShapeDtypeStruct((T, N), BF16),
        compiler_params=_cparams(1),
        name="norm_proj",
    )(h, gain, w)


def _attn_kernel(aq_ref, iq_ref, iwq_ref, ak_ref, av_ref, ikw_ref,
                 taq_ref, tak_ref, tiq_ref, tik_ref, pa_ref, pi_ref, ex_ref, o_ref,
                 kr_ref, kd0_ref, kd1_ref, vaug_ref, qa_ref, qi_ref, wb_ref, score_ref, score_t_ref,
                 th_ref, cut_ref, s_ref, m_ref, acc_ref, bound_ref, ksq_ref, tie_flag_ref, *, topk, tq, tk):
    S = ak_ref.shape[1]
    qi = pl.program_id(1)
    n_chunks = (qi * tq + tq + tk - 1) // tk
    reps = tk // LANES
    rows_all = N_ATTN_HEADS * tq
    ones_sq = jnp.ones((LANES, LANES), BF16)

    @pl.when(qi == 0)
    def _():
        rows = min(256, S)

        def body(r, _):
            r0 = pl.multiple_of(r * rows, rows)
            sl = pl.ds(r0, rows)
            kr = _rope(ak_ref[0, sl, :], tak_ref[0, sl, :LANES], tak_ref[0, sl, LANES:], pa_ref[...])
            kr_ref[sl, :] = kr.astype(BF16)
            ksq = _dot((kr * kr).astype(BF16), ones_sq).reshape(rows // 8, 8, LANES)
            ksq_ref[...] = jnp.maximum(jnp.where(r == 0, 0.0, ksq_ref[...]), jnp.max(ksq, axis=0))
            lane_r = lax.broadcasted_iota(jnp.int32, (rows, LANES), 1)
            ik = _rope(ikw_ref[0, sl, :], tik_ref[0, sl, :LANES], tik_ref[0, sl, LANES:], pi_ref[...])
            ik = jnp.where(lane_r < IDX_HEAD_DIM, ik, 0.0)
            kd0_ref[sl, :] = ik.astype(BF16)
            kd1_ref[sl, :] = pltpu.roll(ik, IDX_HEAD_DIM, 1).astype(BF16)
            vaug_ref[sl, :] = jnp.concatenate([av_ref[0, sl, :], jnp.ones((rows, LANES), BF16)], axis=1)
            return 0
        lax.fori_loop(0, S // rows, body, 0)

    ca = taq_ref[0, :, :LANES] * (ATTN_HEAD_DIM ** -0.5)
    sa = taq_ref[0, :, LANES:] * (ATTN_HEAD_DIM ** -0.5)
    k_norm = jnp.sqrt(jnp.max(ksq_ref[...], axis=0, keepdims=True))
    for h in range(N_ATTN_HEADS):
        x = aq_ref[0, :, h * LANES:(h + 1) * LANES]
        qf = _rope(x, ca, sa, pa_ref[...])
        qa_ref[h * tq:(h + 1) * tq, :] = qf.astype(BF16)
        q_norm = jnp.sqrt(_dot((qf * qf).astype(BF16), ones_sq))
        bound_ref[h * tq:(h + 1) * tq, :] = q_norm * k_norm * BOUND_SLACK
    ci, si = tiq_ref[0, :, :LANES], tiq_ref[0, :, LANES:]
    for j in range(IDX_WIDTH // LANES):
        x = iq_ref[0, :, j * LANES:(j + 1) * LANES]
        qi_ref[:, j * LANES:(j + 1) * LANES] = _rope(x, ci, si, pi_ref[...]).astype(BF16)
    iw_b = jnp.dot(iwq_ref[0], ex_ref[...], preferred_element_type=F32)
    for h in range(N_IDX_HEADS):
        wb_ref[h] = iw_b[:, h * LANES:(h + 1) * LANES]

    col_in_chunk = lax.broadcasted_iota(jnp.int32, (tq, tk), 1)

    def lane_groups(x, op):
        r = x[:, :LANES]
        for g in range(1, reps):
            r = op(r, x[:, g * LANES:(g + 1) * LANES])
        return r

    def tile_lanes(x):
        return jnp.concatenate([x] * reps, axis=1) if reps > 1 else x

    key_t = lax.broadcasted_iota(jnp.int32, (tk, tq), 0)

    def fold8(x, op):
        x3 = x.reshape(tk // 8, 8, x.shape[1])
        return jnp.max(x3, axis=0) if op == "max" else (jnp.min(x3, axis=0) if op == "min" else jnp.sum(x3, axis=0))

    rb = min(LANES, tq)
    row_b = lax.broadcasted_iota(jnp.int32, (rb, tk), 0)
    col_b = lax.broadcasted_iota(jnp.int32, (rb, tk), 1)
    key_tb = lax.broadcasted_iota(jnp.int32, (tk, rb), 0)
    qry_tb = lax.broadcasted_iota(jnp.int32, (tk, rb), 1)

    def a_body(c, carry):
        rmax, rmin = carry
        c0 = pl.multiple_of(c * tk, tk)
        k0 = kd0_ref[pl.ds(c0, tk), :]
        k1 = kd1_ref[pl.ds(c0, tk), :]
        maxs, mins = [], []
        for r in range(tq // rb):
            rows = slice(r * rb, (r + 1) * rb)
            q0 = qi * tq + r * rb
            acc = jnp.zeros((rb, tk), F32)
            for j in range(IDX_WIDTH // LANES):
                qp = qi_ref[rows, j * LANES:(j + 1) * LANES]
                for par, kd in ((0, k0), (1, k1)):
                    logits = _dot_t(qp, kd)
                    acc = acc + tile_lanes(wb_ref[2 * j + par, rows, :]) * jnp.maximum(logits, 0.0)
            causal = (c0 + col_b) <= (q0 + row_b)
            score_ref[c, rows, :] = jnp.where(causal, acc, -jnp.inf)
            acc_t = acc.T
            causal_t = (c0 + key_tb) <= (q0 + qry_tb)
            sc_t = jnp.where(causal_t, acc_t, -jnp.inf)
            score_t_ref[pl.ds(c0, tk), rows] = sc_t
            maxs.append(fold8(sc_t, "max"))
            mins.append(fold8(jnp.where(causal_t, acc_t, jnp.inf), "min"))
        rmax = jnp.maximum(rmax, jnp.concatenate(maxs, axis=1))
        rmin = jnp.minimum(rmin, jnp.concatenate(mins, axis=1))
        return rmax, rmin

    extremes = (jnp.full((8, tq), -jnp.inf, F32), jnp.full((8, tq), jnp.inf, F32))
    extremes = lax.fori_loop(0, n_chunks // 2, lambda cp, e: a_body(2 * cp + 1, a_body(2 * cp, e)), extremes)
    rmax, rmin = lax.fori_loop(0, n_chunks % 2, lambda _, e: a_body(n_chunks - 1, e), extremes)

    vec = (1, tq)
    kf = float(topk)
    n_keys = qi * tq + lax.broadcasted_iota(jnp.int32, vec, 1) + 1

    lo0 = jnp.min(rmin, axis=0, keepdims=True)
    top = jnp.max(rmax, axis=0, keepdims=True)
    hi0 = top + jnp.maximum(jnp.abs(top) * 1e-6, 1e-30)
    key_f = key_t.astype(F32)

    def threshold_search(n_rows):
        def count_t(pred):
            accs = [jnp.zeros((8, tq), F32) for _ in range(4)]
            i = 0
            for r0 in range(0, n_rows, tk):
                ind = jnp.where(pred(r0, score_t_ref[r0:r0 + tk, :]), 1.0, 0.0)
                for g in range(tk // 8):
                    accs[i % 4] = accs[i % 4] + ind[g * 8:(g + 1) * 8, :]
                    i += 1
            return jnp.sum((accs[0] + accs[1]) + (accs[2] + accs[3]), axis=0, keepdims=True)

        def b_cond(carry):
            return jnp.logical_and(carry[1] > 0.0, carry[0] < SEARCH_MAX_STEPS)

        def b_body(carry):
            state = (carry[0],) + tuple(carry[2:])
            flag = jnp.max(1.0 - carry[5])
            for _ in range(steps_per_check):
                state = b_step(*state)
            return (state[0], flag) + tuple(state[1:])

        def b_step(it, lo, hi, chi, done, th, tie):
            mid = 0.5 * lo + 0.5 * hi
            cnt = count_t(lambda r0, sc: sc >= mid)
            active = done == 0.0
            adjacent = jnp.logical_or(mid <= lo, mid >= hi)
            tie_now = jnp.logical_and(active, adjacent)
            live = jnp.logical_and(active, jnp.logical_not(adjacent))
            hit = jnp.logical_and(live, cnt == kf)
            above = jnp.logical_and(live, cnt > kf)
            below = jnp.logical_and(live, cnt < kf)
            th = jnp.where(tie_now, lo, jnp.where(hit, mid, th))
            tie = jnp.where(tie_now, 1.0, tie)
            lo = jnp.where(above, mid, lo)
            hi = jnp.where(below, mid, hi)
            chi = jnp.where(below, cnt, chi)
            done = jnp.where(jnp.logical_or(tie_now, hit), 1.0, done)
            return it + 1, lo, hi, chi, done, th, tie

        steps_per_check = 2 if n_rows <= 4 * tk else 1

        init = (jnp.int32(0), jnp.float32(1.0), lo0, hi0,
                jnp.zeros(vec, F32), jnp.where(n_keys <= topk, 1.0, 0.0),
                jnp.full(vec, THETA_ALL, F32), jnp.zeros(vec, F32))
        _, _, _, _, chi, _, th, tie = lax.while_loop(b_cond, b_body, init)

        def tie_cut():
            need = kf - chi

            def body(_, carry):
                lo_i, hi_i = carry
                mid = jnp.floor(0.5 * (lo_i + hi_i))
                cnt = count_t(lambda r0, sc: jnp.logical_and(sc == th, key_f + float(r0) <= mid))
                ge = cnt >= need
                return jnp.where(ge, lo_i, mid), jnp.where(ge, mid, hi_i)

            n_iter = int(math.ceil(math.log2(S))) + 1
            _, hi_i = lax.fori_loop(0, n_iter, body, (jnp.full(vec, -1.0, F32), jnp.full(vec, float(S - 1), F32)))
            return jnp.where(tie > 0.0, hi_i, CUT_ALL)

        th_ref[...] = th
        has_tie = jnp.max(tie) > 0.0
        tie_flag_ref[0] = has_tie.astype(jnp.int32)
        cut_ref[...] = lax.cond(has_tie, tie_cut, lambda: jnp.full(vec, CUT_ALL, F32))

    for v in range(S // tk):
        pl.when(n_chunks == v + 1)(functools.partial(threshold_search, (v + 1) * tk))

    th_c = jnp.broadcast_to(th_ref[...], (LANES, tq)).T
    cut_c = jnp.broadcast_to(cut_ref[...], (LANES, tq)).T
    th_t = tile_lanes(th_c)
    cut_t = tile_lanes(cut_c)

    colf_in_chunk = col_in_chunk.astype(F32)

    def masked_logits(c, with_ties):
        c0 = pl.multiple_of(c * tk, tk)
        sc = score_ref[c]
        if with_ties:
            colf = colf_in_chunk + (c * tk).astype(F32)
            sel = jnp.logical_or(sc > th_t, jnp.logical_and(sc == th_t, colf <= cut_t))
        else:
            sel = sc >= th_t
        bias = jnp.where(sel, 0.0, MASK_NEG)
        s = _dot_t(qa_ref[...], kr_ref[pl.ds(c0, tk), :])
        return (s.reshape(N_ATTN_HEADS, tq, tk) + bias[None]).reshape(rows_all, tk)

    def accumulate(c, p):
        c0 = pl.multiple_of(c * tk, tk)
        acc_ref[...] += _dot(p.astype(BF16), vaug_ref[pl.ds(c0, tk), :])

    def one_pass(with_ties):
        acc_ref[...] = jnp.zeros(acc_ref.shape, F32)
        shift = tile_lanes(bound_ref[...])

        def pair(cp, _):
            c = 2 * cp
            p0 = jnp.exp(masked_logits(c, with_ties) - shift).astype(BF16)
            p1 = jnp.exp(masked_logits(c + 1, with_ties) - shift).astype(BF16)
            r0 = pl.multiple_of(c * tk, 2 * tk)
            acc_ref[...] += _dot(jnp.concatenate([p0, p1], axis=1), vaug_ref[pl.ds(r0, 2 * tk), :])
            return 0

        def last(_, __):
            accumulate(n_chunks - 1, jnp.exp(masked_logits(n_chunks - 1, with_ties) - shift))
            return 0
        lax.fori_loop(0, n_chunks // 2, pair, 0)
        lax.fori_loop(0, n_chunks % 2, last, 0)

    def two_pass():
        m_ref[...] = jnp.full(m_ref.shape, MASK_NEG, F32)

        def body1(c, _):
            s = masked_logits(c, True)
            s_ref[c] = s
            m_ref[...] = jnp.maximum(m_ref[...], lane_groups(s, jnp.maximum))
            return 0
        lax.fori_loop(0, n_chunks, body1, 0)
        m_ref[...] = jnp.broadcast_to(jnp.max(m_ref[...], axis=1, keepdims=True), m_ref.shape)
        acc_ref[...] = jnp.zeros(acc_ref.shape, F32)

        def body2(c, _):
            accumulate(c, jnp.exp(s_ref[c] - tile_lanes(m_ref[...])))
            return 0
        lax.fori_loop(0, n_chunks, body2, 0)

    has_tie = tie_flag_ref[0] > 0
    pl.when(has_tie)(functools.partial(one_pass, True))
    pl.when(jnp.logical_not(has_tie))(functools.partial(one_pass, False))
    l_min = jnp.min(acc_ref[:, LANES:])
    pl.when(jnp.logical_not(l_min > DENOM_SAFE))(two_pass)
    out = acc_ref[:, :LANES] / acc_ref[:, LANES:]
    for h in range(N_ATTN_HEADS):
        o_ref[0, :, h * LANES:(h + 1) * LANES] = out[h * tq:(h + 1) * tq, :].astype(o_ref.dtype)


def _attention(z3, tab_a, tab_i, topk):
    B, S, _ = z3.shape
    tq = min(256, S)
    tk = min(256, S)
    assert S % (2 * tk) == 0 and S % tq == 0 and tk % tq == 0, "key chunks are consumed in pairs"
    qblk = lambda w, col: pl.BlockSpec((1, tq, w), lambda b, q: (b, q, col // w))
    kblk = lambda w, col: pl.BlockSpec((1, S, w), lambda b, q: (b, 0, col // w))
    tab_q = pl.BlockSpec((1, tq, 2 * LANES), lambda b, q: (b, q, 0))
    tab_k = pl.BlockSpec((1, S, 2 * LANES), lambda b, q: (b, 0, 0))
    rows = N_ATTN_HEADS * tq
    const = lambda a: pl.BlockSpec(a.shape, lambda b, q: (0, 0))
    perm_a = _swap_matrix(ATTN_HEAD_DIM, ATTN_ROT_HALF)
    perm_i = _swap_matrix(IDX_HEAD_DIM, IDX_ROT_HALF)
    idx_w_scale = (N_IDX_HEADS ** -0.5) * (IDX_HEAD_DIM ** -0.5)
    assert math.frexp(idx_w_scale)[0] == 0.5, "the folded scale must be a power of two to stay exact in bf16"
    expand = jnp.where(jnp.arange(LANES)[:, None] == IDX_HEAD_DIM + jnp.arange(N_IDX_HEADS * LANES)[None, :] // LANES,
                       idx_w_scale, 0.0).astype(BF16)
    return pl.pallas_call(
        functools.partial(_attn_kernel, topk=topk, tq=tq, tk=tk),
        grid=(B, S // tq),
        in_specs=[qblk(ATTN_WIDTH, COL_AQ), qblk(IDX_WIDTH, COL_IQ), qblk(LANES, COL_IKW),
                  kblk(LANES, COL_AK), kblk(LANES, COL_AV), kblk(LANES, COL_IKW),
                  tab_q, tab_k, tab_q, tab_k, const(perm_a), const(perm_i), const(expand)],
        out_specs=pl.BlockSpec((1, tq, ATTN_WIDTH), lambda b, q: (b, q, 0)),
        out_shape=jax.ShapeDtypeStruct((B, S, ATTN_WIDTH), BF16),
        scratch_shapes=[pltpu.VMEM((S, LANES), BF16),
                        pltpu.VMEM((S, LANES), BF16),
                        pltpu.VMEM((S, LANES), BF16),
                        pltpu.VMEM((S, 2 * LANES), BF16),
                        pltpu.VMEM((rows, LANES), BF16),
                        pltpu.VMEM((tq, IDX_WIDTH), BF16),
                        pltpu.VMEM((N_IDX_HEADS, tq, LANES), F32),
                        pltpu.VMEM((S // tk, tq, tk), F32),
                        pltpu.VMEM((S, tq), F32),
                        pltpu.VMEM((1, tq), F32),
                        pltpu.VMEM((1, tq), F32),
                        pltpu.VMEM((S // tk, rows, tk), F32),
                        pltpu.VMEM((rows, LANES), F32),
                        pltpu.VMEM((rows, 2 * LANES), F32),
                        pltpu.VMEM((rows, LANES), F32),
                        pltpu.VMEM((8, LANES), F32),
                        pltpu.SMEM((1,), jnp.int32)],
        compiler_params=_cparams(2),
        name="dsa_attention",
    )(z3, z3, z3, z3, z3, z3, tab_a, tab_a, tab_i, tab_i, perm_a, perm_i, expand)


def _ret_kernel(rq_ref, rk_ref, rv_ref, rg_ref, tab_ref, decay_ref, zeta_ref, xi_ref, gch_ref, gn_ref,
                o_ref, state_ref, *, chunk):
    S = rq_ref.shape[1]
    C = chunk
    half = RET_KEY_DIM
    state_ref[...] = jnp.zeros(state_ref.shape, F32)
    lane = lax.broadcasted_iota(jnp.int32, (C, LANES), 1)
    first = lane < half

    def body(c, _):
        r0 = pl.multiple_of(c * C, C)
        sl = pl.ds(r0, C)
        cos, sin = tab_ref[0, sl, :LANES], tab_ref[0, sl, LANES:]
        cos_k, sin_k = cos * (RET_KEY_DIM ** -0.5), sin * (RET_KEY_DIM ** -0.5)
        first_half = lane % RET_KEY_DIM < RET_KEY_DIM // 2

        def rope(x, c, s):
            partner = jnp.where(first_half, pltpu.roll(x, LANES - RET_KEY_DIM // 2, 1),
                                pltpu.roll(x, RET_KEY_DIM // 2, 1))
            return x * c + partner * s

        for j in range(N_RET_HEADS // 2):
            q = rope(rq_ref[0, sl, j * LANES:(j + 1) * LANES].astype(F32), cos, sin)
            k = rope(rk_ref[0, sl, j * LANES:(j + 1) * LANES].astype(F32), cos_k, sin_k)
            kzt = (k * zeta_ref[j]).T.astype(BF16)
            state = state_ref[j]
            state_b = state.astype(BF16)
            new_rows = []
            for par in range(2):
                h = 2 * j + par
                keep = first if par == 0 else jnp.logical_not(first)
                qh = jnp.where(keep, q, 0.0).astype(BF16)
                kh = jnp.where(keep, k, 0.0).astype(BF16)
                v = rv_ref[0, sl, h * LANES:(h + 1) * LANES]
                inner = _dot_t(qh, kh) * decay_ref[h]
                o = _dot(inner.astype(BF16), v) + _dot(qh, state_b) * xi_ref[h]
                new_rows.append(_dot(kzt[par * half:(par + 1) * half, :], v))
                mu = jnp.mean(o, axis=-1, keepdims=True)
                d = o - mu
                var = jnp.mean(d * d, axis=-1, keepdims=True)
                y = d * lax.rsqrt(var + GN_EPS) * gn_ref[:, h * LANES:(h + 1) * LANES]
                g = rg_ref[0, sl, h * LANES:(h + 1) * LANES].astype(F32)
                o_ref[0, sl, h * LANES:(h + 1) * LANES] = (y * (g * jax.nn.sigmoid(g))).astype(o_ref.dtype)
            state_ref[j] = gch_ref[j] * state + jnp.concatenate(new_rows, axis=0)
        return 0

    lax.fori_loop(0, S // C, body, 0)


def _retention(z3, tab_r, consts, gn_w, chunk):
    B, S, _ = z3.shape
    decay, zeta, xi, gch = consts
    blk = lambda w, col: pl.BlockSpec((1, S, w), lambda b: (b, 0, col // w))
    const = lambda a: pl.BlockSpec(a.shape, lambda b: (0,) * a.ndim)
    return pl.pallas_call(
        functools.partial(_ret_kernel, chunk=chunk),
        grid=(B,),
        in_specs=[blk(RET_KEY_WIDTH, COL_RQ), blk(RET_KEY_WIDTH, COL_RK), blk(RET_WIDTH, COL_RV),
                  blk(RET_WIDTH, COL_RG), pl.BlockSpec((1, S, 2 * LANES), lambda b: (b, 0, 0)),
                  const(decay), const(zeta), const(xi), const(gch), const(gn_w)],
        out_specs=pl.BlockSpec((1, S, RET_WIDTH), lambda b: (b, 0, 0)),
        out_shape=jax.ShapeDtypeStruct((B, S, RET_WIDTH), BF16),
        scratch_shapes=[pltpu.VMEM((N_RET_HEADS // 2, LANES, RET_VAL_DIM), F32)],
        compiler_params=_cparams(1),
        name="retention",
    )(z3, z3, z3, z3, tab_r, decay, zeta, xi, gch, gn_w)


def _retention_consts(chunk):
    C = chunk
    log_g = jnp.log(1.0 - 2.0 ** (-5.0 - jnp.arange(N_RET_HEADS, dtype=F32)))
    i = jnp.arange(C, dtype=F32)
    diff = i[:, None] - i[None, :]
    decay = jnp.where(diff[None] >= 0, jnp.exp(jnp.maximum(diff, 0.0)[None] * log_g[:, None, None]), 0.0)
    zeta = jnp.exp((C - 1.0 - i)[None, :] * log_g[:, None])
    xi = jnp.exp((i + 1.0)[None, :] * log_g[:, None])
    g_chunk = jnp.exp(C * log_g)
    pair = lambda a: a.reshape(N_RET_HEADS // 2, 2, -1)
    zeta_p = jnp.repeat(pair(zeta).transpose(0, 2, 1), RET_KEY_DIM, axis=2)
    xi_b = jnp.broadcast_to(xi[:, :, None], (N_RET_HEADS, C, RET_VAL_DIM))
    gch = jnp.broadcast_to(jnp.repeat(pair(g_chunk), RET_KEY_DIM, axis=1).reshape(N_RET_HEADS // 2, LANES, 1),
                           (N_RET_HEADS // 2, LANES, RET_VAL_DIM))
    return decay.astype(F32), zeta_p.astype(F32), xi_b.astype(F32), gch.astype(F32)


def _out_proj_kernel(attn_ref, ret_ref, h_ref, wa_ref, wr_ref, g1_ref, g2_ref, h1_ref, m_ref, *, sub):
    for r0 in range(0, h_ref.shape[0], sub):
        sl = slice(r0, r0 + sub)
        mix = _dot(attn_ref[sl, :], wa_ref[...]) + _dot(ret_ref[sl, :], wr_ref[...])
        h1 = h_ref[sl, :] + _rms(mix, g1_ref[...])
        h1_ref[sl, :] = h1
        m_ref[sl, :] = _rms(h1, g2_ref[...]).astype(m_ref.dtype)


def _out_proj(attn, ret, h, wo, g1, g2):
    T, D = h.shape
    tm = min(512, T)
    wa_rows, wr_rows = attn.shape[1], ret.shape[1]
    assert wa_rows == wr_rows and wo.shape[0] == wa_rows + wr_rows
    row = lambda w: pl.BlockSpec((tm, w), lambda i: (i, 0))
    const = lambda a: pl.BlockSpec(a.shape, lambda i: (0, 0))
    return pl.pallas_call(
        functools.partial(_out_proj_kernel, sub=min(256, tm)),
        grid=(T // tm,),
        in_specs=[row(wa_rows), row(wr_rows), row(D),
                  pl.BlockSpec((wa_rows, D), lambda i: (0, 0)), pl.BlockSpec((wr_rows, D), lambda i: (1, 0)),
                  const(g1), const(g2)],
        out_specs=[row(D), row(D)],
        out_shape=[jax.ShapeDtypeStruct((T, D), F32), jax.ShapeDtypeStruct((T, D), BF16)],
        compiler_params=_cparams(1),
        name="out_proj",
    )(attn, ret, h, wo, wo, g1, g2)


def _ffn_kernel(m_ref, w1_ref, w2_ref, y_ref):
    def step(first):
        u = jnp.maximum(_dot(m_ref[...], w1_ref[...]), 0.0)
        part = _dot((u * u).astype(BF16), w2_ref[...])
        if first:
            y_ref[...] = part
        else:
            y_ref[...] += part

    pl.when(pl.program_id(1) == 0)(functools.partial(step, True))
    pl.when(pl.program_id(1) > 0)(functools.partial(step, False))


def _ffn(m, w1, w2):
    T, D = m.shape
    F = w1.shape[1]
    tm = min(1024, T)
    tf = min(1024, F)
    return pl.pallas_call(
        _ffn_kernel,
        grid=(T // tm, F // tf),
        in_specs=[pl.BlockSpec((tm, D), lambda i, f: (i, 0)),
                  pl.BlockSpec((D, tf), lambda i, f: (0, f)),
                  pl.BlockSpec((tf, D), lambda i, f: (f, 0))],
        out_specs=pl.BlockSpec((tm, D), lambda i, f: (i, 0)),
        out_shape=jax.ShapeDtypeStruct((T, D), F32),
        compiler_params=_cparams(2),
        name="ffn",
    )(m, w1, w2)


def _ple_kernel(h_ref, y_ref, p_ref, wg_ref, wp_ref, gff_ref, g_ref, o_ref, *, sub):
    for r0 in range(0, h_ref.shape[0], sub):
        sl = slice(r0, r0 + sub)
        h = h_ref[sl, :] + _rms(y_ref[sl, :], gff_ref[...])
        gate = jax.nn.sigmoid(_dot(h.astype(BF16), wg_ref[...]))
        e = _dot(p_ref[sl, :].astype(BF16), wp_ref[...])
        o_ref[sl, :] = h + _rms(gate * e, g_ref[...])


def _ple(h, y, p, layer, wg, wp, g_ff, g):
    T, D = h.shape
    tm = min(512, T)
    row = lambda w: pl.BlockSpec((tm, w), lambda i: (i, 0))
    const = lambda a: pl.BlockSpec(a.shape, lambda i: (0, 0))
    return pl.pallas_call(
        functools.partial(_ple_kernel, sub=min(256, tm)),
        grid=(T // tm,),
        in_specs=[row(D), row(D), pl.BlockSpec((None, tm, p.shape[2]), lambda i: (layer, i, 0)),
                  const(wg), const(wp), const(g_ff), const(g)],
        out_specs=row(D),
        out_shape=jax.ShapeDtypeStruct((T, D), F32),
        compiler_params=_cparams(1),
        name="ple",
    )(h, y, p, wg, wp, g_ff, g)


def _rope_table(positions, half, theta, group):
    inv = theta ** (-jnp.arange(half, dtype=F32) / half)
    ang = positions.astype(F32)[..., None] * inv
    compact = jnp.concatenate([jnp.cos(ang), jnp.sin(ang), jnp.ones_like(ang[..., :1])], axis=-1)
    spread = np.zeros((2 * half + 1, 2 * LANES), np.float32)
    for lane in range(LANES):
        d = lane % group
        if d < 2 * half:
            spread[d % half, lane] = 1.0
            spread[half + d % half, LANES + lane] = -1.0 if d < half else 1.0
        else:
            spread[2 * half, lane] = 1.0
    return jnp.einsum("bsk,kn->bsn", compact, jnp.asarray(spread), precision=lax.Precision.HIGHEST)


def _cast_kernel(x_ref, o_ref):
    o_ref[...] = x_ref[...].astype(o_ref.dtype)


def _to_bf16(w, layer):
    _, R, C = w.shape
    tr = max(16, min(R, CAST_BLOCK_BYTES // (4 * C) // 16 * 16))
    while R % tr:
        tr -= 16
    return pl.pallas_call(
        _cast_kernel,
        grid=(R // tr,),
        in_specs=[pl.BlockSpec((None, tr, C), lambda r: (layer, r, 0))],
        out_specs=pl.BlockSpec((tr, C), lambda r: (r, 0)),
        out_shape=jax.ShapeDtypeStruct((R, C), BF16),
        compiler_params=_cparams(1),
        name="to_bf16",
    )(w)


def _reorder_w_in(w):
    a0 = ATTN_WIDTH
    a1 = a0 + 2 * ATTN_HEAD_DIM
    a2 = a1 + IDX_WIDTH
    a3 = a2 + IDX_HEAD_DIM + N_IDX_HEADS
    a4 = a3 + 2 * RET_KEY_WIDTH
    pad = jnp.zeros((w.shape[0], Z_WIDTH - Z_USED + LANES - IDX_HEAD_DIM - N_IDX_HEADS), BF16)
    return jnp.concatenate([w[:, :a0], w[:, a1:a2], w[:, a4:], w[:, a3:a4], w[:, a0:a1], w[:, a2:a3], pad], axis=1)


def kernel(x, p, positions, w_in, w_out, w_ff1, w_ff2, w_ple, w_ple_gate, pre_mix_norm, post_mix_norm,
           pre_ff_norm, post_ff_norm, ple_norm, ret_gn):
    B, S, D = x.shape
    depth = w_in.shape[0]
    T = B * S
    topk = min(TOPK_MAX, S // 4)
    ret_chunk = min(256, S)
    tab_a = _rope_table(positions, ATTN_ROT_HALF, ROPE_THETA, ATTN_HEAD_DIM)
    tab_i = _rope_table(positions, IDX_ROT_HALF, ROPE_THETA, IDX_HEAD_DIM)
    tab_r = _rope_table(positions, RET_KEY_DIM // 2, RET_THETA, RET_KEY_DIM)
    ret_consts = _retention_consts(ret_chunk)
    vec = lambda a: a.reshape(1, -1).astype(F32)

    h = x.reshape(T, D)
    for i in range(depth):
        z = _norm_proj(h, vec(pre_mix_norm[i]), _reorder_w_in(w_in[i].astype(BF16)))
        z3 = z.reshape(B, S, Z_WIDTH)
        attn = _attention(z3, tab_a, tab_i, topk)
        ret = _retention(z3, tab_r, ret_consts, vec(ret_gn[i]), ret_chunk)
        h, m = _out_proj(attn.reshape(T, ATTN_WIDTH), ret.reshape(T, RET_WIDTH), h,
                         _to_bf16(w_out, i), vec(post_mix_norm[i]), vec(pre_ff_norm[i]))
        y = _ffn(m, _to_bf16(w_ff1, i), _to_bf16(w_ff2, i))
        h = _ple(h, y, p.reshape(depth, T, -1), i, _to_bf16(w_ple_gate, i), _to_bf16(w_ple, i),
                 vec(post_ff_norm[i]), vec(ple_norm[i]))
    return h.reshape(B, S, D)
```

```python
import functools
import math

import jax
import jax.numpy as jnp
import numpy as np
from jax import lax
from jax.experimental import pallas as pl
from jax.experimental.pallas import tpu as pltpu

N_ATTN_HEADS = 8
ATTN_HEAD_DIM = 128
ATTN_ROT_HALF = 16
ROPE_THETA = 500000.0
TOPK_MAX = 256
N_IDX_HEADS = 16
IDX_HEAD_DIM = 64
IDX_ROT_HALF = 8
N_RET_HEADS = 8
RET_KEY_DIM = 64
RET_VAL_DIM = 128
RET_THETA = 10000.0
ATTN_WIDTH = N_ATTN_HEADS * ATTN_HEAD_DIM
IDX_WIDTH = N_IDX_HEADS * IDX_HEAD_DIM
RET_KEY_WIDTH = N_RET_HEADS * RET_KEY_DIM
RET_WIDTH = N_RET_HEADS * RET_VAL_DIM
RMS_EPS = 1e-6
GN_EPS = 1e-5

LANES = 128
COL_AQ = 0
COL_IQ = COL_AQ + ATTN_WIDTH
COL_RV = COL_IQ + IDX_WIDTH
COL_RG = COL_RV + RET_WIDTH
COL_RQ = COL_RG + RET_WIDTH
COL_RK = COL_RQ + RET_KEY_WIDTH
COL_AK = COL_RK + RET_KEY_WIDTH
COL_AV = COL_AK + ATTN_HEAD_DIM
COL_IKW = COL_AV + ATTN_HEAD_DIM
Z_USED = COL_IKW + LANES
Z_TILE_N = 512
Z_WIDTH = -(-Z_USED // Z_TILE_N) * Z_TILE_N

MASK_NEG = -1e30
THETA_ALL = -3e38
CUT_ALL = 1e9
BOUND_SLACK = 1.05
DENOM_SAFE = 1e-20
SEARCH_MAX_STEPS = 320
CAST_BLOCK_BYTES = 4 * 1024 * 1024
V7X_VMEM_BYTES = 64 * 1024 * 1024
VMEM_LIMIT = V7X_VMEM_BYTES - 8 * 1024 * 1024

BF16 = jnp.bfloat16
F32 = jnp.float32


def _cparams(n_grid):
    return pltpu.CompilerParams(dimension_semantics=("arbitrary",) * n_grid,
                                vmem_limit_bytes=VMEM_LIMIT)


def _rms(x, gain):
    ms = jnp.mean(x * x, axis=-1, keepdims=True)
    return x * lax.rsqrt(ms + RMS_EPS) * gain


def _rope(xb, cos, sin, swap):
    return xb.astype(F32) * cos + jnp.dot(xb, swap, preferred_element_type=F32) * sin


def _swap_matrix(group, half):
    l = jnp.arange(LANES)
    d = l % group
    partner = jnp.where(d < half, l + half, jnp.where(d < 2 * half, l - half, -1))
    return (jnp.arange(LANES)[:, None] == partner[None, :]).astype(BF16)


def _dot_t(a, b):
    return lax.dot_general(a, b, (((1,), (1,)), ((), ())), preferred_element_type=F32)


def _dot(a, b):
    return jnp.dot(a, b, preferred_element_type=F32)


def _norm_proj_kernel(h_ref, g_ref, w_ref, z_ref, *, sub, tn):
    for r0 in range(0, h_ref.shape[0], sub):
        sl = slice(r0, r0 + sub)
        a = _rms(h_ref[sl, :], g_ref[...]).astype(BF16)
        for c0 in range(0, w_ref.shape[1], tn):
            z_ref[sl, c0:c0 + tn] = _dot(a, w_ref[:, c0:c0 + tn]).astype(z_ref.dtype)


def _norm_proj(h, gain, w):
    T, D = h.shape
    N = w.shape[1]
    tm = min(512, T)
    return pl.pallas_call(
        functools.partial(_norm_proj_kernel, sub=min(256, tm), tn=Z_TILE_N),
        grid=(T // tm,),
        in_specs=[pl.BlockSpec((tm, D), lambda i: (i, 0)),
                  pl.BlockSpec((1, D), lambda i: (0, 0)),
                  pl.BlockSpec((D, N), lambda i: (0, 0), pipeline_mode=pl.Buffered(1))],
        out_specs=pl.BlockSpec((tm, N), lambda i: (i, 0)),
        out_shape=jax.ShapeDtypeStruct((T, N), BF16),
        compiler_params=_cparams(1),
        name="norm_proj",
    )(h, gain, w)


def _attn_kernel(aq_ref, iq_ref, iwq_ref, ak_ref, av_ref, ikw_ref,
                 taq_ref, tak_ref, tiq_ref, tik_ref, pa_ref, pi_ref, ex_ref, o_ref,
                 kr_ref, kd0_ref, kd1_ref, vaug_ref, qa_ref, qi_ref, wb_ref, score_ref, score_t_ref,
                 th_ref, cut_ref, s_ref, m_ref, acc_ref, bound_ref, ksq_ref, tie_flag_ref, *, topk, tq, tk):
    S = ak_ref.shape[1]
    qi = pl.program_id(1)
    n_chunks = (qi * tq + tq + tk - 1) // tk
    reps = tk // LANES
    rows_all = N_ATTN_HEADS * tq
    ones_sq = jnp.ones((LANES, LANES), BF16)

    @pl.when(qi == 0)
    def _():
        rows = min(256, S)

        def body(r, _):
            r0 = pl.multiple_of(r * rows, rows)
            sl = pl.ds(r0, rows)
            kr = _rope(ak_ref[0, sl, :], tak_ref[0, sl, :LANES], tak_ref[0, sl, LANES:], pa_ref[...])
            kr_ref[sl, :] = kr.astype(BF16)
            ksq = _dot((kr * kr).astype(BF16), ones_sq).reshape(rows // 8, 8, LANES)
            ksq_ref[...] = jnp.maximum(jnp.where(r == 0, 0.0, ksq_ref[...]), jnp.max(ksq, axis=0))
            lane_r = lax.broadcasted_iota(jnp.int32, (rows, LANES), 1)
            ik = _rope(ikw_ref[0, sl, :], tik_ref[0, sl, :LANES], tik_ref[0, sl, LANES:], pi_ref[...])
            ik = jnp.where(lane_r < IDX_HEAD_DIM, ik, 0.0)
            kd0_ref[sl, :] = ik.astype(BF16)
            kd1_ref[sl, :] = pltpu.roll(ik, IDX_HEAD_DIM, 1).astype(BF16)
            vaug_ref[sl, :] = jnp.concatenate([av_ref[0, sl, :], jnp.ones((rows, LANES), BF16)], axis=1)
            return 0
        lax.fori_loop(0, S // rows, body, 0)

    ca = taq_ref[0, :, :LANES] * (ATTN_HEAD_DIM ** -0.5)
    sa = taq_ref[0, :, LANES:] * (ATTN_HEAD_DIM ** -0.5)
    k_norm = jnp.sqrt(jnp.max(ksq_ref[...], axis=0, keepdims=True))
    for h in range(N_ATTN_HEADS):
        x = aq_ref[0, :, h * LANES:(h + 1) * LANES]
        qf = _rope(x, ca, sa, pa_ref[...])
        qa_ref[h * tq:(h + 1) * tq, :] = qf.astype(BF16)
        q_norm = jnp.sqrt(_dot((qf * qf).astype(BF16), ones_sq))
        bound_ref[h * tq:(h + 1) * tq, :] = q_norm * k_norm * BOUND_SLACK
    ci, si = tiq_ref[0, :, :LANES], tiq_ref[0, :, LANES:]
    for j in range(IDX_WIDTH // LANES):
        x = iq_ref[0, :, j * LANES:(j + 1) * LANES]
        qi_ref[:, j * LANES:(j + 1) * LANES] = _rope(x, ci, si, pi_ref[...]).astype(BF16)
    iw_b = jnp.dot(iwq_ref[0], ex_ref[...], preferred_element_type=F32)
    for h in range(N_IDX_HEADS):
        wb_ref[h] = iw_b[:, h * LANES:(h + 1) * LANES]

    col_in_chunk = lax.broadcasted_iota(jnp.int32, (tq, tk), 1)

    def lane_groups(x, op):
        r = x[:, :LANES]
        for g in range(1, reps):
            r = op(r, x[:, g * LANES:(g + 1) * LANES])
        return r

    def tile_lanes(x):
        return jnp.concatenate([x] * reps, axis=1) if reps > 1 else x

    key_t = lax.broadcasted_iota(jnp.int32, (tk, tq), 0)

    def fold8(x, op):
        x3 = x.reshape(tk // 8, 8, x.shape[1])
        return jnp.max(x3, axis=0) if op == "max" else (jnp.min(x3, axis=0) if op == "min" else jnp.sum(x3, axis=0))

    rb = min(LANES, tq)
    row_b = lax.broadcasted_iota(jnp.int32, (rb, tk), 0)
    col_b = lax.broadcasted_iota(jnp.int32, (rb, tk), 1)
    key_tb = lax.broadcasted_iota(jnp.int32, (tk, rb), 0)
    qry_tb = lax.broadcasted_iota(jnp.int32, (tk, rb), 1)

    def a_body(c, carry):
        rmax, rmin = carry
        c0 = pl.multiple_of(c * tk, tk)
        k0 = kd0_ref[pl.ds(c0, tk), :]
        k1 = kd1_ref[pl.ds(c0, tk), :]
        maxs, mins = [], []
        for r in range(tq // rb):
            rows = slice(r * rb, (r + 1) * rb)
            q0 = qi * tq + r * rb
            acc = jnp.zeros((rb, tk), F32)
            for j in range(IDX_WIDTH // LANES):
                qp = qi_ref[rows, j * LANES:(j + 1) * LANES]
                for par, kd in ((0, k0), (1, k1)):
                    logits = _dot_t(qp, kd)
                    acc = acc + tile_lanes(wb_ref[2 * j + par, rows, :]) * jnp.maximum(logits, 0.0)
            causal = (c0 + col_b) <= (q0 + row_b)
            score_ref[c, rows, :] = jnp.where(causal, acc, -jnp.inf)
            acc_t = acc.T
            causal_t = (c0 + key_tb) <= (q0 + qry_tb)
            sc_t = jnp.where(causal_t, acc_t, -jnp.inf)
            score_t_ref[pl.ds(c0, tk), rows] = sc_t
            maxs.append(fold8(sc_t, "max"))
            mins.append(fold8(jnp.where(causal_t, acc_t, jnp.inf), "min"))
        rmax = jnp.maximum(rmax, jnp.concatenate(maxs, axis=1))
        rmin = jnp.minimum(rmin, jnp.concatenate(mins, axis=1))
        return rmax, rmin

    extremes = (jnp.full((8, tq), -jnp.inf, F32), jnp.full((8, tq), jnp.inf, F32))
    extremes = lax.fori_loop(0, n_chunks // 2, lambda cp, e: a_body(2 * cp + 1, a_body(2 * cp, e)), extremes)
    rmax, rmin = lax.fori_loop(0, n_chunks % 2, lambda _, e: a_body(n_chunks - 1, e), extremes)

    vec = (1, tq)
    kf = float(topk)
    n_keys = qi * tq + lax.broadcasted_iota(jnp.int32, vec, 1) + 1

    lo0 = jnp.min(rmin, axis=0, keepdims=True)
    top = jnp.max(rmax, axis=0, keepdims=True)
    hi0 = top + jnp.maximum(jnp.abs(top) * 1e-6, 1e-30)
    key_f = key_t.astype(F32)

    def threshold_search(n_rows):
        def count_t(pred):
            accs = [jnp.zeros((8, tq), F32) for _ in range(4)]
            i = 0
            for r0 in range(0, n_rows, tk):
                ind = jnp.where(pred(r0, score_t_ref[r0:r0 + tk, :]), 1.0, 0.0)
                for g in range(tk // 8):
                    accs[i % 4] = accs[i % 4] + ind[g * 8:(g + 1) * 8, :]
                    i += 1
            return jnp.sum((accs[0] + accs[1]) + (accs[2] + accs[3]), axis=0, keepdims=True)

        def b_cond(carry):
            return jnp.logical_and(carry[1] > 0.0, carry[0] < SEARCH_MAX_STEPS)

        def b_body(carry):
            state = (carry[0],) + tuple(carry[2:])
            flag = jnp.max(1.0 - carry[5])
            for _ in range(steps_per_check):
                state = b_step(*state)
            return (state[0], flag) + tuple(state[1:])

        def b_step(it, lo, hi, chi, done, th, tie):
            mid = 0.5 * lo + 0.5 * hi
            cnt = count_t(lambda r0, sc: sc >= mid)
            active = done == 0.0
            adjacent = jnp.logical_or(mid <= lo, mid >= hi)
            tie_now = jnp.logical_and(active, adjacent)
            live = jnp.logical_and(active, jnp.logical_not(adjacent))
            hit = jnp.logical_and(live, cnt == kf)
            above = jnp.logical_and(live, cnt > kf)
            below = jnp.logical_and(live, cnt < kf)
            th = jnp.where(tie_now, lo, jnp.where(hit, mid, th))
            tie = jnp.where(tie_now, 1.0, tie)
            lo = jnp.where(above, mid, lo)
            hi = jnp.where(below, mid, hi)
            chi = jnp.where(below, cnt, chi)
            done = jnp.where(jnp.logical_or(tie_now, hit), 1.0, done)
            return it + 1, lo, hi, chi, done, th, tie

        steps_per_check = 2 if n_rows <= 4 * tk else 1

        init = (jnp.int32(0), jnp.float32(1.0), lo0, hi0,
                jnp.zeros(vec, F32), jnp.where(n_keys <= topk, 1.0, 0.0),
                jnp.full(vec, THETA_ALL, F32), jnp.zeros(vec, F32))
        _, _, _, _, chi, _, th, tie = lax.while_loop(b_cond, b_body, init)

        def tie_cut():
            need = kf - chi

            def body(_, carry):
                lo_i, hi_i = carry
                mid = jnp.floor(0.5 * (lo_i + hi_i))
                cnt = count_t(lambda r0, sc: jnp.logical_and(sc == th, key_f + float(r0) <= mid))
                ge = cnt >= need
                return jnp.where(ge, lo_i, mid), jnp.where(ge, mid, hi_i)

            n_iter = int(math.ceil(math.log2(S))) + 1
            _, hi_i = lax.fori_loop(0, n_iter, body, (jnp.full(vec, -1.0, F32), jnp.full(vec, float(S - 1), F32)))
            return jnp.where(tie > 0.0, hi_i, CUT_ALL)

        th_ref[...] = th
        has_tie = jnp.max(tie) > 0.0
        tie_flag_ref[0] = has_tie.astype(jnp.int32)
        cut_ref[...] = lax.cond(has_tie, tie_cut, lambda: jnp.full(vec, CUT_ALL, F32))

    for v in range(S // tk):
        pl.when(n_chunks == v + 1)(functools.partial(threshold_search, (v + 1) * tk))

    th_c = jnp.broadcast_to(th_ref[...], (LANES, tq)).T
    cut_c = jnp.broadcast_to(cut_ref[...], (LANES, tq)).T
    th_t = tile_lanes(th_c)
    cut_t = tile_lanes(cut_c)

    colf_in_chunk = col_in_chunk.astype(F32)

    def masked_logits(c, with_ties):
        c0 = pl.multiple_of(c * tk, tk)
        sc = score_ref[c]
        if with_ties:
            colf = colf_in_chunk + (c * tk).astype(F32)
            sel = jnp.logical_or(sc > th_t, jnp.logical_and(sc == th_t, colf <= cut_t))
        else:
            sel = sc >= th_t
        bias = jnp.where(sel, 0.0, MASK_NEG)
        s = _dot_t(qa_ref[...], kr_ref[pl.ds(c0, tk), :])
        return (s.reshape(N_ATTN_HEADS, tq, tk) + bias[None]).reshape(rows_all, tk)

    def accumulate(c, p):
        c0 = pl.multiple_of(c * tk, tk)
        acc_ref[...] += _dot(p.astype(BF16), vaug_ref[pl.ds(c0, tk), :])

    def one_pass(with_ties):
        acc_ref[...] = jnp.zeros(acc_ref.shape, F32)
        shift = tile_lanes(bound_ref[...])

        def pair(cp, _):
            c = 2 * cp
            p0 = jnp.exp(masked_logits(c, with_ties) - shift).astype(BF16)
            p1 = jnp.exp(masked_logits(c + 1, with_ties) - shift).astype(BF16)
            r0 = pl.multiple_of(c * tk, 2 * tk)
            acc_ref[...] += _dot(jnp.concatenate([p0, p1], axis=1), vaug_ref[pl.ds(r0, 2 * tk), :])
            return 0

        def last(_, __):
            accumulate(n_chunks - 1, jnp.exp(masked_logits(n_chunks - 1, with_ties) - shift))
            return 0
        lax.fori_loop(0, n_chunks // 2, pair, 0)
        lax.fori_loop(0, n_chunks % 2, last, 0)

    def two_pass():
        m_ref[...] = jnp.full(m_ref.shape, MASK_NEG, F32)

        def body1(c, _):
            s = masked_logits(c, True)
            s_ref[c] = s
            m_ref[...] = jnp.maximum(m_ref[...], lane_groups(s, jnp.maximum))
            return 0
        lax.fori_loop(0, n_chunks, body1, 0)
        m_ref[...] = jnp.broadcast_to(jnp.max(m_ref[...], axis=1, keepdims=True), m_ref.shape)
        acc_ref[...] = jnp.zeros(acc_ref.shape, F32)

        def body2(c, _):
            accumulate(c, jnp.exp(s_ref[c] - tile_lanes(m_ref[...])))
            return 0
        lax.fori_loop(0, n_chunks, body2, 0)

    has_tie = tie_flag_ref[0] > 0
    pl.when(has_tie)(functools.partial(one_pass, True))
    pl.when(jnp.logical_not(has_tie))(functools.partial(one_pass, False))
    l_min = jnp.min(acc_ref[:, LANES:])
    pl.when(jnp.logical_not(l_min > DENOM_SAFE))(two_pass)
    out = acc_ref[:, :LANES] / acc_ref[:, LANES:]
    for h in range(N_ATTN_HEADS):
        o_ref[0, :, h * LANES:(h + 1) * LANES] = out[h * tq:(h + 1) * tq, :].astype(o_ref.dtype)


def _attention(z3, tab_a, tab_i, topk):
    B, S, _ = z3.shape
    tq = min(256, S)
    tk = min(256, S)
    assert S % (2 * tk) == 0 and S % tq == 0 and tk % tq == 0, "key chunks are consumed in pairs"
    qblk = lambda w, col: pl.BlockSpec((1, tq, w), lambda b, q: (b, q, col // w))
    kblk = lambda w, col: pl.BlockSpec((1, S, w), lambda b, q: (b, 0, col // w))
    tab_q = pl.BlockSpec((1, tq, 2 * LANES), lambda b, q: (b, q, 0))
    tab_k = pl.BlockSpec((1, S, 2 * LANES), lambda b, q: (b, 0, 0))
    rows = N_ATTN_HEADS * tq
    const = lambda a: pl.BlockSpec(a.shape, lambda b, q: (0, 0))
    perm_a = _swap_matrix(ATTN_HEAD_DIM, ATTN_ROT_HALF)
    perm_i = _swap_matrix(IDX_HEAD_DIM, IDX_ROT_HALF)
    idx_w_scale = (N_IDX_HEADS ** -0.5) * (IDX_HEAD_DIM ** -0.5)
    assert math.frexp(idx_w_scale)[0] == 0.5, "the folded scale must be a power of two to stay exact in bf16"
    expand = jnp.where(jnp.arange(LANES)[:, None] == IDX_HEAD_DIM + jnp.arange(N_IDX_HEADS * LANES)[None, :] // LANES,
                       idx_w_scale, 0.0).astype(BF16)
    return pl.pallas_call(
        functools.partial(_attn_kernel, topk=topk, tq=tq, tk=tk),
        grid=(B, S // tq),
        in_specs=[qblk(ATTN_WIDTH, COL_AQ), qblk(IDX_WIDTH, COL_IQ), qblk(LANES, COL_IKW),
                  kblk(LANES, COL_AK), kblk(LANES, COL_AV), kblk(LANES, COL_IKW),
                  tab_q, tab_k, tab_q, tab_k, const(perm_a), const(perm_i), const(expand)],
        out_specs=pl.BlockSpec((1, tq, ATTN_WIDTH), lambda b, q: (b, q, 0)),
        out_shape=jax.ShapeDtypeStruct((B, S, ATTN_WIDTH + RET_WIDTH), BF16),
        scratch_shapes=[pltpu.VMEM((S, LANES), BF16),
                        pltpu.VMEM((S, LANES), BF16),
                        pltpu.VMEM((S, LANES), BF16),
                        pltpu.VMEM((S, 2 * LANES), BF16),
                        pltpu.VMEM((rows, LANES), BF16),
                        pltpu.VMEM((tq, IDX_WIDTH), BF16),
                        pltpu.VMEM((N_IDX_HEADS, tq, LANES), F32),
                        pltpu.VMEM((S // tk, tq, tk), F32),
                        pltpu.VMEM((S, tq), F32),
                        pltpu.VMEM((1, tq), F32),
                        pltpu.VMEM((1, tq), F32),
                        pltpu.VMEM((S // tk, rows, tk), F32),
                        pltpu.VMEM((rows, LANES), F32),
                        pltpu.VMEM((rows, 2 * LANES), F32),
                        pltpu.VMEM((rows, LANES), F32),
                        pltpu.VMEM((8, LANES), F32),
                        pltpu.SMEM((1,), jnp.int32)],
        compiler_params=_cparams(2),
        name="dsa_attention",
    )(z3, z3, z3, z3, z3, z3, tab_a, tab_a, tab_i, tab_i, perm_a, perm_i, expand)


def _ret_kernel(rq_ref, rk_ref, rv_ref, rg_ref, tab_ref, decay_ref, zeta_ref, xi_ref, gch_ref, gn_ref,
                mix_hbm_ref, o_ref, state_ref, *, chunk):
    del mix_hbm_ref
    S = rq_ref.shape[1]
    C = chunk
    half = RET_KEY_DIM
    state_ref[...] = jnp.zeros(state_ref.shape, F32)
    lane = lax.broadcasted_iota(jnp.int32, (C, LANES), 1)
    first = lane < half

    def body(c, _):
        r0 = pl.multiple_of(c * C, C)
        sl = pl.ds(r0, C)
        cos, sin = tab_ref[0, sl, :LANES], tab_ref[0, sl, LANES:]
        cos_k, sin_k = cos * (RET_KEY_DIM ** -0.5), sin * (RET_KEY_DIM ** -0.5)
        first_half = lane % RET_KEY_DIM < RET_KEY_DIM // 2

        def rope(x, c, s):
            partner = jnp.where(first_half, pltpu.roll(x, LANES - RET_KEY_DIM // 2, 1),
                                pltpu.roll(x, RET_KEY_DIM // 2, 1))
            return x * c + partner * s

        for j in range(N_RET_HEADS // 2):
            q = rope(rq_ref[0, sl, j * LANES:(j + 1) * LANES].astype(F32), cos, sin)
            k = rope(rk_ref[0, sl, j * LANES:(j + 1) * LANES].astype(F32), cos_k, sin_k)
            kzt = (k * zeta_ref[j]).T.astype(BF16)
            state = state_ref[j]
            state_b = state.astype(BF16)
            new_rows = []
            for par in range(2):
                h = 2 * j + par
                keep = first if par == 0 else jnp.logical_not(first)
                qh = jnp.where(keep, q, 0.0).astype(BF16)
                kh = jnp.where(keep, k, 0.0).astype(BF16)
                v = rv_ref[0, sl, h * LANES:(h + 1) * LANES]
                inner = _dot_t(qh, kh) * decay_ref[h]
                o = _dot(inner.astype(BF16), v) + _dot(qh, state_b) * xi_ref[h]
                new_rows.append(_dot(kzt[par * half:(par + 1) * half, :], v))
                mu = jnp.mean(o, axis=-1, keepdims=True)
                d = o - mu
                var = jnp.mean(d * d, axis=-1, keepdims=True)
                y = d * lax.rsqrt(var + GN_EPS) * gn_ref[:, h * LANES:(h + 1) * LANES]
                g = rg_ref[0, sl, h * LANES:(h + 1) * LANES].astype(F32)
                o_ref[0, sl, h * LANES:(h + 1) * LANES] = (y * (g * jax.nn.sigmoid(g))).astype(o_ref.dtype)
            state_ref[j] = gch_ref[j] * state + jnp.concatenate(new_rows, axis=0)
        return 0

    lax.fori_loop(0, S // C, body, 0)


def _retention(z3, mix, tab_r, consts, gn_w, chunk):
    B, S, _ = z3.shape
    assert mix.shape[2] == 2 * RET_WIDTH
    decay, zeta, xi, gch = consts
    blk = lambda w, col: pl.BlockSpec((1, S, w), lambda b: (b, 0, col // w))
    const = lambda a: pl.BlockSpec(a.shape, lambda b: (0,) * a.ndim)
    return pl.pallas_call(
        functools.partial(_ret_kernel, chunk=chunk),
        grid=(B,),
        in_specs=[blk(RET_KEY_WIDTH, COL_RQ), blk(RET_KEY_WIDTH, COL_RK), blk(RET_WIDTH, COL_RV),
                  blk(RET_WIDTH, COL_RG), pl.BlockSpec((1, S, 2 * LANES), lambda b: (b, 0, 0)),
                  const(decay), const(zeta), const(xi), const(gch), const(gn_w),
                  pl.BlockSpec(memory_space=pl.ANY)],
        out_specs=pl.BlockSpec((1, S, RET_WIDTH), lambda b: (b, 0, 1)),
        out_shape=jax.ShapeDtypeStruct(mix.shape, mix.dtype),
        input_output_aliases={10: 0},
        scratch_shapes=[pltpu.VMEM((N_RET_HEADS // 2, LANES, RET_VAL_DIM), F32)],
        compiler_params=_cparams(1),
        name="retention",
    )(z3, z3, z3, z3, tab_r, decay, zeta, xi, gch, gn_w, mix)


def _retention_consts(chunk):
    C = chunk
    log_g = jnp.log(1.0 - 2.0 ** (-5.0 - jnp.arange(N_RET_HEADS, dtype=F32)))
    i = jnp.arange(C, dtype=F32)
    diff = i[:, None] - i[None, :]
    decay = jnp.where(diff[None] >= 0, jnp.exp(jnp.maximum(diff, 0.0)[None] * log_g[:, None, None]), 0.0)
    zeta = jnp.exp((C - 1.0 - i)[None, :] * log_g[:, None])
    xi = jnp.exp((i + 1.0)[None, :] * log_g[:, None])
    g_chunk = jnp.exp(C * log_g)
    pair = lambda a: a.reshape(N_RET_HEADS // 2, 2, -1)
    zeta_p = jnp.repeat(pair(zeta).transpose(0, 2, 1), RET_KEY_DIM, axis=2)
    xi_b = jnp.broadcast_to(xi[:, :, None], (N_RET_HEADS, C, RET_VAL_DIM))
    gch = jnp.broadcast_to(jnp.repeat(pair(g_chunk), RET_KEY_DIM, axis=1).reshape(N_RET_HEADS // 2, LANES, 1),
                           (N_RET_HEADS // 2, LANES, RET_VAL_DIM))
    return decay.astype(F32), zeta_p.astype(F32), xi_b.astype(F32), gch.astype(F32)


def _out_proj_kernel(mix_ref, h_ref, wo_ref, g1_ref, g2_ref, h1_ref, m_ref, *, sub):
    for r0 in range(0, h_ref.shape[0], sub):
        sl = slice(r0, r0 + sub)
        mix = _dot(mix_ref[sl, :], wo_ref[...])
        h1 = h_ref[sl, :] + _rms(mix, g1_ref[...])
        h1_ref[sl, :] = h1
        m_ref[sl, :] = _rms(h1, g2_ref[...]).astype(m_ref.dtype)


def _out_proj(mix, h, wo, g1, g2):
    T, D = h.shape
    tm = min(512, T)
    assert wo.shape[0] == mix.shape[1]
    row = lambda w: pl.BlockSpec((tm, w), lambda i: (i, 0))
    const = lambda a: pl.BlockSpec(a.shape, lambda i: (0, 0))
    return pl.pallas_call(
        functools.partial(_out_proj_kernel, sub=min(256, tm)),
        grid=(T // tm,),
        in_specs=[row(mix.shape[1]), row(D), const(wo), const(g1), const(g2)],
        out_specs=[row(D), row(D)],
        out_shape=[jax.ShapeDtypeStruct((T, D), F32), jax.ShapeDtypeStruct((T, D), BF16)],
        compiler_params=_cparams(1),
        name="out_proj",
    )(mix, h, wo, g1, g2)


def _ffn_kernel(m_ref, w1_ref, w2_ref, y_ref):
    def step(first):
        u = jnp.maximum(_dot(m_ref[...], w1_ref[...]), 0.0)
        part = _dot((u * u).astype(BF16), w2_ref[...])
        if first:
            y_ref[...] = part
        else:
            y_ref[...] += part

    pl.when(pl.program_id(1) == 0)(functools.partial(step, True))
    pl.when(pl.program_id(1) > 0)(functools.partial(step, False))


def _ffn(m, w1, w2):
    T, D = m.shape
    F = w1.shape[1]
    tm = min(1024, T)
    tf = min(1024, F)
    return pl.pallas_call(
        _ffn_kernel,
        grid=(T // tm, F // tf),
        in_specs=[pl.BlockSpec((tm, D), lambda i, f: (i, 0)),
                  pl.BlockSpec((D, tf), lambda i, f: (0, f)),
                  pl.BlockSpec((tf, D), lambda i, f: (f, 0))],
        out_specs=pl.BlockSpec((tm, D), lambda i, f: (i, 0)),
        out_shape=jax.ShapeDtypeStruct((T, D), F32),
        compiler_params=_cparams(2),
        name="ffn",
    )(m, w1, w2)


def _ple_kernel(h_ref, y_ref, p_ref, wg_ref, wp_ref, gff_ref, g_ref, o_ref, *, sub):
    for r0 in range(0, h_ref.shape[0], sub):
        sl = slice(r0, r0 + sub)
        h = h_ref[sl, :] + _rms(y_ref[sl, :], gff_ref[...])
        gate = jax.nn.sigmoid(_dot(h.astype(BF16), wg_ref[...]))
        e = _dot(p_ref[sl, :].astype(BF16), wp_ref[...])
        o_ref[sl, :] = h + _rms(gate * e, g_ref[...])


def _ple(h, y, p, layer, wg, wp, g_ff, g):
    T, D = h.shape
    tm = min(512, T)
    row = lambda w: pl.BlockSpec((tm, w), lambda i: (i, 0))
    const = lambda a: pl.BlockSpec(a.shape, lambda i: (0, 0))
    return pl.pallas_call(
        functools.partial(_ple_kernel, sub=min(256, tm)),
        grid=(T // tm,),
        in_specs=[row(D), row(D), pl.BlockSpec((None, tm, p.shape[2]), lambda i: (layer, i, 0)),
                  const(wg), const(wp), const(g_ff), const(g)],
        out_specs=row(D),
        out_shape=jax.ShapeDtypeStruct((T, D), F32),
        compiler_params=_cparams(1),
        name="ple",
    )(h, y, p, wg, wp, g_ff, g)


def _rope_table(positions, half, theta, group):
    inv = theta ** (-jnp.arange(half, dtype=F32) / half)
    ang = positions.astype(F32)[..., None] * inv
    compact = jnp.concatenate([jnp.cos(ang), jnp.sin(ang), jnp.ones_like(ang[..., :1])], axis=-1)
    spread = np.zeros((2 * half + 1, 2 * LANES), np.float32)
    for lane in range(LANES):
        d = lane % group
        if d < 2 * half:
            spread[d % half, lane] = 1.0
            spread[half + d % half, LANES + lane] = -1.0 if d < half else 1.0
        else:
            spread[2 * half, lane] = 1.0
    return jnp.einsum("bsk,kn->bsn", compact, jnp.asarray(spread), precision=lax.Precision.HIGHEST)


def _cast_kernel(x_ref, o_ref):
    o_ref[...] = x_ref[...].astype(o_ref.dtype)


def _to_bf16(w, layer):
    _, R, C = w.shape
    tr = max(16, min(R, CAST_BLOCK_BYTES // (4 * C) // 16 * 16))
    while R % tr:
        tr -= 16
    return pl.pallas_call(
        _cast_kernel,
        grid=(R // tr,),
        in_specs=[pl.BlockSpec((None, tr, C), lambda r: (layer, r, 0))],
        out_specs=pl.BlockSpec((tr, C), lambda r: (r, 0)),
        out_shape=jax.ShapeDtypeStruct((R, C), BF16),
        compiler_params=_cparams(1),
        name="to_bf16",
    )(w)


def _reorder_w_in(w):
    a0 = ATTN_WIDTH
    a1 = a0 + 2 * ATTN_HEAD_DIM
    a2 = a1 + IDX_WIDTH
    a3 = a2 + IDX_HEAD_DIM + N_IDX_HEADS
    a4 = a3 + 2 * RET_KEY_WIDTH
    pad = jnp.zeros((w.shape[0], Z_WIDTH - Z_USED + LANES - IDX_HEAD_DIM - N_IDX_HEADS), BF16)
    return jnp.concatenate([w[:, :a0], w[:, a1:a2], w[:, a4:], w[:, a3:a4], w[:, a0:a1], w[:, a2:a3], pad], axis=1)


def kernel(x, p, positions, w_in, w_out, w_ff1, w_ff2, w_ple, w_ple_gate, pre_mix_norm, post_mix_norm,
           pre_ff_norm, post_ff_norm, ple_norm, ret_gn):
    B, S, D = x.shape
    depth = w_in.shape[0]
    T = B * S
    topk = min(TOPK_MAX, S // 4)
    ret_chunk = min(256, S)
    tab_a = _rope_table(positions, ATTN_ROT_HALF, ROPE_THETA, ATTN_HEAD_DIM)
    tab_i = _rope_table(positions, IDX_ROT_HALF, ROPE_THETA, IDX_HEAD_DIM)
    tab_r = _rope_table(positions, RET_KEY_DIM // 2, RET_THETA, RET_KEY_DIM)
    ret_consts = _retention_consts(ret_chunk)
    vec = lambda a: a.reshape(1, -1).astype(F32)

    h = x.reshape(T, D)
    for i in range(depth):
        z = _norm_proj(h, vec(pre_mix_norm[i]), _reorder_w_in(w_in[i].astype(BF16)))
        z3 = z.reshape(B, S, Z_WIDTH)
        mix = _attention(z3, tab_a, tab_i, topk)
        mix = _retention(z3, mix, tab_r, ret_consts, vec(ret_gn[i]), ret_chunk)
        h, m = _out_proj(mix.reshape(T, ATTN_WIDTH + RET_WIDTH), h,
                         _to_bf16(w_out, i), vec(post_mix_norm[i]), vec(pre_ff_norm[i]))
        y = _ffn(m, _to_bf16(w_ff1, i), _to_bf16(w_ff2, i))
        h = _ple(h, y, p.reshape(depth, T, -1), i, _to_bf16(w_ple_gate, i), _to_bf16(w_ple, i),
                 vec(post_ff_norm[i]), vec(ple_norm[i]))
    return h.reshape(B, S, D)
```
